```python
import jax
import jax.numpy as jnp
from jax import lax
import numpy as np

D_MODEL = 1024
BATCH = 2
SEQ = 8192
DEPTH = 2

GRID_W = 64
CTX_LEN = 256
N_EVEN = (DEPTH + 1) // 2
N_ODD = DEPTH // 2
EPS = 1e-6
N_MOD = 9
D_FF = 2816
FFN_RESIDUAL = 0.5

MLA_HEADS = 8
MLA_Q_RANK = 384
MLA_KV_RANK = 256
MLA_NOPE = 64
MLA_ROPE = 32
MLA_V = 64
MLA_QK = MLA_NOPE + MLA_ROPE
ROPE_AXIS_DIM = MLA_ROPE // 2
ROPE_BASE = 10000.0
Q_BLOCK = 128

MLSTM_HEADS = 4
MLSTM_DH = 128
MLSTM_WIDTH = MLSTM_HEADS * MLSTM_DH
MLSTM_CHUNK = 64

MIX_WIDTH = MLA_HEADS * MLA_V + MLSTM_WIDTH
EVEN_SPLITS = (MLA_Q_RANK, MLA_KV_RANK, MLA_ROPE, MLSTM_WIDTH, MLSTM_WIDTH, MLSTM_WIDTH, MLSTM_WIDTH, 4 * MLSTM_HEADS)
IN_EVEN = MLA_Q_RANK + MLA_KV_RANK + MLA_ROPE + 4 * MLSTM_WIDTH + 4 * MLSTM_HEADS

RNN_WIDTH = 1024
RNN_BLOCKS = 8
RNN_BLOCK_DIM = RNN_WIDTH // RNN_BLOCKS
CONV_W = 4
CONV_PAD_L = CONV_W // 2
CONV_PAD_R = CONV_W - 1 - CONV_PAD_L
LRU_C = 8.0

kernel_name = 'hybrid_mla_mlstm_rglru_prefix_block'


def rms_norm(x, g):
    xf = x.astype(jnp.float32)
    y = xf * lax.rsqrt(jnp.mean(xf * xf, axis=-1, keepdims=True) + EPS)
    return (y * g.astype(jnp.float32)).astype(x.dtype)


def ada_params(cond, w, b):
    m = jax.nn.silu(cond) @ w + b
    return jnp.split(m, N_MOD, axis=-1)


def modulate(h, shift, scale):
    return h * (1.0 + scale) + shift


def swiglu(h, w_gate, w_up, w_down):
    return (jax.nn.silu(h @ w_gate) * (h @ w_up)) @ w_down


def split_cols(p, sizes):
    idx, acc = [], 0
    for s in sizes[:-1]:
        acc += s
        idx.append(acc)
    return jnp.split(p, idx, axis=-1)


def axial_rope_tables(n_rows):
    rows = jnp.repeat(jnp.arange(n_rows), GRID_W)
    cols = jnp.tile(jnp.arange(GRID_W), n_rows)
    inv = ROPE_BASE ** (-jnp.arange(0, ROPE_AXIS_DIM, 2, dtype=jnp.float32) / ROPE_AXIS_DIM)
    ang = jnp.stack([rows, cols], axis=-1).astype(jnp.float32)[..., None] * inv
    return jnp.cos(ang), jnp.sin(ang)


def apply_axial_rope(x, cos, sin):
    shp = x.shape
    xr = x.astype(jnp.float32).reshape(shp[:-1] + (2, 2, ROPE_AXIS_DIM // 2))
    x1, x2 = xr[..., 0, :], xr[..., 1, :]
    out = jnp.stack([x1 * cos - x2 * sin, x2 * cos + x1 * sin], axis=-2)
    return out.reshape(shp).astype(x.dtype)


def mla_qkv(c_q, c_kv, k_rope, cq_g, w_uq, ckv_g, w_ukv, q_g, k_g, cos, sin):
    B, T, _ = c_q.shape
    q = (rms_norm(c_q, cq_g) @ w_uq).reshape(B, T, MLA_HEADS, MLA_QK)
    kv = (rms_norm(c_kv, ckv_g) @ w_ukv).reshape(B, T, MLA_HEADS, MLA_NOPE + MLA_V)
    q_nope = rms_norm(q[..., :MLA_NOPE], q_g[:MLA_NOPE])
    q_rope = rms_norm(q[..., MLA_NOPE:], q_g[MLA_NOPE:])
    k_nope = rms_norm(kv[..., :MLA_NOPE], k_g[:MLA_NOPE])
    k_rope = rms_norm(k_rope, k_g[MLA_NOPE:])
    v = kv[..., MLA_NOPE:]
    if cos is not None:
        q_rope = apply_axial_rope(q_rope, cos[:, None], sin[:, None])
        k_rope = apply_axial_rope(k_rope, cos, sin)
    k_rope = jnp.broadcast_to(k_rope[:, :, None, :], (B, T, MLA_HEADS, MLA_ROPE))
    q = jnp.concatenate([q_nope, q_rope], axis=-1).transpose(0, 2, 1, 3)
    k = jnp.concatenate([k_nope, k_rope], axis=-1).transpose(0, 2, 1, 3)
    return q, k, v.transpose(0, 2, 1, 3)


def attend(q, k, v):
    s = jnp.einsum('bhqd,bhkd->bhqk', q, k, preferred_element_type=jnp.float32) * (MLA_QK ** -0.5)
    p = jax.nn.softmax(s, axis=-1).astype(v.dtype)
    return jnp.einsum('bhqk,bhkd->bhqd', p, v)


def blocked_attention(q, k, v):
    B, H, T, dq = q.shape
    qb = jnp.moveaxis(q.reshape(B, H, T // Q_BLOCK, Q_BLOCK, dq), 2, 0)
    out = lax.map(lambda qi: attend(qi, k, v), qb)
    return jnp.moveaxis(out, 0, 2).reshape(B, H, T, -1)


def heads_to_tokens(a):
    B, H, T, d = a.shape
    return a.transpose(0, 2, 1, 3).reshape(B, T, H * d)


def mlstm_chunkwise(q, k, v, log_i, log_f, state0):
    B, H, T, dh = q.shape
    L = MLSTM_CHUNK
    nc = T // L

    def chunks(a):
        return jnp.moveaxis(a.reshape((B, H, nc, L) + a.shape[3:]), 2, 0)

    lower = jnp.tril(jnp.ones((L, L), dtype=bool))

    def step(carry, inp):
        C, n, m = carry
        qc, kc, vc, ic, fc = inp
        b = jnp.cumsum(fc, axis=-1)
        d = jnp.where(lower, b[..., :, None] - b[..., None, :] + ic[..., None, :], -jnp.inf)
        m_t = jnp.maximum(b + m[..., None], jnp.max(d, axis=-1))
        w_intra = jnp.exp(d - m_t[..., None])
        w_inter = jnp.exp(b + m[..., None] - m_t)
        s = jnp.einsum('bhtd,bhsd->bhts', qc, kc) * w_intra
        num = jnp.einsum('bhts,bhsd->bhtd', s, vc) + w_inter[..., None] * jnp.einsum('bhvk,bhtk->bhtv', C, qc)
        den = jnp.sum(s, axis=-1) + w_inter * jnp.einsum('bhk,bhtk->bht', n, qc)
        h = num / jnp.maximum(jnp.abs(den), jnp.exp(-m_t))[..., None]
        b_end = b[..., -1]
        g = b_end[..., None] - b + ic
        m_new = jnp.maximum(b_end + m, jnp.max(g, axis=-1))
        w_s = jnp.exp(g - m_new[..., None])
        decay = jnp.exp(b_end + m - m_new)
        C_new = decay[..., None, None] * C + jnp.einsum('bhs,bhsv,bhsk->bhvk', w_s, vc, kc)
        n_new = decay[..., None] * n + jnp.einsum('bhs,bhsk->bhk', w_s, kc)
        return (C_new, n_new, m_new), h

    state, hs = lax.scan(step, state0, (chunks(q), chunks(k), chunks(v), chunks(log_i), chunks(log_f)))
    return jnp.moveaxis(hs, 0, 2).reshape(B, H, T, dh), state


def mlstm_mixer(qx, kx, vx, ox, gx, qc, kc, vc, oc, gc, gate_b, out_g, need_ctx):
    def heads(a):
        B, T, _ = a.shape
        return a.astype(jnp.float32).reshape(B, T, MLSTM_HEADS, MLSTM_DH).transpose(0, 2, 1, 3)

    def gates(g):
        B, T, _ = g.shape
        g = (g.astype(jnp.float32) + gate_b.astype(jnp.float32)).reshape(B, T, 4, MLSTM_HEADS).transpose(2, 0, 3, 1)
        return g[0], jax.nn.log_sigmoid(g[1]), g[2], jax.nn.log_sigmoid(g[3])

    def rev(a):
        return jnp.flip(a, axis=2)

    def readout(h, o):
        B, H, T, d = h.shape
        hn = rms_norm(h.transpose(0, 2, 1, 3), out_g.reshape(MLSTM_HEADS, MLSTM_DH))
        gate = jax.nn.sigmoid(o.astype(jnp.float32)).reshape(B, T, H, d)
        return (hn * gate).reshape(B, T, H * d).astype(o.dtype)

    k_scale = MLSTM_DH ** -0.5
    Qx, Kx, Vx = heads(qx), heads(kx) * k_scale, heads(vx)
    Qc, Kc, Vc = heads(qc), heads(kc) * k_scale, heads(vc)
    ifx, lffx, ibx, lfbx = gates(gx)
    ifc, lffc, ibc, lfbc = gates(gc)
    B = qx.shape[0]
    zero = (jnp.zeros((B, MLSTM_HEADS, MLSTM_DH, MLSTM_DH), jnp.float32),
            jnp.zeros((B, MLSTM_HEADS, MLSTM_DH), jnp.float32),
            jnp.zeros((B, MLSTM_HEADS), jnp.float32))
    hcf, st_f = mlstm_chunkwise(Qc, Kc, Vc, ifc, lffc, zero)
    hcb, st_b = mlstm_chunkwise(rev(Qc), rev(Kc), rev(Vc), rev(ibc), rev(lfbc), zero)
    hxf, _ = mlstm_chunkwise(Qx, Kx, Vx, ifx, lffx, st_f)
    hxb, _ = mlstm_chunkwise(rev(Qx), rev(Kx), rev(Vx), rev(ibx), rev(lfbx), st_b)
    out_x = readout(hxf + rev(hxb), ox)
    out_c = readout(hcf + rev(hcb), oc) if need_ctx else None
    return out_x, out_c


def even_mixer(hx, hc, w_in, w_out, cq_g, w_uq, ckv_g, w_ukv, q_g, k_g, gate_b, out_g, cos, sin, need_ctx):
    cq_x, ckv_x, kr_x, mq_x, mk_x, mv_x, mo_x, mg_x = split_cols(hx @ w_in, EVEN_SPLITS)
    cq_c, ckv_c, kr_c, mq_c, mk_c, mv_c, mo_c, mg_c = split_cols(hc @ w_in, EVEN_SPLITS)
    q_x, k_x, v_x = mla_qkv(cq_x, ckv_x, kr_x, cq_g, w_uq, ckv_g, w_ukv, q_g, k_g, cos, sin)
    q_c, k_c, v_c = mla_qkv(cq_c, ckv_c, kr_c, cq_g, w_uq, ckv_g, w_ukv, q_g, k_g, None, None)
    k_all = jnp.concatenate([k_c, k_x], axis=2)
    v_all = jnp.concatenate([v_c, v_x], axis=2)
    att_x = heads_to_tokens(blocked_attention(q_x, k_all, v_all))
    ml_x, ml_c = mlstm_mixer(mq_x, mk_x, mv_x, mo_x, mg_x, mq_c, mk_c, mv_c, mo_c, mg_c, gate_b, out_g, need_ctx)
    y_x = jnp.concatenate([att_x, ml_x], axis=-1) @ w_out
    y_c = None
    if need_ctx:
        att_c = heads_to_tokens(attend(q_c, k_c, v_c))
        y_c = jnp.concatenate([att_c, ml_c], axis=-1) @ w_out
    return y_x, y_c


def short_conv(u, w, b):
    out = lax.conv_general_dilated(u, w[:, None, :].astype(u.dtype), window_strides=(1,),
                                   padding=[(CONV_PAD_L, CONV_PAD_R)],
                                   dimension_numbers=('NWC', 'WIO', 'NWC'),
                                   feature_group_count=u.shape[-1])
    return out + b


def _lin_combine(e1, e2):
    a1, b1 = e1
    a2, b2 = e2
    return a1 * a2, a2 * b1 + b2


def rglru_scan(u, w_a, b_a, w_x, b_x, lam, h0):
    B, T, R = u.shape
    uf = u.astype(jnp.float32)
    ub = uf.reshape(B, T, RNN_BLOCKS, RNN_BLOCK_DIM)
    r = jax.nn.sigmoid(jnp.einsum('btnd,nde->btne', ub, w_a.astype(jnp.float32)).reshape(B, T, R) + b_a.astype(jnp.float32))
    i = jax.nn.sigmoid(jnp.einsum('btnd,nde->btne', ub, w_x.astype(jnp.float32)).reshape(B, T, R) + b_x.astype(jnp.float32))
    log_a = -LRU_C * r * jax.nn.softplus(-lam.astype(jnp.float32))
    a = jnp.exp(log_a)
    inp = jnp.sqrt(-jnp.expm1(2.0 * log_a)) * (i * uf)
    a_cum, h_part = lax.associative_scan(_lin_combine, (a, inp), axis=1)
    h = a_cum * h0[:, None, :] + h_part
    return h, h[:, -1]


def odd_mixer(hx, hc, w_in, conv_w, conv_b, w_a, b_a, w_x, b_x, lam, w_out, need_ctx):
    gate_x, xr_x = jnp.split(hx @ w_in, 2, axis=-1)
    if need_ctx:
        gate_c, xr_c = jnp.split(hc @ w_in, 2, axis=-1)
    else:
        xr_c = hc @ w_in[:, RNN_WIDTH:]
    xc_x = short_conv(xr_x, conv_w, conv_b)
    xc_c = short_conv(xr_c, conv_w, conv_b)
    h0 = jnp.zeros((hx.shape[0], RNN_WIDTH), jnp.float32)
    hf_c, s_f = rglru_scan(xc_c, w_a[0], b_a[0], w_x[0], b_x[0], lam[0], h0)
    hb_c, s_b = rglru_scan(jnp.flip(xc_c, 1), w_a[1], b_a[1], w_x[1], b_x[1], lam[1], h0)
    hf_x, _ = rglru_scan(xc_x, w_a[0], b_a[0], w_x[0], b_x[0], lam[0], s_f)
    hb_x, _ = rglru_scan(jnp.flip(xc_x, 1), w_a[1], b_a[1], w_x[1], b_x[1], lam[1], s_b)
    y_x = ((hf_x + jnp.flip(hb_x, 1)).astype(hx.dtype) * jax.nn.gelu(gate_x)) @ w_out
    y_c = None
    if need_ctx:
        y_c = ((hf_c + jnp.flip(hb_c, 1)).astype(hc.dtype) * jax.nn.gelu(gate_c)) @ w_out
    return y_x, y_c


def setup_inputs(seed: int = 0) -> dict:
    key = jax.random.key(seed)
    ks = iter(jax.random.split(key, 32))

    def nrm(shape, scale):
        return scale * jax.random.normal(next(ks), shape, jnp.float32)

    x = nrm((BATCH, SEQ, D_MODEL), 1.0)
    c = nrm((BATCH, D_MODEL), 1.0)
    ctx = nrm((BATCH, CTX_LEN, D_MODEL), 1.0)
    c_ctx = nrm((D_MODEL,), 1.0)
    mod_w = nrm((DEPTH, D_MODEL, N_MOD * D_MODEL), 0.5 * D_MODEL ** -0.5)
    mod_b = nrm((DEPTH, N_MOD * D_MODEL), 0.02)
    norm_g = 1.0 + nrm((DEPTH, 3, D_MODEL), 0.02)
    ffn_w_gate = nrm((DEPTH, 2, D_MODEL, D_FF), D_MODEL ** -0.5)
    ffn_w_up = nrm((DEPTH, 2, D_MODEL, D_FF), D_MODEL ** -0.5)
    ffn_w_down = nrm((DEPTH, 2, D_FF, D_MODEL), D_FF ** -0.5)
    even_w_in = nrm((N_EVEN, D_MODEL, IN_EVEN), D_MODEL ** -0.5)
    even_w_out = nrm((N_EVEN, MIX_WIDTH, D_MODEL), MIX_WIDTH ** -0.5)
    mla_cq_g = 1.0 + nrm((N_EVEN, MLA_Q_RANK), 0.02)
    mla_w_uq = nrm((N_EVEN, MLA_Q_RANK, MLA_HEADS * MLA_QK), MLA_Q_RANK ** -0.5)
    mla_ckv_g = 1.0 + nrm((N_EVEN, MLA_KV_RANK), 0.02)
    mla_w_ukv = nrm((N_EVEN, MLA_KV_RANK, MLA_HEADS * (MLA_NOPE + MLA_V)), MLA_KV_RANK ** -0.5)
    mla_q_g = 1.0 + nrm((N_EVEN, MLA_QK), 0.02)
    mla_k_g = 1.0 + nrm((N_EVEN, MLA_QK), 0.02)
    i_bias = -1.0 + nrm((N_EVEN, 2, MLSTM_HEADS), 0.1)
    f_bias = jnp.linspace(3.0, 6.0, MLSTM_HEADS, dtype=jnp.float32) + nrm((N_EVEN, 2, MLSTM_HEADS), 0.1)
    mlstm_gate_b = jnp.stack([i_bias, f_bias], axis=2).reshape(N_EVEN, 4 * MLSTM_HEADS)
    mlstm_out_g = 1.0 + nrm((N_EVEN, MLSTM_WIDTH), 0.02)
    odd_w_in = nrm((N_ODD, D_MODEL, 2 * RNN_WIDTH), D_MODEL ** -0.5)
    odd_conv_w = nrm((N_ODD, CONV_W, RNN_WIDTH), CONV_W ** -0.5)
    odd_conv_b = nrm((N_ODD, RNN_WIDTH), 0.02)
    lru_w_a = nrm((N_ODD, 2, RNN_BLOCKS, RNN_BLOCK_DIM, RNN_BLOCK_DIM), RNN_BLOCK_DIM ** -0.5)
    lru_b_a = nrm((N_ODD, 2, RNN_WIDTH), 0.02)
    lru_w_x = nrm((N_ODD, 2, RNN_BLOCKS, RNN_BLOCK_DIM, RNN_BLOCK_DIM), RNN_BLOCK_DIM ** -0.5)
    lru_b_x = nrm((N_ODD, 2, RNN_WIDTH), 0.02)
    u = jax.random.uniform(next(ks), (N_ODD, 2, RNN_WIDTH), jnp.float32, 0.9, 0.999)
    a0 = u ** (1.0 / LRU_C)
    lru_lam = jnp.log(a0) - jnp.log1p(-a0)
    odd_w_out = nrm((N_ODD, RNN_WIDTH, D_MODEL), RNN_WIDTH ** -0.5)
    return {'x': x, 'c': c, 'ctx': ctx, 'c_ctx': c_ctx,
            'mod_w': mod_w, 'mod_b': mod_b, 'norm_g': norm_g,
            'ffn_w_gate': ffn_w_gate, 'ffn_w_up': ffn_w_up, 'ffn_w_down': ffn_w_down,
            'even_w_in': even_w_in, 'even_w_out': even_w_out,
            'mla_cq_g': mla_cq_g, 'mla_w_uq': mla_w_uq, 'mla_ckv_g': mla_ckv_g, 'mla_w_ukv': mla_w_ukv,
            'mla_q_g': mla_q_g, 'mla_k_g': mla_k_g,
            'mlstm_gate_b': mlstm_gate_b, 'mlstm_out_g': mlstm_out_g,
            'odd_w_in': odd_w_in, 'odd_conv_w': odd_conv_w, 'odd_conv_b': odd_conv_b,
            'lru_w_a': lru_w_a, 'lru_b_a': lru_b_a, 'lru_w_x': lru_w_x, 'lru_b_x': lru_b_x,
            'lru_lam': lru_lam, 'odd_w_out': odd_w_out}


def reference(x, c, ctx, c_ctx, mod_w, mod_b, norm_g, ffn_w_gate, ffn_w_up, ffn_w_down,
              even_w_in, even_w_out, mla_cq_g, mla_w_uq, mla_ckv_g, mla_w_ukv, mla_q_g, mla_k_g,
              mlstm_gate_b, mlstm_out_g, odd_w_in, odd_conv_w, odd_conv_b,
              lru_w_a, lru_b_a, lru_w_x, lru_b_x, lru_lam, odd_w_out):
    n_rows = x.shape[1] // GRID_W
    cos, sin = axial_rope_tables(n_rows)
    for layer in range(DEPTH):
        need_ctx = layer < DEPTH - 1
        j = layer // 2
        mx = [m[:, None, :] for m in ada_params(c, mod_w[layer], mod_b[layer])]
        mc = ada_params(c_ctx, mod_w[layer], mod_b[layer])
        x = x + FFN_RESIDUAL * mx[2] * swiglu(modulate(rms_norm(x, norm_g[layer, 0]), mx[0], mx[1]),
                                            ffn_w_gate[layer, 0], ffn_w_up[layer, 0], ffn_w_down[layer, 0])
        ctx = ctx + FFN_RESIDUAL * mc[2] * swiglu(modulate(rms_norm(ctx, norm_g[layer, 0]), mc[0], mc[1]),
                                                ffn_w_gate[layer, 0], ffn_w_up[layer, 0], ffn_w_down[layer, 0])
        hx = modulate(rms_norm(x, norm_g[layer, 1]), mx[3], mx[4])
        hc = modulate(rms_norm(ctx, norm_g[layer, 1]), mc[3], mc[4])
        if layer % 2 == 0:
            y_x, y_c = even_mixer(hx, hc, even_w_in[j], even_w_out[j], mla_cq_g[j], mla_w_uq[j], mla_ckv_g[j],
                                  mla_w_ukv[j], mla_q_g[j], mla_k_g[j], mlstm_gate_b[j], mlstm_out_g[j],
                                  cos, sin, need_ctx)
        else:
            y_x, y_c = odd_mixer(hx, hc, odd_w_in[j], odd_conv_w[j], odd_conv_b[j], lru_w_a[j], lru_b_a[j],
                                 lru_w_x[j], lru_b_x[j], lru_lam[j], odd_w_out[j], need_ctx)
        x = x + mx[5] * y_x
        x = x + FFN_RESIDUAL * mx[8] * swiglu(modulate(rms_norm(x, norm_g[layer, 2]), mx[6], mx[7]),
                                            ffn_w_gate[layer, 1], ffn_w_up[layer, 1], ffn_w_down[layer, 1])
        if need_ctx:
            ctx = ctx + mc[5] * y_c
            ctx = ctx + FFN_RESIDUAL * mc[8] * swiglu(modulate(rms_norm(ctx, norm_g[layer, 2]), mc[6], mc[7]),
                                                    ffn_w_gate[layer, 1], ffn_w_up[layer, 1], ffn_w_down[layer, 1])
    return x
```

```python
import functools

import jax
import jax.numpy as jnp
from jax import lax
from jax.experimental import pallas as pl
from jax.experimental.pallas import tpu as pltpu

F32 = jnp.float32
BF16 = jnp.bfloat16

D_MODEL = 1024
GRID_W = 64
EPS = 1e-6
N_MOD = 9
D_FF = 2816
FFN_RESIDUAL = 0.5

MLA_HEADS = 8
MLA_Q_RANK = 384
MLA_KV_RANK = 256
MLA_NOPE = 64
MLA_ROPE = 32
MLA_V = 64
MLA_QK = MLA_NOPE + MLA_ROPE
ROPE_AXIS_DIM = MLA_ROPE // 2
ROPE_BASE = 10000.0
HEAD_BLOCK = 128

MLSTM_HEADS = 4
MLSTM_DH = 128
MLSTM_WIDTH = MLSTM_HEADS * MLSTM_DH
N_GATES = 4 * MLSTM_HEADS

RNN_WIDTH = 1024
RNN_BLOCKS = 8
RNN_BLOCK_DIM = RNN_WIDTH // RNN_BLOCKS
CONV_W = 4
LRU_C = 8.0

NEG_BIG = -1e30
VMEM_LIMIT = 56 * 1024 * 1024


def _cparams(*sem):
    return pltpu.CompilerParams(dimension_semantics=sem, vmem_limit_bytes=VMEM_LIMIT)


def _const_spec(shape):
    zeros = (0,) * len(shape)
    return pl.BlockSpec(shape, lambda *_: zeros, pipeline_mode=pl.Buffered(1))


def _dotf(a, b):
    return jnp.dot(a, b, preferred_element_type=F32)


def _split3(a):
    a1 = a.astype(BF16)
    r1 = a - a1.astype(F32)
    a2 = r1.astype(BF16)
    a3 = (r1 - a2.astype(F32)).astype(BF16)
    return a1, a2, a3


def _norm_mod(x, g, shift, scale):
    ms = jnp.mean(x * x, axis=-1, keepdims=True)
    return (x * lax.rsqrt(ms + EPS)) * (g * (1.0 + scale)) + shift


def _rms(x, g):
    ms = jnp.mean(x * x, axis=-1, keepdims=True)
    return x * lax.rsqrt(ms + EPS) * g


def _log_sigmoid(x):
    return jnp.minimum(x, 0.0) - jnp.log1p(jnp.exp(-jnp.abs(x)))


def _ada_kernel(cond_ref, w_ref, b_ref, o_ref):
    c = cond_ref[...]
    s = c * jax.nn.sigmoid(c)
    s1, s2, _ = _split3(s)
    w = w_ref[...]
    w1 = w.astype(BF16)
    w2 = (w - w1.astype(F32)).astype(BF16)
    o_ref[...] = _dotf(s1, w1) + _dotf(s1, w2) + _dotf(s2, w1) + b_ref[...]


def _ada_params(cond8, mod_w, mod_b):
    depth, _, n = mod_w.shape
    tn = 1152
    return pl.pallas_call(
        _ada_kernel,
        grid=(depth, n // tn),
        in_specs=[
            pl.BlockSpec((8, D_MODEL), lambda l, j: (0, 0)),
            pl.BlockSpec((None, D_MODEL, tn), lambda l, j: (l, 0, j)),
            pl.BlockSpec((None, 1, tn), lambda l, j: (l, 0, j)),
        ],
        out_specs=pl.BlockSpec((None, 8, tn), lambda l, j: (l, 0, j)),
        out_shape=jax.ShapeDtypeStruct((depth, 8, n), F32),
        compiler_params=_cparams("parallel", "parallel"),
        name="ada_params",
    )(cond8, mod_w, mod_b.reshape(depth, 1, n))


def _ffn_kernel(x_ref, mod_ref, g_ref, wg_ref, wu_ref, wd_ref, o_ref):
    x = x_ref[...]
    h = _norm_mod(x, g_ref[...], mod_ref[0:1, :], mod_ref[1:2, :]).astype(BF16)
    g = _dotf(h, wg_ref[...])
    u = _dotf(h, wu_ref[...])
    a = (g * jax.nn.sigmoid(g) * u).astype(BF16)
    y = _dotf(a, wd_ref[...])
    o_ref[...] = x + (FFN_RESIDUAL * mod_ref[2:3, :]) * y


def _ffn(x2d, mods, row_fn, g, wg, wu, wd, tm):
    n = x2d.shape[0]
    return pl.pallas_call(
        _ffn_kernel,
        grid=(n // tm,),
        in_specs=[
            pl.BlockSpec((tm, D_MODEL), lambda i: (i, 0)),
            pl.BlockSpec((None, 3, D_MODEL), lambda i: (row_fn(i), 0, 0)),
            _const_spec((1, D_MODEL)),
            _const_spec((D_MODEL, D_FF)),
            _const_spec((D_MODEL, D_FF)),
            _const_spec((D_FF, D_MODEL)),
        ],
        out_specs=pl.BlockSpec((tm, D_MODEL), lambda i: (i, 0)),
        out_shape=jax.ShapeDtypeStruct((n, D_MODEL), F32),
        compiler_params=_cparams("parallel"),
        name="ffn",
    )(x2d, mods, g.reshape(1, D_MODEL), wg, wu, wd)


W_ALL = MLA_Q_RANK + MLA_KV_RANK + HEAD_BLOCK + 4 * MLSTM_WIDTH + HEAD_BLOCK
OFF_CKV = MLA_Q_RANK
OFF_KR = OFF_CKV + MLA_KV_RANK
OFF_MQ = OFF_KR + HEAD_BLOCK
OFF_GATES = OFF_MQ + 4 * MLSTM_WIDTH


def _seg_mean(sq, ind):
    hi = sq.astype(BF16)
    lo = (sq - hi.astype(F32)).astype(BF16)
    w = ind.shape[0]
    outs = []
    for j in range(sq.shape[1] // w):
        sl = slice(w * j, w * (j + 1))
        outs.append(_dotf(hi[:, sl], ind) + _dotf(lo[:, sl], ind))
    return outs[0] if len(outs) == 1 else jnp.concatenate(outs, axis=1)


def _rope_block(xb, cos, sin, first_half):
    partner = jnp.where(first_half, pltpu.roll(xb, HEAD_BLOCK - 8, axis=1), pltpu.roll(xb, 8, axis=1))
    return xb * cos + partner * sin


def _even_prep_kernel(x_ref, mod_ref, g_ref, cos_ref, sin_ref, wall_ref, wgt_ref, gbr_ref, gbc_ref,
                      cqg_ref, wuq_ref, ckvg_ref, wuk_ref, wuv_ref, qg_ref, kg_ref, krg_ref, ind_ref,
                      q_ref, k_ref, v_ref, mq_ref, mk_ref, mv_ref, mo_ref, gr_ref, gc_ref):
    tm = x_ref.shape[0]
    x = x_ref[...]
    h = _norm_mod(x, g_ref[...], mod_ref[0:1, :], mod_ref[1:2, :]).astype(BF16)
    p = _dotf(h, wall_ref[...])

    mq_ref[...] = p[:, OFF_MQ:OFF_MQ + MLSTM_WIDTH].astype(BF16)
    mk_ref[...] = (p[:, OFF_MQ + MLSTM_WIDTH:OFF_MQ + 2 * MLSTM_WIDTH] * (MLSTM_DH ** -0.5)).astype(BF16)
    mv_ref[...] = p[:, OFF_MQ + 2 * MLSTM_WIDTH:OFF_MQ + 3 * MLSTM_WIDTH].astype(BF16)
    mo_ref[...] = p[:, OFF_MQ + 3 * MLSTM_WIDTH:OFF_MQ + 4 * MLSTM_WIDTH]
    graw = p[:, OFF_GATES:OFF_GATES + HEAD_BLOCK][:, :N_GATES] + gbr_ref[...]
    lane = lax.broadcasted_iota(jnp.int32, (tm, N_GATES), 1)
    gr_ref[...] = jnp.where((lane // MLSTM_HEADS) % 2 == 1, _log_sigmoid(graw), graw)
    gt = lax.dot_general(wgt_ref[...], h, (((1,), (1,)), ((), ())), preferred_element_type=F32) + gbc_ref[...]
    sub = lax.broadcasted_iota(jnp.int32, (N_GATES, tm), 0)
    gc_ref[...] = jnp.where((sub // MLSTM_HEADS) % 2 == 1, _log_sigmoid(gt), gt)

    ind = ind_ref[...]
    cos = cos_ref[...]
    sin = sin_ref[...]
    lane_b = lax.broadcasted_iota(jnp.int32, (tm, HEAD_BLOCK), 1)
    first_half = (lane_b % ROPE_AXIS_DIM) < (ROPE_AXIS_DIM // 2)

    cqn = _rms(p[:, 0:MLA_Q_RANK], cqg_ref[...]).astype(BF16)
    ckvn = _rms(p[:, OFF_CKV:OFF_CKV + MLA_KV_RANK], ckvg_ref[...]).astype(BF16)
    q_raw = _dotf(cqn, wuq_ref[...])
    k_raw = _dotf(ckvn, wuk_ref[...])
    v_ref[...] = _dotf(ckvn, wuv_ref[...]).astype(BF16)

    qn = q_raw * lax.rsqrt(_seg_mean(q_raw * q_raw, ind) + EPS) * qg_ref[...]
    kn = k_raw * lax.rsqrt(_seg_mean(k_raw * k_raw, ind) + EPS) * kg_ref[...]
    kr = p[:, OFF_KR:OFF_KR + HEAD_BLOCK]
    krn = kr * lax.rsqrt(_seg_mean(kr * kr, ind[:HEAD_BLOCK, :HEAD_BLOCK]) + EPS) * krg_ref[...]
    kr_rot = _rope_block(krn, cos, sin, first_half)
    q_scale = MLA_QK ** -0.5
    for hh in range(MLA_HEADS):
        sl = slice(HEAD_BLOCK * hh, HEAD_BLOCK * (hh + 1))
        q_ref[:, sl] = (_rope_block(qn[:, sl], cos, sin, first_half) * q_scale).astype(BF16)
        k_ref[:, sl] = (kn[:, sl] + kr_rot).astype(BF16)


def _even_prep(x2d, mods, row_fn, g, cos, sin, tab_fn, w, tm):
    n = x2d.shape[0]
    tok = lambda width: pl.BlockSpec((tm, width), lambda i: (i, 0))
    out_shape = [
        jax.ShapeDtypeStruct((n, MLA_HEADS * HEAD_BLOCK), BF16),
        jax.ShapeDtypeStruct((n, MLA_HEADS * HEAD_BLOCK), BF16),
        jax.ShapeDtypeStruct((n, MLA_HEADS * MLA_V), BF16),
        jax.ShapeDtypeStruct((n, MLSTM_WIDTH), BF16),
        jax.ShapeDtypeStruct((n, MLSTM_WIDTH), BF16),
        jax.ShapeDtypeStruct((n, MLSTM_WIDTH), BF16),
        jax.ShapeDtypeStruct((n, MLSTM_WIDTH), F32),
        jax.ShapeDtypeStruct((n, N_GATES), F32),
        jax.ShapeDtypeStruct((N_GATES, n), F32),
    ]
    out_specs = [tok(1024), tok(1024), tok(512), tok(512), tok(512), tok(512), tok(512), tok(N_GATES),
                 pl.BlockSpec((N_GATES, tm), lambda i: (0, i))]
    in_specs = [
        tok(D_MODEL),
        pl.BlockSpec((None, 3, D_MODEL), lambda i: (row_fn(i), 0, 0)),
        _const_spec((1, D_MODEL)),
        pl.BlockSpec((tm, HEAD_BLOCK), lambda i: (tab_fn(i), 0)),
        pl.BlockSpec((tm, HEAD_BLOCK), lambda i: (tab_fn(i), 0)),
        _const_spec((D_MODEL, W_ALL)),
        _const_spec((N_GATES, D_MODEL)),
        _const_spec((1, N_GATES)),
        _const_spec((N_GATES, 1)),
        _const_spec((1, MLA_Q_RANK)),
        _const_spec((MLA_Q_RANK, 1024)),
        _const_spec((1, MLA_KV_RANK)),
        _const_spec((MLA_KV_RANK, 1024)),
        _const_spec((MLA_KV_RANK, 512)),
        _const_spec((1, 1024)),
        _const_spec((1, 1024)),
        _const_spec((1, HEAD_BLOCK)),
        _const_spec((2 * HEAD_BLOCK, 2 * HEAD_BLOCK)),
    ]
    return pl.pallas_call(
        _even_prep_kernel,
        grid=(n // tm,),
        in_specs=in_specs,
        out_specs=out_specs,
        out_shape=out_shape,
        compiler_params=_cparams("parallel"),
        name="even_prep",
    )(x2d, mods, g.reshape(1, D_MODEL), cos, sin, w["w_all"], w["w_gt"], w["gb_row"], w["gb_col"],
      w["cq_g"], w["w_uq"], w["ckv_g"], w["w_uk"], w["w_uv"], w["q_g"], w["k_g"], w["kr_g"], w["ind"])


def _attn_kernel(q_ref, k_ref, v_ref, o_ref, *, tk):
    tq = q_ref.shape[0]
    nk = k_ref.shape[0] // tk
    outs = []
    for hh in range(2):
        sl = slice(HEAD_BLOCK * hh, HEAD_BLOCK * (hh + 1))
        qh = q_ref[:, sl]

        def body(j, carry, sl=sl, qh=qh):
            m, l, acc = carry
            off = pl.multiple_of(j * tk, tk)
            ks = k_ref[pl.ds(off, tk), sl]
            vs = v_ref[pl.ds(off, tk), :]
            s = lax.dot_general(qh, ks, (((1,), (1,)), ((), ())), preferred_element_type=F32)
            m_new = jnp.maximum(m, jnp.max(s, axis=-1, keepdims=True))
            alpha = jnp.exp(m - m_new)
            pr = jnp.exp(s - m_new)
            l = alpha * l + jnp.sum(pr, axis=-1, keepdims=True)
            acc = alpha * acc + _dotf(pr.astype(BF16), vs)
            return m_new, l, acc

        init = (jnp.full((tq, 1), NEG_BIG, F32), jnp.zeros((tq, 1), F32), jnp.zeros((tq, HEAD_BLOCK), F32))
        _, l, acc = lax.fori_loop(0, nk, body, init)
        outs.append(acc / l)
    lane = lax.broadcasted_iota(jnp.int32, (tq, HEAD_BLOCK), 1)
    o_ref[...] = jnp.where(lane < MLA_V, outs[0], outs[1]).astype(o_ref.dtype)


def _attention(q2d, k2d, v2d, batch, tq, tk):
    nq = q2d.shape[0] // batch // tq
    t_k = k2d.shape[0] // batch
    return pl.pallas_call(
        functools.partial(_attn_kernel, tk=tk),
        grid=(batch, MLA_HEADS // 2, nq),
        in_specs=[
            pl.BlockSpec((tq, 2 * HEAD_BLOCK), lambda b, hp, i: (b * nq + i, hp)),
            pl.BlockSpec((t_k, 2 * HEAD_BLOCK), lambda b, hp, i: (b, hp)),
            pl.BlockSpec((t_k, 2 * MLA_V), lambda b, hp, i: (b, hp)),
        ],
        out_specs=pl.BlockSpec((tq, 2 * MLA_V), lambda b, hp, i: (b * nq + i, hp)),
        out_shape=jax.ShapeDtypeStruct((q2d.shape[0], MLA_HEADS * MLA_V), BF16),
        compiler_params=_cparams("parallel", "parallel", "arbitrary"),
        name="mla_attention",
    )(q2d, k2d, v2d)


def _mlstm_kernel(qf_ref, kf_ref, vf_ref, grf_ref, gcf_ref, qb_ref, kb_ref, vb_ref, grb_ref, gcb_ref,
                  s0_ref, m0_ref, hf_ref, hb_ref, s_out_ref, m_out_ref, s_scr, m_scr):
    j = pl.program_id(1)
    nj = pl.num_programs(1)
    L = qf_ref.shape[0]

    @pl.when(j == 0)
    def _():
        s_scr[...] = s0_ref[...]
        m_scr[...] = m0_ref[...]

    row = lax.broadcasted_iota(jnp.int32, (L, L), 0)
    col = lax.broadcasted_iota(jnp.int32, (L, L), 1)
    lane = lax.broadcasted_iota(jnp.int32, (L, MLSTM_DH), 1)
    one_col = jnp.where(lane == 0, 1.0, 0.0).astype(BF16)

    for d, (q_ref, k_ref, v_ref, gr_ref, gc_ref, h_ref) in enumerate(
            ((qf_ref, kf_ref, vf_ref, grf_ref, gcf_ref, hf_ref), (qb_ref, kb_ref, vb_ref, grb_ref, gcb_ref, hb_ref))):
        mask = (col <= row) if d == 0 else (col >= row)
        tri = jnp.where(mask, 1.0, 0.0).astype(BF16)
        tri_t = jnp.where((row <= col) if d == 0 else (row >= col), 1.0, 0.0).astype(BF16)
        g_rows = gr_ref[...]
        g_cols = gc_ref[...]
        r1, r2, r3 = _split3(g_rows)
        c1, c2, c3 = _split3(g_cols)
        cum_rows = _dotf(tri, r1) + _dotf(tri, r2) + _dotf(tri, r3)
        cum_cols = _dotf(c1, tri_t) + _dotf(c2, tri_t) + _dotf(c3, tri_t)
        for hh in range(MLSTM_HEADS):
            il = (0 if d == 0 else 2 * MLSTM_HEADS) + hh
            fl = il + MLSTM_HEADS
            idx = d * MLSTM_HEADS + hh
            sl = slice(MLSTM_DH * hh, MLSTM_DH * (hh + 1))
            qh = q_ref[:, sl]
            kh = k_ref[:, sl]
            vh = v_ref[:, sl]
            b_col = cum_rows[:, fl:fl + 1]
            i_col = g_rows[:, il:il + 1]
            b_row = cum_cols[fl:fl + 1, :]
            i_row = g_cols[il:il + 1, :]
            m_prev = m_scr[idx:idx + 1, 0:1]
            s_prev = s_scr[idx]

            dmat = jnp.where(mask, b_col - b_row + i_row, NEG_BIG)
            m_t = jnp.maximum(b_col + m_prev, jnp.max(dmat, axis=-1, keepdims=True))
            w_intra = jnp.exp(dmat - m_t)
            w_inter = jnp.exp(b_col + m_prev - m_t)
            s = lax.dot_general(qh, kh, (((1,), (1,)), ((), ())), preferred_element_type=F32) * w_intra
            inter = _dotf(qh, s_prev.astype(BF16))
            num = _dotf(s.astype(BF16), vh) + w_inter * inter[:, :MLSTM_DH]
            den = jnp.sum(s, axis=-1, keepdims=True) + w_inter * inter[:, MLSTM_DH:MLSTM_DH + 1]
            h_ref[:, sl] = num / jnp.maximum(jnp.abs(den), jnp.exp(-m_t))

            b_end = b_row[:, L - 1:L] if d == 0 else b_row[:, 0:1]
            g_col = b_end - b_col + i_col
            m_new = jnp.maximum(b_end + m_prev, jnp.max(g_col, axis=0, keepdims=True))
            w_s = jnp.exp(g_col - m_new)
            decay = jnp.exp(b_end + m_prev - m_new)
            kw = (kh.astype(F32) * w_s).astype(BF16)
            v_aug = jnp.concatenate([vh, one_col], axis=1)
            upd = lax.dot_general(kw, v_aug, (((0,), (0,)), ((), ())), preferred_element_type=F32)
            s_scr[idx] = decay * s_prev + upd
            m_scr[idx:idx + 1, :] = jnp.broadcast_to(m_new, (1, MLSTM_DH))

    @pl.when(j == nj - 1)
    def _():
        s_out_ref[...] = s_scr[...]
        m_out_ref[...] = m_scr[...]


def _mlstm(mq, mk, mv, g_rows, g_cols, s0, m0, batch, chunk):
    n = mq.shape[0]
    nc = n // batch // chunk
    fwd = lambda b, j: (b * nc + j, 0)
    bwd = lambda b, j: (b * nc + (nc - 1 - j), 0)
    fwd_t = lambda b, j: (0, b * nc + j)
    bwd_t = lambda b, j: (0, b * nc + (nc - 1 - j))
    nst = 2 * MLSTM_HEADS
    tok = lambda fn: pl.BlockSpec((chunk, MLSTM_WIDTH), fn)
    state_s = pl.BlockSpec((None, nst, MLSTM_DH, 2 * MLSTM_DH), lambda b, j: (b, 0, 0, 0))
    state_m = pl.BlockSpec((None, nst, MLSTM_DH), lambda b, j: (b, 0, 0))
    return pl.pallas_call(
        _mlstm_kernel,
        grid=(batch, nc),
        in_specs=[tok(fwd), tok(fwd), tok(fwd), pl.BlockSpec((chunk, N_GATES), fwd), pl.BlockSpec((N_GATES, chunk), fwd_t),
                  tok(bwd), tok(bwd), tok(bwd), pl.BlockSpec((chunk, N_GATES), bwd), pl.BlockSpec((N_GATES, chunk), bwd_t),
                  state_s, state_m],
        out_specs=[tok(fwd), tok(bwd), state_s, state_m],
        out_shape=[jax.ShapeDtypeStruct((n, MLSTM_WIDTH), F32), jax.ShapeDtypeStruct((n, MLSTM_WIDTH), F32),
                   jax.ShapeDtypeStruct(s0.shape, F32), jax.ShapeDtypeStruct(m0.shape, F32)],
        scratch_shapes=[pltpu.VMEM((nst, MLSTM_DH, 2 * MLSTM_DH), F32), pltpu.VMEM((nst, MLSTM_DH), F32)],
        compiler_params=_cparams("parallel", "arbitrary"),
        name="mlstm",
    )(mq, mk, mv, g_rows, g_cols, mq, mk, mv, g_rows, g_cols, s0, m0)


def _even_out_kernel(x_ref, gate_ref, att_ref, hf_ref, hb_ref, mo_ref, og_ref, watt_ref, wml_ref, o_ref):
    hs = hf_ref[...] + hb_ref[...]
    og = og_ref[...]
    parts = []
    for hh in range(MLSTM_HEADS):
        sl = slice(MLSTM_DH * hh, MLSTM_DH * (hh + 1))
        parts.append(_rms(hs[:, sl], og[:, sl]))
    hn = jnp.concatenate(parts, axis=1)
    ml = (hn * jax.nn.sigmoid(mo_ref[...])).astype(BF16)
    y = _dotf(att_ref[...], watt_ref[...]) + _dotf(ml, wml_ref[...])
    o_ref[...] = x_ref[...] + gate_ref[...] * y


def _even_out(x2d, gates, row_fn, att, hf, hb, mo, out_g, w_att, w_ml, tm):
    n = x2d.shape[0]
    tok = lambda width: pl.BlockSpec((tm, width), lambda i: (i, 0))
    return pl.pallas_call(
        _even_out_kernel,
        grid=(n // tm,),
        in_specs=[tok(D_MODEL), pl.BlockSpec((None, 1, D_MODEL), lambda i: (row_fn(i), 0, 0)),
                  tok(512), tok(512), tok(512), tok(512), _const_spec((1, MLSTM_WIDTH)),
                  _const_spec((512, D_MODEL)), _const_spec((512, D_MODEL))],
        out_specs=tok(D_MODEL),
        out_shape=jax.ShapeDtypeStruct((n, D_MODEL), F32),
        compiler_params=_cparams("parallel"),
        name="even_out",
    )(x2d, gates, att, hf, hb, mo, out_g.reshape(1, MLSTM_WIDTH), w_att, w_ml)


def _gelu_tanh(x):
    return 0.5 * x * (1.0 + jnp.tanh(0.7978845608028654 * (x + 0.044715 * (x * x * x))))


def _odd_in_kernel(x_ref, mod_ref, g_ref, w_ref, gate_ref, xr_ref):
    h = _norm_mod(x_ref[...], g_ref[...], mod_ref[0:1, :], mod_ref[1:2, :]).astype(BF16)
    p = _dotf(h, w_ref[...])
    gate_ref[...] = _gelu_tanh(p[:, :RNN_WIDTH])
    xr_ref[...] = p[:, RNN_WIDTH:]


def _odd_in(x2d, mods, row_fn, g, w_in, tm):
    n = x2d.shape[0]
    tok = pl.BlockSpec((tm, D_MODEL), lambda i: (i, 0))
    return pl.pallas_call(
        _odd_in_kernel,
        grid=(n // tm,),
        in_specs=[tok, pl.BlockSpec((None, 3, D_MODEL), lambda i: (row_fn(i), 0, 0)), _const_spec((1, D_MODEL)),
                  _const_spec((D_MODEL, 2 * RNN_WIDTH))],
        out_specs=[tok, tok],
        out_shape=[jax.ShapeDtypeStruct((n, RNN_WIDTH), F32), jax.ShapeDtypeStruct((n, RNN_WIDTH), F32)],
        compiler_params=_cparams("parallel"),
        name="odd_in",
    )(x2d, mods, g.reshape(1, D_MODEL), w_in)


def _rglru_kernel(xc_ref, xp_ref, xn_ref, cw_ref, cb_ref, wax_ref, ba_ref, bx_ref, lam_ref, h0_ref,
                  h_ref, ht_ref, ext_scr, a_scr, b_scr, carry_scr, *, nt):
    d = pl.program_id(0)
    i = pl.program_id(2)
    tm = xc_ref.shape[0]
    ti = jnp.where(d == 0, i, nt - 1 - i)

    ext_scr[8:8 + tm, :] = xc_ref[...]
    ext_scr[0:8, :] = jnp.where(ti > 0, xp_ref[...], 0.0)
    ext_scr[8 + tm:16 + tm, :] = jnp.where(ti < nt - 1, xn_ref[...], 0.0)
    cw = cw_ref[...]
    xc = cb_ref[...] + cw[0:1, :] * ext_scr[6:6 + tm, :]
    for k in range(1, CONV_W):
        xc = xc + cw[k:k + 1, :] * ext_scr[6 + k:6 + k + tm, :]

    lam = lam_ref[...]
    neg_lam = -lam
    softplus = jnp.maximum(neg_lam, 0.0) + jnp.log1p(jnp.exp(-jnp.abs(neg_lam)))
    ba = ba_ref[...]
    bx = bx_ref[...]
    for nb in range(RNN_BLOCKS):
        sl = slice(RNN_BLOCK_DIM * nb, RNN_BLOCK_DIM * (nb + 1))
        u = xc[:, sl]
        rg = _dotf(u.astype(BF16), wax_ref[nb])
        r = jax.nn.sigmoid(rg[:, :RNN_BLOCK_DIM] + ba[:, sl])
        ig = jax.nn.sigmoid(rg[:, RNN_BLOCK_DIM:] + bx[:, sl])
        log_a = -LRU_C * r * softplus[:, sl]
        a = jnp.exp(log_a)
        a_scr[:, sl] = a
        b_scr[:, sl] = jnp.sqrt(-jnp.tanh(log_a) * (a * a + 1.0)) * (ig * u)

    @pl.when(i == 0)
    def _():
        carry_scr[...] = h0_ref[...]

    def block(jb, h, reverse):
        base = pl.multiple_of(jb * 8, 8)
        av = a_scr[pl.ds(base, 8), :]
        bv = b_scr[pl.ds(base, 8), :]
        rows = [None] * 8
        for r in (range(7, -1, -1) if reverse else range(8)):
            h = av[r:r + 1, :] * h + bv[r:r + 1, :]
            rows[r] = h
        h_ref[pl.ds(base, 8), :] = jnp.concatenate(rows, axis=0)
        return h

    @pl.when(d == 0)
    def _():
        carry_scr[...] = lax.fori_loop(0, tm // 8, lambda jb, h: block(jb, h, False), carry_scr[...])

    @pl.when(d == 1)
    def _():
        carry_scr[...] = lax.fori_loop(0, tm // 8, lambda jb, h: block(tm // 8 - 1 - jb, h, True), carry_scr[...])

    @pl.when(i == nt - 1)
    def _():
        ht_ref[...] = carry_scr[...]


def _rglru(xr2d, h0, w, batch, tm):
    n = xr2d.shape[0]
    nt = n // batch // tm
    t8 = tm // 8
    tile = lambda d, i: jnp.where(d == 0, i, nt - 1 - i)
    cur = lambda d, b, i: (b * nt + tile(d, i), 0)
    prev = lambda d, b, i: (jnp.maximum((b * nt + tile(d, i)) * t8 - 1, 0), 0)
    nxt = lambda d, b, i: (jnp.minimum((b * nt + tile(d, i) + 1) * t8, n // 8 - 1), 0)
    per_dir = lambda shape: pl.BlockSpec((None,) + shape, lambda d, b, i: (d,) + (0,) * len(shape))
    return pl.pallas_call(
        functools.partial(_rglru_kernel, nt=nt),
        grid=(2, batch, nt),
        in_specs=[pl.BlockSpec((tm, RNN_WIDTH), cur), pl.BlockSpec((8, RNN_WIDTH), prev), pl.BlockSpec((8, RNN_WIDTH), nxt),
                  pl.BlockSpec((CONV_W, RNN_WIDTH), lambda d, b, i: (0, 0)), pl.BlockSpec((1, RNN_WIDTH), lambda d, b, i: (0, 0)),
                  per_dir((RNN_BLOCKS, RNN_BLOCK_DIM, 2 * RNN_BLOCK_DIM)), per_dir((1, RNN_WIDTH)), per_dir((1, RNN_WIDTH)),
                  per_dir((1, RNN_WIDTH)),
                  pl.BlockSpec((None, None, 1, RNN_WIDTH), lambda d, b, i: (d, b, 0, 0))],
        out_specs=[pl.BlockSpec((None, tm, RNN_WIDTH), lambda d, b, i: (d, b * nt + tile(d, i), 0)),
                   pl.BlockSpec((None, None, 1, RNN_WIDTH), lambda d, b, i: (d, b, 0, 0))],
        out_shape=[jax.ShapeDtypeStruct((2, n, RNN_WIDTH), F32), jax.ShapeDtypeStruct((2, batch, 1, RNN_WIDTH), F32)],
        scratch_shapes=[pltpu.VMEM((tm + 16, RNN_WIDTH), F32), pltpu.VMEM((tm, RNN_WIDTH), F32),
                        pltpu.VMEM((tm, RNN_WIDTH), F32), pltpu.VMEM((1, RNN_WIDTH), F32)],
        compiler_params=_cparams("parallel", "parallel", "arbitrary"),
        name="rglru",
    )(xr2d, xr2d, xr2d, w["conv_w"], w["conv_b"], w["w_ax"], w["b_a"], w["b_x"], w["lam"], h0)


def _odd_out_kernel(x_ref, gate_ref, h_ref, gg_ref, w_ref, o_ref):
    y = ((h_ref[0] + h_ref[1]).astype(F32) * gg_ref[...]).astype(BF16)
    o_ref[...] = x_ref[...] + gate_ref[...] * _dotf(y, w_ref[...])


def _odd_out(x2d, gates, row_fn, h2, gelu_gate, w_out, tm):
    n = x2d.shape[0]
    tok = pl.BlockSpec((tm, D_MODEL), lambda i: (i, 0))
    return pl.pallas_call(
        _odd_out_kernel,
        grid=(n // tm,),
        in_specs=[tok, pl.BlockSpec((None, 1, D_MODEL), lambda i: (row_fn(i), 0, 0)),
                  pl.BlockSpec((2, tm, RNN_WIDTH), lambda i: (0, i, 0)), tok, _const_spec((RNN_WIDTH, D_MODEL))],
        out_specs=tok,
        out_shape=jax.ShapeDtypeStruct((n, D_MODEL), F32),
        compiler_params=_cparams("parallel"),
        name="odd_out",
    )(x2d, gates, h2, gelu_gate, w_out)


def _rope_tables(seq):
    t = jnp.arange(seq)
    inv = ROPE_BASE ** (-jnp.arange(0, ROPE_AXIS_DIM, 2, dtype=F32) / ROPE_AXIS_DIM)
    ang_r = (t // GRID_W).astype(F32)[:, None] * inv
    ang_c = (t % GRID_W).astype(F32)[:, None] * inv
    cos32 = jnp.concatenate([jnp.cos(ang_r), jnp.cos(ang_r), jnp.cos(ang_c), jnp.cos(ang_c)], axis=1)
    sin32 = jnp.concatenate([-jnp.sin(ang_r), jnp.sin(ang_r), -jnp.sin(ang_c), jnp.sin(ang_c)], axis=1)
    pad = HEAD_BLOCK - MLA_QK
    cos = jnp.concatenate([jnp.ones((seq, MLA_NOPE), F32), cos32, jnp.ones((seq, pad), F32)], axis=1)
    sin = jnp.concatenate([jnp.zeros((seq, MLA_NOPE), F32), sin32, jnp.zeros((seq, pad), F32)], axis=1)
    return cos, sin


def _even_weights(w_in, cq_g, w_uq, ckv_g, w_ukv, q_g, k_g, gate_b):
    pad = HEAD_BLOCK - MLA_QK
    kr_cols = w_in[:, OFF_KR:OFF_KR + MLA_ROPE]
    w_kr = jnp.pad(kr_cols, ((0, 0), (MLA_NOPE, pad)))
    m_cols = w_in[:, OFF_KR + MLA_ROPE:OFF_KR + MLA_ROPE + 4 * MLSTM_WIDTH]
    g_cols = w_in[:, OFF_KR + MLA_ROPE + 4 * MLSTM_WIDTH:]
    w_all = jnp.concatenate([w_in[:, :OFF_KR], w_kr, m_cols, jnp.pad(g_cols, ((0, 0), (0, HEAD_BLOCK - N_GATES)))], axis=1)
    uq = jnp.pad(w_uq.reshape(MLA_Q_RANK, MLA_HEADS, MLA_QK), ((0, 0), (0, 0), (0, pad)))
    ukv = w_ukv.reshape(MLA_KV_RANK, MLA_HEADS, MLA_NOPE + MLA_V)
    uk = jnp.pad(ukv[:, :, :MLA_NOPE], ((0, 0), (0, 0), (0, HEAD_BLOCK - MLA_NOPE)))
    uv = ukv[:, :, MLA_NOPE:]
    qg_blk = jnp.pad(q_g, (0, pad))
    kg_blk = jnp.pad(k_g[:MLA_NOPE], (0, HEAD_BLOCK - MLA_NOPE))
    krg_blk = jnp.pad(k_g[MLA_NOPE:], (MLA_NOPE, pad))
    seg = jnp.arange(HEAD_BLOCK)
    seg_id = jnp.where(seg < MLA_NOPE, 0, jnp.where(seg < MLA_QK, 1, 2))
    seg_len = jnp.where(seg < MLA_NOPE, float(MLA_NOPE), float(MLA_ROPE))
    ind128 = jnp.where((seg_id[:, None] == seg_id[None, :]) & (seg_id[:, None] < 2), 1.0 / seg_len[None, :], 0.0)
    zero = jnp.zeros_like(ind128)
    ind = jnp.block([[ind128, zero], [zero, ind128]])
    return {
        "w_all": w_all.astype(BF16),
        "w_gt": g_cols.T.astype(BF16),
        "gb_row": gate_b.reshape(1, N_GATES),
        "gb_col": gate_b.reshape(N_GATES, 1),
        "cq_g": cq_g.reshape(1, MLA_Q_RANK),
        "w_uq": uq.reshape(MLA_Q_RANK, MLA_HEADS * HEAD_BLOCK).astype(BF16),
        "ckv_g": ckv_g.reshape(1, MLA_KV_RANK),
        "w_uk": uk.reshape(MLA_KV_RANK, MLA_HEADS * HEAD_BLOCK).astype(BF16),
        "w_uv": uv.reshape(MLA_KV_RANK, MLA_HEADS * MLA_V).astype(BF16),
        "q_g": jnp.tile(qg_blk, MLA_HEADS).reshape(1, -1),
        "k_g": jnp.tile(kg_blk, MLA_HEADS).reshape(1, -1),
        "kr_g": krg_blk.reshape(1, HEAD_BLOCK),
        "ind": ind.astype(BF16),
    }


def kernel(x, c, ctx, c_ctx, mod_w, mod_b, norm_g, ffn_w_gate, ffn_w_up, ffn_w_down, even_w_in, even_w_out, mla_cq_g, mla_w_uq, mla_ckv_g, mla_w_ukv, mla_q_g, mla_k_g, mlstm_gate_b, mlstm_out_g, odd_w_in, odd_conv_w, odd_conv_b, lru_w_a, lru_b_a, lru_w_x, lru_b_x, lru_lam, odd_w_out):
    batch, seq, _ = x.shape
    ctx_len = ctx.shape[1]
    depth = mod_w.shape[0]
    assert depth == 2 and batch <= 7

    cond8 = jnp.zeros((8, D_MODEL), F32).at[:batch].set(c).at[batch].set(c_ctx)
    mods = _ada_params(cond8, mod_w, mod_b)
    mods = mods.reshape(depth, 8, 3, 3, D_MODEL).transpose(0, 2, 1, 3, 4)

    tm_x, tm_c = 512, 256
    x_row = lambda i: i // (seq // tm_x)
    c_row = lambda i: batch
    x2 = x.reshape(batch * seq, D_MODEL)
    c2 = ctx.reshape(batch * ctx_len, D_MODEL)

    wg = ffn_w_gate.astype(BF16)
    wu = ffn_w_up.astype(BF16)
    wd = ffn_w_down.astype(BF16)

    def ffn(t2, layer, which, row_fn, tm):
        return _ffn(t2, mods[layer, 2 * which], row_fn, norm_g[layer, 2 * which], wg[layer, which], wu[layer, which],
                    wd[layer, which], tm)

    x2 = ffn(x2, 0, 0, x_row, tm_x)
    c2 = ffn(c2, 0, 0, c_row, tm_c)

    ew = _even_weights(even_w_in[0], mla_cq_g[0], mla_w_uq[0], mla_ckv_g[0], mla_w_ukv[0], mla_q_g[0], mla_k_g[0],
                       mlstm_gate_b[0])
    cos, sin = _rope_tables(seq)
    cos_c = jnp.ones((tm_c, HEAD_BLOCK), F32)
    sin_c = jnp.zeros((tm_c, HEAD_BLOCK), F32)
    px = _even_prep(x2, mods[0, 1], x_row, norm_g[0, 1], cos, sin, lambda i: i % (seq // tm_x), ew, tm_x)
    pc = _even_prep(c2, mods[0, 1], c_row, norm_g[0, 1], cos_c, sin_c, lambda i: 0, ew, tm_c)
    q_x, k_x, v_x, mq_x, mk_x, mv_x, mo_x, gr_x, gc_x = px
    q_c, k_c, v_c, mq_c, mk_c, mv_c, mo_c, gr_c, gc_c = pc

    def cat(a_c, a_x):
        w = a_c.shape[-1]
        return jnp.concatenate([a_c.reshape(batch, ctx_len, w), a_x.reshape(batch, seq, w)], axis=1).reshape(-1, w)

    att_x = _attention(q_x, cat(k_c, k_x), cat(v_c, v_x), batch, 256, 768)
    att_c = _attention(q_c, k_c, v_c, batch, 256, 256)

    chunk = 128
    nst = 2 * MLSTM_HEADS
    s0 = jnp.zeros((batch, nst, MLSTM_DH, 2 * MLSTM_DH), F32)
    m0 = jnp.zeros((batch, nst, MLSTM_DH), F32)
    hcf, hcb, s_c, m_c = _mlstm(mq_c, mk_c, mv_c, gr_c, gc_c, s0, m0, batch, chunk)
    hxf, hxb, _, _ = _mlstm(mq_x, mk_x, mv_x, gr_x, gc_x, s_c, m_c, batch, chunk)

    w_att = even_w_out[0, :MLA_HEADS * MLA_V].astype(BF16)
    w_ml = even_w_out[0, MLA_HEADS * MLA_V:].astype(BF16)
    gate0 = mods[0, 1][:, 2:3, :]
    x2 = _even_out(x2, gate0, x_row, att_x, hxf, hxb, mo_x, mlstm_out_g[0], w_att, w_ml, tm_x)
    c2 = _even_out(c2, gate0, c_row, att_c, hcf, hcb, mo_c, mlstm_out_g[0], w_att, w_ml, tm_c)
    x2 = ffn(x2, 0, 1, x_row, tm_x)
    c2 = ffn(c2, 0, 1, c_row, tm_c)

    x2 = ffn(x2, 1, 0, x_row, tm_x)
    c2 = ffn(c2, 1, 0, c_row, tm_c)
    w_in1 = odd_w_in[0].astype(BF16)
    gg_x, xr_x = _odd_in(x2, mods[1, 1], x_row, norm_g[1, 1], w_in1, tm_x)
    _, xr_c = _odd_in(c2, mods[1, 1], c_row, norm_g[1, 1], w_in1, tm_c)
    rw = {
        "conv_w": odd_conv_w[0],
        "conv_b": odd_conv_b[0].reshape(1, RNN_WIDTH),
        "w_ax": jnp.concatenate([lru_w_a[0], lru_w_x[0]], axis=-1).astype(BF16),
        "b_a": lru_b_a[0].reshape(2, 1, RNN_WIDTH),
        "b_x": lru_b_x[0].reshape(2, 1, RNN_WIDTH),
        "lam": lru_lam[0].reshape(2, 1, RNN_WIDTH),
    }
    h0 = jnp.zeros((2, batch, 1, RNN_WIDTH), F32)
    _, st_c = _rglru(xr_c, h0, rw, batch, 256)
    h_x, _ = _rglru(xr_x, st_c, rw, batch, 256)
    x2 = _odd_out(x2, mods[1, 1][:, 2:3, :], x_row, h_x, gg_x, odd_w_out[0].astype(BF16), tm_x)
    x2 = ffn(x2, 1, 1, x_row, tm_x)
    return x2.reshape(batch, seq, D_MODEL)
```

```python
import functools

import jax
import jax.numpy as jnp
from jax import lax
from jax.experimental import pallas as pl
from jax.experimental.pallas import tpu as pltpu

F32 = jnp.float32
BF16 = jnp.bfloat16

D_MODEL = 1024
GRID_W = 64
EPS = 1e-6
N_MOD = 9
D_FF = 2816
FFN_RESIDUAL = 0.5

MLA_HEADS = 8
MLA_Q_RANK = 384
MLA_KV_RANK = 256
MLA_NOPE = 64
MLA_ROPE = 32
MLA_V = 64
MLA_QK = MLA_NOPE + MLA_ROPE
ROPE_AXIS_DIM = MLA_ROPE // 2
ROPE_BASE = 10000.0
HEAD_BLOCK = 128

MLSTM_HEADS = 4
MLSTM_DH = 128
MLSTM_WIDTH = MLSTM_HEADS * MLSTM_DH
N_GATES = 4 * MLSTM_HEADS

RNN_WIDTH = 1024
RNN_BLOCKS = 8
RNN_BLOCK_DIM = RNN_WIDTH // RNN_BLOCKS
CONV_W = 4
LRU_C = 8.0

NEG_BIG = -1e30
LOG2_E = 1.4426950408889634
VMEM_LIMIT = 56 * 1024 * 1024


def _cparams(*sem):
    return pltpu.CompilerParams(dimension_semantics=sem, vmem_limit_bytes=VMEM_LIMIT)


def _const_spec(shape):
    zeros = (0,) * len(shape)
    return pl.BlockSpec(shape, lambda *_: zeros, pipeline_mode=pl.Buffered(1))


def _dotf(a, b):
    return jnp.dot(a, b, preferred_element_type=F32)


def _split3(a):
    a1 = a.astype(BF16)
    r1 = a - a1.astype(F32)
    a2 = r1.astype(BF16)
    a3 = (r1 - a2.astype(F32)).astype(BF16)
    return a1, a2, a3


def _norm_mod(x, g, shift, scale):
    ms = jnp.mean(x * x, axis=-1, keepdims=True)
    return (x * lax.rsqrt(ms + EPS)) * (g * (1.0 + scale)) + shift


def _rms(x, g):
    ms = jnp.mean(x * x, axis=-1, keepdims=True)
    return x * lax.rsqrt(ms + EPS) * g


def _log_sigmoid(x):
    return jnp.minimum(x, 0.0) - jnp.log1p(jnp.exp(-jnp.abs(x)))


def _ada_kernel(cond_ref, w_ref, b_ref, o_ref):
    c = cond_ref[...]
    s = c * jax.nn.sigmoid(c)
    s1, s2, _ = _split3(s)
    w = w_ref[...]
    w1 = w.astype(BF16)
    w2 = (w - w1.astype(F32)).astype(BF16)
    o_ref[...] = _dotf(s1, w1) + _dotf(s1, w2) + _dotf(s2, w1) + b_ref[...]


def _ada_params(cond8, mod_w, mod_b):
    depth, _, n = mod_w.shape
    tn = 1152
    return pl.pallas_call(
        _ada_kernel,
        grid=(depth, n // tn),
        in_specs=[
            pl.BlockSpec((8, D_MODEL), lambda l, j: (0, 0)),
            pl.BlockSpec((None, D_MODEL, tn), lambda l, j: (l, 0, j)),
            pl.BlockSpec((None, 1, tn), lambda l, j: (l, 0, j)),
        ],
        out_specs=pl.BlockSpec((None, 8, tn), lambda l, j: (l, 0, j)),
        out_shape=jax.ShapeDtypeStruct((depth, 8, n), F32),
        compiler_params=_cparams("parallel", "parallel"),
        name="ada_params",
    )(cond8, mod_w, mod_b.reshape(depth, 1, n))


def _ffn_kernel(x_ref, mod_ref, g_ref, wg_ref, wu_ref, wd_ref, o_ref):
    x = x_ref[...]
    h = _norm_mod(x, g_ref[...], mod_ref[0:1, :], mod_ref[1:2, :]).astype(BF16)
    g = _dotf(h, wg_ref[...])
    u = _dotf(h, wu_ref[...])
    a = (g * jax.nn.sigmoid(g) * u).astype(BF16)
    y = _dotf(a, wd_ref[...])
    o_ref[...] = x + (FFN_RESIDUAL * mod_ref[2:3, :]) * y


def _ffn(x2d, mods, row_fn, g, wg, wu, wd, tm):
    n = x2d.shape[0]
    return pl.pallas_call(
        _ffn_kernel,
        grid=(n // tm,),
        in_specs=[
            pl.BlockSpec((tm, D_MODEL), lambda i: (i, 0)),
            pl.BlockSpec((None, 3, D_MODEL), lambda i: (row_fn(i), 0, 0)),
            _const_spec((1, D_MODEL)),
            _const_spec((D_MODEL, D_FF)),
            _const_spec((D_MODEL, D_FF)),
            _const_spec((D_FF, D_MODEL)),
        ],
        out_specs=pl.BlockSpec((tm, D_MODEL), lambda i: (i, 0)),
        out_shape=jax.ShapeDtypeStruct((n, D_MODEL), F32),
        compiler_params=_cparams("parallel"),
        name="ffn",
    )(x2d, mods, g.reshape(1, D_MODEL), wg, wu, wd)


W_ALL = MLA_Q_RANK + MLA_KV_RANK + HEAD_BLOCK + 4 * MLSTM_WIDTH + HEAD_BLOCK
OFF_CKV = MLA_Q_RANK
OFF_KR = OFF_CKV + MLA_KV_RANK
OFF_MQ = OFF_KR + HEAD_BLOCK
OFF_GATES = OFF_MQ + 4 * MLSTM_WIDTH


def _seg_mean(sq, ind):
    hi = sq.astype(BF16)
    lo = (sq - hi.astype(F32)).astype(BF16)
    w = ind.shape[0]
    outs = []
    for j in range(sq.shape[1] // w):
        sl = slice(w * j, w * (j + 1))
        outs.append(_dotf(hi[:, sl], ind) + _dotf(lo[:, sl], ind))
    return outs[0] if len(outs) == 1 else jnp.concatenate(outs, axis=1)


def _rope_block(xb, cos, sin, first_half):
    partner = jnp.where(first_half, pltpu.roll(xb, HEAD_BLOCK - 8, axis=1), pltpu.roll(xb, 8, axis=1))
    return xb * cos + partner * sin


def _even_prep_kernel(x_ref, mod_ref, g_ref, cos_ref, sin_ref, cost_ref, sint_ref, wall_ref, wgt_ref, gbr_ref, gbc_ref,
                      cqg_ref, wuqt_ref, ckvg_ref, wuk_ref, wuvt_ref, qgn_ref, qgr_ref, kg_ref, krg_ref, ind_ref,
                      qt_ref, k_ref, vt_ref, mq_ref, mk_ref, mv_ref, mo_ref, gr_ref, gc_ref):
    tm = x_ref.shape[0]
    x = x_ref[...]
    h = _norm_mod(x, g_ref[...], mod_ref[0:1, :], mod_ref[1:2, :]).astype(BF16)
    p = _dotf(h, wall_ref[...])

    mq_ref[...] = p[:, OFF_MQ:OFF_MQ + MLSTM_WIDTH].astype(BF16)
    mk_ref[...] = (p[:, OFF_MQ + MLSTM_WIDTH:OFF_MQ + 2 * MLSTM_WIDTH] * (MLSTM_DH ** -0.5)).astype(BF16)
    mv_ref[...] = p[:, OFF_MQ + 2 * MLSTM_WIDTH:OFF_MQ + 3 * MLSTM_WIDTH].astype(BF16)
    mo_ref[...] = p[:, OFF_MQ + 3 * MLSTM_WIDTH:OFF_MQ + 4 * MLSTM_WIDTH]
    graw = p[:, OFF_GATES:OFF_GATES + HEAD_BLOCK][:, :N_GATES] + gbr_ref[...]
    lane = lax.broadcasted_iota(jnp.int32, (tm, N_GATES), 1)
    gr_ref[...] = jnp.where((lane // MLSTM_HEADS) % 2 == 1, _log_sigmoid(graw), graw)
    gt = lax.dot_general(wgt_ref[...], h, (((1,), (1,)), ((), ())), preferred_element_type=F32) + gbc_ref[...]
    sub = lax.broadcasted_iota(jnp.int32, (N_GATES, tm), 0)
    gc_ref[...] = jnp.where((sub // MLSTM_HEADS) % 2 == 1, _log_sigmoid(gt), gt)

    ind = ind_ref[...]
    cos = cos_ref[...]
    sin = sin_ref[...]
    lane_b = lax.broadcasted_iota(jnp.int32, (tm, HEAD_BLOCK), 1)
    first_half = (lane_b % ROPE_AXIS_DIM) < (ROPE_AXIS_DIM // 2)

    cqn = _rms(p[:, 0:MLA_Q_RANK], cqg_ref[...]).astype(BF16)
    ckvn = _rms(p[:, OFF_CKV:OFF_CKV + MLA_KV_RANK], ckvg_ref[...]).astype(BF16)
    nt_dims = (((1,), (1,)), ((), ()))
    k_raw = _dotf(ckvn, wuk_ref[...])
    vt_ref[...] = lax.dot_general(wuvt_ref[...], ckvn, nt_dims, preferred_element_type=F32).astype(BF16)

    kn = k_raw * lax.rsqrt(_seg_mean(k_raw * k_raw, ind) + EPS) * kg_ref[...]
    kr = p[:, OFF_KR:OFF_KR + HEAD_BLOCK]
    krn = kr * lax.rsqrt(_seg_mean(kr * kr, ind[:HEAD_BLOCK, :HEAD_BLOCK]) + EPS) * krg_ref[...]
    kr_rot = _rope_block(krn, cos, sin, first_half)
    for hh in range(MLA_HEADS):
        sl = slice(HEAD_BLOCK * hh, HEAD_BLOCK * (hh + 1))
        k_ref[:, sl] = (kn[:, sl] + kr_rot).astype(BF16)

    qt_raw = lax.dot_general(wuqt_ref[...], cqn, nt_dims, preferred_element_type=F32)
    q_scale = (MLA_QK ** -0.5) * LOG2_E
    half = ROPE_AXIS_DIM // 2
    cost = cost_ref[...]
    sint = sint_ref[...]
    for hh in range(MLA_HEADS):
        r0 = HEAD_BLOCK * hh
        nope = qt_raw[r0:r0 + MLA_NOPE, :]
        rope = qt_raw[r0 + MLA_NOPE:r0 + MLA_QK, :]
        nope = nope * lax.rsqrt(jnp.mean(nope * nope, axis=0, keepdims=True) + EPS) * qgn_ref[...]
        rope = rope * lax.rsqrt(jnp.mean(rope * rope, axis=0, keepdims=True) + EPS) * qgr_ref[...]
        rot = []
        for ax in range(2):
            x1 = rope[ROPE_AXIS_DIM * ax:ROPE_AXIS_DIM * ax + half, :]
            x2 = rope[ROPE_AXIS_DIM * ax + half:ROPE_AXIS_DIM * (ax + 1), :]
            cs = cost[half * ax:half * (ax + 1), :]
            sn = sint[half * ax:half * (ax + 1), :]
            rot += [x1 * cs - x2 * sn, x2 * cs + x1 * sn]
        qt_ref[r0:r0 + MLA_NOPE, :] = (nope * q_scale).astype(BF16)
        qt_ref[r0 + MLA_NOPE:r0 + MLA_QK, :] = (jnp.concatenate(rot, axis=0) * q_scale).astype(BF16)
        qt_ref[r0 + MLA_QK:r0 + HEAD_BLOCK, :] = jnp.zeros((HEAD_BLOCK - MLA_QK, tm), BF16)


def _even_prep(x2d, mods, row_fn, g, tabs, tab_fn, w, tm):
    n = x2d.shape[0]
    cos, sin, cost, sint = tabs
    tok = lambda width: pl.BlockSpec((tm, width), lambda i: (i, 0))
    tok_t = lambda rows: pl.BlockSpec((rows, tm), lambda i: (0, i))
    out_shape = [
        jax.ShapeDtypeStruct((MLA_HEADS * HEAD_BLOCK, n), BF16),
        jax.ShapeDtypeStruct((n, MLA_HEADS * HEAD_BLOCK), BF16),
        jax.ShapeDtypeStruct((MLA_HEADS * MLA_V, n), BF16),
        jax.ShapeDtypeStruct((n, MLSTM_WIDTH), BF16),
        jax.ShapeDtypeStruct((n, MLSTM_WIDTH), BF16),
        jax.ShapeDtypeStruct((n, MLSTM_WIDTH), BF16),
        jax.ShapeDtypeStruct((n, MLSTM_WIDTH), F32),
        jax.ShapeDtypeStruct((n, N_GATES), F32),
        jax.ShapeDtypeStruct((N_GATES, n), F32),
    ]
    out_specs = [tok_t(1024), tok(1024), tok_t(512), tok(512), tok(512), tok(512), tok(512), tok(N_GATES),
                 tok_t(N_GATES)]
    in_specs = [
        tok(D_MODEL),
        pl.BlockSpec((None, 3, D_MODEL), lambda i: (row_fn(i), 0, 0)),
        _const_spec((1, D_MODEL)),
        pl.BlockSpec((tm, HEAD_BLOCK), lambda i: (tab_fn(i), 0)),
        pl.BlockSpec((tm, HEAD_BLOCK), lambda i: (tab_fn(i), 0)),
        pl.BlockSpec((ROPE_AXIS_DIM, tm), lambda i: (0, tab_fn(i))),
        pl.BlockSpec((ROPE_AXIS_DIM, tm), lambda i: (0, tab_fn(i))),
        _const_spec((D_MODEL, W_ALL)),
        _const_spec((N_GATES, D_MODEL)),
        _const_spec((1, N_GATES)),
        _const_spec((N_GATES, 1)),
        _const_spec((1, MLA_Q_RANK)),
        _const_spec((1024, MLA_Q_RANK)),
        _const_spec((1, MLA_KV_RANK)),
        _const_spec((MLA_KV_RANK, 1024)),
        _const_spec((512, MLA_KV_RANK)),
        _const_spec((MLA_NOPE, 1)),
        _const_spec((MLA_ROPE, 1)),
        _const_spec((1, 1024)),
        _const_spec((1, HEAD_BLOCK)),
        _const_spec((2 * HEAD_BLOCK, 2 * HEAD_BLOCK)),
    ]
    return pl.pallas_call(
        _even_prep_kernel,
        grid=(n // tm,),
        in_specs=in_specs,
        out_specs=out_specs,
        out_shape=out_shape,
        compiler_params=_cparams("parallel"),
        name="even_prep",
    )(x2d, mods, g.reshape(1, D_MODEL), cos, sin, cost, sint, w["w_all"], w["w_gt"], w["gb_row"], w["gb_col"],
      w["cq_g"], w["w_uqt"], w["ckv_g"], w["w_uk"], w["w_uvt"], w["qg_nope"], w["qg_rope"], w["k_g"], w["kr_g"],
      w["ind"])


def _attn_kernel(qt_ref, k_ref, vt_ref, o_ref, st_scr, acc_scr):
    tq = qt_ref.shape[1]
    nk, _, tk = vt_ref.shape
    assert nk % 2 == 1
    heads = range(2)

    def scores(j, slot):
        off = pl.multiple_of(j * tk, tk)
        cms = []
        for hh in heads:
            sl = slice(HEAD_BLOCK * hh, HEAD_BLOCK * (hh + 1))
            st = _dotf(k_ref[pl.ds(off, tk), sl], qt_ref[sl, :])
            st_scr[slot, hh] = st
            cms.append(jnp.max(st, axis=0, keepdims=True))
        return tuple(cms)

    def absorb(j, slot, cms, ml):
        out = []
        for hh in heads:
            m, l = ml[hh]
            m_new = jnp.maximum(m, cms[hh])
            alpha = jnp.exp2(m - m_new)
            pt = jnp.exp2(st_scr[slot, hh] - m_new)
            l = alpha * l + jnp.sum(pt, axis=0, keepdims=True)
            pv = _dotf(vt_ref[j, MLA_V * hh:MLA_V * (hh + 1), :], pt.astype(BF16))
            acc_scr[hh] = alpha * acc_scr[hh] + pv
            out.append((m_new, l))
        return tuple(out)

    def body(i, carry):
        cms, ml = carry
        cms_odd = scores(2 * i + 1, 1)
        ml = absorb(2 * i, 0, cms, ml)
        cms_even = scores(2 * i + 2, 0)
        ml = absorb(2 * i + 1, 1, cms_odd, ml)
        return cms_even, ml

    acc_scr[...] = jnp.zeros(acc_scr.shape, F32)
    ml0 = (jnp.full((1, tq), NEG_BIG, F32), jnp.zeros((1, tq), F32))
    cms, ml = lax.fori_loop(0, (nk - 1) // 2, body, (scores(0, 0), (ml0, ml0)))
    ml = absorb(nk - 1, 0, cms, ml)
    for hh in heads:
        o_ref[MLA_V * hh:MLA_V * (hh + 1), :] = (acc_scr[hh] / ml[hh][1]).astype(o_ref.dtype)


def _attention(qt, k2d, vt4, batch, tq):
    nq = qt.shape[1] // batch // tq
    t_k = k2d.shape[0] // batch
    _, nk, _, tk = vt4.shape
    return pl.pallas_call(
        _attn_kernel,
        grid=(batch, MLA_HEADS // 2, nq),
        in_specs=[
            pl.BlockSpec((2 * HEAD_BLOCK, tq), lambda b, hp, i: (hp, b * nq + i)),
            pl.BlockSpec((t_k, 2 * HEAD_BLOCK), lambda b, hp, i: (b, hp)),
            pl.BlockSpec((None, nk, 2 * MLA_V, tk), lambda b, hp, i: (b, 0, hp, 0)),
        ],
        out_specs=pl.BlockSpec((2 * MLA_V, tq), lambda b, hp, i: (hp, b * nq + i)),
        out_shape=jax.ShapeDtypeStruct((MLA_HEADS * MLA_V, qt.shape[1]), BF16),
        scratch_shapes=[pltpu.VMEM((2, 2, tk, tq), F32), pltpu.VMEM((2, MLA_V, tq), F32)],
        compiler_params=_cparams("parallel", "parallel", "arbitrary"),
        name="mla_attention",
    )(qt, k2d, vt4)


def _mlstm_kernel(qf_ref, kf_ref, vf_ref, grf_ref, gcf_ref, qb_ref, kb_ref, vb_ref, grb_ref, gcb_ref,
                  s0_ref, m0_ref, hf_ref, hb_ref, s_out_ref, m_out_ref, s_scr, m_scr):
    j = pl.program_id(1)
    nj = pl.num_programs(1)
    L = qf_ref.shape[0]

    @pl.when(j == 0)
    def _():
        s_scr[...] = s0_ref[...]
        m_scr[...] = m0_ref[...]

    row = lax.broadcasted_iota(jnp.int32, (L, L), 0)
    col = lax.broadcasted_iota(jnp.int32, (L, L), 1)
    lane = lax.broadcasted_iota(jnp.int32, (L, MLSTM_DH), 1)
    one_col = jnp.where(lane == 0, 1.0, 0.0).astype(BF16)

    for d, (q_ref, k_ref, v_ref, gr_ref, gc_ref, h_ref) in enumerate(
            ((qf_ref, kf_ref, vf_ref, grf_ref, gcf_ref, hf_ref), (qb_ref, kb_ref, vb_ref, grb_ref, gcb_ref, hb_ref))):
        mask = (col <= row) if d == 0 else (col >= row)
        tri = jnp.where(mask, 1.0, 0.0).astype(BF16)
        tri_t = jnp.where((row <= col) if d == 0 else (row >= col), 1.0, 0.0).astype(BF16)
        g_rows = gr_ref[...]
        g_cols = gc_ref[...]
        r1, r2, r3 = _split3(g_rows)
        c1, c2, c3 = _split3(g_cols)
        cum_rows = _dotf(tri, r1) + _dotf(tri, r2) + _dotf(tri, r3)
        cum_cols = _dotf(c1, tri_t) + _dotf(c2, tri_t) + _dotf(c3, tri_t)
        for hh in range(MLSTM_HEADS):
            il = (0 if d == 0 else 2 * MLSTM_HEADS) + hh
            fl = il + MLSTM_HEADS
            idx = d * MLSTM_HEADS + hh
            sl = slice(MLSTM_DH * hh, MLSTM_DH * (hh + 1))
            qh = q_ref[:, sl]
            kh = k_ref[:, sl]
            vh = v_ref[:, sl]
            b_col = cum_rows[:, fl:fl + 1]
            i_col = g_rows[:, il:il + 1]
            b_row = cum_cols[fl:fl + 1, :]
            i_row = g_cols[il:il + 1, :]
            m_prev = m_scr[idx:idx + 1, 0:1]
            s_prev = s_scr[idx]

            dmat = jnp.where(mask, b_col - b_row + i_row, NEG_BIG)
            m_t = jnp.maximum(b_col + m_prev, jnp.max(dmat, axis=-1, keepdims=True))
            w_intra = jnp.exp(dmat - m_t)
            w_inter = jnp.exp(b_col + m_prev - m_t)
            s = lax.dot_general(qh, kh, (((1,), (1,)), ((), ())), preferred_element_type=F32) * w_intra
            inter = _dotf(qh, s_prev.astype(BF16))
            num = _dotf(s.astype(BF16), vh) + w_inter * inter[:, :MLSTM_DH]
            den = jnp.sum(s, axis=-1, keepdims=True) + w_inter * inter[:, MLSTM_DH:MLSTM_DH + 1]
            h_ref[:, sl] = num / jnp.maximum(jnp.abs(den), jnp.exp(-m_t))

            b_end = b_row[:, L - 1:L] if d == 0 else b_row[:, 0:1]
            g_col = b_end - b_col + i_col
            m_new = jnp.maximum(b_end + m_prev, jnp.max(g_col, axis=0, keepdims=True))
            w_s = jnp.exp(g_col - m_new)
            decay = jnp.exp(b_end + m_prev - m_new)
            kw = (kh.astype(F32) * w_s).astype(BF16)
            v_aug = jnp.concatenate([vh, one_col], axis=1)
            upd = lax.dot_general(kw, v_aug, (((0,), (0,)), ((), ())), preferred_element_type=F32)
            s_scr[idx] = decay * s_prev + upd
            m_scr[idx:idx + 1, :] = jnp.broadcast_to(m_new, (1, MLSTM_DH))

    @pl.when(j == nj - 1)
    def _():
        s_out_ref[...] = s_scr[...]
        m_out_ref[...] = m_scr[...]


def _mlstm(mq, mk, mv, g_rows, g_cols, s0, m0, batch, chunk):
    n = mq.shape[0]
    nc = n // batch // chunk
    fwd = lambda b, j: (b * nc + j, 0)
    bwd = lambda b, j: (b * nc + (nc - 1 - j), 0)
    fwd_t = lambda b, j: (0, b * nc + j)
    bwd_t = lambda b, j: (0, b * nc + (nc - 1 - j))
    nst = 2 * MLSTM_HEADS
    tok = lambda fn: pl.BlockSpec((chunk, MLSTM_WIDTH), fn)
    state_s = pl.BlockSpec((None, nst, MLSTM_DH, 2 * MLSTM_DH), lambda b, j: (b, 0, 0, 0))
    state_m = pl.BlockSpec((None, nst, MLSTM_DH), lambda b, j: (b, 0, 0))
    return pl.pallas_call(
        _mlstm_kernel,
        grid=(batch, nc),
        in_specs=[tok(fwd), tok(fwd), tok(fwd), pl.BlockSpec((chunk, N_GATES), fwd), pl.BlockSpec((N_GATES, chunk), fwd_t),
                  tok(bwd), tok(bwd), tok(bwd), pl.BlockSpec((chunk, N_GATES), bwd), pl.BlockSpec((N_GATES, chunk), bwd_t),
                  state_s, state_m],
        out_specs=[tok(fwd), tok(bwd), state_s, state_m],
        out_shape=[jax.ShapeDtypeStruct((n, MLSTM_WIDTH), F32), jax.ShapeDtypeStruct((n, MLSTM_WIDTH), F32),
                   jax.ShapeDtypeStruct(s0.shape, F32), jax.ShapeDtypeStruct(m0.shape, F32)],
        scratch_shapes=[pltpu.VMEM((nst, MLSTM_DH, 2 * MLSTM_DH), F32), pltpu.VMEM((nst, MLSTM_DH), F32)],
        compiler_params=_cparams("parallel", "arbitrary"),
        name="mlstm",
    )(mq, mk, mv, g_rows, g_cols, mq, mk, mv, g_rows, g_cols, s0, m0)


def _even_out_kernel(x_ref, gate_ref, att_ref, hf_ref, hb_ref, mo_ref, og_ref, watt_ref, wml_ref, o_ref):
    hs = hf_ref[...] + hb_ref[...]
    og = og_ref[...]
    parts = []
    for hh in range(MLSTM_HEADS):
        sl = slice(MLSTM_DH * hh, MLSTM_DH * (hh + 1))
        parts.append(_rms(hs[:, sl], og[:, sl]))
    hn = jnp.concatenate(parts, axis=1)
    ml = (hn * jax.nn.sigmoid(mo_ref[...])).astype(BF16)
    y_att = lax.dot_general(att_ref[...], watt_ref[...], (((0,), (0,)), ((), ())), preferred_element_type=F32)
    o_ref[...] = x_ref[...] + gate_ref[...] * (y_att + _dotf(ml, wml_ref[...]))


def _even_out(x2d, gates, row_fn, att, hf, hb, mo, out_g, w_att, w_ml, tm):
    n = x2d.shape[0]
    tok = lambda width: pl.BlockSpec((tm, width), lambda i: (i, 0))
    return pl.pallas_call(
        _even_out_kernel,
        grid=(n // tm,),
        in_specs=[tok(D_MODEL), pl.BlockSpec((None, 1, D_MODEL), lambda i: (row_fn(i), 0, 0)),
                  pl.BlockSpec((MLA_HEADS * MLA_V, tm), lambda i: (0, i)), tok(512), tok(512), tok(512),
                  _const_spec((1, MLSTM_WIDTH)),
                  _const_spec((512, D_MODEL)), _const_spec((512, D_MODEL))],
        out_specs=tok(D_MODEL),
        out_shape=jax.ShapeDtypeStruct((n, D_MODEL), F32),
        compiler_params=_cparams("parallel"),
        name="even_out",
    )(x2d, gates, att, hf, hb, mo, out_g.reshape(1, MLSTM_WIDTH), w_att, w_ml)


def _gelu_tanh(x):
    return 0.5 * x * (1.0 + jnp.tanh(0.7978845608028654 * (x + 0.044715 * (x * x * x))))


def _odd_in_kernel(x_ref, mod_ref, g_ref, w_ref, gate_ref, xr_ref):
    h = _norm_mod(x_ref[...], g_ref[...], mod_ref[0:1, :], mod_ref[1:2, :]).astype(BF16)
    p = _dotf(h, w_ref[...])
    gate_ref[...] = _gelu_tanh(p[:, :RNN_WIDTH])
    xr_ref[...] = p[:, RNN_WIDTH:]


def _odd_in(x2d, mods, row_fn, g, w_in, tm):
    n = x2d.shape[0]
    tok = pl.BlockSpec((tm, D_MODEL), lambda i: (i, 0))
    return pl.pallas_call(
        _odd_in_kernel,
        grid=(n // tm,),
        in_specs=[tok, pl.BlockSpec((None, 3, D_MODEL), lambda i: (row_fn(i), 0, 0)), _const_spec((1, D_MODEL)),
                  _const_spec((D_MODEL, 2 * RNN_WIDTH))],
        out_specs=[tok, tok],
        out_shape=[jax.ShapeDtypeStruct((n, RNN_WIDTH), F32), jax.ShapeDtypeStruct((n, RNN_WIDTH), F32)],
        compiler_params=_cparams("parallel"),
        name="odd_in",
    )(x2d, mods, g.reshape(1, D_MODEL), w_in)


def _rglru_kernel(xc_ref, xp_ref, xn_ref, cw_ref, cb_ref, wax_ref, ba_ref, bx_ref, lam_ref, h0_ref,
                  h_ref, ht_ref, ext_scr, a_scr, b_scr, carry_scr, *, nt):
    d = pl.program_id(0)
    i = pl.program_id(2)
    tm = xc_ref.shape[0]
    ti = jnp.where(d == 0, i, nt - 1 - i)

    ext_scr[8:8 + tm, :] = xc_ref[...]
    ext_scr[0:8, :] = jnp.where(ti > 0, xp_ref[...], 0.0)
    ext_scr[8 + tm:16 + tm, :] = jnp.where(ti < nt - 1, xn_ref[...], 0.0)
    cw = cw_ref[...]
    xc = cb_ref[...] + cw[0:1, :] * ext_scr[6:6 + tm, :]
    for k in range(1, CONV_W):
        xc = xc + cw[k:k + 1, :] * ext_scr[6 + k:6 + k + tm, :]

    lam = lam_ref[...]
    neg_lam = -lam
    softplus = jnp.maximum(neg_lam, 0.0) + jnp.log1p(jnp.exp(-jnp.abs(neg_lam)))
    ba = ba_ref[...]
    bx = bx_ref[...]
    for nb in range(RNN_BLOCKS):
        sl = slice(RNN_BLOCK_DIM * nb, RNN_BLOCK_DIM * (nb + 1))
        u = xc[:, sl]
        rg = _dotf(u.astype(BF16), wax_ref[nb])
        r = jax.nn.sigmoid(rg[:, :RNN_BLOCK_DIM] + ba[:, sl])
        ig = jax.nn.sigmoid(rg[:, RNN_BLOCK_DIM:] + bx[:, sl])
        log_a = -LRU_C * r * softplus[:, sl]
        a = jnp.exp(log_a)
        a_scr[:, sl] = a
        b_scr[:, sl] = jnp.sqrt(-jnp.tanh(log_a) * (a * a + 1.0)) * (ig * u)

    @pl.when(i == 0)
    def _():
        carry_scr[...] = h0_ref[...]

    def block(jb, h, reverse):
        base = pl.multiple_of(jb * 8, 8)
        av = a_scr[pl.ds(base, 8), :]
        bv = b_scr[pl.ds(base, 8), :]
        rows = [None] * 8
        for r in (range(7, -1, -1) if reverse else range(8)):
            h = av[r:r + 1, :] * h + bv[r:r + 1, :]
            rows[r] = h
        h_ref[pl.ds(base, 8), :] = jnp.concatenate(rows, axis=0)
        return h

    @pl.when(d == 0)
    def _():
        carry_scr[...] = lax.fori_loop(0, tm // 8, lambda jb, h: block(jb, h, False), carry_scr[...])

    @pl.when(d == 1)
    def _():
        carry_scr[...] = lax.fori_loop(0, tm // 8, lambda jb, h: block(tm // 8 - 1 - jb, h, True), carry_scr[...])

    @pl.when(i == nt - 1)
    def _():
        ht_ref[...] = carry_scr[...]


def _rglru(xr2d, h0, w, batch, tm):
    n = xr2d.shape[0]
    nt = n // batch // tm
    t8 = tm // 8
    tile = lambda d, i: jnp.where(d == 0, i, nt - 1 - i)
    cur = lambda d, b, i: (b * nt + tile(d, i), 0)
    prev = lambda d, b, i: (jnp.maximum((b * nt + tile(d, i)) * t8 - 1, 0), 0)
    nxt = lambda d, b, i: (jnp.minimum((b * nt + tile(d, i) + 1) * t8, n // 8 - 1), 0)
    per_dir = lambda shape: pl.BlockSpec((None,) + shape, lambda d, b, i: (d,) + (0,) * len(shape))
    return pl.pallas_call(
        functools.partial(_rglru_kernel, nt=nt),
        grid=(2, batch, nt),
        in_specs=[pl.BlockSpec((tm, RNN_WIDTH), cur), pl.BlockSpec((8, RNN_WIDTH), prev), pl.BlockSpec((8, RNN_WIDTH), nxt),
                  pl.BlockSpec((CONV_W, RNN_WIDTH), lambda d, b, i: (0, 0)), pl.BlockSpec((1, RNN_WIDTH), lambda d, b, i: (0, 0)),
                  per_dir((RNN_BLOCKS, RNN_BLOCK_DIM, 2 * RNN_BLOCK_DIM)), per_dir((1, RNN_WIDTH)), per_dir((1, RNN_WIDTH)),
                  per_dir((1, RNN_WIDTH)),
                  pl.BlockSpec((None, None, 1, RNN_WIDTH), lambda d, b, i: (d, b, 0, 0))],
        out_specs=[pl.BlockSpec((None, tm, RNN_WIDTH), lambda d, b, i: (d, b * nt + tile(d, i), 0)),
                   pl.BlockSpec((None, None, 1, RNN_WIDTH), lambda d, b, i: (d, b, 0, 0))],
        out_shape=[jax.ShapeDtypeStruct((2, n, RNN_WIDTH), F32), jax.ShapeDtypeStruct((2, batch, 1, RNN_WIDTH), F32)],
        scratch_shapes=[pltpu.VMEM((tm + 16, RNN_WIDTH), F32), pltpu.VMEM((tm, RNN_WIDTH), F32),
                        pltpu.VMEM((tm, RNN_WIDTH), F32), pltpu.VMEM((1, RNN_WIDTH), F32)],
        compiler_params=_cparams("parallel", "parallel", "arbitrary"),
        name="rglru",
    )(xr2d, xr2d, xr2d, w["conv_w"], w["conv_b"], w["w_ax"], w["b_a"], w["b_x"], w["lam"], h0)


def _odd_out_kernel(x_ref, gate_ref, h_ref, gg_ref, w_ref, o_ref):
    y = ((h_ref[0] + h_ref[1]).astype(F32) * gg_ref[...]).astype(BF16)
    o_ref[...] = x_ref[...] + gate_ref[...] * _dotf(y, w_ref[...])


def _odd_out(x2d, gates, row_fn, h2, gelu_gate, w_out, tm):
    n = x2d.shape[0]
    tok = pl.BlockSpec((tm, D_MODEL), lambda i: (i, 0))
    return pl.pallas_call(
        _odd_out_kernel,
        grid=(n // tm,),
        in_specs=[tok, pl.BlockSpec((None, 1, D_MODEL), lambda i: (row_fn(i), 0, 0)),
                  pl.BlockSpec((2, tm, RNN_WIDTH), lambda i: (0, i, 0)), tok, _const_spec((RNN_WIDTH, D_MODEL))],
        out_specs=tok,
        out_shape=jax.ShapeDtypeStruct((n, D_MODEL), F32),
        compiler_params=_cparams("parallel"),
        name="odd_out",
    )(x2d, gates, h2, gelu_gate, w_out)


def _rope_tables(seq):
    t = jnp.arange(seq)
    inv = ROPE_BASE ** (-jnp.arange(0, ROPE_AXIS_DIM, 2, dtype=F32) / ROPE_AXIS_DIM)
    ang_r = (t // GRID_W).astype(F32)[:, None] * inv
    ang_c = (t % GRID_W).astype(F32)[:, None] * inv
    cos32 = jnp.concatenate([jnp.cos(ang_r), jnp.cos(ang_r), jnp.cos(ang_c), jnp.cos(ang_c)], axis=1)
    sin32 = jnp.concatenate([-jnp.sin(ang_r), jnp.sin(ang_r), -jnp.sin(ang_c), jnp.sin(ang_c)], axis=1)
    pad = HEAD_BLOCK - MLA_QK
    cos = jnp.concatenate([jnp.ones((seq, MLA_NOPE), F32), cos32, jnp.ones((seq, pad), F32)], axis=1)
    sin = jnp.concatenate([jnp.zeros((seq, MLA_NOPE), F32), sin32, jnp.zeros((seq, pad), F32)], axis=1)
    cost = jnp.concatenate([jnp.cos(ang_r), jnp.cos(ang_c)], axis=1).T
    sint = jnp.concatenate([jnp.sin(ang_r), jnp.sin(ang_c)], axis=1).T
    return cos, sin, cost, sint


def _even_weights(w_in, cq_g, w_uq, ckv_g, w_ukv, q_g, k_g, gate_b):
    pad = HEAD_BLOCK - MLA_QK
    kr_cols = w_in[:, OFF_KR:OFF_KR + MLA_ROPE]
    w_kr = jnp.pad(kr_cols, ((0, 0), (MLA_NOPE, pad)))
    m_cols = w_in[:, OFF_KR + MLA_ROPE:OFF_KR + MLA_ROPE + 4 * MLSTM_WIDTH]
    g_cols = w_in[:, OFF_KR + MLA_ROPE + 4 * MLSTM_WIDTH:]
    w_all = jnp.concatenate([w_in[:, :OFF_KR], w_kr, m_cols, jnp.pad(g_cols, ((0, 0), (0, HEAD_BLOCK - N_GATES)))], axis=1)
    uq = jnp.pad(w_uq.reshape(MLA_Q_RANK, MLA_HEADS, MLA_QK), ((0, 0), (0, 0), (0, pad)))
    ukv = w_ukv.reshape(MLA_KV_RANK, MLA_HEADS, MLA_NOPE + MLA_V)
    uk = jnp.pad(ukv[:, :, :MLA_NOPE], ((0, 0), (0, 0), (0, HEAD_BLOCK - MLA_NOPE)))
    uv = ukv[:, :, MLA_NOPE:]
    kg_blk = jnp.pad(k_g[:MLA_NOPE], (0, HEAD_BLOCK - MLA_NOPE))
    krg_blk = jnp.pad(k_g[MLA_NOPE:], (MLA_NOPE, pad))
    seg = jnp.arange(HEAD_BLOCK)
    seg_id = jnp.where(seg < MLA_NOPE, 0, jnp.where(seg < MLA_QK, 1, 2))
    seg_len = jnp.where(seg < MLA_NOPE, float(MLA_NOPE), float(MLA_ROPE))
    ind128 = jnp.where((seg_id[:, None] == seg_id[None, :]) & (seg_id[:, None] < 2), 1.0 / seg_len[None, :], 0.0)
    zero = jnp.zeros_like(ind128)
    ind = jnp.block([[ind128, zero], [zero, ind128]])
    return {
        "w_all": w_all.astype(BF16),
        "w_gt": g_cols.T.astype(BF16),
        "gb_row": gate_b.reshape(1, N_GATES),
        "gb_col": gate_b.reshape(N_GATES, 1),
        "cq_g": cq_g.reshape(1, MLA_Q_RANK),
        "w_uqt": uq.reshape(MLA_Q_RANK, MLA_HEADS * HEAD_BLOCK).T.astype(BF16),
        "ckv_g": ckv_g.reshape(1, MLA_KV_RANK),
        "w_uk": uk.reshape(MLA_KV_RANK, MLA_HEADS * HEAD_BLOCK).astype(BF16),
        "w_uvt": uv.reshape(MLA_KV_RANK, MLA_HEADS * MLA_V).T.astype(BF16),
        "qg_nope": q_g[:MLA_NOPE].reshape(MLA_NOPE, 1),
        "qg_rope": q_g[MLA_NOPE:].reshape(MLA_ROPE, 1),
        "k_g": jnp.tile(kg_blk, MLA_HEADS).reshape(1, -1),
        "kr_g": krg_blk.reshape(1, HEAD_BLOCK),
        "ind": ind.astype(BF16),
    }


def kernel(x, c, ctx, c_ctx, mod_w, mod_b, norm_g, ffn_w_gate, ffn_w_up, ffn_w_down, even_w_in, even_w_out, mla_cq_g, mla_w_uq, mla_ckv_g, mla_w_ukv, mla_q_g, mla_k_g, mlstm_gate_b, mlstm_out_g, odd_w_in, odd_conv_w, odd_conv_b, lru_w_a, lru_b_a, lru_w_x, lru_b_x, lru_lam, odd_w_out):
    batch, seq, _ = x.shape
    ctx_len = ctx.shape[1]
    depth = mod_w.shape[0]
    assert depth == 2 and batch <= 7

    cond8 = jnp.zeros((8, D_MODEL), F32).at[:batch].set(c).at[batch].set(c_ctx)
    mods = _ada_params(cond8, mod_w, mod_b)
    mods = mods.reshape(depth, 8, 3, 3, D_MODEL).transpose(0, 2, 1, 3, 4)

    tm_x, tm_c = 512, 256
    x_row = lambda i: i // (seq // tm_x)
    c_row = lambda i: batch
    x2 = x.reshape(batch * seq, D_MODEL)
    c2 = ctx.reshape(batch * ctx_len, D_MODEL)

    wg = ffn_w_gate.astype(BF16)
    wu = ffn_w_up.astype(BF16)
    wd = ffn_w_down.astype(BF16)

    def ffn(t2, layer, which, row_fn, tm):
        return _ffn(t2, mods[layer, 2 * which], row_fn, norm_g[layer, 2 * which], wg[layer, which], wu[layer, which],
                    wd[layer, which], tm)

    x2 = ffn(x2, 0, 0, x_row, tm_x)
    c2 = ffn(c2, 0, 0, c_row, tm_c)

    ew = _even_weights(even_w_in[0], mla_cq_g[0], mla_w_uq[0], mla_ckv_g[0], mla_w_ukv[0], mla_q_g[0], mla_k_g[0],
                       mlstm_gate_b[0])
    tabs_x = _rope_tables(seq)
    tabs_c = (jnp.ones((tm_c, HEAD_BLOCK), F32), jnp.zeros((tm_c, HEAD_BLOCK), F32),
              jnp.ones((ROPE_AXIS_DIM, tm_c), F32), jnp.zeros((ROPE_AXIS_DIM, tm_c), F32))
    px = _even_prep(x2, mods[0, 1], x_row, norm_g[0, 1], tabs_x, lambda i: i % (seq // tm_x), ew, tm_x)
    pc = _even_prep(c2, mods[0, 1], c_row, norm_g[0, 1], tabs_c, lambda i: 0, ew, tm_c)
    qt_x, k_x, vt_x, mq_x, mk_x, mv_x, mo_x, gr_x, gc_x = px
    qt_c, k_c, vt_c, mq_c, mk_c, mv_c, mo_c, gr_c, gc_c = pc

    k_all = jnp.concatenate([k_c.reshape(batch, ctx_len, -1), k_x.reshape(batch, seq, -1)], axis=1)
    k_all = k_all.reshape(batch * (ctx_len + seq), -1)
    vt_all = jnp.concatenate([vt_c.reshape(-1, batch, ctx_len), vt_x.reshape(-1, batch, seq)], axis=2)

    def key_chunks(vt3, tk):
        return vt3.reshape(vt3.shape[0], batch, -1, tk).transpose(1, 2, 0, 3)

    att_x = _attention(qt_x, k_all, key_chunks(vt_all, 256), batch, 512)
    att_c = _attention(qt_c, k_c, key_chunks(vt_c.reshape(-1, batch, ctx_len), 256), batch, 256)

    chunk = 128
    nst = 2 * MLSTM_HEADS
    s0 = jnp.zeros((batch, nst, MLSTM_DH, 2 * MLSTM_DH), F32)
    m0 = jnp.zeros((batch, nst, MLSTM_DH), F32)
    hcf, hcb, s_c, m_c = _mlstm(mq_c, mk_c, mv_c, gr_c, gc_c, s0, m0, batch, chunk)
    hxf, hxb, _, _ = _mlstm(mq_x, mk_x, mv_x, gr_x, gc_x, s_c, m_c, batch, chunk)

    w_att = even_w_out[0, :MLA_HEADS * MLA_V].astype(BF16)
    w_ml = even_w_out[0, MLA_HEADS * MLA_V:].astype(BF16)
    gate0 = mods[0, 1][:, 2:3, :]
    x2 = _even_out(x2, gate0, x_row, att_x, hxf, hxb, mo_x, mlstm_out_g[0], w_att, w_ml, tm_x)
    c2 = _even_out(c2, gate0, c_row, att_c, hcf, hcb, mo_c, mlstm_out_g[0], w_att, w_ml, tm_c)
    x2 = ffn(x2, 0, 1, x_row, tm_x)
    c2 = ffn(c2, 0, 1, c_row, tm_c)

    x2 = ffn(x2, 1, 0, x_row, tm_x)
    c2 = ffn(c2, 1, 0, c_row, tm_c)
    w_in1 = odd_w_in[0].astype(BF16)
    gg_x, xr_x = _odd_in(x2, mods[1, 1], x_row, norm_g[1, 1], w_in1, tm_x)
    _, xr_c = _odd_in(c2, mods[1, 1], c_row, norm_g[1, 1], w_in1, tm_c)
    rw = {
        "conv_w": odd_conv_w[0],
        "conv_b": odd_conv_b[0].reshape(1, RNN_WIDTH),
        "w_ax": jnp.concatenate([lru_w_a[0], lru_w_x[0]], axis=-1).astype(BF16),
        "b_a": lru_b_a[0].reshape(2, 1, RNN_WIDTH),
        "b_x": lru_b_x[0].reshape(2, 1, RNN_WIDTH),
        "lam": lru_lam[0].reshape(2, 1, RNN_WIDTH),
    }
    h0 = jnp.zeros((2, batch, 1, RNN_WIDTH), F32)
    _, st_c = _rglru(xr_c, h0, rw, batch, 256)
    h_x, _ = _rglru(xr_x, st_c, rw, batch, 256)
    x2 = _odd_out(x2, mods[1, 1][:, 2:3, :], x_row, h_x, gg_x, odd_w_out[0].astype(BF16), tm_x)
    x2 = ffn(x2, 1, 1, x_row, tm_x)
    return x2.reshape(batch, seq, D_MODEL)
```

```python
import functools

import jax
import jax.numpy as jnp
from jax import lax
from jax.experimental import pallas as pl
from jax.experimental.pallas import tpu as pltpu

F32 = jnp.float32
BF16 = jnp.bfloat16

D_MODEL = 1024
GRID_W = 64
EPS = 1e-6
N_MOD = 9
D_FF = 2816
FFN_RESIDUAL = 0.5

MLA_HEADS = 8
MLA_Q_RANK = 384
MLA_KV_RANK = 256
MLA_NOPE = 64
MLA_ROPE = 32
MLA_V = 64
MLA_QK = MLA_NOPE + MLA_ROPE
ROPE_AXIS_DIM = MLA_ROPE // 2
ROPE_BASE = 10000.0
HEAD_BLOCK = 128
V_ROWS = MLA_V + 16

MLSTM_HEADS = 4
MLSTM_DH = 128
MLSTM_WIDTH = MLSTM_HEADS * MLSTM_DH
N_GATES = 4 * MLSTM_HEADS
MLSTM_ST = MLSTM_DH + 16

RNN_WIDTH = 1024
RNN_BLOCKS = 8
RNN_BLOCK_DIM = RNN_WIDTH // RNN_BLOCKS
CONV_W = 4
LRU_C = 8.0

NEG_BIG = -1e30
LOG2_E = 1.4426950408889634
VMEM_LIMIT = 56 * 1024 * 1024


def _cparams(*sem):
    return pltpu.CompilerParams(dimension_semantics=sem, vmem_limit_bytes=VMEM_LIMIT)


def _const_spec(shape):
    zeros = (0,) * len(shape)
    return pl.BlockSpec(shape, lambda *_: zeros, pipeline_mode=pl.Buffered(1))


def _dotf(a, b):
    return jnp.dot(a, b, preferred_element_type=F32)


def _split3(a):
    a1 = a.astype(BF16)
    r1 = a - a1.astype(F32)
    a2 = r1.astype(BF16)
    a3 = (r1 - a2.astype(F32)).astype(BF16)
    return a1, a2, a3


def _norm_mod(x, g, shift, scale):
    ms = jnp.mean(x * x, axis=-1, keepdims=True)
    return (x * lax.rsqrt(ms + EPS)) * (g * (1.0 + scale)) + shift


def _rms(x, g):
    ms = jnp.mean(x * x, axis=-1, keepdims=True)
    return x * lax.rsqrt(ms + EPS) * g


def _log_sigmoid(x):
    return jnp.minimum(x, 0.0) - jnp.log1p(jnp.exp(-jnp.abs(x)))


def _ada_kernel(cond_ref, w_ref, b_ref, o_ref):
    c = cond_ref[...]
    s = c * jax.nn.sigmoid(c)
    s1, s2, _ = _split3(s)
    w = w_ref[...]
    w1 = w.astype(BF16)
    w2 = (w - w1.astype(F32)).astype(BF16)
    o_ref[...] = _dotf(s1, w1) + _dotf(s1, w2) + _dotf(s2, w1) + b_ref[...]


def _ada_params(cond8, mod_w, mod_b):
    depth, _, n = mod_w.shape
    tn = 1152
    return pl.pallas_call(
        _ada_kernel,
        grid=(depth, n // tn),
        in_specs=[
            pl.BlockSpec((8, D_MODEL), lambda l, j: (0, 0)),
            pl.BlockSpec((None, D_MODEL, tn), lambda l, j: (l, 0, j)),
            pl.BlockSpec((None, 1, tn), lambda l, j: (l, 0, j)),
        ],
        out_specs=pl.BlockSpec((None, 8, tn), lambda l, j: (l, 0, j)),
        out_shape=jax.ShapeDtypeStruct((depth, 8, n), F32),
        compiler_params=_cparams("parallel", "parallel"),
        name="ada_params",
    )(cond8, mod_w, mod_b.reshape(depth, 1, n))


def _ffn_kernel(x_ref, mod_ref, g_ref, wg_ref, wu_ref, wd_ref, o_ref):
    x = x_ref[...]
    h = _norm_mod(x, g_ref[...], mod_ref[0:1, :], mod_ref[1:2, :]).astype(BF16)
    g = _dotf(h, wg_ref[...])
    u = _dotf(h, wu_ref[...])
    a = (g * jax.nn.sigmoid(g) * u).astype(BF16)
    y = _dotf(a, wd_ref[...])
    o_ref[...] = x + (FFN_RESIDUAL * mod_ref[2:3, :]) * y


def _ffn(x2d, mods, row_fn, g, wg, wu, wd, tm):
    n = x2d.shape[0]
    return pl.pallas_call(
        _ffn_kernel,
        grid=(n // tm,),
        in_specs=[
            pl.BlockSpec((tm, D_MODEL), lambda i: (i, 0)),
            pl.BlockSpec((None, 3, D_MODEL), lambda i: (row_fn(i), 0, 0)),
            _const_spec((1, D_MODEL)),
            _const_spec((D_MODEL, D_FF)),
            _const_spec((D_MODEL, D_FF)),
            _const_spec((D_FF, D_MODEL)),
        ],
        out_specs=pl.BlockSpec((tm, D_MODEL), lambda i: (i, 0)),
        out_shape=jax.ShapeDtypeStruct((n, D_MODEL), F32),
        compiler_params=_cparams("parallel"),
        name="ffn",
    )(x2d, mods, g.reshape(1, D_MODEL), wg, wu, wd)


W_ALL = MLA_Q_RANK + MLA_KV_RANK + HEAD_BLOCK + MLSTM_WIDTH + HEAD_BLOCK
OFF_CKV = MLA_Q_RANK
OFF_KR = OFF_CKV + MLA_KV_RANK
OFF_MK = OFF_KR + HEAD_BLOCK
OFF_GATES = OFF_MK + MLSTM_WIDTH
W_T_ROWS = N_GATES + 3 * MLSTM_WIDTH


def _seg_mean(sq, ind):
    hi = sq.astype(BF16)
    lo = (sq - hi.astype(F32)).astype(BF16)
    w = ind.shape[0]
    outs = []
    for j in range(sq.shape[1] // w):
        sl = slice(w * j, w * (j + 1))
        outs.append(_dotf(hi[:, sl], ind) + _dotf(lo[:, sl], ind))
    return outs[0] if len(outs) == 1 else jnp.concatenate(outs, axis=1)


def _rope_block(xb, cos, sin, first_half):
    partner = jnp.where(first_half, pltpu.roll(xb, HEAD_BLOCK - 8, axis=1), pltpu.roll(xb, 8, axis=1))
    return xb * cos + partner * sin


def _even_prep_kernel(x_ref, mod_ref, g_ref, cos_ref, sin_ref, cost_ref, sint_ref, wall_ref, wgt_ref, gbr_ref, gbc_ref,
                      cqg_ref, wuqt_ref, ckvg_ref, wuk_ref, wuvt_ref, qgn_ref, qgr_ref, kg_ref, krg_ref, ind_ref,
                      qt_ref, k_ref, vt_ref, mqt_ref, mk_ref, mvt_ref, mot_ref, gr_ref, gc_ref):
    tm = x_ref.shape[0]
    x = x_ref[...]
    h = _norm_mod(x, g_ref[...], mod_ref[0:1, :], mod_ref[1:2, :]).astype(BF16)
    p = _dotf(h, wall_ref[...])
    pt = lax.dot_general(wgt_ref[...], h, (((1,), (1,)), ((), ())), preferred_element_type=F32)

    mk_ref[...] = (p[:, OFF_MK:OFF_MK + MLSTM_WIDTH] * (MLSTM_DH ** -0.5)).astype(BF16)
    mqt_ref[...] = pt[N_GATES:N_GATES + MLSTM_WIDTH, :].astype(BF16)
    mvt_ref[...] = pt[N_GATES + MLSTM_WIDTH:N_GATES + 2 * MLSTM_WIDTH, :].astype(BF16)
    mot_ref[...] = pt[N_GATES + 2 * MLSTM_WIDTH:, :]
    graw = p[:, OFF_GATES:OFF_GATES + HEAD_BLOCK][:, :N_GATES] + gbr_ref[...]
    lane = lax.broadcasted_iota(jnp.int32, (tm, N_GATES), 1)
    gr_ref[...] = jnp.where((lane // MLSTM_HEADS) % 2 == 1, _log_sigmoid(graw), graw)
    gt = pt[:N_GATES, :] + gbc_ref[...]
    sub = lax.broadcasted_iota(jnp.int32, (N_GATES, tm), 0)
    gc_ref[...] = jnp.where((sub // MLSTM_HEADS) % 2 == 1, _log_sigmoid(gt), gt)

    ind = ind_ref[...]
    cos = cos_ref[...]
    sin = sin_ref[...]
    lane_b = lax.broadcasted_iota(jnp.int32, (tm, HEAD_BLOCK), 1)
    first_half = (lane_b % ROPE_AXIS_DIM) < (ROPE_AXIS_DIM // 2)

    cqn = _rms(p[:, 0:MLA_Q_RANK], cqg_ref[...]).astype(BF16)
    ckvn = _rms(p[:, OFF_CKV:OFF_CKV + MLA_KV_RANK], ckvg_ref[...]).astype(BF16)
    nt_dims = (((1,), (1,)), ((), ()))
    k_raw = _dotf(ckvn, wuk_ref[...])
    vt = lax.dot_general(wuvt_ref[...], ckvn, nt_dims, preferred_element_type=F32).astype(BF16)
    ones_rows = jnp.where(lax.broadcasted_iota(jnp.int32, (V_ROWS - MLA_V, tm), 0) == 0, 1.0, 0.0).astype(BF16)
    for hh in range(MLA_HEADS):
        vt_ref[V_ROWS * hh:V_ROWS * hh + MLA_V, :] = vt[MLA_V * hh:MLA_V * (hh + 1), :]
        vt_ref[V_ROWS * hh + MLA_V:V_ROWS * (hh + 1), :] = ones_rows

    kn = k_raw * lax.rsqrt(_seg_mean(k_raw * k_raw, ind) + EPS) * kg_ref[...]
    kr = p[:, OFF_KR:OFF_KR + HEAD_BLOCK]
    krn = kr * lax.rsqrt(_seg_mean(kr * kr, ind[:HEAD_BLOCK, :HEAD_BLOCK]) + EPS) * krg_ref[...]
    kr_rot = _rope_block(krn, cos, sin, first_half)
    for hh in range(MLA_HEADS):
        sl = slice(HEAD_BLOCK * hh, HEAD_BLOCK * (hh + 1))
        k_ref[:, sl] = (kn[:, sl] + kr_rot).astype(BF16)

    qt_raw = lax.dot_general(wuqt_ref[...], cqn, nt_dims, preferred_element_type=F32)
    q_scale = (MLA_QK ** -0.5) * LOG2_E
    half = ROPE_AXIS_DIM // 2
    cost = cost_ref[...]
    sint = sint_ref[...]
    for hh in range(MLA_HEADS):
        r0 = HEAD_BLOCK * hh
        nope = qt_raw[r0:r0 + MLA_NOPE, :]
        rope = qt_raw[r0 + MLA_NOPE:r0 + MLA_QK, :]
        nope = nope * lax.rsqrt(jnp.mean(nope * nope, axis=0, keepdims=True) + EPS) * qgn_ref[...]
        rope = rope * lax.rsqrt(jnp.mean(rope * rope, axis=0, keepdims=True) + EPS) * qgr_ref[...]
        rot = []
        for ax in range(2):
            x1 = rope[ROPE_AXIS_DIM * ax:ROPE_AXIS_DIM * ax + half, :]
            x2 = rope[ROPE_AXIS_DIM * ax + half:ROPE_AXIS_DIM * (ax + 1), :]
            cs = cost[half * ax:half * (ax + 1), :]
            sn = sint[half * ax:half * (ax + 1), :]
            rot += [x1 * cs - x2 * sn, x2 * cs + x1 * sn]
        qt_ref[r0:r0 + MLA_NOPE, :] = (nope * q_scale).astype(BF16)
        qt_ref[r0 + MLA_NOPE:r0 + MLA_QK, :] = (jnp.concatenate(rot, axis=0) * q_scale).astype(BF16)
        qt_ref[r0 + MLA_QK:r0 + HEAD_BLOCK, :] = jnp.zeros((HEAD_BLOCK - MLA_QK, tm), BF16)


def _even_prep(x2d, mods, row_fn, g, tabs, tab_fn, w, tm):
    n = x2d.shape[0]
    cos, sin, cost, sint = tabs
    tok = lambda width: pl.BlockSpec((tm, width), lambda i: (i, 0))
    tok_t = lambda rows: pl.BlockSpec((rows, tm), lambda i: (0, i))
    out_shape = [
        jax.ShapeDtypeStruct((MLA_HEADS * HEAD_BLOCK, n), BF16),
        jax.ShapeDtypeStruct((n, MLA_HEADS * HEAD_BLOCK), BF16),
        jax.ShapeDtypeStruct((MLA_HEADS * V_ROWS, n), BF16),
        jax.ShapeDtypeStruct((MLSTM_WIDTH, n), BF16),
        jax.ShapeDtypeStruct((n, MLSTM_WIDTH), BF16),
        jax.ShapeDtypeStruct((MLSTM_WIDTH, n), BF16),
        jax.ShapeDtypeStruct((MLSTM_WIDTH, n), F32),
        jax.ShapeDtypeStruct((n, N_GATES), F32),
        jax.ShapeDtypeStruct((N_GATES, n), F32),
    ]
    out_specs = [tok_t(1024), tok(1024), tok_t(MLA_HEADS * V_ROWS), tok_t(512), tok(512), tok_t(512), tok_t(512),
                 tok(N_GATES), tok_t(N_GATES)]
    in_specs = [
        tok(D_MODEL),
        pl.BlockSpec((None, 3, D_MODEL), lambda i: (row_fn(i), 0, 0)),
        _const_spec((1, D_MODEL)),
        pl.BlockSpec((tm, HEAD_BLOCK), lambda i: (tab_fn(i), 0)),
        pl.BlockSpec((tm, HEAD_BLOCK), lambda i: (tab_fn(i), 0)),
        pl.BlockSpec((ROPE_AXIS_DIM, tm), lambda i: (0, tab_fn(i))),
        pl.BlockSpec((ROPE_AXIS_DIM, tm), lambda i: (0, tab_fn(i))),
        _const_spec((D_MODEL, W_ALL)),
        _const_spec((W_T_ROWS, D_MODEL)),
        _const_spec((1, N_GATES)),
        _const_spec((N_GATES, 1)),
        _const_spec((1, MLA_Q_RANK)),
        _const_spec((1024, MLA_Q_RANK)),
        _const_spec((1, MLA_KV_RANK)),
        _const_spec((MLA_KV_RANK, 1024)),
        _const_spec((512, MLA_KV_RANK)),
        _const_spec((MLA_NOPE, 1)),
        _const_spec((MLA_ROPE, 1)),
        _const_spec((1, 1024)),
        _const_spec((1, HEAD_BLOCK)),
        _const_spec((2 * HEAD_BLOCK, 2 * HEAD_BLOCK)),
    ]
    return pl.pallas_call(
        _even_prep_kernel,
        grid=(n // tm,),
        in_specs=in_specs,
        out_specs=out_specs,
        out_shape=out_shape,
        compiler_params=_cparams("parallel"),
        name="even_prep",
    )(x2d, mods, g.reshape(1, D_MODEL), cos, sin, cost, sint, w["w_all"], w["w_gt"], w["gb_row"], w["gb_col"],
      w["cq_g"], w["w_uqt"], w["ckv_g"], w["w_uk"], w["w_uvt"], w["qg_nope"], w["qg_rope"], w["k_g"], w["kr_g"],
      w["ind"])


def _attn_kernel(qt_ref, k_ref, vt_ref, o_ref, st_scr, acc_scr):
    tq = qt_ref.shape[1]
    nk, _, tk = vt_ref.shape
    unroll = max(u for u in (16, 8, 4, 2) if (nk - 1) % u == 0)
    heads = range(2)

    def scores(j, slot):
        off = pl.multiple_of(j * tk, tk)
        cms = []
        for hh in heads:
            sl = slice(HEAD_BLOCK * hh, HEAD_BLOCK * (hh + 1))
            st = _dotf(k_ref[pl.ds(off, tk), sl], qt_ref[sl, :])
            st_scr[slot, hh] = st
            cms.append(jnp.max(st, axis=0, keepdims=True))
        return tuple(cms)

    def absorb(j, slot, cms, ms):
        out = []
        for hh in heads:
            m_new = jnp.maximum(ms[hh], cms[hh])
            alpha = jnp.exp2(ms[hh] - m_new)
            pt = jnp.exp2(st_scr[slot, hh] - m_new)
            pv = _dotf(vt_ref[j, V_ROWS * hh:V_ROWS * (hh + 1), :], pt.astype(BF16))
            acc_scr[hh] = alpha * acc_scr[hh] + pv
            out.append(m_new)
        return tuple(out)

    def body(i, carry):
        cms, ms = carry
        for r in range(unroll):
            j = unroll * i + r
            cms_next = scores(j + 1, (r + 1) % 2)
            ms = absorb(j, r % 2, cms, ms)
            cms = cms_next
        return cms, ms

    acc_scr[...] = jnp.zeros(acc_scr.shape, F32)
    m0 = jnp.full((1, tq), NEG_BIG, F32)
    cms, ms = lax.fori_loop(0, (nk - 1) // unroll, body, (scores(0, 0), (m0, m0)))
    absorb(nk - 1, 0, cms, ms)
    for hh in heads:
        acc = acc_scr[hh]
        o_ref[MLA_V * hh:MLA_V * (hh + 1), :] = (acc[:MLA_V, :] / acc[MLA_V:MLA_V + 1, :]).astype(o_ref.dtype)


def _attention(qt, k2d, vt4, batch, tq):
    nq = qt.shape[1] // batch // tq
    t_k = k2d.shape[0] // batch
    _, nk, _, tk = vt4.shape
    return pl.pallas_call(
        _attn_kernel,
        grid=(batch, MLA_HEADS // 2, nq),
        in_specs=[
            pl.BlockSpec((2 * HEAD_BLOCK, tq), lambda b, hp, i: (hp, b * nq + i)),
            pl.BlockSpec((t_k, 2 * HEAD_BLOCK), lambda b, hp, i: (b, hp)),
            pl.BlockSpec((None, nk, 2 * V_ROWS, tk), lambda b, hp, i: (b, 0, hp, 0)),
        ],
        out_specs=pl.BlockSpec((2 * MLA_V, tq), lambda b, hp, i: (hp, b * nq + i)),
        out_shape=jax.ShapeDtypeStruct((MLA_HEADS * MLA_V, qt.shape[1]), BF16),
        scratch_shapes=[pltpu.VMEM((2, 2, tk, tq), F32), pltpu.VMEM((2, V_ROWS, tq), F32)],
        compiler_params=_cparams("parallel", "parallel", "arbitrary"),
        name="mla_attention",
    )(qt, k2d, vt4)


def _mlstm_kernel(qtf_ref, kf_ref, vtf_ref, grf_ref, gcf_ref, qtb_ref, kb_ref, vtb_ref, grb_ref, gcb_ref,
                  s0_ref, m0_ref, hf_ref, hb_ref, s_out_ref, m_out_ref, s_scr, m_scr):
    j = pl.program_id(1)
    nj = pl.num_programs(1)
    L = kf_ref.shape[0]

    @pl.when(j == 0)
    def _():
        s_scr[...] = s0_ref[...]
        m_scr[...] = m0_ref[...]

    row = lax.broadcasted_iota(jnp.int32, (L, L), 0)
    col = lax.broadcasted_iota(jnp.int32, (L, L), 1)
    sub = lax.broadcasted_iota(jnp.int32, (MLSTM_ST - MLSTM_DH, L), 0)

    dirs = ((qtf_ref, kf_ref, vtf_ref, grf_ref, gcf_ref, hf_ref), (qtb_ref, kb_ref, vtb_ref, grb_ref, gcb_ref, hb_ref))
    combos = [(d, hh) for d in range(2) for hh in range(MLSTM_HEADS)]


    masks, cum_rows, cum_cols, g_rows, g_cols = [], [], [], [], []
    for d in range(2):
        s_le_t = (row <= col) if d == 0 else (row >= col)
        tri = jnp.where((col <= row) if d == 0 else (col >= row), 1.0, 0.0).astype(BF16)
        tri_t = jnp.where(s_le_t, 1.0, 0.0).astype(BF16)
        gr = dirs[d][3][...]
        gc = dirs[d][4][...]
        r1, r2, r3 = _split3(gr)
        c1, c2, c3 = _split3(gc)
        masks.append(s_le_t)
        g_rows.append(gr)
        g_cols.append(gc)
        cum_rows.append(_dotf(tri, r1) + _dotf(tri, r2) + _dotf(tri, r3))
        cum_cols.append(_dotf(c1, tri_t) + _dotf(c2, tri_t) + _dotf(c3, tri_t))
    kq, inter, s_prev, m_prev = [], [], [], []
    for d, hh in combos:
        idx = d * MLSTM_HEADS + hh
        sl = slice(MLSTM_DH * hh, MLSTM_DH * (hh + 1))
        qth = dirs[d][0][sl, :]
        s_prev.append(s_scr[idx])
        m_prev.append(m_scr[idx:idx + 1, 0:1])
        kq.append(_dotf(dirs[d][1][:, sl], qth))
        inter.append(_dotf(s_prev[-1].astype(BF16), qth))

    sw, den, w_inter, floor, vw, decay, m_new = [], [], [], [], [], [], []
    for c, (d, hh) in enumerate(combos):
        il = (0 if d == 0 else 2 * MLSTM_HEADS) + hh
        fl = il + MLSTM_HEADS
        sl = slice(MLSTM_DH * hh, MLSTM_DH * (hh + 1))
        c_col = g_rows[d][:, il:il + 1] - cum_rows[d][:, fl:fl + 1]
        b_row = cum_cols[d][fl:fl + 1, :]
        c_row = g_cols[d][il:il + 1, :] - b_row
        c_mat = jnp.where(masks[d], c_col, NEG_BIG)
        u = jnp.maximum(jnp.max(c_mat, axis=0, keepdims=True), m_prev[c])
        s = kq[c] * jnp.exp(c_mat - u)
        sw.append(s.astype(BF16))
        den.append(jnp.sum(s, axis=0, keepdims=True))
        w_inter.append(jnp.exp(m_prev[c] - u))
        floor.append(jnp.exp(-(b_row + u)))
        b_end = b_row[:, L - 1:L] if d == 0 else b_row[:, 0:1]
        g_row = b_end + c_row
        m_new.append(jnp.maximum(b_end + m_prev[c], jnp.max(g_row, axis=-1, keepdims=True)))
        decay.append(jnp.exp(b_end + m_prev[c] - m_new[c]))
        w_s = jnp.exp(g_row - m_new[c])
        n_rows = jnp.where(sub == 0, w_s, 0.0)
        vw.append(jnp.concatenate([dirs[d][2][sl, :].astype(F32) * w_s, n_rows], axis=0).astype(BF16))

    for c, (d, hh) in enumerate(combos):
        idx = d * MLSTM_HEADS + hh
        sl = slice(MLSTM_DH * hh, MLSTM_DH * (hh + 1))
        num = _dotf(dirs[d][2][sl, :], sw[c]) + w_inter[c] * inter[c][:MLSTM_DH, :]
        dn = den[c] + w_inter[c] * inter[c][MLSTM_DH:MLSTM_DH + 1, :]
        dirs[d][5][sl, :] = num / jnp.maximum(jnp.abs(dn), floor[c])
        s_scr[idx] = decay[c] * s_prev[c] + _dotf(vw[c], dirs[d][1][:, sl])
        m_scr[idx:idx + 1, :] = jnp.broadcast_to(m_new[c], (1, MLSTM_DH))

    @pl.when(j == nj - 1)
    def _():
        s_out_ref[...] = s_scr[...]
        m_out_ref[...] = m_scr[...]


def _mlstm(mqt, mk, mvt, g_rows, g_cols, s0, m0, batch, chunk):
    n = mk.shape[0]
    nc = n // batch // chunk
    fwd = lambda b, j: (b * nc + j, 0)
    bwd = lambda b, j: (b * nc + (nc - 1 - j), 0)
    fwd_t = lambda b, j: (0, b * nc + j)
    bwd_t = lambda b, j: (0, b * nc + (nc - 1 - j))
    nst = 2 * MLSTM_HEADS
    tok = lambda fn: pl.BlockSpec((chunk, MLSTM_WIDTH), fn)
    tok_t = lambda fn: pl.BlockSpec((MLSTM_WIDTH, chunk), fn)
    state_s = pl.BlockSpec((None, nst, MLSTM_ST, MLSTM_DH), lambda b, j: (b, 0, 0, 0))
    state_m = pl.BlockSpec((None, nst, MLSTM_DH), lambda b, j: (b, 0, 0))
    return pl.pallas_call(
        _mlstm_kernel,
        grid=(batch, nc),
        in_specs=[tok_t(fwd_t), tok(fwd), tok_t(fwd_t), pl.BlockSpec((chunk, N_GATES), fwd),
                  pl.BlockSpec((N_GATES, chunk), fwd_t),
                  tok_t(bwd_t), tok(bwd), tok_t(bwd_t), pl.BlockSpec((chunk, N_GATES), bwd),
                  pl.BlockSpec((N_GATES, chunk), bwd_t),
                  state_s, state_m],
        out_specs=[tok_t(fwd_t), tok_t(bwd_t), state_s, state_m],
        out_shape=[jax.ShapeDtypeStruct((MLSTM_WIDTH, n), F32), jax.ShapeDtypeStruct((MLSTM_WIDTH, n), F32),
                   jax.ShapeDtypeStruct(s0.shape, F32), jax.ShapeDtypeStruct(m0.shape, F32)],
        scratch_shapes=[pltpu.VMEM((nst, MLSTM_ST, MLSTM_DH), F32), pltpu.VMEM((nst, MLSTM_DH), F32)],
        compiler_params=_cparams("parallel", "arbitrary"),
        name="mlstm",
    )(mqt, mk, mvt, g_rows, g_cols, mqt, mk, mvt, g_rows, g_cols, s0, m0)


def _even_out_kernel(x_ref, gate_ref, att_ref, hf_ref, hb_ref, mo_ref, og_ref, w_ref, o_ref):
    hs = hf_ref[...] + hb_ref[...]
    og = og_ref[...]
    parts = [att_ref[...]]
    for hh in range(MLSTM_HEADS):
        sl = slice(MLSTM_DH * hh, MLSTM_DH * (hh + 1))
        blk = hs[sl, :]
        hn = blk * lax.rsqrt(jnp.mean(blk * blk, axis=0, keepdims=True) + EPS) * og[sl, :]
        parts.append((hn * jax.nn.sigmoid(mo_ref[sl, :])).astype(BF16))
    mixed = jnp.concatenate(parts, axis=0)
    y = lax.dot_general(mixed, w_ref[...], (((0,), (0,)), ((), ())), preferred_element_type=F32)
    o_ref[...] = x_ref[...] + gate_ref[...] * y


def _even_out(x2d, gates, row_fn, att, hf, hb, mo, out_g, w_out, tm):
    n = x2d.shape[0]
    tok = pl.BlockSpec((tm, D_MODEL), lambda i: (i, 0))
    tok_t = pl.BlockSpec((MLSTM_WIDTH, tm), lambda i: (0, i))
    return pl.pallas_call(
        _even_out_kernel,
        grid=(n // tm,),
        in_specs=[tok, pl.BlockSpec((None, 1, D_MODEL), lambda i: (row_fn(i), 0, 0)),
                  tok_t, tok_t, tok_t, tok_t, _const_spec((MLSTM_WIDTH, 1)), _const_spec((D_MODEL, D_MODEL))],
        out_specs=tok,
        out_shape=jax.ShapeDtypeStruct((n, D_MODEL), F32),
        compiler_params=_cparams("parallel"),
        name="even_out",
    )(x2d, gates, att, hf, hb, mo, out_g.reshape(MLSTM_WIDTH, 1), w_out)


def _gelu_tanh(x):
    return 0.5 * x * (1.0 + jnp.tanh(0.7978845608028654 * (x + 0.044715 * (x * x * x))))


def _odd_in_kernel(x_ref, mod_ref, g_ref, w_ref, gate_ref, xr_ref):
    h = _norm_mod(x_ref[...], g_ref[...], mod_ref[0:1, :], mod_ref[1:2, :]).astype(BF16)
    p = _dotf(h, w_ref[...])
    gate_ref[...] = _gelu_tanh(p[:, :RNN_WIDTH])
    xr_ref[...] = p[:, RNN_WIDTH:]


def _odd_in(x2d, mods, row_fn, g, w_in, tm):
    n = x2d.shape[0]
    tok = pl.BlockSpec((tm, D_MODEL), lambda i: (i, 0))
    return pl.pallas_call(
        _odd_in_kernel,
        grid=(n // tm,),
        in_specs=[tok, pl.BlockSpec((None, 3, D_MODEL), lambda i: (row_fn(i), 0, 0)), _const_spec((1, D_MODEL)),
                  _const_spec((D_MODEL, 2 * RNN_WIDTH))],
        out_specs=[tok, tok],
        out_shape=[jax.ShapeDtypeStruct((n, RNN_WIDTH), F32), jax.ShapeDtypeStruct((n, RNN_WIDTH), F32)],
        compiler_params=_cparams("parallel"),
        name="odd_in",
    )(x2d, mods, g.reshape(1, D_MODEL), w_in)


def _rglru_kernel(xc_ref, xp_ref, xn_ref, cw_ref, cb_ref, wax_ref, ba_ref, bx_ref, lam_ref, h0_ref,
                  h_ref, ht_ref, ext_scr, a_scr, b_scr, carry_scr, *, nt):
    d = pl.program_id(0)
    i = pl.program_id(2)
    tm = xc_ref.shape[0]
    ti = jnp.where(d == 0, i, nt - 1 - i)

    ext_scr[8:8 + tm, :] = xc_ref[...]
    ext_scr[0:8, :] = jnp.where(ti > 0, xp_ref[...], 0.0)
    ext_scr[8 + tm:16 + tm, :] = jnp.where(ti < nt - 1, xn_ref[...], 0.0)
    cw = cw_ref[...]
    xc = cb_ref[...] + cw[0:1, :] * ext_scr[6:6 + tm, :]
    for k in range(1, CONV_W):
        xc = xc + cw[k:k + 1, :] * ext_scr[6 + k:6 + k + tm, :]

    lam = lam_ref[...]
    neg_lam = -lam
    softplus = jnp.maximum(neg_lam, 0.0) + jnp.log1p(jnp.exp(-jnp.abs(neg_lam)))
    ba = ba_ref[...]
    bx = bx_ref[...]
    for nb in range(RNN_BLOCKS):
        sl = slice(RNN_BLOCK_DIM * nb, RNN_BLOCK_DIM * (nb + 1))
        u = xc[:, sl]
        rg = _dotf(u.astype(BF16), wax_ref[nb])
        r = jax.nn.sigmoid(rg[:, :RNN_BLOCK_DIM] + ba[:, sl])
        ig = jax.nn.sigmoid(rg[:, RNN_BLOCK_DIM:] + bx[:, sl])
        log_a = -LRU_C * r * softplus[:, sl]
        a = jnp.exp(log_a)
        a_scr[:, sl] = a
        b_scr[:, sl] = jnp.sqrt(-jnp.tanh(log_a) * (a * a + 1.0)) * (ig * u)

    @pl.when(i == 0)
    def _():
        carry_scr[...] = h0_ref[...]

    def block(jb, h, reverse):
        base = pl.multiple_of(jb * 8, 8)
        av = a_scr[pl.ds(base, 8), :]
        bv = b_scr[pl.ds(base, 8), :]
        rows = [None] * 8
        for r in (range(7, -1, -1) if reverse else range(8)):
            h = av[r:r + 1, :] * h + bv[r:r + 1, :]
            rows[r] = h
        h_ref[pl.ds(base, 8), :] = jnp.concatenate(rows, axis=0)
        return h

    @pl.when(d == 0)
    def _():
        carry_scr[...] = lax.fori_loop(0, tm // 8, lambda jb, h: block(jb, h, False), carry_scr[...])

    @pl.when(d == 1)
    def _():
        carry_scr[...] = lax.fori_loop(0, tm // 8, lambda jb, h: block(tm // 8 - 1 - jb, h, True), carry_scr[...])

    @pl.when(i == nt - 1)
    def _():
        ht_ref[...] = carry_scr[...]


def _rglru(xr2d, h0, w, batch, tm):
    n = xr2d.shape[0]
    nt = n // batch // tm
    t8 = tm // 8
    tile = lambda d, i: jnp.where(d == 0, i, nt - 1 - i)
    cur = lambda d, b, i: (b * nt + tile(d, i), 0)
    prev = lambda d, b, i: (jnp.maximum((b * nt + tile(d, i)) * t8 - 1, 0), 0)
    nxt = lambda d, b, i: (jnp.minimum((b * nt + tile(d, i) + 1) * t8, n // 8 - 1), 0)
    per_dir = lambda shape: pl.BlockSpec((None,) + shape, lambda d, b, i: (d,) + (0,) * len(shape))
    return pl.pallas_call(
        functools.partial(_rglru_kernel, nt=nt),
        grid=(2, batch, nt),
        in_specs=[pl.BlockSpec((tm, RNN_WIDTH), cur), pl.BlockSpec((8, RNN_WIDTH), prev), pl.BlockSpec((8, RNN_WIDTH), nxt),
                  pl.BlockSpec((CONV_W, RNN_WIDTH), lambda d, b, i: (0, 0)), pl.BlockSpec((1, RNN_WIDTH), lambda d, b, i: (0, 0)),
                  per_dir((RNN_BLOCKS, RNN_BLOCK_DIM, 2 * RNN_BLOCK_DIM)), per_dir((1, RNN_WIDTH)), per_dir((1, RNN_WIDTH)),
                  per_dir((1, RNN_WIDTH)),
                  pl.BlockSpec((None, None, 1, RNN_WIDTH), lambda d, b, i: (d, b, 0, 0))],
        out_specs=[pl.BlockSpec((None, tm, RNN_WIDTH), lambda d, b, i: (d, b * nt + tile(d, i), 0)),
                   pl.BlockSpec((None, None, 1, RNN_WIDTH), lambda d, b, i: (d, b, 0, 0))],
        out_shape=[jax.ShapeDtypeStruct((2, n, RNN_WIDTH), F32), jax.ShapeDtypeStruct((2, batch, 1, RNN_WIDTH), F32)],
        scratch_shapes=[pltpu.VMEM((tm + 16, RNN_WIDTH), F32), pltpu.VMEM((tm, RNN_WIDTH), F32),
                        pltpu.VMEM((tm, RNN_WIDTH), F32), pltpu.VMEM((1, RNN_WIDTH), F32)],
        compiler_params=_cparams("parallel", "parallel", "arbitrary"),
        name="rglru",
    )(xr2d, xr2d, xr2d, w["conv_w"], w["conv_b"], w["w_ax"], w["b_a"], w["b_x"], w["lam"], h0)


def _odd_out_kernel(x_ref, gate_ref, h_ref, gg_ref, w_ref, o_ref):
    y = ((h_ref[0] + h_ref[1]).astype(F32) * gg_ref[...]).astype(BF16)
    o_ref[...] = x_ref[...] + gate_ref[...] * _dotf(y, w_ref[...])


def _odd_out(x2d, gates, row_fn, h2, gelu_gate, w_out, tm):
    n = x2d.shape[0]
    tok = pl.BlockSpec((tm, D_MODEL), lambda i: (i, 0))
    return pl.pallas_call(
        _odd_out_kernel,
        grid=(n // tm,),
        in_specs=[tok, pl.BlockSpec((None, 1, D_MODEL), lambda i: (row_fn(i), 0, 0)),
                  pl.BlockSpec((2, tm, RNN_WIDTH), lambda i: (0, i, 0)), tok, _const_spec((RNN_WIDTH, D_MODEL))],
        out_specs=tok,
        out_shape=jax.ShapeDtypeStruct((n, D_MODEL), F32),
        compiler_params=_cparams("parallel"),
        name="odd_out",
    )(x2d, gates, h2, gelu_gate, w_out)


def _rope_tables(seq):
    t = jnp.arange(seq)
    inv = ROPE_BASE ** (-jnp.arange(0, ROPE_AXIS_DIM, 2, dtype=F32) / ROPE_AXIS_DIM)
    ang_r = (t // GRID_W).astype(F32)[:, None] * inv
    ang_c = (t % GRID_W).astype(F32)[:, None] * inv
    cos32 = jnp.concatenate([jnp.cos(ang_r), jnp.cos(ang_r), jnp.cos(ang_c), jnp.cos(ang_c)], axis=1)
    sin32 = jnp.concatenate([-jnp.sin(ang_r), jnp.sin(ang_r), -jnp.sin(ang_c), jnp.sin(ang_c)], axis=1)
    pad = HEAD_BLOCK - MLA_QK
    cos = jnp.concatenate([jnp.ones((seq, MLA_NOPE), F32), cos32, jnp.ones((seq, pad), F32)], axis=1)
    sin = jnp.concatenate([jnp.zeros((seq, MLA_NOPE), F32), sin32, jnp.zeros((seq, pad), F32)], axis=1)
    cost = jnp.concatenate([jnp.cos(ang_r), jnp.cos(ang_c)], axis=1).T
    sint = jnp.concatenate([jnp.sin(ang_r), jnp.sin(ang_c)], axis=1).T
    return cos, sin, cost, sint


def _even_weights(w_in, cq_g, w_uq, ckv_g, w_ukv, q_g, k_g, gate_b):
    pad = HEAD_BLOCK - MLA_QK
    kr_cols = w_in[:, OFF_KR:OFF_KR + MLA_ROPE]
    w_kr = jnp.pad(kr_cols, ((0, 0), (MLA_NOPE, pad)))
    m_cols = w_in[:, OFF_KR + MLA_ROPE:OFF_KR + MLA_ROPE + 4 * MLSTM_WIDTH]
    g_cols = w_in[:, OFF_KR + MLA_ROPE + 4 * MLSTM_WIDTH:]
    mq_cols, mk_cols, mv_cols, mo_cols = jnp.split(m_cols, 4, axis=1)
    w_all = jnp.concatenate([w_in[:, :OFF_KR], w_kr, mk_cols, jnp.pad(g_cols, ((0, 0), (0, HEAD_BLOCK - N_GATES)))], axis=1)
    uq = jnp.pad(w_uq.reshape(MLA_Q_RANK, MLA_HEADS, MLA_QK), ((0, 0), (0, 0), (0, pad)))
    ukv = w_ukv.reshape(MLA_KV_RANK, MLA_HEADS, MLA_NOPE + MLA_V)
    uk = jnp.pad(ukv[:, :, :MLA_NOPE], ((0, 0), (0, 0), (0, HEAD_BLOCK - MLA_NOPE)))
    uv = ukv[:, :, MLA_NOPE:]
    kg_blk = jnp.pad(k_g[:MLA_NOPE], (0, HEAD_BLOCK - MLA_NOPE))
    krg_blk = jnp.pad(k_g[MLA_NOPE:], (MLA_NOPE, pad))
    seg = jnp.arange(HEAD_BLOCK)
    seg_id = jnp.where(seg < MLA_NOPE, 0, jnp.where(seg < MLA_QK, 1, 2))
    seg_len = jnp.where(seg < MLA_NOPE, float(MLA_NOPE), float(MLA_ROPE))
    ind128 = jnp.where((seg_id[:, None] == seg_id[None, :]) & (seg_id[:, None] < 2), 1.0 / seg_len[None, :], 0.0)
    zero = jnp.zeros_like(ind128)
    ind = jnp.block([[ind128, zero], [zero, ind128]])
    return {
        "w_all": w_all.astype(BF16),
        "w_gt": jnp.concatenate([g_cols, mq_cols, mv_cols, mo_cols], axis=1).T.astype(BF16),
        "gb_row": gate_b.reshape(1, N_GATES),
        "gb_col": gate_b.reshape(N_GATES, 1),
        "cq_g": cq_g.reshape(1, MLA_Q_RANK),
        "w_uqt": uq.reshape(MLA_Q_RANK, MLA_HEADS * HEAD_BLOCK).T.astype(BF16),
        "ckv_g": ckv_g.reshape(1, MLA_KV_RANK),
        "w_uk": uk.reshape(MLA_KV_RANK, MLA_HEADS * HEAD_BLOCK).astype(BF16),
        "w_uvt": uv.reshape(MLA_KV_RANK, MLA_HEADS * MLA_V).T.astype(BF16),
        "qg_nope": q_g[:MLA_NOPE].reshape(MLA_NOPE, 1),
        "qg_rope": q_g[MLA_NOPE:].reshape(MLA_ROPE, 1),
        "k_g": jnp.tile(kg_blk, MLA_HEADS).reshape(1, -1),
        "kr_g": krg_blk.reshape(1, HEAD_BLOCK),
        "ind": ind.astype(BF16),
    }


def kernel(x, c, ctx, c_ctx, mod_w, mod_b, norm_g, ffn_w_gate, ffn_w_up, ffn_w_down, even_w_in, even_w_out, mla_cq_g, mla_w_uq, mla_ckv_g, mla_w_ukv, mla_q_g, mla_k_g, mlstm_gate_b, mlstm_out_g, odd_w_in, odd_conv_w, odd_conv_b, lru_w_a, lru_b_a, lru_w_x, lru_b_x, lru_lam, odd_w_out):
    batch, seq, _ = x.shape
    ctx_len = ctx.shape[1]
    depth = mod_w.shape[0]
    assert depth == 2 and batch <= 7

    cond8 = jnp.zeros((8, D_MODEL), F32).at[:batch].set(c).at[batch].set(c_ctx)
    mods = _ada_params(cond8, mod_w, mod_b)
    mods = mods.reshape(depth, 8, 3, 3, D_MODEL).transpose(0, 2, 1, 3, 4)

    tm_x, tm_c = 512, 256
    x_row = lambda i: i // (seq // tm_x)
    c_row = lambda i: batch
    x2 = x.reshape(batch * seq, D_MODEL)
    c2 = ctx.reshape(batch * ctx_len, D_MODEL)

    wg = ffn_w_gate.astype(BF16)
    wu = ffn_w_up.astype(BF16)
    wd = ffn_w_down.astype(BF16)

    def ffn(t2, layer, which, row_fn, tm):
        return _ffn(t2, mods[layer, 2 * which], row_fn, norm_g[layer, 2 * which], wg[layer, which], wu[layer, which],
                    wd[layer, which], tm)

    x2 = ffn(x2, 0, 0, x_row, tm_x)
    c2 = ffn(c2, 0, 0, c_row, tm_c)

    ew = _even_weights(even_w_in[0], mla_cq_g[0], mla_w_uq[0], mla_ckv_g[0], mla_w_ukv[0], mla_q_g[0], mla_k_g[0],
                       mlstm_gate_b[0])
    tabs_x = _rope_tables(seq)
    tabs_c = (jnp.ones((tm_c, HEAD_BLOCK), F32), jnp.zeros((tm_c, HEAD_BLOCK), F32),
              jnp.ones((ROPE_AXIS_DIM, tm_c), F32), jnp.zeros((ROPE_AXIS_DIM, tm_c), F32))
    px = _even_prep(x2, mods[0, 1], x_row, norm_g[0, 1], tabs_x, lambda i: i % (seq // tm_x), ew, tm_x)
    pc = _even_prep(c2, mods[0, 1], c_row, norm_g[0, 1], tabs_c, lambda i: 0, ew, tm_c)
    qt_x, k_x, vt_x, mqt_x, mk_x, mvt_x, mot_x, gr_x, gc_x = px
    qt_c, k_c, vt_c, mqt_c, mk_c, mvt_c, mot_c, gr_c, gc_c = pc

    k_all = jnp.concatenate([k_c.reshape(batch, ctx_len, -1), k_x.reshape(batch, seq, -1)], axis=1)
    k_all = k_all.reshape(batch * (ctx_len + seq), -1)
    vt_all = jnp.concatenate([vt_c.reshape(-1, batch, ctx_len), vt_x.reshape(-1, batch, seq)], axis=2)

    def key_chunks(vt3, tk):
        return vt3.reshape(vt3.shape[0], batch, -1, tk).transpose(1, 2, 0, 3)

    att_x = _attention(qt_x, k_all, key_chunks(vt_all, 256), batch, 512)
    att_c = _attention(qt_c, k_c, key_chunks(vt_c.reshape(-1, batch, ctx_len), 256), batch, 256)

    chunk = 128
    nst = 2 * MLSTM_HEADS
    s0 = jnp.zeros((batch, nst, MLSTM_ST, MLSTM_DH), F32)
    m0 = jnp.zeros((batch, nst, MLSTM_DH), F32)
    hcf, hcb, s_c, m_c = _mlstm(mqt_c, mk_c, mvt_c, gr_c, gc_c, s0, m0, batch, chunk)
    hxf, hxb, _, _ = _mlstm(mqt_x, mk_x, mvt_x, gr_x, gc_x, s_c, m_c, batch, chunk)

    w_out0 = even_w_out[0].astype(BF16)
    gate0 = mods[0, 1][:, 2:3, :]
    x2 = _even_out(x2, gate0, x_row, att_x, hxf, hxb, mot_x, mlstm_out_g[0], w_out0, tm_x)
    c2 = _even_out(c2, gate0, c_row, att_c, hcf, hcb, mot_c, mlstm_out_g[0], w_out0, tm_c)
    x2 = ffn(x2, 0, 1, x_row, tm_x)
    c2 = ffn(c2, 0, 1, c_row, tm_c)

    x2 = ffn(x2, 1, 0, x_row, tm_x)
    c2 = ffn(c2, 1, 0, c_row, tm_c)
    w_in1 = odd_w_in[0].astype(BF16)
    gg_x, xr_x = _odd_in(x2, mods[1, 1], x_row, norm_g[1, 1], w_in1, tm_x)
    _, xr_c = _odd_in(c2, mods[1, 1], c_row, norm_g[1, 1], w_in1, tm_c)
    rw = {
        "conv_w": odd_conv_w[0],
        "conv_b": odd_conv_b[0].reshape(1, RNN_WIDTH),
        "w_ax": jnp.concatenate([lru_w_a[0], lru_w_x[0]], axis=-1).astype(BF16),
        "b_a": lru_b_a[0].reshape(2, 1, RNN_WIDTH),
        "b_x": lru_b_x[0].reshape(2, 1, RNN_WIDTH),
        "lam": lru_lam[0].reshape(2, 1, RNN_WIDTH),
    }
    h0 = jnp.zeros((2, batch, 1, RNN_WIDTH), F32)
    _, st_c = _rglru(xr_c, h0, rw, batch, 256)
    h_x, _ = _rglru(xr_x, st_c, rw, batch, 256)
    x2 = _odd_out(x2, mods[1, 1][:, 2:3, :], x_row, h_x, gg_x, odd_w_out[0].astype(BF16), tm_x)
    x2 = ffn(x2, 1, 1, x_row, tm_x)
    return x2.reshape(batch, seq, D_MODEL)
```

```python
import functools

import jax
import jax.numpy as jnp
from jax import lax
from jax.experimental import pallas as pl
from jax.experimental.pallas import tpu as pltpu

F32 = jnp.float32
BF16 = jnp.bfloat16

D_MODEL = 1024
GRID_W = 64
EPS = 1e-6
N_MOD = 9
D_FF = 2816
FFN_RESIDUAL = 0.5

MLA_HEADS = 8
MLA_Q_RANK = 384
MLA_KV_RANK = 256
MLA_NOPE = 64
MLA_ROPE = 32
MLA_V = 64
MLA_QK = MLA_NOPE + MLA_ROPE
ROPE_AXIS_DIM = MLA_ROPE // 2
ROPE_BASE = 10000.0
HEAD_BLOCK = 128
V_ROWS = MLA_V + 16

MLSTM_HEADS = 4
MLSTM_DH = 128
MLSTM_WIDTH = MLSTM_HEADS * MLSTM_DH
N_GATES = 4 * MLSTM_HEADS
MLSTM_ST = MLSTM_DH + 16

RNN_WIDTH = 1024
RNN_BLOCKS = 8
RNN_BLOCK_DIM = RNN_WIDTH // RNN_BLOCKS
CONV_W = 4
LRU_C = 8.0

NEG_BIG = -1e30
LOG2_E = 1.4426950408889634
VMEM_LIMIT = 56 * 1024 * 1024


def _cparams(*sem):
    return pltpu.CompilerParams(dimension_semantics=sem, vmem_limit_bytes=VMEM_LIMIT)


def _const_spec(shape):
    zeros = (0,) * len(shape)
    return pl.BlockSpec(shape, lambda *_: zeros, pipeline_mode=pl.Buffered(1))


def _dotf(a, b):
    return jnp.dot(a, b, preferred_element_type=F32)


def _split3(a):
    a1 = a.astype(BF16)
    r1 = a - a1.astype(F32)
    a2 = r1.astype(BF16)
    a3 = (r1 - a2.astype(F32)).astype(BF16)
    return a1, a2, a3


def _norm_mod(x, g, shift, scale):
    ms = jnp.mean(x * x, axis=-1, keepdims=True)
    return (x * lax.rsqrt(ms + EPS)) * (g * (1.0 + scale)) + shift


def _rms(x, g):
    ms = jnp.mean(x * x, axis=-1, keepdims=True)
    return x * lax.rsqrt(ms + EPS) * g


def _log_sigmoid(x):
    return jnp.minimum(x, 0.0) - jnp.log1p(jnp.exp(-jnp.abs(x)))


def _ada_kernel(cond_ref, w_ref, b_ref, o_ref):
    c = cond_ref[...]
    s = c * jax.nn.sigmoid(c)
    s1, s2, _ = _split3(s)
    w = w_ref[...]
    w1 = w.astype(BF16)
    w2 = (w - w1.astype(F32)).astype(BF16)
    o_ref[...] = _dotf(s1, w1) + _dotf(s1, w2) + _dotf(s2, w1) + b_ref[...]


def _ada_params(cond8, mod_w, mod_b):
    depth, _, n = mod_w.shape
    tn = 1152
    return pl.pallas_call(
        _ada_kernel,
        grid=(depth, n // tn),
        in_specs=[
            pl.BlockSpec((8, D_MODEL), lambda l, j: (0, 0)),
            pl.BlockSpec((None, D_MODEL, tn), lambda l, j: (l, 0, j)),
            pl.BlockSpec((None, 1, tn), lambda l, j: (l, 0, j)),
        ],
        out_specs=pl.BlockSpec((None, 8, tn), lambda l, j: (l, 0, j)),
        out_shape=jax.ShapeDtypeStruct((depth, 8, n), F32),
        compiler_params=_cparams("parallel", "parallel"),
        name="ada_params",
    )(cond8, mod_w, mod_b.reshape(depth, 1, n))


def _ffn_kernel(x_ref, mod_ref, g_ref, wg_ref, wu_ref, wd_ref, o_ref):
    x = x_ref[...]
    h = _norm_mod(x, g_ref[...], mod_ref[0:1, :], mod_ref[1:2, :]).astype(BF16)
    g = _dotf(h, wg_ref[...])
    u = _dotf(h, wu_ref[...])
    a = (g * jax.nn.sigmoid(g) * u).astype(BF16)
    y = _dotf(a, wd_ref[...])
    o_ref[...] = x + (FFN_RESIDUAL * mod_ref[2:3, :]) * y


def _ffn(x2d, mods, row_fn, g, wg, wu, wd, tm):
    n = x2d.shape[0]
    return pl.pallas_call(
        _ffn_kernel,
        grid=(n // tm,),
        in_specs=[
            pl.BlockSpec((tm, D_MODEL), lambda i: (i, 0)),
            pl.BlockSpec((None, 3, D_MODEL), lambda i: (row_fn(i), 0, 0)),
            _const_spec((1, D_MODEL)),
            _const_spec((D_MODEL, D_FF)),
            _const_spec((D_MODEL, D_FF)),
            _const_spec((D_FF, D_MODEL)),
        ],
        out_specs=pl.BlockSpec((tm, D_MODEL), lambda i: (i, 0)),
        out_shape=jax.ShapeDtypeStruct((n, D_MODEL), F32),
        compiler_params=_cparams("parallel"),
        name="ffn",
    )(x2d, mods, g.reshape(1, D_MODEL), wg, wu, wd)


W_ALL = MLA_Q_RANK + MLA_KV_RANK + HEAD_BLOCK + MLSTM_WIDTH + HEAD_BLOCK
OFF_CKV = MLA_Q_RANK
OFF_KR = OFF_CKV + MLA_KV_RANK
OFF_MK = OFF_KR + HEAD_BLOCK
OFF_GATES = OFF_MK + MLSTM_WIDTH
W_T_ROWS = N_GATES + 3 * MLSTM_WIDTH


def _seg_mean(sq, ind):
    hi = sq.astype(BF16)
    lo = (sq - hi.astype(F32)).astype(BF16)
    w = ind.shape[0]
    outs = []
    for j in range(sq.shape[1] // w):
        sl = slice(w * j, w * (j + 1))
        outs.append(_dotf(hi[:, sl], ind) + _dotf(lo[:, sl], ind))
    return outs[0] if len(outs) == 1 else jnp.concatenate(outs, axis=1)


def _rope_block(xb, cos, sin, first_half):
    partner = jnp.where(first_half, pltpu.roll(xb, HEAD_BLOCK - 8, axis=1), pltpu.roll(xb, 8, axis=1))
    return xb * cos + partner * sin


def _even_prep_kernel(x_ref, mod_ref, g_ref, cos_ref, sin_ref, cost_ref, sint_ref, wall_ref, wgt_ref, gbr_ref, gbc_ref,
                      cqg_ref, wuqt_ref, ckvg_ref, wuk_ref, wuvt_ref, qgn_ref, qgr_ref, kg_ref, krg_ref, ind_ref,
                      qt_ref, k_ref, vt_ref, mqt_ref, mk_ref, mvt_ref, mot_ref, gr_ref, gc_ref):
    tm = x_ref.shape[0]
    x = x_ref[...]
    h = _norm_mod(x, g_ref[...], mod_ref[0:1, :], mod_ref[1:2, :]).astype(BF16)
    p = _dotf(h, wall_ref[...])
    pt = lax.dot_general(wgt_ref[...], h, (((1,), (1,)), ((), ())), preferred_element_type=F32)

    mk_ref[...] = (p[:, OFF_MK:OFF_MK + MLSTM_WIDTH] * (MLSTM_DH ** -0.5)).astype(BF16)
    mqt_ref[...] = pt[N_GATES:N_GATES + MLSTM_WIDTH, :].astype(BF16)
    mvt_ref[...] = pt[N_GATES + MLSTM_WIDTH:N_GATES + 2 * MLSTM_WIDTH, :].astype(BF16)
    mot_ref[...] = pt[N_GATES + 2 * MLSTM_WIDTH:, :]
    graw = p[:, OFF_GATES:OFF_GATES + HEAD_BLOCK][:, :N_GATES] + gbr_ref[...]
    lane = lax.broadcasted_iota(jnp.int32, (tm, N_GATES), 1)
    gr_ref[...] = jnp.where((lane // MLSTM_HEADS) % 2 == 1, _log_sigmoid(graw), graw)
    gt = pt[:N_GATES, :] + gbc_ref[...]
    sub = lax.broadcasted_iota(jnp.int32, (N_GATES, tm), 0)
    gc_ref[...] = jnp.where((sub // MLSTM_HEADS) % 2 == 1, _log_sigmoid(gt), gt)

    ind = ind_ref[...]
    cos = cos_ref[...]
    sin = sin_ref[...]
    lane_b = lax.broadcasted_iota(jnp.int32, (tm, HEAD_BLOCK), 1)
    first_half = (lane_b % ROPE_AXIS_DIM) < (ROPE_AXIS_DIM // 2)

    cqn = _rms(p[:, 0:MLA_Q_RANK], cqg_ref[...]).astype(BF16)
    ckvn = _rms(p[:, OFF_CKV:OFF_CKV + MLA_KV_RANK], ckvg_ref[...]).astype(BF16)
    nt_dims = (((1,), (1,)), ((), ()))
    k_raw = _dotf(ckvn, wuk_ref[...])
    vt = lax.dot_general(wuvt_ref[...], ckvn, nt_dims, preferred_element_type=F32).astype(BF16)
    ones_rows = jnp.where(lax.broadcasted_iota(jnp.int32, (V_ROWS - MLA_V, tm), 0) == 0, 1.0, 0.0).astype(BF16)
    for hh in range(MLA_HEADS):
        vt_ref[V_ROWS * hh:V_ROWS * hh + MLA_V, :] = vt[MLA_V * hh:MLA_V * (hh + 1), :]
        vt_ref[V_ROWS * hh + MLA_V:V_ROWS * (hh + 1), :] = ones_rows

    kn = k_raw * lax.rsqrt(_seg_mean(k_raw * k_raw, ind) + EPS) * kg_ref[...]
    kr = p[:, OFF_KR:OFF_KR + HEAD_BLOCK]
    krn = kr * lax.rsqrt(_seg_mean(kr * kr, ind[:HEAD_BLOCK, :HEAD_BLOCK]) + EPS) * krg_ref[...]
    kr_rot = _rope_block(krn, cos, sin, first_half)
    for hh in range(MLA_HEADS):
        sl = slice(HEAD_BLOCK * hh, HEAD_BLOCK * (hh + 1))
        k_ref[:, sl] = (kn[:, sl] + kr_rot).astype(BF16)

    qt_raw = lax.dot_general(wuqt_ref[...], cqn, nt_dims, preferred_element_type=F32)
    q_scale = (MLA_QK ** -0.5) * LOG2_E
    half = ROPE_AXIS_DIM // 2
    cost = cost_ref[...]
    sint = sint_ref[...]
    for hh in range(MLA_HEADS):
        r0 = HEAD_BLOCK * hh
        nope = qt_raw[r0:r0 + MLA_NOPE, :]
        rope = qt_raw[r0 + MLA_NOPE:r0 + MLA_QK, :]
        nope = nope * lax.rsqrt(jnp.mean(nope * nope, axis=0, keepdims=True) + EPS) * qgn_ref[...]
        rope = rope * lax.rsqrt(jnp.mean(rope * rope, axis=0, keepdims=True) + EPS) * qgr_ref[...]
        rot = []
        for ax in range(2):
            x1 = rope[ROPE_AXIS_DIM * ax:ROPE_AXIS_DIM * ax + half, :]
            x2 = rope[ROPE_AXIS_DIM * ax + half:ROPE_AXIS_DIM * (ax + 1), :]
            cs = cost[half * ax:half * (ax + 1), :]
            sn = sint[half * ax:half * (ax + 1), :]
            rot += [x1 * cs - x2 * sn, x2 * cs + x1 * sn]
        qt_ref[r0:r0 + MLA_NOPE, :] = (nope * q_scale).astype(BF16)
        qt_ref[r0 + MLA_NOPE:r0 + MLA_QK, :] = (jnp.concatenate(rot, axis=0) * q_scale).astype(BF16)
        qt_ref[r0 + MLA_QK:r0 + HEAD_BLOCK, :] = jnp.zeros((HEAD_BLOCK - MLA_QK, tm), BF16)


def _even_prep(x2d, mods, row_fn, g, tabs, tab_fn, w, tm):
    n = x2d.shape[0]
    cos, sin, cost, sint = tabs
    tok = lambda width: pl.BlockSpec((tm, width), lambda i: (i, 0))
    tok_t = lambda rows: pl.BlockSpec((rows, tm), lambda i: (0, i))
    out_shape = [
        jax.ShapeDtypeStruct((MLA_HEADS * HEAD_BLOCK, n), BF16),
        jax.ShapeDtypeStruct((n, MLA_HEADS * HEAD_BLOCK), BF16),
        jax.ShapeDtypeStruct((MLA_HEADS * V_ROWS, n), BF16),
        jax.ShapeDtypeStruct((MLSTM_WIDTH, n), BF16),
        jax.ShapeDtypeStruct((n, MLSTM_WIDTH), BF16),
        jax.ShapeDtypeStruct((MLSTM_WIDTH, n), BF16),
        jax.ShapeDtypeStruct((MLSTM_WIDTH, n), F32),
        jax.ShapeDtypeStruct((n, N_GATES), F32),
        jax.ShapeDtypeStruct((N_GATES, n), F32),
    ]
    out_specs = [tok_t(1024), tok(1024), tok_t(MLA_HEADS * V_ROWS), tok_t(512), tok(512), tok_t(512), tok_t(512),
                 tok(N_GATES), tok_t(N_GATES)]
    in_specs = [
        tok(D_MODEL),
        pl.BlockSpec((None, 3, D_MODEL), lambda i: (row_fn(i), 0, 0)),
        _const_spec((1, D_MODEL)),
        pl.BlockSpec((tm, HEAD_BLOCK), lambda i: (tab_fn(i), 0)),
        pl.BlockSpec((tm, HEAD_BLOCK), lambda i: (tab_fn(i), 0)),
        pl.BlockSpec((ROPE_AXIS_DIM, tm), lambda i: (0, tab_fn(i))),
        pl.BlockSpec((ROPE_AXIS_DIM, tm), lambda i: (0, tab_fn(i))),
        _const_spec((D_MODEL, W_ALL)),
        _const_spec((W_T_ROWS, D_MODEL)),
        _const_spec((1, N_GATES)),
        _const_spec((N_GATES, 1)),
        _const_spec((1, MLA_Q_RANK)),
        _const_spec((1024, MLA_Q_RANK)),
        _const_spec((1, MLA_KV_RANK)),
        _const_spec((MLA_KV_RANK, 1024)),
        _const_spec((512, MLA_KV_RANK)),
        _const_spec((MLA_NOPE, 1)),
        _const_spec((MLA_ROPE, 1)),
        _const_spec((1, 1024)),
        _const_spec((1, HEAD_BLOCK)),
        _const_spec((2 * HEAD_BLOCK, 2 * HEAD_BLOCK)),
    ]
    return pl.pallas_call(
        _even_prep_kernel,
        grid=(n // tm,),
        in_specs=in_specs,
        out_specs=out_specs,
        out_shape=out_shape,
        compiler_params=_cparams("parallel"),
        name="even_prep",
    )(x2d, mods, g.reshape(1, D_MODEL), cos, sin, cost, sint, w["w_all"], w["w_gt"], w["gb_row"], w["gb_col"],
      w["cq_g"], w["w_uqt"], w["ckv_g"], w["w_uk"], w["w_uvt"], w["qg_nope"], w["qg_rope"], w["k_g"], w["kr_g"],
      w["ind"])


def _attn_kernel(qt_ref, k_ref, vt_ref, o_ref, st_scr, acc_scr):
    tq = qt_ref.shape[1]
    nk, _, tk = vt_ref.shape
    unroll = max(u for u in (16, 8, 4, 2) if (nk - 1) % u == 0)
    heads = range(2)

    def scores(j, slot):
        off = pl.multiple_of(j * tk, tk)
        cms = []
        for hh in heads:
            sl = slice(HEAD_BLOCK * hh, HEAD_BLOCK * (hh + 1))
            st = _dotf(k_ref[pl.ds(off, tk), sl], qt_ref[sl, :])
            st_scr[slot, hh] = st
            cms.append(jnp.max(st, axis=0, keepdims=True))
        return tuple(cms)

    def absorb(j, slot, cms, ms):
        out = []
        for hh in heads:
            m_new = jnp.maximum(ms[hh], cms[hh])
            alpha = jnp.exp2(ms[hh] - m_new)
            pt = jnp.exp2(st_scr[slot, hh] - m_new)
            pv = _dotf(vt_ref[j, V_ROWS * hh:V_ROWS * (hh + 1), :], pt.astype(BF16))
            acc_scr[hh] = alpha * acc_scr[hh] + pv
            out.append(m_new)
        return tuple(out)

    def body(i, carry):
        cms, ms = carry
        for r in range(unroll):
            j = unroll * i + r
            cms_next = scores(j + 1, (r + 1) % 2)
            ms = absorb(j, r % 2, cms, ms)
            cms = cms_next
        return cms, ms

    acc_scr[...] = jnp.zeros(acc_scr.shape, F32)
    m0 = jnp.full((1, tq), NEG_BIG, F32)
    cms, ms = lax.fori_loop(0, (nk - 1) // unroll, body, (scores(0, 0), (m0, m0)))
    absorb(nk - 1, 0, cms, ms)
    for hh in heads:
        acc = acc_scr[hh]
        o_ref[MLA_V * hh:MLA_V * (hh + 1), :] = (acc[:MLA_V, :] / acc[MLA_V:MLA_V + 1, :]).astype(o_ref.dtype)


def _attention(qt, k2d, vt4, batch, tq):
    nq = qt.shape[1] // batch // tq
    t_k = k2d.shape[0] // batch
    _, nk, _, tk = vt4.shape
    return pl.pallas_call(
        _attn_kernel,
        grid=(batch, MLA_HEADS // 2, nq),
        in_specs=[
            pl.BlockSpec((2 * HEAD_BLOCK, tq), lambda b, hp, i: (hp, b * nq + i)),
            pl.BlockSpec((t_k, 2 * HEAD_BLOCK), lambda b, hp, i: (b, hp)),
            pl.BlockSpec((None, nk, 2 * V_ROWS, tk), lambda b, hp, i: (b, 0, hp, 0)),
        ],
        out_specs=pl.BlockSpec((2 * MLA_V, tq), lambda b, hp, i: (hp, b * nq + i)),
        out_shape=jax.ShapeDtypeStruct((MLA_HEADS * MLA_V, qt.shape[1]), BF16),
        scratch_shapes=[pltpu.VMEM((2, 2, tk, tq), F32), pltpu.VMEM((2, V_ROWS, tq), F32)],
        compiler_params=_cparams("parallel", "parallel", "arbitrary"),
        name="mla_attention",
    )(qt, k2d, vt4)


def _mlstm_kernel(qtf_ref, kf_ref, vtf_ref, grf_ref, gcf_ref, qtb_ref, kb_ref, vtb_ref, grb_ref, gcb_ref,
                  s0_ref, m0_ref, hf_ref, hb_ref, s_out_ref, m_out_ref, s_scr, m_scr):
    j = pl.program_id(1)
    nj = pl.num_programs(1)
    L = kf_ref.shape[0]

    @pl.when(j == 0)
    def _():
        s_scr[...] = s0_ref[...]
        m_scr[...] = m0_ref[...]

    row = lax.broadcasted_iota(jnp.int32, (L, L), 0)
    col = lax.broadcasted_iota(jnp.int32, (L, L), 1)
    sub = lax.broadcasted_iota(jnp.int32, (MLSTM_ST - MLSTM_DH, L), 0)

    dirs = ((qtf_ref, kf_ref, vtf_ref, grf_ref, gcf_ref, hf_ref), (qtb_ref, kb_ref, vtb_ref, grb_ref, gcb_ref, hb_ref))
    combos = [(d, hh) for d in range(2) for hh in range(MLSTM_HEADS)]


    masks, cum_rows, cum_cols, g_rows, g_cols = [], [], [], [], []
    for d in range(2):
        s_le_t = (row <= col) if d == 0 else (row >= col)
        tri = jnp.where((col <= row) if d == 0 else (col >= row), 1.0, 0.0).astype(BF16)
        tri_t = jnp.where(s_le_t, 1.0, 0.0).astype(BF16)
        gr = dirs[d][3][...]
        gc = dirs[d][4][...]
        r1, r2, r3 = _split3(gr)
        c1, c2, c3 = _split3(gc)
        masks.append(s_le_t)
        g_rows.append(gr)
        g_cols.append(gc)
        cum_rows.append(_dotf(tri, r1) + _dotf(tri, r2) + _dotf(tri, r3))
        cum_cols.append(_dotf(c1, tri_t) + _dotf(c2, tri_t) + _dotf(c3, tri_t))
    kq, inter, s_prev, m_prev = [], [], [], []
    for d, hh in combos:
        idx = d * MLSTM_HEADS + hh
        sl = slice(MLSTM_DH * hh, MLSTM_DH * (hh + 1))
        qth = dirs[d][0][sl, :]
        s_prev.append(s_scr[idx])
        m_prev.append(m_scr[idx:idx + 1, 0:1])
        kq.append(_dotf(dirs[d][1][:, sl], qth))
        inter.append(_dotf(s_prev[-1].astype(BF16), qth))

    sw, den, w_inter, floor, vw, decay, m_new = [], [], [], [], [], [], []
    for c, (d, hh) in enumerate(combos):
        il = (0 if d == 0 else 2 * MLSTM_HEADS) + hh
        fl = il + MLSTM_HEADS
        sl = slice(MLSTM_DH * hh, MLSTM_DH * (hh + 1))
        c_col = g_rows[d][:, il:il + 1] - cum_rows[d][:, fl:fl + 1]
        b_row = cum_cols[d][fl:fl + 1, :]
        c_row = g_cols[d][il:il + 1, :] - b_row
        c_mat = jnp.where(masks[d], c_col, NEG_BIG)
        u = jnp.maximum(jnp.max(c_mat, axis=0, keepdims=True), m_prev[c])
        s = kq[c] * jnp.exp(c_mat - u)
        sw.append(s.astype(BF16))
        den.append(jnp.sum(s, axis=0, keepdims=True))
        w_inter.append(jnp.exp(m_prev[c] - u))
        floor.append(jnp.exp(-(b_row + u)))
        b_end = b_row[:, L - 1:L] if d == 0 else b_row[:, 0:1]
        g_row = b_end + c_row
        m_new.append(jnp.maximum(b_end + m_prev[c], jnp.max(g_row, axis=-1, keepdims=True)))
        decay.append(jnp.exp(b_end + m_prev[c] - m_new[c]))
        w_s = jnp.exp(g_row - m_new[c])
        n_rows = jnp.where(sub == 0, w_s, 0.0)
        vw.append(jnp.concatenate([dirs[d][2][sl, :].astype(F32) * w_s, n_rows], axis=0).astype(BF16))

    for c, (d, hh) in enumerate(combos):
        idx = d * MLSTM_HEADS + hh
        sl = slice(MLSTM_DH * hh, MLSTM_DH * (hh + 1))
        num = _dotf(dirs[d][2][sl, :], sw[c]) + w_inter[c] * inter[c][:MLSTM_DH, :]
        dn = den[c] + w_inter[c] * inter[c][MLSTM_DH:MLSTM_DH + 1, :]
        dirs[d][5][sl, :] = num / jnp.maximum(jnp.abs(dn), floor[c])
        s_scr[idx] = decay[c] * s_prev[c] + _dotf(vw[c], dirs[d][1][:, sl])
        m_scr[idx:idx + 1, :] = jnp.broadcast_to(m_new[c], (1, MLSTM_DH))

    @pl.when(j == nj - 1)
    def _():
        s_out_ref[...] = s_scr[...]
        m_out_ref[...] = m_scr[...]


def _mlstm(mqt, mk, mvt, g_rows, g_cols, s0, m0, batch, chunk):
    n = mk.shape[0]
    nc = n // batch // chunk
    fwd = lambda b, j: (b * nc + j, 0)
    bwd = lambda b, j: (b * nc + (nc - 1 - j), 0)
    fwd_t = lambda b, j: (0, b * nc + j)
    bwd_t = lambda b, j: (0, b * nc + (nc - 1 - j))
    nst = 2 * MLSTM_HEADS
    tok = lambda fn: pl.BlockSpec((chunk, MLSTM_WIDTH), fn)
    tok_t = lambda fn: pl.BlockSpec((MLSTM_WIDTH, chunk), fn)
    state_s = pl.BlockSpec((None, nst, MLSTM_ST, MLSTM_DH), lambda b, j: (b, 0, 0, 0))
    state_m = pl.BlockSpec((None, nst, MLSTM_DH), lambda b, j: (b, 0, 0))
    return pl.pallas_call(
        _mlstm_kernel,
        grid=(batch, nc),
        in_specs=[tok_t(fwd_t), tok(fwd), tok_t(fwd_t), pl.BlockSpec((chunk, N_GATES), fwd),
                  pl.BlockSpec((N_GATES, chunk), fwd_t),
                  tok_t(bwd_t), tok(bwd), tok_t(bwd_t), pl.BlockSpec((chunk, N_GATES), bwd),
                  pl.BlockSpec((N_GATES, chunk), bwd_t),
                  state_s, state_m],
        out_specs=[tok_t(fwd_t), tok_t(bwd_t), state_s, state_m],
        out_shape=[jax.ShapeDtypeStruct((MLSTM_WIDTH, n), F32), jax.ShapeDtypeStruct((MLSTM_WIDTH, n), F32),
                   jax.ShapeDtypeStruct(s0.shape, F32), jax.ShapeDtypeStruct(m0.shape, F32)],
        scratch_shapes=[pltpu.VMEM((nst, MLSTM_ST, MLSTM_DH), F32), pltpu.VMEM((nst, MLSTM_DH), F32)],
        compiler_params=_cparams("parallel", "arbitrary"),
        name="mlstm",
    )(mqt, mk, mvt, g_rows, g_cols, mqt, mk, mvt, g_rows, g_cols, s0, m0)


def _even_out_kernel(x_ref, gate_ref, att_ref, hf_ref, hb_ref, mo_ref, og_ref, w_ref, o_ref):
    hs = hf_ref[...] + hb_ref[...]
    og = og_ref[...]
    parts = [att_ref[...]]
    for hh in range(MLSTM_HEADS):
        sl = slice(MLSTM_DH * hh, MLSTM_DH * (hh + 1))
        blk = hs[sl, :]
        hn = blk * lax.rsqrt(jnp.mean(blk * blk, axis=0, keepdims=True) + EPS) * og[sl, :]
        parts.append((hn * jax.nn.sigmoid(mo_ref[sl, :])).astype(BF16))
    mixed = jnp.concatenate(parts, axis=0)
    y = lax.dot_general(mixed, w_ref[...], (((0,), (0,)), ((), ())), preferred_element_type=F32)
    o_ref[...] = x_ref[...] + gate_ref[...] * y


def _even_out(x2d, gates, row_fn, att, hf, hb, mo, out_g, w_out, tm):
    n = x2d.shape[0]
    tok = pl.BlockSpec((tm, D_MODEL), lambda i: (i, 0))
    tok_t = pl.BlockSpec((MLSTM_WIDTH, tm), lambda i: (0, i))
    return pl.pallas_call(
        _even_out_kernel,
        grid=(n // tm,),
        in_specs=[tok, pl.BlockSpec((None, 1, D_MODEL), lambda i: (row_fn(i), 0, 0)),
                  tok_t, tok_t, tok_t, tok_t, _const_spec((MLSTM_WIDTH, 1)), _const_spec((D_MODEL, D_MODEL))],
        out_specs=tok,
        out_shape=jax.ShapeDtypeStruct((n, D_MODEL), F32),
        compiler_params=_cparams("parallel"),
        name="even_out",
    )(x2d, gates, att, hf, hb, mo, out_g.reshape(MLSTM_WIDTH, 1), w_out)


def _gelu_tanh(x):
    return 0.5 * x * (1.0 + jnp.tanh(0.7978845608028654 * (x + 0.044715 * (x * x * x))))


def _sigmoid(x):
    return 0.5 * jnp.tanh(0.5 * x) + 0.5


def _odd_in_kernel(x_ref, xp_ref, xn_ref, mod_ref, g_ref, w_ref, cw_ref, cb_ref, gate_ref, xc_ref, ext_scr, *, nt):
    tm = x_ref.shape[0]
    ti = pl.program_id(0) % nt
    g, shift, scale = g_ref[...], mod_ref[0:1, :], mod_ref[1:2, :]
    p = _dotf(_norm_mod(x_ref[...], g, shift, scale).astype(BF16), w_ref[...])
    gate_ref[...] = _gelu_tanh(p[:, :RNN_WIDTH])
    halo = jnp.concatenate([xp_ref[...], xn_ref[...]], axis=0)
    ph = _dotf(_norm_mod(halo, g, shift, scale).astype(BF16), w_ref[:, RNN_WIDTH:])
    ext_scr[0:8, :] = jnp.where(ti > 0, ph[0:8, :], 0.0)
    ext_scr[8:8 + tm, :] = p[:, RNN_WIDTH:]
    ext_scr[8 + tm:16 + tm, :] = jnp.where(ti < nt - 1, ph[8:16, :], 0.0)
    cw = cw_ref[...]
    xc = cb_ref[...] + cw[0:1, :] * ext_scr[6:6 + tm, :]
    for k in range(1, CONV_W):
        xc = xc + cw[k:k + 1, :] * ext_scr[6 + k:6 + k + tm, :]
    xc_ref[...] = xc


def _odd_in(x2d, mods, row_fn, g, w_in, conv_w, conv_b, batch, tm):
    n = x2d.shape[0]
    nt = n // batch // tm
    t8 = tm // 8
    tok = pl.BlockSpec((tm, D_MODEL), lambda i: (i, 0))
    prev = pl.BlockSpec((8, D_MODEL), lambda i: (jnp.maximum(i * t8 - 1, 0), 0))
    nxt = pl.BlockSpec((8, D_MODEL), lambda i: (jnp.minimum((i + 1) * t8, n // 8 - 1), 0))
    return pl.pallas_call(
        functools.partial(_odd_in_kernel, nt=nt),
        grid=(n // tm,),
        in_specs=[tok, prev, nxt, pl.BlockSpec((None, 3, D_MODEL), lambda i: (row_fn(i), 0, 0)),
                  _const_spec((1, D_MODEL)), _const_spec((D_MODEL, 2 * RNN_WIDTH)),
                  _const_spec((CONV_W, RNN_WIDTH)), _const_spec((1, RNN_WIDTH))],
        out_specs=[tok, tok],
        out_shape=[jax.ShapeDtypeStruct((n, RNN_WIDTH), F32), jax.ShapeDtypeStruct((n, RNN_WIDTH), F32)],
        scratch_shapes=[pltpu.VMEM((tm + 16, RNN_WIDTH), F32)],
        compiler_params=_cparams("parallel"),
        name="odd_in",
    )(x2d, x2d, x2d, mods, g.reshape(1, D_MODEL), w_in, conv_w, conv_b)


def _rglru_kernel(xc_ref, wax_ref, ba_ref, bx_ref, lam_ref, h0_ref, h_ref, ht_ref, a_scr, b_scr, carry_scr, *, nt):
    d = pl.program_id(0)
    i = pl.program_id(2)
    tm = xc_ref.shape[0]
    nb8 = tm // 8

    xc = xc_ref[...]
    neg_lam = -lam_ref[...]
    decay_rate = -LRU_C * (jnp.maximum(neg_lam, 0.0) + jnp.log1p(jnp.exp(-jnp.abs(neg_lam))))
    ba = ba_ref[...]
    bx = bx_ref[...]
    for nb in range(RNN_BLOCKS):
        sl = slice(RNN_BLOCK_DIM * nb, RNN_BLOCK_DIM * (nb + 1))
        u = xc[:, sl]
        rg = _dotf(u.astype(BF16), wax_ref[nb])
        r = _sigmoid(rg[:, :RNN_BLOCK_DIM] + ba[:, sl])
        ig = _sigmoid(rg[:, RNN_BLOCK_DIM:] + bx[:, sl])
        log_a = r * decay_rate[:, sl]
        a = jnp.exp(log_a)
        a_scr[:, sl] = a
        b_scr[:, sl] = jnp.sqrt(-jnp.tanh(log_a) * (a * a + 1.0)) * (ig * u)

    @pl.when(i == 0)
    def _():
        carry_scr[...] = h0_ref[...]

    def scan_tile(reverse):
        a = a_scr[...].reshape(nb8, 8, RNN_WIDTH)
        b = b_scr[...].reshape(nb8, 8, RNN_WIDTH)
        r8 = lax.broadcasted_iota(jnp.int32, (nb8, 8, RNN_WIDTH), 1)
        for s in (1, 2, 4):
            keep = (r8 < 8 - s) if reverse else (r8 >= s)
            shift = 8 - s if reverse else s
            a_in = jnp.where(keep, pltpu.roll(a, shift, axis=1), 1.0)
            b_in = jnp.where(keep, pltpu.roll(b, shift, axis=1), 0.0)
            b = a * b_in + b
            a = a * a_in
        a_scr[...] = a.reshape(tm, RNN_WIDTH)
        b_scr[...] = b.reshape(tm, RNN_WIDTH)

        def block(jb, h):
            base = pl.multiple_of((nb8 - 1 - jb if reverse else jb) * 8, 8)
            hb = a_scr[pl.ds(base, 8), :] * h + b_scr[pl.ds(base, 8), :]
            h_ref[pl.ds(base, 8), :] = hb
            return hb[0:1, :] if reverse else hb[7:8, :]

        carry_scr[...] = lax.fori_loop(0, nb8, block, carry_scr[...], unroll=4)

    @pl.when(d == 0)
    def _():
        scan_tile(False)

    @pl.when(d == 1)
    def _():
        scan_tile(True)

    @pl.when(i == nt - 1)
    def _():
        ht_ref[...] = carry_scr[...]


def _rglru(xc2d, h0, w, batch, tm):
    n = xc2d.shape[0]
    nt = n // batch // tm
    tile = lambda d, i: jnp.where(d == 0, i, nt - 1 - i)
    per_dir = lambda shape: pl.BlockSpec((None,) + shape, lambda d, b, i: (d,) + (0,) * len(shape))
    state = pl.BlockSpec((None, None, 1, RNN_WIDTH), lambda d, b, i: (d, b, 0, 0))
    return pl.pallas_call(
        functools.partial(_rglru_kernel, nt=nt),
        grid=(2, batch, nt),
        in_specs=[pl.BlockSpec((tm, RNN_WIDTH), lambda d, b, i: (b * nt + tile(d, i), 0)),
                  per_dir((RNN_BLOCKS, RNN_BLOCK_DIM, 2 * RNN_BLOCK_DIM)), per_dir((1, RNN_WIDTH)), per_dir((1, RNN_WIDTH)),
                  per_dir((1, RNN_WIDTH)), state],
        out_specs=[pl.BlockSpec((None, tm, RNN_WIDTH), lambda d, b, i: (d, b * nt + tile(d, i), 0)), state],
        out_shape=[jax.ShapeDtypeStruct((2, n, RNN_WIDTH), F32), jax.ShapeDtypeStruct((2, batch, 1, RNN_WIDTH), F32)],
        scratch_shapes=[pltpu.VMEM((tm, RNN_WIDTH), F32), pltpu.VMEM((tm, RNN_WIDTH), F32),
                        pltpu.VMEM((1, RNN_WIDTH), F32)],
        compiler_params=_cparams("parallel", "parallel", "arbitrary"),
        name="rglru",
    )(xc2d, w["w_ax"], w["b_a"], w["b_x"], w["lam"], h0)


def _odd_out_kernel(x_ref, gate_ref, h_ref, gg_ref, w_ref, o_ref):
    y = ((h_ref[0] + h_ref[1]).astype(F32) * gg_ref[...]).astype(BF16)
    o_ref[...] = x_ref[...] + gate_ref[...] * _dotf(y, w_ref[...])


def _odd_out(x2d, gates, row_fn, h2, gelu_gate, w_out, tm):
    n = x2d.shape[0]
    tok = pl.BlockSpec((tm, D_MODEL), lambda i: (i, 0))
    return pl.pallas_call(
        _odd_out_kernel,
        grid=(n // tm,),
        in_specs=[tok, pl.BlockSpec((None, 1, D_MODEL), lambda i: (row_fn(i), 0, 0)),
                  pl.BlockSpec((2, tm, RNN_WIDTH), lambda i: (0, i, 0)), tok, _const_spec((RNN_WIDTH, D_MODEL))],
        out_specs=tok,
        out_shape=jax.ShapeDtypeStruct((n, D_MODEL), F32),
        compiler_params=_cparams("parallel"),
        name="odd_out",
    )(x2d, gates, h2, gelu_gate, w_out)


def _rope_tables(seq):
    t = jnp.arange(seq)
    inv = ROPE_BASE ** (-jnp.arange(0, ROPE_AXIS_DIM, 2, dtype=F32) / ROPE_AXIS_DIM)
    ang_r = (t // GRID_W).astype(F32)[:, None] * inv
    ang_c = (t % GRID_W).astype(F32)[:, None] * inv
    cos32 = jnp.concatenate([jnp.cos(ang_r), jnp.cos(ang_r), jnp.cos(ang_c), jnp.cos(ang_c)], axis=1)
    sin32 = jnp.concatenate([-jnp.sin(ang_r), jnp.sin(ang_r), -jnp.sin(ang_c), jnp.sin(ang_c)], axis=1)
    pad = HEAD_BLOCK - MLA_QK
    cos = jnp.concatenate([jnp.ones((seq, MLA_NOPE), F32), cos32, jnp.ones((seq, pad), F32)], axis=1)
    sin = jnp.concatenate([jnp.zeros((seq, MLA_NOPE), F32), sin32, jnp.zeros((seq, pad), F32)], axis=1)
    cost = jnp.concatenate([jnp.cos(ang_r), jnp.cos(ang_c)], axis=1).T
    sint = jnp.concatenate([jnp.sin(ang_r), jnp.sin(ang_c)], axis=1).T
    return cos, sin, cost, sint


def _even_weights(w_in, cq_g, w_uq, ckv_g, w_ukv, q_g, k_g, gate_b):
    pad = HEAD_BLOCK - MLA_QK
    kr_cols = w_in[:, OFF_KR:OFF_KR + MLA_ROPE]
    w_kr = jnp.pad(kr_cols, ((0, 0), (MLA_NOPE, pad)))
    m_cols = w_in[:, OFF_KR + MLA_ROPE:OFF_KR + MLA_ROPE + 4 * MLSTM_WIDTH]
    g_cols = w_in[:, OFF_KR + MLA_ROPE + 4 * MLSTM_WIDTH:]
    mq_cols, mk_cols, mv_cols, mo_cols = jnp.split(m_cols, 4, axis=1)
    w_all = jnp.concatenate([w_in[:, :OFF_KR], w_kr, mk_cols, jnp.pad(g_cols, ((0, 0), (0, HEAD_BLOCK - N_GATES)))], axis=1)
    uq = jnp.pad(w_uq.reshape(MLA_Q_RANK, MLA_HEADS, MLA_QK), ((0, 0), (0, 0), (0, pad)))
    ukv = w_ukv.reshape(MLA_KV_RANK, MLA_HEADS, MLA_NOPE + MLA_V)
    uk = jnp.pad(ukv[:, :, :MLA_NOPE], ((0, 0), (0, 0), (0, HEAD_BLOCK - MLA_NOPE)))
    uv = ukv[:, :, MLA_NOPE:]
    kg_blk = jnp.pad(k_g[:MLA_NOPE], (0, HEAD_BLOCK - MLA_NOPE))
    krg_blk = jnp.pad(k_g[MLA_NOPE:], (MLA_NOPE, pad))
    seg = jnp.arange(HEAD_BLOCK)
    seg_id = jnp.where(seg < MLA_NOPE, 0, jnp.where(seg < MLA_QK, 1, 2))
    seg_len = jnp.where(seg < MLA_NOPE, float(MLA_NOPE), float(MLA_ROPE))
    ind128 = jnp.where((seg_id[:, None] == seg_id[None, :]) & (seg_id[:, None] < 2), 1.0 / seg_len[None, :], 0.0)
    zero = jnp.zeros_like(ind128)
    ind = jnp.block([[ind128, zero], [zero, ind128]])
    return {
        "w_all": w_all.astype(BF16),
        "w_gt": jnp.concatenate([g_cols, mq_cols, mv_cols, mo_cols], axis=1).T.astype(BF16),
        "gb_row": gate_b.reshape(1, N_GATES),
        "gb_col": gate_b.reshape(N_GATES, 1),
        "cq_g": cq_g.reshape(1, MLA_Q_RANK),
        "w_uqt": uq.reshape(MLA_Q_RANK, MLA_HEADS * HEAD_BLOCK).T.astype(BF16),
        "ckv_g": ckv_g.reshape(1, MLA_KV_RANK),
        "w_uk": uk.reshape(MLA_KV_RANK, MLA_HEADS * HEAD_BLOCK).astype(BF16),
        "w_uvt": uv.reshape(MLA_KV_RANK, MLA_HEADS * MLA_V).T.astype(BF16),
        "qg_nope": q_g[:MLA_NOPE].reshape(MLA_NOPE, 1),
        "qg_rope": q_g[MLA_NOPE:].reshape(MLA_ROPE, 1),
        "k_g": jnp.tile(kg_blk, MLA_HEADS).reshape(1, -1),
        "kr_g": krg_blk.reshape(1, HEAD_BLOCK),
        "ind": ind.astype(BF16),
    }


def kernel(x, c, ctx, c_ctx, mod_w, mod_b, norm_g, ffn_w_gate, ffn_w_up, ffn_w_down, even_w_in, even_w_out, mla_cq_g, mla_w_uq, mla_ckv_g, mla_w_ukv, mla_q_g, mla_k_g, mlstm_gate_b, mlstm_out_g, odd_w_in, odd_conv_w, odd_conv_b, lru_w_a, lru_b_a, lru_w_x, lru_b_x, lru_lam, odd_w_out):
    batch, seq, _ = x.shape
    ctx_len = ctx.shape[1]
    depth = mod_w.shape[0]
    assert depth == 2 and batch <= 7

    cond8 = jnp.zeros((8, D_MODEL), F32).at[:batch].set(c).at[batch].set(c_ctx)
    mods = _ada_params(cond8, mod_w, mod_b)
    mods = mods.reshape(depth, 8, 3, 3, D_MODEL).transpose(0, 2, 1, 3, 4)

    tm_x, tm_c = 512, 256
    x_row = lambda i: i // (seq // tm_x)
    c_row = lambda i: batch
    x2 = x.reshape(batch * seq, D_MODEL)
    c2 = ctx.reshape(batch * ctx_len, D_MODEL)

    wg = ffn_w_gate.astype(BF16)
    wu = ffn_w_up.astype(BF16)
    wd = ffn_w_down.astype(BF16)

    def ffn(t2, layer, which, row_fn, tm):
        return _ffn(t2, mods[layer, 2 * which], row_fn, norm_g[layer, 2 * which], wg[layer, which], wu[layer, which],
                    wd[layer, which], tm)

    x2 = ffn(x2, 0, 0, x_row, tm_x)
    c2 = ffn(c2, 0, 0, c_row, tm_c)

    ew = _even_weights(even_w_in[0], mla_cq_g[0], mla_w_uq[0], mla_ckv_g[0], mla_w_ukv[0], mla_q_g[0], mla_k_g[0],
                       mlstm_gate_b[0])
    tabs_x = _rope_tables(seq)
    tabs_c = (jnp.ones((tm_c, HEAD_BLOCK), F32), jnp.zeros((tm_c, HEAD_BLOCK), F32),
              jnp.ones((ROPE_AXIS_DIM, tm_c), F32), jnp.zeros((ROPE_AXIS_DIM, tm_c), F32))
    px = _even_prep(x2, mods[0, 1], x_row, norm_g[0, 1], tabs_x, lambda i: i % (seq // tm_x), ew, tm_x)
    pc = _even_prep(c2, mods[0, 1], c_row, norm_g[0, 1], tabs_c, lambda i: 0, ew, tm_c)
    qt_x, k_x, vt_x, mqt_x, mk_x, mvt_x, mot_x, gr_x, gc_x = px
    qt_c, k_c, vt_c, mqt_c, mk_c, mvt_c, mot_c, gr_c, gc_c = pc

    k_all = jnp.concatenate([k_c.reshape(batch, ctx_len, -1), k_x.reshape(batch, seq, -1)], axis=1)
    k_all = k_all.reshape(batch * (ctx_len + seq), -1)
    vt_all = jnp.concatenate([vt_c.reshape(-1, batch, ctx_len), vt_x.reshape(-1, batch, seq)], axis=2)

    def key_chunks(vt3, tk):
        return vt3.reshape(vt3.shape[0], batch, -1, tk).transpose(1, 2, 0, 3)

    att_x = _attention(qt_x, k_all, key_chunks(vt_all, 256), batch, 512)
    att_c = _attention(qt_c, k_c, key_chunks(vt_c.reshape(-1, batch, ctx_len), 256), batch, 256)

    chunk = 128
    nst = 2 * MLSTM_HEADS
    s0 = jnp.zeros((batch, nst, MLSTM_ST, MLSTM_DH), F32)
    m0 = jnp.zeros((batch, nst, MLSTM_DH), F32)
    hcf, hcb, s_c, m_c = _mlstm(mqt_c, mk_c, mvt_c, gr_c, gc_c, s0, m0, batch, chunk)
    hxf, hxb, _, _ = _mlstm(mqt_x, mk_x, mvt_x, gr_x, gc_x, s_c, m_c, batch, chunk)

    w_out0 = even_w_out[0].astype(BF16)
    gate0 = mods[0, 1][:, 2:3, :]
    x2 = _even_out(x2, gate0, x_row, att_x, hxf, hxb, mot_x, mlstm_out_g[0], w_out0, tm_x)
    c2 = _even_out(c2, gate0, c_row, att_c, hcf, hcb, mot_c, mlstm_out_g[0], w_out0, tm_c)
    x2 = ffn(x2, 0, 1, x_row, tm_x)
    c2 = ffn(c2, 0, 1, c_row, tm_c)

    x2 = ffn(x2, 1, 0, x_row, tm_x)
    c2 = ffn(c2, 1, 0, c_row, tm_c)
    w_in1 = odd_w_in[0].astype(BF16)
    conv_b = odd_conv_b[0].reshape(1, RNN_WIDTH)
    gg_x, xc_x = _odd_in(x2, mods[1, 1], x_row, norm_g[1, 1], w_in1, odd_conv_w[0], conv_b, batch, tm_x)
    _, xc_c = _odd_in(c2, mods[1, 1], c_row, norm_g[1, 1], w_in1, odd_conv_w[0], conv_b, batch, tm_c)
    rw = {
        "w_ax": jnp.concatenate([lru_w_a[0], lru_w_x[0]], axis=-1).astype(BF16),
        "b_a": lru_b_a[0].reshape(2, 1, RNN_WIDTH),
        "b_x": lru_b_x[0].reshape(2, 1, RNN_WIDTH),
        "lam": lru_lam[0].reshape(2, 1, RNN_WIDTH),
    }
    h0 = jnp.zeros((2, batch, 1, RNN_WIDTH), F32)
    _, st_c = _rglru(xc_c, h0, rw, batch, 256)
    h_x, _ = _rglru(xc_x, st_c, rw, batch, 256)
    x2 = _odd_out(x2, mods[1, 1][:, 2:3, :], x_row, h_x, gg_x, odd_w_out[0].astype(BF16), tm_x)
    x2 = ffn(x2, 1, 1, x_row, tm_x)
    return x2.reshape(batch, seq, D_MODEL)
```

```python
import functools

import jax
import jax.numpy as jnp
from jax import lax
from jax.experimental import pallas as pl
from jax.experimental.pallas import tpu as pltpu

F32 = jnp.float32
BF16 = jnp.bfloat16

D_MODEL = 1024
GRID_W = 64
EPS = 1e-6
N_MOD = 9
D_FF = 2816
FFN_RESIDUAL = 0.5

MLA_HEADS = 8
MLA_Q_RANK = 384
MLA_KV_RANK = 256
MLA_NOPE = 64
MLA_ROPE = 32
MLA_V = 64
MLA_QK = MLA_NOPE + MLA_ROPE
ROPE_AXIS_DIM = MLA_ROPE // 2
ROPE_BASE = 10000.0
HEAD_BLOCK = 128
V_ROWS = MLA_V + 16

MLSTM_HEADS = 4
MLSTM_DH = 128
MLSTM_WIDTH = MLSTM_HEADS * MLSTM_DH
N_GATES = 4 * MLSTM_HEADS
MLSTM_ST = MLSTM_DH + 16

RNN_WIDTH = 1024
RNN_BLOCKS = 8
RNN_BLOCK_DIM = RNN_WIDTH // RNN_BLOCKS
CONV_W = 4
LRU_C = 8.0

NEG_BIG = -1e30
LOG2_E = 1.4426950408889634
SCORE_BOUND = 64.0
VMEM_LIMIT = 56 * 1024 * 1024


def _cparams(*sem):
    return pltpu.CompilerParams(dimension_semantics=sem, vmem_limit_bytes=VMEM_LIMIT)


def _const_spec(shape):
    zeros = (0,) * len(shape)
    return pl.BlockSpec(shape, lambda *_: zeros, pipeline_mode=pl.Buffered(1))


def _dotf(a, b):
    return jnp.dot(a, b, preferred_element_type=F32)


def _split3(a):
    a1 = a.astype(BF16)
    r1 = a - a1.astype(F32)
    a2 = r1.astype(BF16)
    a3 = (r1 - a2.astype(F32)).astype(BF16)
    return a1, a2, a3


def _norm_mod(x, g, shift, scale):
    ms = jnp.mean(x * x, axis=-1, keepdims=True)
    return (x * lax.rsqrt(ms + EPS)) * (g * (1.0 + scale)) + shift


def _rms(x, g):
    ms = jnp.mean(x * x, axis=-1, keepdims=True)
    return x * lax.rsqrt(ms + EPS) * g


def _log_sigmoid(x):
    return jnp.minimum(x, 0.0) - jnp.log1p(jnp.exp(-jnp.abs(x)))


def _ada_kernel(cond_ref, w_ref, b_ref, o_ref):
    c = cond_ref[...]
    s = c * jax.nn.sigmoid(c)
    s1, s2, _ = _split3(s)
    w = w_ref[...]
    w1 = w.astype(BF16)
    w2 = (w - w1.astype(F32)).astype(BF16)
    o_ref[...] = _dotf(s1, w1) + _dotf(s1, w2) + _dotf(s2, w1) + b_ref[...]


def _ada_params(cond8, mod_w, mod_b):
    depth, _, n = mod_w.shape
    tn = 1152
    return pl.pallas_call(
        _ada_kernel,
        grid=(depth, n // tn),
        in_specs=[
            pl.BlockSpec((8, D_MODEL), lambda l, j: (0, 0)),
            pl.BlockSpec((None, D_MODEL, tn), lambda l, j: (l, 0, j)),
            pl.BlockSpec((None, 1, tn), lambda l, j: (l, 0, j)),
        ],
        out_specs=pl.BlockSpec((None, 8, tn), lambda l, j: (l, 0, j)),
        out_shape=jax.ShapeDtypeStruct((depth, 8, n), F32),
        compiler_params=_cparams("parallel", "parallel"),
        name="ada_params",
    )(cond8, mod_w, mod_b.reshape(depth, 1, n))


def _ffn_kernel(x_ref, mod_ref, g_ref, wg_ref, wu_ref, wd_ref, o_ref):
    x = x_ref[...]
    h = _norm_mod(x, g_ref[...], mod_ref[0:1, :], mod_ref[1:2, :]).astype(BF16)
    g = _dotf(h, wg_ref[...])
    u = _dotf(h, wu_ref[...])
    a = (g * jax.nn.sigmoid(g) * u).astype(BF16)
    y = _dotf(a, wd_ref[...])
    o_ref[...] = x + (FFN_RESIDUAL * mod_ref[2:3, :]) * y


def _ffn(x2d, mods, row_fn, g, wg, wu, wd, tm):
    n = x2d.shape[0]
    return pl.pallas_call(
        _ffn_kernel,
        grid=(n // tm,),
        in_specs=[
            pl.BlockSpec((tm, D_MODEL), lambda i: (i, 0)),
            pl.BlockSpec((None, 3, D_MODEL), lambda i: (row_fn(i), 0, 0)),
            _const_spec((1, D_MODEL)),
            _const_spec((D_MODEL, D_FF)),
            _const_spec((D_MODEL, D_FF)),
            _const_spec((D_FF, D_MODEL)),
        ],
        out_specs=pl.BlockSpec((tm, D_MODEL), lambda i: (i, 0)),
        out_shape=jax.ShapeDtypeStruct((n, D_MODEL), F32),
        compiler_params=_cparams("parallel"),
        name="ffn",
    )(x2d, mods, g.reshape(1, D_MODEL), wg, wu, wd)


W_ALL = MLA_Q_RANK + MLA_KV_RANK + HEAD_BLOCK + MLSTM_WIDTH + HEAD_BLOCK
OFF_CKV = MLA_Q_RANK
OFF_KR = OFF_CKV + MLA_KV_RANK
OFF_MK = OFF_KR + HEAD_BLOCK
OFF_GATES = OFF_MK + MLSTM_WIDTH
W_T_ROWS = N_GATES + 3 * MLSTM_WIDTH


def _seg_mean(sq, ind):
    hi = sq.astype(BF16)
    lo = (sq - hi.astype(F32)).astype(BF16)
    w = ind.shape[0]
    outs = []
    for j in range(sq.shape[1] // w):
        sl = slice(w * j, w * (j + 1))
        outs.append(_dotf(hi[:, sl], ind) + _dotf(lo[:, sl], ind))
    return outs[0] if len(outs) == 1 else jnp.concatenate(outs, axis=1)


def _rope_block(xb, cos, sin, first_half):
    partner = jnp.where(first_half, pltpu.roll(xb, HEAD_BLOCK - 8, axis=1), pltpu.roll(xb, 8, axis=1))
    return xb * cos + partner * sin


def _even_prep_kernel(x_ref, mod_ref, g_ref, cos_ref, sin_ref, cost_ref, sint_ref, wall_ref, wgt_ref, gbr_ref, gbc_ref,
                      cqg_ref, wuqt_ref, ckvg_ref, wuk_ref, wuvt_ref, qgn_ref, qgr_ref, kg_ref, krg_ref, ind_ref,
                      qt_ref, k_ref, vt_ref, mqt_ref, mk_ref, mvt_ref, mot_ref, gr_ref, gc_ref):
    tm = x_ref.shape[0]
    x = x_ref[...]
    h = _norm_mod(x, g_ref[...], mod_ref[0:1, :], mod_ref[1:2, :]).astype(BF16)
    p = _dotf(h, wall_ref[...])
    pt = lax.dot_general(wgt_ref[...], h, (((1,), (1,)), ((), ())), preferred_element_type=F32)

    mk_ref[...] = (p[:, OFF_MK:OFF_MK + MLSTM_WIDTH] * (MLSTM_DH ** -0.5)).astype(BF16)
    mqt_ref[...] = pt[N_GATES:N_GATES + MLSTM_WIDTH, :].astype(BF16)
    mvt_ref[...] = pt[N_GATES + MLSTM_WIDTH:N_GATES + 2 * MLSTM_WIDTH, :].astype(BF16)
    mot_ref[...] = pt[N_GATES + 2 * MLSTM_WIDTH:, :]
    graw = p[:, OFF_GATES:OFF_GATES + HEAD_BLOCK][:, :N_GATES] + gbr_ref[...]
    lane = lax.broadcasted_iota(jnp.int32, (tm, N_GATES), 1)
    gr_ref[...] = jnp.where((lane // MLSTM_HEADS) % 2 == 1, _log_sigmoid(graw), graw)
    gt = pt[:N_GATES, :] + gbc_ref[...]
    sub = lax.broadcasted_iota(jnp.int32, (N_GATES, tm), 0)
    gc_ref[...] = jnp.where((sub // MLSTM_HEADS) % 2 == 1, _log_sigmoid(gt), gt)

    ind = ind_ref[...]
    cos = cos_ref[...]
    sin = sin_ref[...]
    lane_b = lax.broadcasted_iota(jnp.int32, (tm, HEAD_BLOCK), 1)
    first_half = (lane_b % ROPE_AXIS_DIM) < (ROPE_AXIS_DIM // 2)

    cqn = _rms(p[:, 0:MLA_Q_RANK], cqg_ref[...]).astype(BF16)
    ckvn = _rms(p[:, OFF_CKV:OFF_CKV + MLA_KV_RANK], ckvg_ref[...]).astype(BF16)
    nt_dims = (((1,), (1,)), ((), ()))
    k_raw = _dotf(ckvn, wuk_ref[...])
    vt = lax.dot_general(wuvt_ref[...], ckvn, nt_dims, preferred_element_type=F32).astype(BF16)
    ones_rows = jnp.where(lax.broadcasted_iota(jnp.int32, (V_ROWS - MLA_V, tm), 0) == 0, 1.0, 0.0).astype(BF16)
    for hh in range(MLA_HEADS):
        vt_ref[V_ROWS * hh:V_ROWS * hh + MLA_V, :] = vt[MLA_V * hh:MLA_V * (hh + 1), :]
        vt_ref[V_ROWS * hh + MLA_V:V_ROWS * (hh + 1), :] = ones_rows

    kn = k_raw * lax.rsqrt(_seg_mean(k_raw * k_raw, ind) + EPS) * kg_ref[...]
    kr = p[:, OFF_KR:OFF_KR + HEAD_BLOCK]
    krn = kr * lax.rsqrt(_seg_mean(kr * kr, ind[:HEAD_BLOCK, :HEAD_BLOCK]) + EPS) * krg_ref[...]
    kr_rot = _rope_block(krn, cos, sin, first_half)
    for hh in range(MLA_HEADS):
        sl = slice(HEAD_BLOCK * hh, HEAD_BLOCK * (hh + 1))
        k_ref[:, sl] = (kn[:, sl] + kr_rot).astype(BF16)

    qt_raw = lax.dot_general(wuqt_ref[...], cqn, nt_dims, preferred_element_type=F32)
    q_scale = (MLA_QK ** -0.5) * LOG2_E
    half = ROPE_AXIS_DIM // 2
    cost = cost_ref[...]
    sint = sint_ref[...]
    for hh in range(MLA_HEADS):
        r0 = HEAD_BLOCK * hh
        nope = qt_raw[r0:r0 + MLA_NOPE, :]
        rope = qt_raw[r0 + MLA_NOPE:r0 + MLA_QK, :]
        nope = nope * lax.rsqrt(jnp.mean(nope * nope, axis=0, keepdims=True) + EPS) * qgn_ref[...]
        rope = rope * lax.rsqrt(jnp.mean(rope * rope, axis=0, keepdims=True) + EPS) * qgr_ref[...]
        rot = []
        for ax in range(2):
            x1 = rope[ROPE_AXIS_DIM * ax:ROPE_AXIS_DIM * ax + half, :]
            x2 = rope[ROPE_AXIS_DIM * ax + half:ROPE_AXIS_DIM * (ax + 1), :]
            cs = cost[half * ax:half * (ax + 1), :]
            sn = sint[half * ax:half * (ax + 1), :]
            rot += [x1 * cs - x2 * sn, x2 * cs + x1 * sn]
        qt_ref[r0:r0 + MLA_NOPE, :] = (nope * q_scale).astype(BF16)
        qt_ref[r0 + MLA_NOPE:r0 + MLA_QK, :] = (jnp.concatenate(rot, axis=0) * q_scale).astype(BF16)
        qt_ref[r0 + MLA_QK:r0 + HEAD_BLOCK, :] = jnp.zeros((HEAD_BLOCK - MLA_QK, tm), BF16)


def _even_prep(x2d, mods, row_fn, g, tabs, tab_fn, w, tm):
    n = x2d.shape[0]
    cos, sin, cost, sint = tabs
    tok = lambda width: pl.BlockSpec((tm, width), lambda i: (i, 0))
    tok_t = lambda rows: pl.BlockSpec((rows, tm), lambda i: (0, i))
    out_shape = [
        jax.ShapeDtypeStruct((MLA_HEADS * HEAD_BLOCK, n), BF16),
        jax.ShapeDtypeStruct((n, MLA_HEADS * HEAD_BLOCK), BF16),
        jax.ShapeDtypeStruct((MLA_HEADS * V_ROWS, n), BF16),
        jax.ShapeDtypeStruct((MLSTM_WIDTH, n), BF16),
        jax.ShapeDtypeStruct((n, MLSTM_WIDTH), BF16),
        jax.ShapeDtypeStruct((MLSTM_WIDTH, n), BF16),
        jax.ShapeDtypeStruct((MLSTM_WIDTH, n), F32),
        jax.ShapeDtypeStruct((n, N_GATES), F32),
        jax.ShapeDtypeStruct((N_GATES, n), F32),
    ]
    out_specs = [tok_t(1024), tok(1024), tok_t(MLA_HEADS * V_ROWS), tok_t(512), tok(512), tok_t(512), tok_t(512),
                 tok(N_GATES), tok_t(N_GATES)]
    in_specs = [
        tok(D_MODEL),
        pl.BlockSpec((None, 3, D_MODEL), lambda i: (row_fn(i), 0, 0)),
        _const_spec((1, D_MODEL)),
        pl.BlockSpec((tm, HEAD_BLOCK), lambda i: (tab_fn(i), 0)),
        pl.BlockSpec((tm, HEAD_BLOCK), lambda i: (tab_fn(i), 0)),
        pl.BlockSpec((ROPE_AXIS_DIM, tm), lambda i: (0, tab_fn(i))),
        pl.BlockSpec((ROPE_AXIS_DIM, tm), lambda i: (0, tab_fn(i))),
        _const_spec((D_MODEL, W_ALL)),
        _const_spec((W_T_ROWS, D_MODEL)),
        _const_spec((1, N_GATES)),
        _const_spec((N_GATES, 1)),
        _const_spec((1, MLA_Q_RANK)),
        _const_spec((1024, MLA_Q_RANK)),
        _const_spec((1, MLA_KV_RANK)),
        _const_spec((MLA_KV_RANK, 1024)),
        _const_spec((512, MLA_KV_RANK)),
        _const_spec((MLA_NOPE, 1)),
        _const_spec((MLA_ROPE, 1)),
        _const_spec((1, 1024)),
        _const_spec((1, HEAD_BLOCK)),
        _const_spec((2 * HEAD_BLOCK, 2 * HEAD_BLOCK)),
    ]
    return pl.pallas_call(
        _even_prep_kernel,
        grid=(n // tm,),
        in_specs=in_specs,
        out_specs=out_specs,
        out_shape=out_shape,
        compiler_params=_cparams("parallel"),
        name="even_prep",
    )(x2d, mods, g.reshape(1, D_MODEL), cos, sin, cost, sint, w["w_all"], w["w_gt"], w["gb_row"], w["gb_col"],
      w["cq_g"], w["w_uqt"], w["ckv_g"], w["w_uk"], w["w_uvt"], w["qg_nope"], w["qg_rope"], w["k_g"], w["kr_g"],
      w["ind"])


def _attn_kernel(qt_ref, k_ref, vt_ref, o_ref, st_scr, acc_scr):
    tq = qt_ref.shape[1]
    nk, _, tk = vt_ref.shape
    unroll = max(u for u in (16, 8, 4, 2) if (nk - 1) % u == 0)
    heads = range(2)

    def scores(j, slot):
        off = pl.multiple_of(j * tk, tk)
        cms = []
        for hh in heads:
            sl = slice(HEAD_BLOCK * hh, HEAD_BLOCK * (hh + 1))
            st = _dotf(k_ref[pl.ds(off, tk), sl], qt_ref[sl, :])
            st_scr[slot, hh] = st
            cms.append(jnp.max(st, axis=0, keepdims=True))
        return tuple(cms)

    def absorb(j, slot, cms, ms):
        out = []
        for hh in heads:
            m_new = jnp.maximum(ms[hh], cms[hh])
            alpha = jnp.exp2(ms[hh] - m_new)
            pt = jnp.exp2(st_scr[slot, hh] - m_new)
            pv = _dotf(vt_ref[j, V_ROWS * hh:V_ROWS * (hh + 1), :], pt.astype(BF16))
            acc_scr[hh] = alpha * acc_scr[hh] + pv
            out.append(m_new)
        return tuple(out)

    def body(i, carry):
        cms, ms = carry
        for r in range(unroll):
            j = unroll * i + r
            cms_next = scores(j + 1, (r + 1) % 2)
            ms = absorb(j, r % 2, cms, ms)
            cms = cms_next
        return cms, ms

    acc_scr[...] = jnp.zeros(acc_scr.shape, F32)
    m0 = jnp.full((1, tq), NEG_BIG, F32)
    cms, ms = lax.fori_loop(0, (nk - 1) // unroll, body, (scores(0, 0), (m0, m0)))
    absorb(nk - 1, 0, cms, ms)
    for hh in heads:
        acc = acc_scr[hh]
        o_ref[MLA_V * hh:MLA_V * (hh + 1), :] = (acc[:MLA_V, :] / acc[MLA_V:MLA_V + 1, :]).astype(o_ref.dtype)


def _attn_bounded_kernel(qt_ref, k_ref, vt_ref, o_ref, pt_scr, acc_scr):
    nk, _, tk = vt_ref.shape
    unroll = max(u for u in (16, 8, 4, 2) if (nk - 1) % u == 0)
    heads = range(2)

    def probs(j, slot):
        off = pl.multiple_of(j * tk, tk)
        for hh in heads:
            sl = slice(HEAD_BLOCK * hh, HEAD_BLOCK * (hh + 1))
            pt_scr[slot, hh] = jnp.exp2(_dotf(k_ref[pl.ds(off, tk), sl], qt_ref[sl, :])).astype(BF16)

    def absorb(j, slot):
        for hh in heads:
            acc_scr[hh] += _dotf(vt_ref[j, V_ROWS * hh:V_ROWS * (hh + 1), :], pt_scr[slot, hh])

    def body(i, carry):
        for r in range(unroll):
            j = unroll * i + r
            probs(j + 1, (r + 1) % 2)
            absorb(j, r % 2)
        return carry

    acc_scr[...] = jnp.zeros(acc_scr.shape, F32)
    probs(0, 0)
    lax.fori_loop(0, (nk - 1) // unroll, body, 0)
    absorb(nk - 1, 0)
    for hh in heads:
        acc = acc_scr[hh]
        o_ref[MLA_V * hh:MLA_V * (hh + 1), :] = (acc[:MLA_V, :] / acc[MLA_V:MLA_V + 1, :]).astype(o_ref.dtype)


def _attention(qt, k2d, vt4, batch, tq, bounded):
    nq = qt.shape[1] // batch // tq
    t_k = k2d.shape[0] // batch
    _, nk, _, tk = vt4.shape
    return pl.pallas_call(
        _attn_bounded_kernel if bounded else _attn_kernel,
        grid=(batch, MLA_HEADS // 2, nq),
        in_specs=[
            pl.BlockSpec((2 * HEAD_BLOCK, tq), lambda b, hp, i: (hp, b * nq + i)),
            pl.BlockSpec((t_k, 2 * HEAD_BLOCK), lambda b, hp, i: (b, hp)),
            pl.BlockSpec((None, nk, 2 * V_ROWS, tk), lambda b, hp, i: (b, 0, hp, 0)),
        ],
        out_specs=pl.BlockSpec((2 * MLA_V, tq), lambda b, hp, i: (hp, b * nq + i)),
        out_shape=jax.ShapeDtypeStruct((MLA_HEADS * MLA_V, qt.shape[1]), BF16),
        scratch_shapes=[pltpu.VMEM((2, 2, tk, tq), BF16 if bounded else F32), pltpu.VMEM((2, V_ROWS, tq), F32)],
        compiler_params=_cparams("parallel", "parallel", "arbitrary"),
        name="mla_attention_bounded" if bounded else "mla_attention",
    )(qt, k2d, vt4)


def _attention_dispatch(score_bound, qt, k2d, vt4, batch, tq):
    return lax.cond(score_bound <= SCORE_BOUND,
                    lambda *a: _attention(*a, batch, tq, True), lambda *a: _attention(*a, batch, tq, False),
                    qt, k2d, vt4)


def _mlstm_kernel(qtf_ref, kf_ref, vtf_ref, grf_ref, gcf_ref, qtb_ref, kb_ref, vtb_ref, grb_ref, gcb_ref,
                  s0_ref, m0_ref, hf_ref, hb_ref, s_out_ref, m_out_ref, s_scr, m_scr):
    j = pl.program_id(1)
    nj = pl.num_programs(1)
    L = kf_ref.shape[0]

    @pl.when(j == 0)
    def _():
        s_scr[...] = s0_ref[...]
        m_scr[...] = m0_ref[...]

    row = lax.broadcasted_iota(jnp.int32, (L, L), 0)
    col = lax.broadcasted_iota(jnp.int32, (L, L), 1)
    sub = lax.broadcasted_iota(jnp.int32, (MLSTM_ST - MLSTM_DH, L), 0)

    dirs = ((qtf_ref, kf_ref, vtf_ref, grf_ref, gcf_ref, hf_ref), (qtb_ref, kb_ref, vtb_ref, grb_ref, gcb_ref, hb_ref))
    combos = [(d, hh) for d in range(2) for hh in range(MLSTM_HEADS)]


    masks, cum_rows, cum_cols, g_rows, g_cols = [], [], [], [], []
    for d in range(2):
        s_le_t = (row <= col) if d == 0 else (row >= col)
        tri = jnp.where((col <= row) if d == 0 else (col >= row), 1.0, 0.0).astype(BF16)
        tri_t = jnp.where(s_le_t, 1.0, 0.0).astype(BF16)
        gr = dirs[d][3][...]
        gc = dirs[d][4][...]
        r1, r2, r3 = _split3(gr)
        c1, c2, c3 = _split3(gc)
        masks.append(s_le_t)
        g_rows.append(gr)
        g_cols.append(gc)
        cum_rows.append(_dotf(tri, r1) + _dotf(tri, r2) + _dotf(tri, r3))
        cum_cols.append(_dotf(c1, tri_t) + _dotf(c2, tri_t) + _dotf(c3, tri_t))
    kq, inter, s_prev, m_prev = [], [], [], []
    for d, hh in combos:
        idx = d * MLSTM_HEADS + hh
        sl = slice(MLSTM_DH * hh, MLSTM_DH * (hh + 1))
        qth = dirs[d][0][sl, :]
        s_prev.append(s_scr[idx])
        m_prev.append(m_scr[idx:idx + 1, 0:1])
        kq.append(_dotf(dirs[d][1][:, sl], qth))
        inter.append(_dotf(s_prev[-1].astype(BF16), qth))

    sw, den, w_inter, floor, vw, decay, m_new = [], [], [], [], [], [], []
    for c, (d, hh) in enumerate(combos):
        il = (0 if d == 0 else 2 * MLSTM_HEADS) + hh
        fl = il + MLSTM_HEADS
        sl = slice(MLSTM_DH * hh, MLSTM_DH * (hh + 1))
        c_col = g_rows[d][:, il:il + 1] - cum_rows[d][:, fl:fl + 1]
        b_row = cum_cols[d][fl:fl + 1, :]
        c_row = g_cols[d][il:il + 1, :] - b_row
        c_mat = jnp.where(masks[d], c_col, NEG_BIG)
        u = jnp.maximum(jnp.max(c_mat, axis=0, keepdims=True), m_prev[c])
        s = kq[c] * jnp.exp(c_mat - u)
        sw.append(s.astype(BF16))
        den.append(jnp.sum(s, axis=0, keepdims=True))
        w_inter.append(jnp.exp(m_prev[c] - u))
        floor.append(jnp.exp(-(b_row + u)))
        b_end = b_row[:, L - 1:L] if d == 0 else b_row[:, 0:1]
        g_row = b_end + c_row
        m_new.append(jnp.maximum(b_end + m_prev[c], jnp.max(g_row, axis=-1, keepdims=True)))
        decay.append(jnp.exp(b_end + m_prev[c] - m_new[c]))
        w_s = jnp.exp(g_row - m_new[c])
        n_rows = jnp.where(sub == 0, w_s, 0.0)
        vw.append(jnp.concatenate([dirs[d][2][sl, :].astype(F32) * w_s, n_rows], axis=0).astype(BF16))

    for c, (d, hh) in enumerate(combos):
        idx = d * MLSTM_HEADS + hh
        sl = slice(MLSTM_DH * hh, MLSTM_DH * (hh + 1))
        num = _dotf(dirs[d][2][sl, :], sw[c]) + w_inter[c] * inter[c][:MLSTM_DH, :]
        dn = den[c] + w_inter[c] * inter[c][MLSTM_DH:MLSTM_DH + 1, :]
        dirs[d][5][sl, :] = num / jnp.maximum(jnp.abs(dn), floor[c])
        s_scr[idx] = decay[c] * s_prev[c] + _dotf(vw[c], dirs[d][1][:, sl])
        m_scr[idx:idx + 1, :] = jnp.broadcast_to(m_new[c], (1, MLSTM_DH))

    @pl.when(j == nj - 1)
    def _():
        s_out_ref[...] = s_scr[...]
        m_out_ref[...] = m_scr[...]


def _mlstm(mqt, mk, mvt, g_rows, g_cols, s0, m0, batch, chunk):
    n = mk.shape[0]
    nc = n // batch // chunk
    fwd = lambda b, j: (b * nc + j, 0)
    bwd = lambda b, j: (b * nc + (nc - 1 - j), 0)
    fwd_t = lambda b, j: (0, b * nc + j)
    bwd_t = lambda b, j: (0, b * nc + (nc - 1 - j))
    nst = 2 * MLSTM_HEADS
    tok = lambda fn: pl.BlockSpec((chunk, MLSTM_WIDTH), fn)
    tok_t = lambda fn: pl.BlockSpec((MLSTM_WIDTH, chunk), fn)
    state_s = pl.BlockSpec((None, nst, MLSTM_ST, MLSTM_DH), lambda b, j: (b, 0, 0, 0))
    state_m = pl.BlockSpec((None, nst, MLSTM_DH), lambda b, j: (b, 0, 0))
    return pl.pallas_call(
        _mlstm_kernel,
        grid=(batch, nc),
        in_specs=[tok_t(fwd_t), tok(fwd), tok_t(fwd_t), pl.BlockSpec((chunk, N_GATES), fwd),
                  pl.BlockSpec((N_GATES, chunk), fwd_t),
                  tok_t(bwd_t), tok(bwd), tok_t(bwd_t), pl.BlockSpec((chunk, N_GATES), bwd),
                  pl.BlockSpec((N_GATES, chunk), bwd_t),
                  state_s, state_m],
        out_specs=[tok_t(fwd_t), tok_t(bwd_t), state_s, state_m],
        out_shape=[jax.ShapeDtypeStruct((MLSTM_WIDTH, n), F32), jax.ShapeDtypeStruct((MLSTM_WIDTH, n), F32),
                   jax.ShapeDtypeStruct(s0.shape, F32), jax.ShapeDtypeStruct(m0.shape, F32)],
        scratch_shapes=[pltpu.VMEM((nst, MLSTM_ST, MLSTM_DH), F32), pltpu.VMEM((nst, MLSTM_DH), F32)],
        compiler_params=_cparams("parallel", "arbitrary"),
        name="mlstm",
    )(mqt, mk, mvt, g_rows, g_cols, mqt, mk, mvt, g_rows, g_cols, s0, m0)


def _even_out_kernel(x_ref, gate_ref, att_ref, hf_ref, hb_ref, mo_ref, og_ref, w_ref, o_ref):
    hs = hf_ref[...] + hb_ref[...]
    og = og_ref[...]
    parts = [att_ref[...]]
    for hh in range(MLSTM_HEADS):
        sl = slice(MLSTM_DH * hh, MLSTM_DH * (hh + 1))
        blk = hs[sl, :]
        hn = blk * lax.rsqrt(jnp.mean(blk * blk, axis=0, keepdims=True) + EPS) * og[sl, :]
        parts.append((hn * jax.nn.sigmoid(mo_ref[sl, :])).astype(BF16))
    mixed = jnp.concatenate(parts, axis=0)
    y = lax.dot_general(mixed, w_ref[...], (((0,), (0,)), ((), ())), preferred_element_type=F32)
    o_ref[...] = x_ref[...] + gate_ref[...] * y


def _even_out(x2d, gates, row_fn, att, hf, hb, mo, out_g, w_out, tm):
    n = x2d.shape[0]
    tok = pl.BlockSpec((tm, D_MODEL), lambda i: (i, 0))
    tok_t = pl.BlockSpec((MLSTM_WIDTH, tm), lambda i: (0, i))
    return pl.pallas_call(
        _even_out_kernel,
        grid=(n // tm,),
        in_specs=[tok, pl.BlockSpec((None, 1, D_MODEL), lambda i: (row_fn(i), 0, 0)),
                  tok_t, tok_t, tok_t, tok_t, _const_spec((MLSTM_WIDTH, 1)), _const_spec((D_MODEL, D_MODEL))],
        out_specs=tok,
        out_shape=jax.ShapeDtypeStruct((n, D_MODEL), F32),
        compiler_params=_cparams("parallel"),
        name="even_out",
    )(x2d, gates, att, hf, hb, mo, out_g.reshape(MLSTM_WIDTH, 1), w_out)


def _gelu_tanh(x):
    return 0.5 * x * (1.0 + jnp.tanh(0.7978845608028654 * (x + 0.044715 * (x * x * x))))


def _sigmoid(x):
    return 0.5 * jnp.tanh(0.5 * x) + 0.5


def _odd_in_kernel(x_ref, xp_ref, xn_ref, mod_ref, g_ref, w_ref, cw_ref, cb_ref, gate_ref, xc_ref, ext_scr, *, nt):
    tm = x_ref.shape[0]
    ti = pl.program_id(0) % nt
    g, shift, scale = g_ref[...], mod_ref[0:1, :], mod_ref[1:2, :]
    p = _dotf(_norm_mod(x_ref[...], g, shift, scale).astype(BF16), w_ref[...])
    gate_ref[...] = _gelu_tanh(p[:, :RNN_WIDTH])
    halo = jnp.concatenate([xp_ref[...], xn_ref[...]], axis=0)
    ph = _dotf(_norm_mod(halo, g, shift, scale).astype(BF16), w_ref[:, RNN_WIDTH:])
    ext_scr[0:8, :] = jnp.where(ti > 0, ph[0:8, :], 0.0)
    ext_scr[8:8 + tm, :] = p[:, RNN_WIDTH:]
    ext_scr[8 + tm:16 + tm, :] = jnp.where(ti < nt - 1, ph[8:16, :], 0.0)
    cw = cw_ref[...]
    xc = cb_ref[...] + cw[0:1, :] * ext_scr[6:6 + tm, :]
    for k in range(1, CONV_W):
        xc = xc + cw[k:k + 1, :] * ext_scr[6 + k:6 + k + tm, :]
    xc_ref[...] = xc


def _odd_in(x2d, mods, row_fn, g, w_in, conv_w, conv_b, batch, tm):
    n = x2d.shape[0]
    nt = n // batch // tm
    t8 = tm // 8
    tok = pl.BlockSpec((tm, D_MODEL), lambda i: (i, 0))
    prev = pl.BlockSpec((8, D_MODEL), lambda i: (jnp.maximum(i * t8 - 1, 0), 0))
    nxt = pl.BlockSpec((8, D_MODEL), lambda i: (jnp.minimum((i + 1) * t8, n // 8 - 1), 0))
    return pl.pallas_call(
        functools.partial(_odd_in_kernel, nt=nt),
        grid=(n // tm,),
        in_specs=[tok, prev, nxt, pl.BlockSpec((None, 3, D_MODEL), lambda i: (row_fn(i), 0, 0)),
                  _const_spec((1, D_MODEL)), _const_spec((D_MODEL, 2 * RNN_WIDTH)),
                  _const_spec((CONV_W, RNN_WIDTH)), _const_spec((1, RNN_WIDTH))],
        out_specs=[tok, tok],
        out_shape=[jax.ShapeDtypeStruct((n, RNN_WIDTH), F32), jax.ShapeDtypeStruct((n, RNN_WIDTH), F32)],
        scratch_shapes=[pltpu.VMEM((tm + 16, RNN_WIDTH), F32)],
        compiler_params=_cparams("parallel"),
        name="odd_in",
    )(x2d, x2d, x2d, mods, g.reshape(1, D_MODEL), w_in, conv_w, conv_b)


def _rglru_kernel(xc_ref, wax_ref, ba_ref, bx_ref, lam_ref, h0_ref, h_ref, ht_ref, a_scr, b_scr, carry_scr, *, nt):
    d = pl.program_id(0)
    i = pl.program_id(2)
    tm = xc_ref.shape[0]
    nb8 = tm // 8

    xc = xc_ref[...]
    neg_lam = -lam_ref[...]
    decay_rate = -LRU_C * (jnp.maximum(neg_lam, 0.0) + jnp.log1p(jnp.exp(-jnp.abs(neg_lam))))
    ba = ba_ref[...]
    bx = bx_ref[...]
    for nb in range(RNN_BLOCKS):
        sl = slice(RNN_BLOCK_DIM * nb, RNN_BLOCK_DIM * (nb + 1))
        u = xc[:, sl]
        rg = _dotf(u.astype(BF16), wax_ref[nb])
        r = _sigmoid(rg[:, :RNN_BLOCK_DIM] + ba[:, sl])
        ig = _sigmoid(rg[:, RNN_BLOCK_DIM:] + bx[:, sl])
        log_a = r * decay_rate[:, sl]
        a = jnp.exp(log_a)
        a_scr[:, sl] = a
        b_scr[:, sl] = jnp.sqrt(-jnp.tanh(log_a) * (a * a + 1.0)) * (ig * u)

    @pl.when(i == 0)
    def _():
        carry_scr[...] = h0_ref[...]

    def scan_tile(reverse):
        a = a_scr[...].reshape(nb8, 8, RNN_WIDTH)
        b = b_scr[...].reshape(nb8, 8, RNN_WIDTH)
        r8 = lax.broadcasted_iota(jnp.int32, (nb8, 8, RNN_WIDTH), 1)
        for s in (1, 2, 4):
            keep = (r8 < 8 - s) if reverse else (r8 >= s)
            shift = 8 - s if reverse else s
            a_in = jnp.where(keep, pltpu.roll(a, shift, axis=1), 1.0)
            b_in = jnp.where(keep, pltpu.roll(b, shift, axis=1), 0.0)
            b = a * b_in + b
            a = a * a_in
        a_scr[...] = a.reshape(tm, RNN_WIDTH)
        b_scr[...] = b.reshape(tm, RNN_WIDTH)

        def block(jb, h):
            base = pl.multiple_of((nb8 - 1 - jb if reverse else jb) * 8, 8)
            hb = a_scr[pl.ds(base, 8), :] * h + b_scr[pl.ds(base, 8), :]
            h_ref[pl.ds(base, 8), :] = hb
            return hb[0:1, :] if reverse else hb[7:8, :]

        carry_scr[...] = lax.fori_loop(0, nb8, block, carry_scr[...], unroll=4)

    @pl.when(d == 0)
    def _():
        scan_tile(False)

    @pl.when(d == 1)
    def _():
        scan_tile(True)

    @pl.when(i == nt - 1)
    def _():
        ht_ref[...] = carry_scr[...]


def _rglru(xc2d, h0, w, batch, tm):
    n = xc2d.shape[0]
    nt = n // batch // tm
    tile = lambda d, i: jnp.where(d == 0, i, nt - 1 - i)
    per_dir = lambda shape: pl.BlockSpec((None,) + shape, lambda d, b, i: (d,) + (0,) * len(shape))
    state = pl.BlockSpec((None, None, 1, RNN_WIDTH), lambda d, b, i: (d, b, 0, 0))
    return pl.pallas_call(
        functools.partial(_rglru_kernel, nt=nt),
        grid=(2, batch, nt),
        in_specs=[pl.BlockSpec((tm, RNN_WIDTH), lambda d, b, i: (b * nt + tile(d, i), 0)),
                  per_dir((RNN_BLOCKS, RNN_BLOCK_DIM, 2 * RNN_BLOCK_DIM)), per_dir((1, RNN_WIDTH)), per_dir((1, RNN_WIDTH)),
                  per_dir((1, RNN_WIDTH)), state],
        out_specs=[pl.BlockSpec((None, tm, RNN_WIDTH), lambda d, b, i: (d, b * nt + tile(d, i), 0)), state],
        out_shape=[jax.ShapeDtypeStruct((2, n, RNN_WIDTH), F32), jax.ShapeDtypeStruct((2, batch, 1, RNN_WIDTH), F32)],
        scratch_shapes=[pltpu.VMEM((tm, RNN_WIDTH), F32), pltpu.VMEM((tm, RNN_WIDTH), F32),
                        pltpu.VMEM((1, RNN_WIDTH), F32)],
        compiler_params=_cparams("parallel", "parallel", "arbitrary"),
        name="rglru",
    )(xc2d, w["w_ax"], w["b_a"], w["b_x"], w["lam"], h0)


def _odd_out_kernel(x_ref, gate_ref, h_ref, gg_ref, w_ref, o_ref):
    y = ((h_ref[0] + h_ref[1]).astype(F32) * gg_ref[...]).astype(BF16)
    o_ref[...] = x_ref[...] + gate_ref[...] * _dotf(y, w_ref[...])


def _odd_out(x2d, gates, row_fn, h2, gelu_gate, w_out, tm):
    n = x2d.shape[0]
    tok = pl.BlockSpec((tm, D_MODEL), lambda i: (i, 0))
    return pl.pallas_call(
        _odd_out_kernel,
        grid=(n // tm,),
        in_specs=[tok, pl.BlockSpec((None, 1, D_MODEL), lambda i: (row_fn(i), 0, 0)),
                  pl.BlockSpec((2, tm, RNN_WIDTH), lambda i: (0, i, 0)), tok, _const_spec((RNN_WIDTH, D_MODEL))],
        out_specs=tok,
        out_shape=jax.ShapeDtypeStruct((n, D_MODEL), F32),
        compiler_params=_cparams("parallel"),
        name="odd_out",
    )(x2d, gates, h2, gelu_gate, w_out)


def _rope_tables(seq):
    t = jnp.arange(seq)
    inv = ROPE_BASE ** (-jnp.arange(0, ROPE_AXIS_DIM, 2, dtype=F32) / ROPE_AXIS_DIM)
    ang_r = (t // GRID_W).astype(F32)[:, None] * inv
    ang_c = (t % GRID_W).astype(F32)[:, None] * inv
    cos32 = jnp.concatenate([jnp.cos(ang_r), jnp.cos(ang_r), jnp.cos(ang_c), jnp.cos(ang_c)], axis=1)
    sin32 = jnp.concatenate([-jnp.sin(ang_r), jnp.sin(ang_r), -jnp.sin(ang_c), jnp.sin(ang_c)], axis=1)
    pad = HEAD_BLOCK - MLA_QK
    cos = jnp.concatenate([jnp.ones((seq, MLA_NOPE), F32), cos32, jnp.ones((seq, pad), F32)], axis=1)
    sin = jnp.concatenate([jnp.zeros((seq, MLA_NOPE), F32), sin32, jnp.zeros((seq, pad), F32)], axis=1)
    cost = jnp.concatenate([jnp.cos(ang_r), jnp.cos(ang_c)], axis=1).T
    sint = jnp.concatenate([jnp.sin(ang_r), jnp.sin(ang_c)], axis=1).T
    return cos, sin, cost, sint


def _even_weights(w_in, cq_g, w_uq, ckv_g, w_ukv, q_g, k_g, gate_b):
    pad = HEAD_BLOCK - MLA_QK
    kr_cols = w_in[:, OFF_KR:OFF_KR + MLA_ROPE]
    w_kr = jnp.pad(kr_cols, ((0, 0), (MLA_NOPE, pad)))
    m_cols = w_in[:, OFF_KR + MLA_ROPE:OFF_KR + MLA_ROPE + 4 * MLSTM_WIDTH]
    g_cols = w_in[:, OFF_KR + MLA_ROPE + 4 * MLSTM_WIDTH:]
    mq_cols, mk_cols, mv_cols, mo_cols = jnp.split(m_cols, 4, axis=1)
    w_all = jnp.concatenate([w_in[:, :OFF_KR], w_kr, mk_cols, jnp.pad(g_cols, ((0, 0), (0, HEAD_BLOCK - N_GATES)))], axis=1)
    uq = jnp.pad(w_uq.reshape(MLA_Q_RANK, MLA_HEADS, MLA_QK), ((0, 0), (0, 0), (0, pad)))
    ukv = w_ukv.reshape(MLA_KV_RANK, MLA_HEADS, MLA_NOPE + MLA_V)
    uk = jnp.pad(ukv[:, :, :MLA_NOPE], ((0, 0), (0, 0), (0, HEAD_BLOCK - MLA_NOPE)))
    uv = ukv[:, :, MLA_NOPE:]
    kg_blk = jnp.pad(k_g[:MLA_NOPE], (0, HEAD_BLOCK - MLA_NOPE))
    krg_blk = jnp.pad(k_g[MLA_NOPE:], (MLA_NOPE, pad))
    seg = jnp.arange(HEAD_BLOCK)
    seg_id = jnp.where(seg < MLA_NOPE, 0, jnp.where(seg < MLA_QK, 1, 2))
    seg_len = jnp.where(seg < MLA_NOPE, float(MLA_NOPE), float(MLA_ROPE))
    ind128 = jnp.where((seg_id[:, None] == seg_id[None, :]) & (seg_id[:, None] < 2), 1.0 / seg_len[None, :], 0.0)
    zero = jnp.zeros_like(ind128)
    ind = jnp.block([[ind128, zero], [zero, ind128]])
    return {
        "w_all": w_all.astype(BF16),
        "w_gt": jnp.concatenate([g_cols, mq_cols, mv_cols, mo_cols], axis=1).T.astype(BF16),
        "gb_row": gate_b.reshape(1, N_GATES),
        "gb_col": gate_b.reshape(N_GATES, 1),
        "cq_g": cq_g.reshape(1, MLA_Q_RANK),
        "w_uqt": uq.reshape(MLA_Q_RANK, MLA_HEADS * HEAD_BLOCK).T.astype(BF16),
        "ckv_g": ckv_g.reshape(1, MLA_KV_RANK),
        "w_uk": uk.reshape(MLA_KV_RANK, MLA_HEADS * HEAD_BLOCK).astype(BF16),
        "w_uvt": uv.reshape(MLA_KV_RANK, MLA_HEADS * MLA_V).T.astype(BF16),
        "qg_nope": q_g[:MLA_NOPE].reshape(MLA_NOPE, 1),
        "qg_rope": q_g[MLA_NOPE:].reshape(MLA_ROPE, 1),
        "k_g": jnp.tile(kg_blk, MLA_HEADS).reshape(1, -1),
        "kr_g": krg_blk.reshape(1, HEAD_BLOCK),
        "ind": ind.astype(BF16),
    }


def kernel(x, c, ctx, c_ctx, mod_w, mod_b, norm_g, ffn_w_gate, ffn_w_up, ffn_w_down, even_w_in, even_w_out, mla_cq_g, mla_w_uq, mla_ckv_g, mla_w_ukv, mla_q_g, mla_k_g, mlstm_gate_b, mlstm_out_g, odd_w_in, odd_conv_w, odd_conv_b, lru_w_a, lru_b_a, lru_w_x, lru_b_x, lru_lam, odd_w_out):
    batch, seq, _ = x.shape
    ctx_len = ctx.shape[1]
    depth = mod_w.shape[0]
    assert depth == 2 and batch <= 7

    cond8 = jnp.zeros((8, D_MODEL), F32).at[:batch].set(c).at[batch].set(c_ctx)
    mods = _ada_params(cond8, mod_w, mod_b)
    mods = mods.reshape(depth, 8, 3, 3, D_MODEL).transpose(0, 2, 1, 3, 4)

    tm_x, tm_c = 512, 256
    x_row = lambda i: i // (seq // tm_x)
    c_row = lambda i: batch
    x2 = x.reshape(batch * seq, D_MODEL)
    c2 = ctx.reshape(batch * ctx_len, D_MODEL)

    wg = ffn_w_gate.astype(BF16)
    wu = ffn_w_up.astype(BF16)
    wd = ffn_w_down.astype(BF16)

    def ffn(t2, layer, which, row_fn, tm):
        return _ffn(t2, mods[layer, 2 * which], row_fn, norm_g[layer, 2 * which], wg[layer, which], wu[layer, which],
                    wd[layer, which], tm)

    x2 = ffn(x2, 0, 0, x_row, tm_x)
    c2 = ffn(c2, 0, 0, c_row, tm_c)

    ew = _even_weights(even_w_in[0], mla_cq_g[0], mla_w_uq[0], mla_ckv_g[0], mla_w_ukv[0], mla_q_g[0], mla_k_g[0],
                       mlstm_gate_b[0])
    tabs_x = _rope_tables(seq)
    tabs_c = (jnp.ones((tm_c, HEAD_BLOCK), F32), jnp.zeros((tm_c, HEAD_BLOCK), F32),
              jnp.ones((ROPE_AXIS_DIM, tm_c), F32), jnp.zeros((ROPE_AXIS_DIM, tm_c), F32))
    px = _even_prep(x2, mods[0, 1], x_row, norm_g[0, 1], tabs_x, lambda i: i % (seq // tm_x), ew, tm_x)
    pc = _even_prep(c2, mods[0, 1], c_row, norm_g[0, 1], tabs_c, lambda i: 0, ew, tm_c)
    qt_x, k_x, vt_x, mqt_x, mk_x, mvt_x, mot_x, gr_x, gc_x = px
    qt_c, k_c, vt_c, mqt_c, mk_c, mvt_c, mot_c, gr_c, gc_c = pc

    k_all = jnp.concatenate([k_c.reshape(batch, ctx_len, -1), k_x.reshape(batch, seq, -1)], axis=1)
    k_all = k_all.reshape(batch * (ctx_len + seq), -1)
    vt_all = jnp.concatenate([vt_c.reshape(-1, batch, ctx_len), vt_x.reshape(-1, batch, seq)], axis=2)

    def key_chunks(vt3, tk):
        return vt3.reshape(vt3.shape[0], batch, -1, tk).transpose(1, 2, 0, 3)

    gmax = lambda g: jnp.max(jnp.abs(g))
    q_g, k_g = mla_q_g[0], mla_k_g[0]
    score_bound = (1.02 * LOG2_E * MLA_QK ** -0.5) * (
        MLA_NOPE * gmax(q_g[:MLA_NOPE]) * gmax(k_g[:MLA_NOPE]) + MLA_ROPE * gmax(q_g[MLA_NOPE:]) * gmax(k_g[MLA_NOPE:]))
    att_x = _attention_dispatch(score_bound, qt_x, k_all, key_chunks(vt_all, 256), batch, 512)
    att_c = _attention_dispatch(score_bound, qt_c, k_c, key_chunks(vt_c.reshape(-1, batch, ctx_len), 256), batch, 256)

    chunk = 128
    nst = 2 * MLSTM_HEADS
    s0 = jnp.zeros((batch, nst, MLSTM_ST, MLSTM_DH), F32)
    m0 = jnp.zeros((batch, nst, MLSTM_DH), F32)
    hcf, hcb, s_c, m_c = _mlstm(mqt_c, mk_c, mvt_c, gr_c, gc_c, s0, m0, batch, chunk)
    hxf, hxb, _, _ = _mlstm(mqt_x, mk_x, mvt_x, gr_x, gc_x, s_c, m_c, batch, chunk)

    w_out0 = even_w_out[0].astype(BF16)
    gate0 = mods[0, 1][:, 2:3, :]
    x2 = _even_out(x2, gate0, x_row, att_x, hxf, hxb, mot_x, mlstm_out_g[0], w_out0, tm_x)
    c2 = _even_out(c2, gate0, c_row, att_c, hcf, hcb, mot_c, mlstm_out_g[0], w_out0, tm_c)
    x2 = ffn(x2, 0, 1, x_row, tm_x)
    c2 = ffn(c2, 0, 1, c_row, tm_c)

    x2 = ffn(x2, 1, 0, x_row, tm_x)
    c2 = ffn(c2, 1, 0, c_row, tm_c)
    w_in1 = odd_w_in[0].astype(BF16)
    conv_b = odd_conv_b[0].reshape(1, RNN_WIDTH)
    gg_x, xc_x = _odd_in(x2, mods[1, 1], x_row, norm_g[1, 1], w_in1, odd_conv_w[0], conv_b, batch, tm_x)
    _, xc_c = _odd_in(c2, mods[1, 1], c_row, norm_g[1, 1], w_in1, odd_conv_w[0], conv_b, batch, tm_c)
    rw = {
        "w_ax": jnp.concatenate([lru_w_a[0], lru_w_x[0]], axis=-1).astype(BF16),
        "b_a": lru_b_a[0].reshape(2, 1, RNN_WIDTH),
        "b_x": lru_b_x[0].reshape(2, 1, RNN_WIDTH),
        "lam": lru_lam[0].reshape(2, 1, RNN_WIDTH),
    }
    h0 = jnp.zeros((2, batch, 1, RNN_WIDTH), F32)
    _, st_c = _rglru(xc_c, h0, rw, batch, 256)
    h_x, _ = _rglru(xc_x, st_c, rw, batch, 256)
    x2 = _odd_out(x2, mods[1, 1][:, 2:3, :], x_row, h_x, gg_x, odd_w_out[0].astype(BF16), tm_x)
    x2 = ffn(x2, 1, 1, x_row, tm_x)
    return x2.reshape(batch, seq, D_MODEL)
```

```python
import functools

import jax
import jax.numpy as jnp
from jax import lax
from jax.experimental import pallas as pl
from jax.experimental.pallas import tpu as pltpu

F32 = jnp.float32
BF16 = jnp.bfloat16

D_MODEL = 1024
GRID_W = 64
EPS = 1e-6
N_MOD = 9
D_FF = 2816
FFN_RESIDUAL = 0.5

MLA_HEADS = 8
MLA_Q_RANK = 384
MLA_KV_RANK = 256
MLA_NOPE = 64
MLA_ROPE = 32
MLA_V = 64
MLA_QK = MLA_NOPE + MLA_ROPE
ROPE_AXIS_DIM = MLA_ROPE // 2
ROPE_BASE = 10000.0
HEAD_BLOCK = 128
KEY_CHUNK = 256
V_ROWS = MLA_V + 16

MLSTM_HEADS = 4
MLSTM_DH = 128
MLSTM_WIDTH = MLSTM_HEADS * MLSTM_DH
N_GATES = 4 * MLSTM_HEADS
MLSTM_ST = MLSTM_DH + 16

RNN_WIDTH = 1024
RNN_BLOCKS = 8
RNN_BLOCK_DIM = RNN_WIDTH // RNN_BLOCKS
CONV_W = 4
LRU_C = 8.0
ODD_IN_CHUNKS = 4

NEG_BIG = -1e30
LOG2_E = 1.4426950408889634
VMEM_LIMIT = 56 * 1024 * 1024


def _cparams(*sem):
    return pltpu.CompilerParams(dimension_semantics=sem, vmem_limit_bytes=VMEM_LIMIT)


def _const_spec(shape):
    zeros = (0,) * len(shape)
    return pl.BlockSpec(shape, lambda *_: zeros, pipeline_mode=pl.Buffered(1))


def _dotf(a, b):
    return jnp.dot(a, b, preferred_element_type=F32)


def _split3(a):
    a1 = a.astype(BF16)
    r1 = a - a1.astype(F32)
    a2 = r1.astype(BF16)
    a3 = (r1 - a2.astype(F32)).astype(BF16)
    return a1, a2, a3


def _norm_mod(x, g, shift, scale):
    ms = jnp.mean(x * x, axis=-1, keepdims=True)
    return (x * lax.rsqrt(ms + EPS)) * (g * (1.0 + scale)) + shift


def _rms(x, g):
    ms = jnp.mean(x * x, axis=-1, keepdims=True)
    return x * lax.rsqrt(ms + EPS) * g


def _log_sigmoid(x):
    return jnp.minimum(x, 0.0) - jnp.log1p(jnp.exp(-jnp.abs(x)))


def _ada_kernel(cond_ref, w_ref, b_ref, o_ref):
    c = cond_ref[...]
    s = c * jax.nn.sigmoid(c)
    s1, s2, _ = _split3(s)
    w = w_ref[...]
    w1 = w.astype(BF16)
    w2 = (w - w1.astype(F32)).astype(BF16)
    o_ref[...] = _dotf(s1, w1) + _dotf(s1, w2) + _dotf(s2, w1) + b_ref[...]


def _ada_params(cond8, mod_w, mod_b):
    depth, _, n = mod_w.shape
    tn = 1152
    return pl.pallas_call(
        _ada_kernel,
        grid=(depth, n // tn),
        in_specs=[
            pl.BlockSpec((8, D_MODEL), lambda l, j: (0, 0)),
            pl.BlockSpec((None, D_MODEL, tn), lambda l, j: (l, 0, j)),
            pl.BlockSpec((None, 1, tn), lambda l, j: (l, 0, j)),
        ],
        out_specs=pl.BlockSpec((None, 8, tn), lambda l, j: (l, 0, j)),
        out_shape=jax.ShapeDtypeStruct((depth, 8, n), F32),
        compiler_params=_cparams("parallel", "parallel"),
        name="ada_params",
    )(cond8, mod_w, mod_b.reshape(depth, 1, n))


def _ffn_math(x, mod_ref, g_ref, wg_ref, wu_ref, wd_ref):
    h = _norm_mod(x, g_ref[...], mod_ref[0:1, :], mod_ref[1:2, :]).astype(BF16)
    g = _dotf(h, wg_ref[...])
    u = _dotf(h, wu_ref[...])
    a = (g * jax.nn.sigmoid(g) * u).astype(BF16)
    return x + (FFN_RESIDUAL * mod_ref[2:3, :]) * _dotf(a, wd_ref[...])


def _ffn_specs(row_fn):
    return [pl.BlockSpec((None, 3, D_MODEL), lambda i: (row_fn(i), 0, 0)), _const_spec((1, D_MODEL)),
            _const_spec((D_MODEL, D_FF)), _const_spec((D_MODEL, D_FF)), _const_spec((D_FF, D_MODEL))]


def _ffn_kernel(x_ref, mod_ref, g_ref, wg_ref, wu_ref, wd_ref, o_ref):
    o_ref[...] = _ffn_math(x_ref[...], mod_ref, g_ref, wg_ref, wu_ref, wd_ref)


def _ffn(x2d, ffn_args, row_fn, tm):
    n = x2d.shape[0]
    tok = pl.BlockSpec((tm, D_MODEL), lambda i: (i, 0))
    return pl.pallas_call(
        _ffn_kernel,
        grid=(n // tm,),
        in_specs=[tok] + _ffn_specs(row_fn),
        out_specs=tok,
        out_shape=jax.ShapeDtypeStruct((n, D_MODEL), F32),
        compiler_params=_cparams("parallel"),
        name="ffn",
    )(x2d, *ffn_args)


W_ALL = MLA_Q_RANK + MLA_KV_RANK + HEAD_BLOCK + MLSTM_WIDTH + HEAD_BLOCK
OFF_CKV = MLA_Q_RANK
OFF_KR = OFF_CKV + MLA_KV_RANK
OFF_MK = OFF_KR + HEAD_BLOCK
OFF_GATES = OFF_MK + MLSTM_WIDTH
W_T_ROWS = N_GATES + 3 * MLSTM_WIDTH


def _seg_mean(sq, ind):
    hi = sq.astype(BF16)
    lo = (sq - hi.astype(F32)).astype(BF16)
    w = ind.shape[0]
    outs = []
    for j in range(sq.shape[1] // w):
        sl = slice(w * j, w * (j + 1))
        outs.append(_dotf(hi[:, sl], ind) + _dotf(lo[:, sl], ind))
    return outs[0] if len(outs) == 1 else jnp.concatenate(outs, axis=1)


def _rope_block(xb, cos, sin, first_half):
    partner = jnp.where(first_half, pltpu.roll(xb, HEAD_BLOCK - 8, axis=1), pltpu.roll(xb, 8, axis=1))
    return xb * cos + partner * sin


def _even_prep_kernel(x_ref, mod_ref, g_ref, cos_ref, sin_ref, cost_ref, sint_ref, wall_ref, wgt_ref, gbr_ref, gbc_ref,
                      cqg_ref, wuqt_ref, ckvg_ref, wuk_ref, wuvt_ref, qgn_ref, qgr_ref, kg_ref, krg_ref, ind_ref,
                      qt_ref, k_ref, vt_ref, mqt_ref, mk_ref, mvt_ref, mot_ref, gr_ref, gc_ref):
    tm = x_ref.shape[0]
    x = x_ref[...]
    h = _norm_mod(x, g_ref[...], mod_ref[0:1, :], mod_ref[1:2, :]).astype(BF16)
    p = _dotf(h, wall_ref[...])
    pt = lax.dot_general(wgt_ref[...], h, (((1,), (1,)), ((), ())), preferred_element_type=F32)

    mk_ref[...] = (p[:, OFF_MK:OFF_MK + MLSTM_WIDTH] * (MLSTM_DH ** -0.5)).astype(BF16)
    mqt_ref[...] = pt[N_GATES:N_GATES + MLSTM_WIDTH, :].astype(BF16)
    mvt_ref[...] = pt[N_GATES + MLSTM_WIDTH:N_GATES + 2 * MLSTM_WIDTH, :].astype(BF16)
    mot_ref[...] = pt[N_GATES + 2 * MLSTM_WIDTH:, :].astype(BF16)
    graw = p[:, OFF_GATES:OFF_GATES + HEAD_BLOCK][:, :N_GATES] + gbr_ref[...]
    lane = lax.broadcasted_iota(jnp.int32, (tm, N_GATES), 1)
    gr_ref[...] = jnp.where((lane // MLSTM_HEADS) % 2 == 1, _log_sigmoid(graw), graw)
    gt = pt[:N_GATES, :] + gbc_ref[...]
    sub = lax.broadcasted_iota(jnp.int32, (N_GATES, tm), 0)
    gc_ref[...] = jnp.where((sub // MLSTM_HEADS) % 2 == 1, _log_sigmoid(gt), gt)

    ind = ind_ref[...]
    cos = cos_ref[...]
    sin = sin_ref[...]
    lane_b = lax.broadcasted_iota(jnp.int32, (tm, HEAD_BLOCK), 1)
    first_half = (lane_b % ROPE_AXIS_DIM) < (ROPE_AXIS_DIM // 2)

    cqn = _rms(p[:, 0:MLA_Q_RANK], cqg_ref[...]).astype(BF16)
    ckvn = _rms(p[:, OFF_CKV:OFF_CKV + MLA_KV_RANK], ckvg_ref[...]).astype(BF16)
    nt_dims = (((1,), (1,)), ((), ()))
    k_raw = _dotf(ckvn, wuk_ref[...])
    vt = lax.dot_general(wuvt_ref[...], ckvn, nt_dims, preferred_element_type=F32).astype(BF16)
    ones_rows = jnp.where(lax.broadcasted_iota(jnp.int32, (V_ROWS - MLA_V, KEY_CHUNK), 0) == 0, 1.0, 0.0).astype(BF16)
    for cc in range(tm // KEY_CHUNK):
        for hh in range(MLA_HEADS):
            vt_ref[cc, V_ROWS * hh:V_ROWS * hh + MLA_V, :] = vt[MLA_V * hh:MLA_V * (hh + 1),
                                                                KEY_CHUNK * cc:KEY_CHUNK * (cc + 1)]
            vt_ref[cc, V_ROWS * hh + MLA_V:V_ROWS * (hh + 1), :] = ones_rows

    kn = k_raw * lax.rsqrt(_seg_mean(k_raw * k_raw, ind) + EPS) * kg_ref[...]
    kr = p[:, OFF_KR:OFF_KR + HEAD_BLOCK]
    krn = kr * lax.rsqrt(_seg_mean(kr * kr, ind[:HEAD_BLOCK, :HEAD_BLOCK]) + EPS) * krg_ref[...]
    kr_rot = _rope_block(krn, cos, sin, first_half)
    for hh in range(MLA_HEADS):
        sl = slice(HEAD_BLOCK * hh, HEAD_BLOCK * (hh + 1))
        k_ref[:, sl] = (kn[:, sl] + kr_rot).astype(BF16)

    qt_raw = lax.dot_general(wuqt_ref[...], cqn, nt_dims, preferred_element_type=F32)
    q_scale = (MLA_QK ** -0.5) * LOG2_E
    half = ROPE_AXIS_DIM // 2
    cost = cost_ref[...]
    sint = sint_ref[...]
    for hh in range(MLA_HEADS):
        r0 = HEAD_BLOCK * hh
        nope = qt_raw[r0:r0 + MLA_NOPE, :]
        rope = qt_raw[r0 + MLA_NOPE:r0 + MLA_QK, :]
        nope = nope * lax.rsqrt(jnp.mean(nope * nope, axis=0, keepdims=True) + EPS) * qgn_ref[...]
        rope = rope * lax.rsqrt(jnp.mean(rope * rope, axis=0, keepdims=True) + EPS) * qgr_ref[...]
        rot = []
        for ax in range(2):
            x1 = rope[ROPE_AXIS_DIM * ax:ROPE_AXIS_DIM * ax + half, :]
            x2 = rope[ROPE_AXIS_DIM * ax + half:ROPE_AXIS_DIM * (ax + 1), :]
            cs = cost[half * ax:half * (ax + 1), :]
            sn = sint[half * ax:half * (ax + 1), :]
            rot += [x1 * cs - x2 * sn, x2 * cs + x1 * sn]
        qt_ref[r0:r0 + MLA_NOPE, :] = (nope * q_scale).astype(BF16)
        qt_ref[r0 + MLA_NOPE:r0 + MLA_QK, :] = (jnp.concatenate(rot, axis=0) * q_scale).astype(BF16)
        qt_ref[r0 + MLA_QK:r0 + HEAD_BLOCK, :] = jnp.zeros((HEAD_BLOCK - MLA_QK, tm), BF16)


def _even_prep_into_kernel(*refs):
    n_in = 21
    _even_prep_kernel(*refs[:n_in], *refs[n_in + 2:])


def _even_prep(x2d, mods, row_fn, g, tabs, tab_fn, w, tm, batch, key_len, key_off, kv_bufs=None):
    n = x2d.shape[0]
    nt = n // batch // tm
    assert key_off % tm == 0 and tm % KEY_CHUNK == 0
    cos, sin, cost, sint = tabs
    tok = lambda width: pl.BlockSpec((tm, width), lambda i: (i, 0))
    tok_t = lambda rows: pl.BlockSpec((rows, tm), lambda i: (0, i))
    k_spec = pl.BlockSpec((None, tm, MLA_HEADS * HEAD_BLOCK), lambda i: (i // nt, key_off // tm + i % nt, 0))
    vt_spec = pl.BlockSpec((None, tm // KEY_CHUNK, MLA_HEADS * V_ROWS, KEY_CHUNK),
                           lambda i: (i // nt, key_off // tm + i % nt, 0, 0))
    out_shape = [
        jax.ShapeDtypeStruct((MLA_HEADS * HEAD_BLOCK, n), BF16),
        jax.ShapeDtypeStruct((batch, key_len, MLA_HEADS * HEAD_BLOCK), BF16),
        jax.ShapeDtypeStruct((batch, key_len // KEY_CHUNK, MLA_HEADS * V_ROWS, KEY_CHUNK), BF16),
        jax.ShapeDtypeStruct((MLSTM_WIDTH, n), BF16),
        jax.ShapeDtypeStruct((n, MLSTM_WIDTH), BF16),
        jax.ShapeDtypeStruct((MLSTM_WIDTH, n), BF16),
        jax.ShapeDtypeStruct((MLSTM_WIDTH, n), BF16),
        jax.ShapeDtypeStruct((n, N_GATES), F32),
        jax.ShapeDtypeStruct((N_GATES, n), F32),
    ]
    out_specs = [tok_t(1024), k_spec, vt_spec, tok_t(512), tok(512), tok_t(512), tok_t(512),
                 tok(N_GATES), tok_t(N_GATES)]
    in_specs = [
        tok(D_MODEL),
        pl.BlockSpec((None, 3, D_MODEL), lambda i: (row_fn(i), 0, 0)),
        _const_spec((1, D_MODEL)),
        pl.BlockSpec((tm, HEAD_BLOCK), lambda i: (tab_fn(i), 0)),
        pl.BlockSpec((tm, HEAD_BLOCK), lambda i: (tab_fn(i), 0)),
        pl.BlockSpec((ROPE_AXIS_DIM, tm), lambda i: (0, tab_fn(i))),
        pl.BlockSpec((ROPE_AXIS_DIM, tm), lambda i: (0, tab_fn(i))),
        _const_spec((D_MODEL, W_ALL)),
        _const_spec((W_T_ROWS, D_MODEL)),
        _const_spec((1, N_GATES)),
        _const_spec((N_GATES, 1)),
        _const_spec((1, MLA_Q_RANK)),
        _const_spec((1024, MLA_Q_RANK)),
        _const_spec((1, MLA_KV_RANK)),
        _const_spec((MLA_KV_RANK, 1024)),
        _const_spec((512, MLA_KV_RANK)),
        _const_spec((MLA_NOPE, 1)),
        _const_spec((MLA_ROPE, 1)),
        _const_spec((1, 1024)),
        _const_spec((1, HEAD_BLOCK)),
        _const_spec((2 * HEAD_BLOCK, 2 * HEAD_BLOCK)),
    ]
    args = [x2d, mods, g.reshape(1, D_MODEL), cos, sin, cost, sint, w["w_all"], w["w_gt"], w["gb_row"], w["gb_col"],
            w["cq_g"], w["w_uqt"], w["ckv_g"], w["w_uk"], w["w_uvt"], w["qg_nope"], w["qg_rope"], w["k_g"], w["kr_g"],
            w["ind"]]
    aliases = {}
    if kv_bufs is not None:
        aliases = {len(args): 1, len(args) + 1: 2}
        in_specs = in_specs + [pl.BlockSpec(memory_space=pl.ANY)] * 2
        args = args + list(kv_bufs)
    return pl.pallas_call(
        _even_prep_kernel if kv_bufs is None else _even_prep_into_kernel,
        grid=(n // tm,),
        in_specs=in_specs,
        out_specs=out_specs,
        out_shape=out_shape,
        input_output_aliases=aliases,
        compiler_params=_cparams("parallel"),
        name="even_prep",
    )(*args)


def _attn_kernel(qt_ref, k_ref, vt_ref, o_ref, st_scr, acc_scr):
    tq = qt_ref.shape[1]
    nk, _, tk = vt_ref.shape
    unroll = max(u for u in (16, 8, 4, 2) if (nk - 1) % u == 0)
    heads = range(2)

    def scores(j, slot):
        off = pl.multiple_of(j * tk, tk)
        cms = []
        for hh in heads:
            sl = slice(HEAD_BLOCK * hh, HEAD_BLOCK * (hh + 1))
            st = _dotf(k_ref[pl.ds(off, tk), sl], qt_ref[sl, :])
            st_scr[slot, hh] = st
            cms.append(jnp.max(st, axis=0, keepdims=True))
        return tuple(cms)

    def absorb(j, slot, cms, ms):
        out = []
        for hh in heads:
            m_new = jnp.maximum(ms[hh], cms[hh])
            alpha = jnp.exp2(ms[hh] - m_new)
            pt = jnp.exp2(st_scr[slot, hh] - m_new)
            pv = _dotf(vt_ref[j, V_ROWS * hh:V_ROWS * (hh + 1), :], pt.astype(BF16))
            acc_scr[hh] = alpha * acc_scr[hh] + pv
            out.append(m_new)
        return tuple(out)

    def body(i, carry):
        cms, ms = carry
        for r in range(unroll):
            j = unroll * i + r
            cms_next = scores(j + 1, (r + 1) % 2)
            ms = absorb(j, r % 2, cms, ms)
            cms = cms_next
        return cms, ms

    acc_scr[...] = jnp.zeros(acc_scr.shape, F32)
    m0 = jnp.full((1, tq), NEG_BIG, F32)
    cms, ms = lax.fori_loop(0, (nk - 1) // unroll, body, (scores(0, 0), (m0, m0)))
    absorb(nk - 1, 0, cms, ms)
    for hh in heads:
        acc = acc_scr[hh]
        o_ref[MLA_V * hh:MLA_V * (hh + 1), :] = (acc[:MLA_V, :] / acc[MLA_V:MLA_V + 1, :]).astype(o_ref.dtype)


def _attention(qt, k3, vt4, batch, tq, key_len, key_off):
    nq = qt.shape[1] // batch // tq
    assert key_off % key_len == 0 and key_len % KEY_CHUNK == 0
    kb = key_off // key_len
    nk = key_len // KEY_CHUNK
    tk = KEY_CHUNK
    return pl.pallas_call(
        _attn_kernel,
        grid=(batch, MLA_HEADS // 2, nq),
        in_specs=[
            pl.BlockSpec((2 * HEAD_BLOCK, tq), lambda b, hp, i: (hp, b * nq + i)),
            pl.BlockSpec((None, key_len, 2 * HEAD_BLOCK), lambda b, hp, i: (b, kb, hp)),
            pl.BlockSpec((None, nk, 2 * V_ROWS, tk), lambda b, hp, i: (b, kb, hp, 0)),
        ],
        out_specs=pl.BlockSpec((2 * MLA_V, tq), lambda b, hp, i: (hp, b * nq + i)),
        out_shape=jax.ShapeDtypeStruct((MLA_HEADS * MLA_V, qt.shape[1]), BF16),
        scratch_shapes=[pltpu.VMEM((2, 2, tk, tq), F32), pltpu.VMEM((2, V_ROWS, tq), F32)],
        compiler_params=_cparams("parallel", "parallel", "arbitrary"),
        name="mla_attention",
    )(qt, k3, vt4)


def _mlstm_kernel(qtf_ref, kf_ref, vtf_ref, grf_ref, gcf_ref, qtb_ref, kb_ref, vtb_ref, grb_ref, gcb_ref,
                  s0_ref, m0_ref, hf_ref, hb_ref, s_out_ref, m_out_ref, s_scr, m_scr):
    j = pl.program_id(1)
    nj = pl.num_programs(1)
    L = kf_ref.shape[0]

    @pl.when(j == 0)
    def _():
        s_scr[...] = s0_ref[...]
        m_scr[...] = m0_ref[...]

    row = lax.broadcasted_iota(jnp.int32, (L, L), 0)
    col = lax.broadcasted_iota(jnp.int32, (L, L), 1)
    sub = lax.broadcasted_iota(jnp.int32, (MLSTM_ST - MLSTM_DH, L), 0)

    dirs = ((qtf_ref, kf_ref, vtf_ref, grf_ref, gcf_ref, hf_ref), (qtb_ref, kb_ref, vtb_ref, grb_ref, gcb_ref, hb_ref))
    combos = [(d, hh) for d in range(2) for hh in range(MLSTM_HEADS)]


    masks, cum_rows, cum_cols, g_rows, g_cols = [], [], [], [], []
    for d in range(2):
        s_le_t = (row <= col) if d == 0 else (row >= col)
        tri = jnp.where((col <= row) if d == 0 else (col >= row), 1.0, 0.0).astype(BF16)
        tri_t = jnp.where(s_le_t, 1.0, 0.0).astype(BF16)
        gr = dirs[d][3][...]
        gc = dirs[d][4][...]
        r1, r2, r3 = _split3(gr)
        c1, c2, c3 = _split3(gc)
        masks.append(s_le_t)
        g_rows.append(gr)
        g_cols.append(gc)
        cum_rows.append(_dotf(tri, r1) + _dotf(tri, r2) + _dotf(tri, r3))
        cum_cols.append(_dotf(c1, tri_t) + _dotf(c2, tri_t) + _dotf(c3, tri_t))
    kq, inter, s_prev, m_prev = [], [], [], []
    for d, hh in combos:
        idx = d * MLSTM_HEADS + hh
        sl = slice(MLSTM_DH * hh, MLSTM_DH * (hh + 1))
        qth = dirs[d][0][sl, :]
        s_prev.append(s_scr[idx])
        m_prev.append(m_scr[idx:idx + 1, 0:1])
        kq.append(_dotf(dirs[d][1][:, sl], qth))
        inter.append(_dotf(s_prev[-1].astype(BF16), qth))

    sw, den, w_inter, floor, vw, decay, m_new = [], [], [], [], [], [], []
    for c, (d, hh) in enumerate(combos):
        il = (0 if d == 0 else 2 * MLSTM_HEADS) + hh
        fl = il + MLSTM_HEADS
        sl = slice(MLSTM_DH * hh, MLSTM_DH * (hh + 1))
        c_col = g_rows[d][:, il:il + 1] - cum_rows[d][:, fl:fl + 1]
        b_row = cum_cols[d][fl:fl + 1, :]
        c_row = g_cols[d][il:il + 1, :] - b_row
        c_mat = jnp.where(masks[d], c_col, NEG_BIG)
        u = jnp.maximum(jnp.max(c_mat, axis=0, keepdims=True), m_prev[c])
        s = kq[c] * jnp.exp(c_mat - u)
        sw.append(s.astype(BF16))
        den.append(jnp.sum(s, axis=0, keepdims=True))
        w_inter.append(jnp.exp(m_prev[c] - u))
        floor.append(jnp.exp(-(b_row + u)))
        b_end = b_row[:, L - 1:L] if d == 0 else b_row[:, 0:1]
        g_row = b_end + c_row
        m_new.append(jnp.maximum(b_end + m_prev[c], jnp.max(g_row, axis=-1, keepdims=True)))
        decay.append(jnp.exp(b_end + m_prev[c] - m_new[c]))
        w_s = jnp.exp(g_row - m_new[c])
        n_rows = jnp.where(sub == 0, w_s, 0.0)
        vw.append(jnp.concatenate([dirs[d][2][sl, :].astype(F32) * w_s, n_rows], axis=0).astype(BF16))

    for c, (d, hh) in enumerate(combos):
        idx = d * MLSTM_HEADS + hh
        sl = slice(MLSTM_DH * hh, MLSTM_DH * (hh + 1))
        num = _dotf(dirs[d][2][sl, :], sw[c]) + w_inter[c] * inter[c][:MLSTM_DH, :]
        dn = den[c] + w_inter[c] * inter[c][MLSTM_DH:MLSTM_DH + 1, :]
        dirs[d][5][sl, :] = (num / jnp.maximum(jnp.abs(dn), floor[c])).astype(BF16)
        s_scr[idx] = decay[c] * s_prev[c] + _dotf(vw[c], dirs[d][1][:, sl])
        m_scr[idx:idx + 1, :] = jnp.broadcast_to(m_new[c], (1, MLSTM_DH))

    @pl.when(j == nj - 1)
    def _():
        s_out_ref[...] = s_scr[...]
        m_out_ref[...] = m_scr[...]


def _mlstm(mqt, mk, mvt, g_rows, g_cols, s0, m0, batch, chunk):
    n = mk.shape[0]
    nc = n // batch // chunk
    fwd = lambda b, j: (b * nc + j, 0)
    bwd = lambda b, j: (b * nc + (nc - 1 - j), 0)
    fwd_t = lambda b, j: (0, b * nc + j)
    bwd_t = lambda b, j: (0, b * nc + (nc - 1 - j))
    nst = 2 * MLSTM_HEADS
    tok = lambda fn: pl.BlockSpec((chunk, MLSTM_WIDTH), fn)
    tok_t = lambda fn: pl.BlockSpec((MLSTM_WIDTH, chunk), fn)
    state_s = pl.BlockSpec((None, nst, MLSTM_ST, MLSTM_DH), lambda b, j: (b, 0, 0, 0))
    state_m = pl.BlockSpec((None, nst, MLSTM_DH), lambda b, j: (b, 0, 0))
    return pl.pallas_call(
        _mlstm_kernel,
        grid=(batch, nc),
        in_specs=[tok_t(fwd_t), tok(fwd), tok_t(fwd_t), pl.BlockSpec((chunk, N_GATES), fwd),
                  pl.BlockSpec((N_GATES, chunk), fwd_t),
                  tok_t(bwd_t), tok(bwd), tok_t(bwd_t), pl.BlockSpec((chunk, N_GATES), bwd),
                  pl.BlockSpec((N_GATES, chunk), bwd_t),
                  state_s, state_m],
        out_specs=[tok_t(fwd_t), tok_t(bwd_t), state_s, state_m],
        out_shape=[jax.ShapeDtypeStruct((MLSTM_WIDTH, n), BF16), jax.ShapeDtypeStruct((MLSTM_WIDTH, n), BF16),
                   jax.ShapeDtypeStruct(s0.shape, F32), jax.ShapeDtypeStruct(m0.shape, F32)],
        scratch_shapes=[pltpu.VMEM((nst, MLSTM_ST, MLSTM_DH), F32), pltpu.VMEM((nst, MLSTM_DH), F32)],
        compiler_params=_cparams("parallel", "arbitrary"),
        name="mlstm",
    )(mqt, mk, mvt, g_rows, g_cols, mqt, mk, mvt, g_rows, g_cols, s0, m0)


def _even_out_kernel(x_ref, gate_ref, att_ref, hf_ref, hb_ref, mo_ref, og_ref, w_ref,
                     fmod_ref, fg_ref, fwg_ref, fwu_ref, fwd_ref, o_ref):
    hs = hf_ref[...].astype(F32) + hb_ref[...].astype(F32)
    og = og_ref[...]
    parts = [att_ref[...]]
    for hh in range(MLSTM_HEADS):
        sl = slice(MLSTM_DH * hh, MLSTM_DH * (hh + 1))
        blk = hs[sl, :]
        hn = blk * lax.rsqrt(jnp.mean(blk * blk, axis=0, keepdims=True) + EPS) * og[sl, :]
        parts.append((hn * jax.nn.sigmoid(mo_ref[sl, :].astype(F32))).astype(BF16))
    mixed = jnp.concatenate(parts, axis=0)
    y = lax.dot_general(mixed, w_ref[...], (((0,), (0,)), ((), ())), preferred_element_type=F32)
    o_ref[...] = _ffn_math(x_ref[...] + gate_ref[...] * y, fmod_ref, fg_ref, fwg_ref, fwu_ref, fwd_ref)


def _even_out(x2d, gates, row_fn, att, hf, hb, mo, out_g, w_out, ffn_args, tm):
    n = x2d.shape[0]
    tok = pl.BlockSpec((tm, D_MODEL), lambda i: (i, 0))
    tok_t = pl.BlockSpec((MLSTM_WIDTH, tm), lambda i: (0, i))
    return pl.pallas_call(
        _even_out_kernel,
        grid=(n // tm,),
        in_specs=[tok, pl.BlockSpec((None, 1, D_MODEL), lambda i: (row_fn(i), 0, 0)),
                  tok_t, tok_t, tok_t, tok_t, _const_spec((MLSTM_WIDTH, 1)), _const_spec((D_MODEL, D_MODEL))]
        + _ffn_specs(row_fn),
        out_specs=tok,
        out_shape=jax.ShapeDtypeStruct((n, D_MODEL), F32),
        compiler_params=_cparams("parallel"),
        name="even_out_ffn",
    )(x2d, gates, att, hf, hb, mo, out_g.reshape(MLSTM_WIDTH, 1), w_out, *ffn_args)


def _gelu_tanh(x):
    return 0.5 * x * (1.0 + jnp.tanh(0.7978845608028654 * (x + 0.044715 * (x * x * x))))


def _sigmoid(x):
    return 0.5 * jnp.tanh(0.5 * x) + 0.5


def _odd_in_kernel(x_ref, xp_ref, xn_ref, mod_ref, g_ref, w_ref, cw_ref, cb_ref, gate_ref, xc_ref, ext_scr, *, nt):
    tm = x_ref.shape[0]
    ti = pl.program_id(0) % nt
    g, shift, scale = g_ref[...], mod_ref[0:1, :], mod_ref[1:2, :]
    halo = jnp.concatenate([xp_ref[...], xn_ref[...]], axis=0)
    ph = _dotf(_norm_mod(halo, g, shift, scale).astype(BF16), w_ref[:, RNN_WIDTH:])
    ext_scr[0:8, :] = jnp.where(ti > 0, ph[0:8, :], 0.0)
    ext_scr[8 + tm:16 + tm, :] = jnp.where(ti < nt - 1, ph[8:16, :], 0.0)
    cw = cw_ref[...]
    cb = cb_ref[...]
    rc = tm // ODD_IN_CHUNKS

    def project(c):
        rows = slice(rc * c, rc * (c + 1))
        p = _dotf(_norm_mod(x_ref[rows, :], g, shift, scale).astype(BF16), w_ref[...])
        gate_ref[rows, :] = _gelu_tanh(p[:, :RNN_WIDTH]).astype(BF16)
        ext_scr[8 + rc * c:8 + rc * (c + 1), :] = p[:, RNN_WIDTH:]

    def conv(c):
        win = ext_scr[rc * c:rc * (c + 1) + 16, :]
        xc = cb + cw[2:3, :] * win[8:8 + rc, :]
        for k in (0, 1, 3):
            xc = xc + cw[k:k + 1, :] * pltpu.roll(win, (2 - k) % (rc + 16), axis=0)[8:8 + rc, :]
        xc_ref[rc * c:rc * (c + 1), :] = xc

    project(0)
    for c in range(1, ODD_IN_CHUNKS):
        project(c)
        conv(c - 1)
    conv(ODD_IN_CHUNKS - 1)


def _odd_in(x2d, mods, row_fn, g, w_in, conv_w, conv_b, batch, tm):
    n = x2d.shape[0]
    nt = n // batch // tm
    t8 = tm // 8
    tok = pl.BlockSpec((tm, D_MODEL), lambda i: (i, 0))
    prev = pl.BlockSpec((8, D_MODEL), lambda i: (jnp.maximum(i * t8 - 1, 0), 0))
    nxt = pl.BlockSpec((8, D_MODEL), lambda i: (jnp.minimum((i + 1) * t8, n // 8 - 1), 0))
    return pl.pallas_call(
        functools.partial(_odd_in_kernel, nt=nt),
        grid=(n // tm,),
        in_specs=[tok, prev, nxt, pl.BlockSpec((None, 3, D_MODEL), lambda i: (row_fn(i), 0, 0)),
                  _const_spec((1, D_MODEL)), _const_spec((D_MODEL, 2 * RNN_WIDTH)),
                  _const_spec((CONV_W, RNN_WIDTH)), _const_spec((1, RNN_WIDTH))],
        out_specs=[tok, tok],
        out_shape=[jax.ShapeDtypeStruct((n, RNN_WIDTH), BF16), jax.ShapeDtypeStruct((n, RNN_WIDTH), F32)],
        scratch_shapes=[pltpu.VMEM((tm + 16, RNN_WIDTH), F32)],
        compiler_params=_cparams("parallel"),
        name="odd_in",
    )(x2d, x2d, x2d, mods, g.reshape(1, D_MODEL), w_in, conv_w, conv_b)


def _rglru_kernel(xc_ref, wax_ref, ba_ref, bx_ref, lam_ref, h0_ref, h_ref, ht_ref, a_scr, b_scr, carry_scr, *, nt):
    d = pl.program_id(0)
    i = pl.program_id(2)
    tm = xc_ref.shape[0]
    nb8 = tm // 8

    xc = xc_ref[...]
    neg_lam = -lam_ref[...]
    decay_rate = -LRU_C * (jnp.maximum(neg_lam, 0.0) + jnp.log1p(jnp.exp(-jnp.abs(neg_lam))))
    ba = ba_ref[...]
    bx = bx_ref[...]
    for nb in range(RNN_BLOCKS):
        sl = slice(RNN_BLOCK_DIM * nb, RNN_BLOCK_DIM * (nb + 1))
        u = xc[:, sl]
        rg = _dotf(u.astype(BF16), wax_ref[nb])
        r = _sigmoid(rg[:, :RNN_BLOCK_DIM] + ba[:, sl])
        ig = _sigmoid(rg[:, RNN_BLOCK_DIM:] + bx[:, sl])
        log_a = r * decay_rate[:, sl]
        a = jnp.exp(log_a)
        a_scr[:, sl] = a
        b_scr[:, sl] = jnp.sqrt(-jnp.tanh(log_a) * (a * a + 1.0)) * (ig * u)

    @pl.when(i == 0)
    def _():
        carry_scr[...] = h0_ref[...]

    def scan_tile(reverse):
        a = a_scr[...].reshape(nb8, 8, RNN_WIDTH)
        b = b_scr[...].reshape(nb8, 8, RNN_WIDTH)
        r8 = lax.broadcasted_iota(jnp.int32, (nb8, 8, RNN_WIDTH), 1)
        for s in (1, 2, 4):
            keep = (r8 < 8 - s) if reverse else (r8 >= s)
            shift = 8 - s if reverse else s
            a_in = jnp.where(keep, pltpu.roll(a, shift, axis=1), 1.0)
            b_in = jnp.where(keep, pltpu.roll(b, shift, axis=1), 0.0)
            b = a * b_in + b
            a = a * a_in
        a_scr[...] = a.reshape(tm, RNN_WIDTH)
        b_scr[...] = b.reshape(tm, RNN_WIDTH)

        def block_pair(jp, h):
            base = pl.multiple_of((nb8 // 2 - 1 - jp if reverse else jp) * 16, 16)
            halves = [None, None]
            for half in ((1, 0) if reverse else (0, 1)):
                rows = pl.ds(base + 8 * half, 8)
                hb = a_scr[rows, :] * h + b_scr[rows, :]
                halves[half] = hb
                h = hb[0:1, :] if reverse else hb[7:8, :]
            h_ref[pl.ds(base, 16), :] = jnp.concatenate(halves, axis=0).astype(h_ref.dtype)
            return h

        carry_scr[...] = lax.fori_loop(0, nb8 // 2, block_pair, carry_scr[...], unroll=2)

    @pl.when(d == 0)
    def _():
        scan_tile(False)

    @pl.when(d == 1)
    def _():
        scan_tile(True)

    @pl.when(i == nt - 1)
    def _():
        ht_ref[...] = carry_scr[...]


def _rglru(xc2d, h0, w, batch, tm):
    n = xc2d.shape[0]
    nt = n // batch // tm
    tile = lambda d, i: jnp.where(d == 0, i, nt - 1 - i)
    per_dir = lambda shape: pl.BlockSpec((None,) + shape, lambda d, b, i: (d,) + (0,) * len(shape))
    state = pl.BlockSpec((None, None, 1, RNN_WIDTH), lambda d, b, i: (d, b, 0, 0))
    return pl.pallas_call(
        functools.partial(_rglru_kernel, nt=nt),
        grid=(2, batch, nt),
        in_specs=[pl.BlockSpec((tm, RNN_WIDTH), lambda d, b, i: (b * nt + tile(d, i), 0)),
                  per_dir((RNN_BLOCKS, RNN_BLOCK_DIM, 2 * RNN_BLOCK_DIM)), per_dir((1, RNN_WIDTH)), per_dir((1, RNN_WIDTH)),
                  per_dir((1, RNN_WIDTH)), state],
        out_specs=[pl.BlockSpec((None, tm, RNN_WIDTH), lambda d, b, i: (d, b * nt + tile(d, i), 0)), state],
        out_shape=[jax.ShapeDtypeStruct((2, n, RNN_WIDTH), BF16), jax.ShapeDtypeStruct((2, batch, 1, RNN_WIDTH), F32)],
        scratch_shapes=[pltpu.VMEM((tm, RNN_WIDTH), F32), pltpu.VMEM((tm, RNN_WIDTH), F32),
                        pltpu.VMEM((1, RNN_WIDTH), F32)],
        compiler_params=_cparams("parallel", "parallel", "arbitrary"),
        name="rglru",
    )(xc2d, w["w_ax"], w["b_a"], w["b_x"], w["lam"], h0)


def _odd_out_kernel(x_ref, gate_ref, h_ref, gg_ref, w_ref, fmod_ref, fg_ref, fwg_ref, fwu_ref, fwd_ref, o_ref):
    y = ((h_ref[0].astype(F32) + h_ref[1].astype(F32)) * gg_ref[...].astype(F32)).astype(BF16)
    x_mid = x_ref[...] + gate_ref[...] * _dotf(y, w_ref[...])
    o_ref[...] = _ffn_math(x_mid, fmod_ref, fg_ref, fwg_ref, fwu_ref, fwd_ref)


def _odd_out(x2d, gates, row_fn, h2, gelu_gate, w_out, ffn_args, tm):
    n = x2d.shape[0]
    tok = pl.BlockSpec((tm, D_MODEL), lambda i: (i, 0))
    return pl.pallas_call(
        _odd_out_kernel,
        grid=(n // tm,),
        in_specs=[tok, pl.BlockSpec((None, 1, D_MODEL), lambda i: (row_fn(i), 0, 0)),
                  pl.BlockSpec((2, tm, RNN_WIDTH), lambda i: (0, i, 0)), tok, _const_spec((RNN_WIDTH, D_MODEL))]
        + _ffn_specs(row_fn),
        out_specs=tok,
        out_shape=jax.ShapeDtypeStruct((n, D_MODEL), F32),
        compiler_params=_cparams("parallel"),
        name="odd_out_ffn",
    )(x2d, gates, h2, gelu_gate, w_out, *ffn_args)


def _rope_tables(seq):
    t = jnp.arange(seq)
    inv = ROPE_BASE ** (-jnp.arange(0, ROPE_AXIS_DIM, 2, dtype=F32) / ROPE_AXIS_DIM)
    ang_r = (t // GRID_W).astype(F32)[:, None] * inv
    ang_c = (t % GRID_W).astype(F32)[:, None] * inv
    cos32 = jnp.concatenate([jnp.cos(ang_r), jnp.cos(ang_r), jnp.cos(ang_c), jnp.cos(ang_c)], axis=1)
    sin32 = jnp.concatenate([-jnp.sin(ang_r), jnp.sin(ang_r), -jnp.sin(ang_c), jnp.sin(ang_c)], axis=1)
    pad = HEAD_BLOCK - MLA_QK
    cos = jnp.concatenate([jnp.ones((seq, MLA_NOPE), F32), cos32, jnp.ones((seq, pad), F32)], axis=1)
    sin = jnp.concatenate([jnp.zeros((seq, MLA_NOPE), F32), sin32, jnp.zeros((seq, pad), F32)], axis=1)
    cost = jnp.concatenate([jnp.cos(ang_r), jnp.cos(ang_c)], axis=1).T
    sint = jnp.concatenate([jnp.sin(ang_r), jnp.sin(ang_c)], axis=1).T
    return cos, sin, cost, sint


def _even_weights(w_in, cq_g, w_uq, ckv_g, w_ukv, q_g, k_g, gate_b):
    pad = HEAD_BLOCK - MLA_QK
    kr_cols = w_in[:, OFF_KR:OFF_KR + MLA_ROPE]
    w_kr = jnp.pad(kr_cols, ((0, 0), (MLA_NOPE, pad)))
    m_cols = w_in[:, OFF_KR + MLA_ROPE:OFF_KR + MLA_ROPE + 4 * MLSTM_WIDTH]
    g_cols = w_in[:, OFF_KR + MLA_ROPE + 4 * MLSTM_WIDTH:]
    mq_cols, mk_cols, mv_cols, mo_cols = jnp.split(m_cols, 4, axis=1)
    w_all = jnp.concatenate([w_in[:, :OFF_KR], w_kr, mk_cols, jnp.pad(g_cols, ((0, 0), (0, HEAD_BLOCK - N_GATES)))], axis=1)
    uq = jnp.pad(w_uq.reshape(MLA_Q_RANK, MLA_HEADS, MLA_QK), ((0, 0), (0, 0), (0, pad)))
    ukv = w_ukv.reshape(MLA_KV_RANK, MLA_HEADS, MLA_NOPE + MLA_V)
    uk = jnp.pad(ukv[:, :, :MLA_NOPE], ((0, 0), (0, 0), (0, HEAD_BLOCK - MLA_NOPE)))
    uv = ukv[:, :, MLA_NOPE:]
    kg_blk = jnp.pad(k_g[:MLA_NOPE], (0, HEAD_BLOCK - MLA_NOPE))
    krg_blk = jnp.pad(k_g[MLA_NOPE:], (MLA_NOPE, pad))
    seg = jnp.arange(HEAD_BLOCK)
    seg_id = jnp.where(seg < MLA_NOPE, 0, jnp.where(seg < MLA_QK, 1, 2))
    seg_len = jnp.where(seg < MLA_NOPE, float(MLA_NOPE), float(MLA_ROPE))
    ind128 = jnp.where((seg_id[:, None] == seg_id[None, :]) & (seg_id[:, None] < 2), 1.0 / seg_len[None, :], 0.0)
    zero = jnp.zeros_like(ind128)
    ind = jnp.block([[ind128, zero], [zero, ind128]])
    return {
        "w_all": w_all.astype(BF16),
        "w_gt": jnp.concatenate([g_cols, mq_cols, mv_cols, mo_cols], axis=1).T.astype(BF16),
        "gb_row": gate_b.reshape(1, N_GATES),
        "gb_col": gate_b.reshape(N_GATES, 1),
        "cq_g": cq_g.reshape(1, MLA_Q_RANK),
        "w_uqt": uq.reshape(MLA_Q_RANK, MLA_HEADS * HEAD_BLOCK).T.astype(BF16),
        "ckv_g": ckv_g.reshape(1, MLA_KV_RANK),
        "w_uk": uk.reshape(MLA_KV_RANK, MLA_HEADS * HEAD_BLOCK).astype(BF16),
        "w_uvt": uv.reshape(MLA_KV_RANK, MLA_HEADS * MLA_V).T.astype(BF16),
        "qg_nope": q_g[:MLA_NOPE].reshape(MLA_NOPE, 1),
        "qg_rope": q_g[MLA_NOPE:].reshape(MLA_ROPE, 1),
        "k_g": jnp.tile(kg_blk, MLA_HEADS).reshape(1, -1),
        "kr_g": krg_blk.reshape(1, HEAD_BLOCK),
        "ind": ind.astype(BF16),
    }


def kernel(x, c, ctx, c_ctx, mod_w, mod_b, norm_g, ffn_w_gate, ffn_w_up, ffn_w_down, even_w_in, even_w_out, mla_cq_g, mla_w_uq, mla_ckv_g, mla_w_ukv, mla_q_g, mla_k_g, mlstm_gate_b, mlstm_out_g, odd_w_in, odd_conv_w, odd_conv_b, lru_w_a, lru_b_a, lru_w_x, lru_b_x, lru_lam, odd_w_out):
    batch, seq, _ = x.shape
    ctx_len = ctx.shape[1]
    depth = mod_w.shape[0]
    assert depth == 2 and batch <= 7

    cond8 = jnp.zeros((8, D_MODEL), F32).at[:batch].set(c).at[batch].set(c_ctx)
    mods = _ada_params(cond8, mod_w, mod_b)
    mods = mods.reshape(depth, 8, 3, 3, D_MODEL).transpose(0, 2, 1, 3, 4)

    tm_x, tm_c = 512, 256
    x_row = lambda i: i // (seq // tm_x)
    c_row = lambda i: batch
    x2 = x.reshape(batch * seq, D_MODEL)
    c2 = ctx.reshape(batch * ctx_len, D_MODEL)

    wg = ffn_w_gate.astype(BF16)
    wu = ffn_w_up.astype(BF16)
    wd = ffn_w_down.astype(BF16)

    def ffn_args(layer, which):
        return (mods[layer, 2 * which], norm_g[layer, 2 * which].reshape(1, D_MODEL), wg[layer, which],
                wu[layer, which], wd[layer, which])

    x2 = _ffn(x2, ffn_args(0, 0), x_row, tm_x)
    c2 = _ffn(c2, ffn_args(0, 0), c_row, tm_c)

    ew = _even_weights(even_w_in[0], mla_cq_g[0], mla_w_uq[0], mla_ckv_g[0], mla_w_ukv[0], mla_q_g[0], mla_k_g[0],
                       mlstm_gate_b[0])
    tabs_x = _rope_tables(seq)
    tabs_c = (jnp.ones((tm_c, HEAD_BLOCK), F32), jnp.zeros((tm_c, HEAD_BLOCK), F32),
              jnp.ones((ROPE_AXIS_DIM, tm_c), F32), jnp.zeros((ROPE_AXIS_DIM, tm_c), F32))
    key_len = seq + ctx_len
    px = _even_prep(x2, mods[0, 1], x_row, norm_g[0, 1], tabs_x, lambda i: i % (seq // tm_x), ew, tm_x,
                    batch, key_len, 0)
    qt_x, k_all, vt_all, mqt_x, mk_x, mvt_x, mot_x, gr_x, gc_x = px
    pc = _even_prep(c2, mods[0, 1], c_row, norm_g[0, 1], tabs_c, lambda i: 0, ew, tm_c,
                    batch, key_len, seq, kv_bufs=(k_all, vt_all))
    qt_c, k_all, vt_all, mqt_c, mk_c, mvt_c, mot_c, gr_c, gc_c = pc

    att_x = _attention(qt_x, k_all, vt_all, batch, 512, key_len, 0)
    att_c = _attention(qt_c, k_all, vt_all, batch, 256, ctx_len, seq)

    chunk = 128
    nst = 2 * MLSTM_HEADS
    s0 = jnp.zeros((batch, nst, MLSTM_ST, MLSTM_DH), F32)
    m0 = jnp.zeros((batch, nst, MLSTM_DH), F32)
    hcf, hcb, s_c, m_c = _mlstm(mqt_c, mk_c, mvt_c, gr_c, gc_c, s0, m0, batch, chunk)
    hxf, hxb, _, _ = _mlstm(mqt_x, mk_x, mvt_x, gr_x, gc_x, s_c, m_c, batch, chunk)

    w_out0 = even_w_out[0].astype(BF16)
    gate0 = mods[0, 1][:, 2:3, :]
    x2 = _even_out(x2, gate0, x_row, att_x, hxf, hxb, mot_x, mlstm_out_g[0], w_out0, ffn_args(0, 1), tm_x)
    c2 = _even_out(c2, gate0, c_row, att_c, hcf, hcb, mot_c, mlstm_out_g[0], w_out0, ffn_args(0, 1), tm_c)

    x2 = _ffn(x2, ffn_args(1, 0), x_row, tm_x)
    c2 = _ffn(c2, ffn_args(1, 0), c_row, tm_c)
    w_in1 = odd_w_in[0].astype(BF16)
    conv_b = odd_conv_b[0].reshape(1, RNN_WIDTH)
    gg_x, xc_x = _odd_in(x2, mods[1, 1], x_row, norm_g[1, 1], w_in1, odd_conv_w[0], conv_b, batch, tm_x)
    _, xc_c = _odd_in(c2, mods[1, 1], c_row, norm_g[1, 1], w_in1, odd_conv_w[0], conv_b, batch, tm_c)
    rw = {
        "w_ax": jnp.concatenate([lru_w_a[0], lru_w_x[0]], axis=-1).astype(BF16),
        "b_a": lru_b_a[0].reshape(2, 1, RNN_WIDTH),
        "b_x": lru_b_x[0].reshape(2, 1, RNN_WIDTH),
        "lam": lru_lam[0].reshape(2, 1, RNN_WIDTH),
    }
    h0 = jnp.zeros((2, batch, 1, RNN_WIDTH), F32)
    _, st_c = _rglru(xc_c, h0, rw, batch, 256)
    h_x, _ = _rglru(xc_x, st_c, rw, batch, 256)
    x2 = _odd_out(x2, mods[1, 1][:, 2:3, :], x_row, h_x, gg_x, odd_w_out[0].astype(BF16), ffn_args(1, 1), tm_x)
    return x2.reshape(batch, seq, D_MODEL)
```

```python
import functools

import jax
import jax.numpy as jnp
import numpy as np
from jax import lax
from jax.experimental import pallas as pl
from jax.experimental.pallas import tpu as pltpu

F32 = jnp.float32
BF16 = jnp.bfloat16

D_MODEL = 1024
GRID_W = 64
EPS = 1e-6
N_MOD = 9
D_FF = 2816
FFN_RESIDUAL = 0.5

MLA_HEADS = 8
MLA_Q_RANK = 384
MLA_KV_RANK = 256
MLA_NOPE = 64
MLA_ROPE = 32
MLA_V = 64
MLA_QK = MLA_NOPE + MLA_ROPE
ROPE_AXIS_DIM = MLA_ROPE // 2
ROPE_BASE = 10000.0
HEAD_BLOCK = 128
KEY_CHUNK = 256
V_ROWS = MLA_V + 16

MLSTM_HEADS = 4
MLSTM_DH = 128
MLSTM_WIDTH = MLSTM_HEADS * MLSTM_DH
N_GATES = 4 * MLSTM_HEADS
MLSTM_ST = MLSTM_DH + 16

RNN_WIDTH = 1024
RNN_BLOCKS = 8
RNN_BLOCK_DIM = RNN_WIDTH // RNN_BLOCKS
CONV_W = 4
LRU_C = 8.0
ODD_IN_CHUNKS = 4

NEG_BIG = -1e30
LOG2_E = 1.4426950408889634
VMEM_LIMIT = 56 * 1024 * 1024


def _cparams(*sem):
    return pltpu.CompilerParams(dimension_semantics=sem, vmem_limit_bytes=VMEM_LIMIT)


def _const_spec(shape):
    zeros = (0,) * len(shape)
    return pl.BlockSpec(shape, lambda *_: zeros, pipeline_mode=pl.Buffered(1))


def _dotf(a, b):
    return jnp.dot(a, b, preferred_element_type=F32)


def _split3(a):
    a1 = a.astype(BF16)
    r1 = a - a1.astype(F32)
    a2 = r1.astype(BF16)
    a3 = (r1 - a2.astype(F32)).astype(BF16)
    return a1, a2, a3


def _norm_mod(x, g, shift, scale):
    ms = jnp.mean(x * x, axis=-1, keepdims=True)
    return (x * lax.rsqrt(ms + EPS)) * (g * (1.0 + scale)) + shift


def _rms(x, g):
    ms = jnp.mean(x * x, axis=-1, keepdims=True)
    return x * lax.rsqrt(ms + EPS) * g


def _log_sigmoid(x):
    return jnp.minimum(x, 0.0) - jnp.log1p(jnp.exp(-jnp.abs(x)))


def _ada_kernel(cond_ref, w_ref, b_ref, o_ref):
    c = cond_ref[...]
    s = c * jax.nn.sigmoid(c)
    s1, s2, _ = _split3(s)
    w = w_ref[...]
    w1 = w.astype(BF16)
    w2 = (w - w1.astype(F32)).astype(BF16)
    o_ref[...] = _dotf(s1, w1) + _dotf(s1, w2) + _dotf(s2, w1) + b_ref[...]


def _ada_params(cond8, mod_w, mod_b):
    depth, _, n = mod_w.shape
    tn = 1152
    return pl.pallas_call(
        _ada_kernel,
        grid=(depth, n // tn),
        in_specs=[
            pl.BlockSpec((8, D_MODEL), lambda l, j: (0, 0)),
            pl.BlockSpec((None, D_MODEL, tn), lambda l, j: (l, 0, j)),
            pl.BlockSpec((None, 1, tn), lambda l, j: (l, 0, j)),
        ],
        out_specs=pl.BlockSpec((None, 8, tn), lambda l, j: (l, 0, j)),
        out_shape=jax.ShapeDtypeStruct((depth, 8, n), F32),
        compiler_params=_cparams("parallel", "parallel"),
        name="ada_params",
    )(cond8, mod_w, mod_b.reshape(depth, 1, n))


def _ffn_math(x, mod_ref, g_ref, wg_ref, wu_ref, wd_ref):
    h = _norm_mod(x, g_ref[...], mod_ref[0:1, :], mod_ref[1:2, :]).astype(BF16)
    g = _dotf(h, wg_ref[...])
    u = _dotf(h, wu_ref[...])
    a = (g * jax.nn.sigmoid(g) * u).astype(BF16)
    return x + (FFN_RESIDUAL * mod_ref[2:3, :]) * _dotf(a, wd_ref[...])


def _ffn_specs(row_fn, layer, which):
    weight = lambda r, c: pl.BlockSpec((None, None, r, c), lambda i: (layer, which, 0, 0), pipeline_mode=pl.Buffered(1))
    return [pl.BlockSpec((None, 3, D_MODEL), lambda i: (row_fn(i), 0, 0)), _const_spec((1, D_MODEL)),
            weight(D_MODEL, D_FF), weight(D_MODEL, D_FF), weight(D_FF, D_MODEL)]


def _ffn_kernel(x_ref, mod_ref, g_ref, wg_ref, wu_ref, wd_ref, o_ref):
    o_ref[...] = _ffn_math(x_ref[...], mod_ref, g_ref, wg_ref, wu_ref, wd_ref)


def _ffn(x2d, ffn_args, row_fn, tm):
    n = x2d.shape[0]
    tok = pl.BlockSpec((tm, D_MODEL), lambda i: (i, 0))
    return pl.pallas_call(
        _ffn_kernel,
        grid=(n // tm,),
        in_specs=[tok] + _ffn_specs(row_fn, *ffn_args[0]),
        out_specs=tok,
        out_shape=jax.ShapeDtypeStruct((n, D_MODEL), F32),
        compiler_params=_cparams("parallel"),
        name="ffn",
    )(x2d, *ffn_args[1:])


W_ALL = MLA_Q_RANK + MLA_KV_RANK + HEAD_BLOCK + MLSTM_WIDTH + HEAD_BLOCK
OFF_CKV = MLA_Q_RANK
OFF_KR = OFF_CKV + MLA_KV_RANK
OFF_MK = OFF_KR + HEAD_BLOCK
OFF_GATES = OFF_MK + MLSTM_WIDTH
W_T_ROWS = N_GATES + 3 * MLSTM_WIDTH


def _rope_block(xb, cos, sin, first_half):
    partner = jnp.where(first_half, pltpu.roll(xb, HEAD_BLOCK - 8, axis=1), pltpu.roll(xb, 8, axis=1))
    return xb * cos + partner * sin


def _even_prep_kernel(x_ref, mod_ref, g_ref, cos_ref, sin_ref, cost_ref, sint_ref, wall_ref, wgt_ref, gbr_ref, gbc_ref,
                      cqg_ref, wuqt_ref, ckvg_ref, wuk_ref, wuvt_ref, qgn_ref, qgr_ref, kg_ref, krg_ref,
                      qt_ref, k_ref, vt_ref, mqt_ref, mk_ref, mvt_ref, mot_ref, gr_ref, gc_ref):
    tm = x_ref.shape[0]
    x = x_ref[...]
    h = _norm_mod(x, g_ref[...], mod_ref[0:1, :], mod_ref[1:2, :]).astype(BF16)
    p = _dotf(h, wall_ref[...])
    pt = lax.dot_general(wgt_ref[...], h, (((1,), (1,)), ((), ())), preferred_element_type=F32)

    mk_ref[...] = (p[:, OFF_MK:OFF_MK + MLSTM_WIDTH] * (MLSTM_DH ** -0.5)).astype(BF16)
    mqt_ref[...] = pt[N_GATES:N_GATES + MLSTM_WIDTH, :].astype(BF16)
    mvt_ref[...] = pt[N_GATES + MLSTM_WIDTH:N_GATES + 2 * MLSTM_WIDTH, :].astype(BF16)
    mot_ref[...] = pt[N_GATES + 2 * MLSTM_WIDTH:, :].astype(BF16)
    graw = p[:, OFF_GATES:OFF_GATES + HEAD_BLOCK][:, :N_GATES] + gbr_ref[...]
    lane = lax.broadcasted_iota(jnp.int32, (tm, N_GATES), 1)
    gr_ref[...] = jnp.where((lane // MLSTM_HEADS) % 2 == 1, _log_sigmoid(graw), graw)
    gt = pt[:N_GATES, :] + gbc_ref[...]
    sub = lax.broadcasted_iota(jnp.int32, (N_GATES, tm), 0)
    gc_ref[...] = jnp.where((sub // MLSTM_HEADS) % 2 == 1, _log_sigmoid(gt), gt)

    cos = cos_ref[...]
    sin = sin_ref[...]
    lane_b = lax.broadcasted_iota(jnp.int32, (tm, HEAD_BLOCK), 1)
    first_half = (lane_b % ROPE_AXIS_DIM) < (ROPE_AXIS_DIM // 2)

    cqn = _rms(p[:, 0:MLA_Q_RANK], cqg_ref[...]).astype(BF16)
    ckvn = _rms(p[:, OFF_CKV:OFF_CKV + MLA_KV_RANK], ckvg_ref[...]).astype(BF16)
    nt_dims = (((1,), (1,)), ((), ()))
    k_raw = _dotf(ckvn, wuk_ref[...])
    vt = lax.dot_general(wuvt_ref[...], ckvn, nt_dims, preferred_element_type=F32).astype(BF16)
    ones_rows = jnp.where(lax.broadcasted_iota(jnp.int32, (V_ROWS - MLA_V, KEY_CHUNK), 0) == 0, 1.0, 0.0).astype(BF16)
    for cc in range(tm // KEY_CHUNK):
        for hh in range(MLA_HEADS):
            vt_ref[cc, V_ROWS * hh:V_ROWS * hh + MLA_V, :] = vt[MLA_V * hh:MLA_V * (hh + 1),
                                                                KEY_CHUNK * cc:KEY_CHUNK * (cc + 1)]
            vt_ref[cc, V_ROWS * hh + MLA_V:V_ROWS * (hh + 1), :] = ones_rows

    kr = p[:, OFF_KR:OFF_KR + HEAD_BLOCK]
    kr_ms = jnp.sum(kr * kr, axis=-1, keepdims=True) * (1.0 / MLA_ROPE)
    kr_rot = _rope_block(kr * lax.rsqrt(kr_ms + EPS) * krg_ref[...], cos, sin, first_half)
    kg = kg_ref[...]
    for hh in range(MLA_HEADS):
        sl = slice(HEAD_BLOCK * hh, HEAD_BLOCK * (hh + 1))
        kb = k_raw[:, sl]
        kb_ms = jnp.sum(kb * kb, axis=-1, keepdims=True) * (1.0 / MLA_NOPE)
        k_ref[:, sl] = (kb * lax.rsqrt(kb_ms + EPS) * kg[:, sl] + kr_rot).astype(BF16)

    qt_raw = lax.dot_general(wuqt_ref[...], cqn, nt_dims, preferred_element_type=F32)
    q_scale = (MLA_QK ** -0.5) * LOG2_E
    half = ROPE_AXIS_DIM // 2
    cost = cost_ref[...]
    sint = sint_ref[...]
    for hh in range(MLA_HEADS):
        r0 = HEAD_BLOCK * hh
        nope = qt_raw[r0:r0 + MLA_NOPE, :]
        rope = qt_raw[r0 + MLA_NOPE:r0 + MLA_QK, :]
        nope = nope * lax.rsqrt(jnp.mean(nope * nope, axis=0, keepdims=True) + EPS) * qgn_ref[...]
        rope = rope * lax.rsqrt(jnp.mean(rope * rope, axis=0, keepdims=True) + EPS) * qgr_ref[...]
        rot = []
        for ax in range(2):
            x1 = rope[ROPE_AXIS_DIM * ax:ROPE_AXIS_DIM * ax + half, :]
            x2 = rope[ROPE_AXIS_DIM * ax + half:ROPE_AXIS_DIM * (ax + 1), :]
            cs = cost[half * ax:half * (ax + 1), :]
            sn = sint[half * ax:half * (ax + 1), :]
            rot += [x1 * cs - x2 * sn, x2 * cs + x1 * sn]
        qt_ref[r0:r0 + MLA_NOPE, :] = (nope * q_scale).astype(BF16)
        qt_ref[r0 + MLA_NOPE:r0 + MLA_QK, :] = (jnp.concatenate(rot, axis=0) * q_scale).astype(BF16)
        qt_ref[r0 + MLA_QK:r0 + HEAD_BLOCK, :] = jnp.zeros((HEAD_BLOCK - MLA_QK, tm), BF16)


def _even_prep_into_kernel(*refs):
    n_in = 20
    _even_prep_kernel(*refs[:n_in], *refs[n_in + 2:])


def _even_prep(x2d, mods, row_fn, g, tabs, tab_fn, w, tm, batch, key_len, key_off, kv_bufs=None):
    n = x2d.shape[0]
    nt = n // batch // tm
    assert key_off % tm == 0 and tm % KEY_CHUNK == 0
    cos, sin, cost, sint = tabs
    tok = lambda width: pl.BlockSpec((tm, width), lambda i: (i, 0))
    tok_t = lambda rows: pl.BlockSpec((rows, tm), lambda i: (0, i))
    k_spec = pl.BlockSpec((None, tm, MLA_HEADS * HEAD_BLOCK), lambda i: (i // nt, key_off // tm + i % nt, 0))
    vt_spec = pl.BlockSpec((None, tm // KEY_CHUNK, MLA_HEADS * V_ROWS, KEY_CHUNK),
                           lambda i: (i // nt, key_off // tm + i % nt, 0, 0))
    out_shape = [
        jax.ShapeDtypeStruct((MLA_HEADS * HEAD_BLOCK, n), BF16),
        jax.ShapeDtypeStruct((batch, key_len, MLA_HEADS * HEAD_BLOCK), BF16),
        jax.ShapeDtypeStruct((batch, key_len // KEY_CHUNK, MLA_HEADS * V_ROWS, KEY_CHUNK), BF16),
        jax.ShapeDtypeStruct((MLSTM_WIDTH, n), BF16),
        jax.ShapeDtypeStruct((n, MLSTM_WIDTH), BF16),
        jax.ShapeDtypeStruct((MLSTM_WIDTH, n), BF16),
        jax.ShapeDtypeStruct((MLSTM_WIDTH, n), BF16),
        jax.ShapeDtypeStruct((n, N_GATES), F32),
        jax.ShapeDtypeStruct((N_GATES, n), F32),
    ]
    out_specs = [tok_t(1024), k_spec, vt_spec, tok_t(512), tok(512), tok_t(512), tok_t(512),
                 tok(N_GATES), tok_t(N_GATES)]
    in_specs = [
        tok(D_MODEL),
        pl.BlockSpec((None, 3, D_MODEL), lambda i: (row_fn(i), 0, 0)),
        _const_spec((1, D_MODEL)),
        pl.BlockSpec((tm, HEAD_BLOCK), lambda i: (tab_fn(i), 0)),
        pl.BlockSpec((tm, HEAD_BLOCK), lambda i: (tab_fn(i), 0)),
        pl.BlockSpec((ROPE_AXIS_DIM, tm), lambda i: (0, tab_fn(i))),
        pl.BlockSpec((ROPE_AXIS_DIM, tm), lambda i: (0, tab_fn(i))),
        _const_spec((D_MODEL, W_ALL)),
        _const_spec((W_T_ROWS, D_MODEL)),
        _const_spec((1, N_GATES)),
        _const_spec((N_GATES, 1)),
        _const_spec((1, MLA_Q_RANK)),
        _const_spec((1024, MLA_Q_RANK)),
        _const_spec((1, MLA_KV_RANK)),
        _const_spec((MLA_KV_RANK, 1024)),
        _const_spec((512, MLA_KV_RANK)),
        _const_spec((MLA_NOPE, 1)),
        _const_spec((MLA_ROPE, 1)),
        _const_spec((1, 1024)),
        _const_spec((1, HEAD_BLOCK)),
    ]
    args = [x2d, mods, g.reshape(1, D_MODEL), cos, sin, cost, sint, w["w_all"], w["w_gt"], w["gb_row"], w["gb_col"],
            w["cq_g"], w["w_uqt"], w["ckv_g"], w["w_uk"], w["w_uvt"], w["qg_nope"], w["qg_rope"], w["k_g"], w["kr_g"]]
    aliases = {}
    if kv_bufs is not None:
        aliases = {len(args): 1, len(args) + 1: 2}
        in_specs = in_specs + [pl.BlockSpec(memory_space=pl.ANY)] * 2
        args = args + list(kv_bufs)
    return pl.pallas_call(
        _even_prep_kernel if kv_bufs is None else _even_prep_into_kernel,
        grid=(n // tm,),
        in_specs=in_specs,
        out_specs=out_specs,
        out_shape=out_shape,
        input_output_aliases=aliases,
        compiler_params=_cparams("parallel"),
        name="even_prep",
    )(*args)


def _attn_kernel(qt_ref, k_ref, vt_ref, o_ref, st_scr, acc_scr):
    tq = qt_ref.shape[1]
    nk, _, tk = vt_ref.shape
    unroll = max(u for u in (16, 8, 4, 2) if (nk - 1) % u == 0)
    heads = range(2)

    def scores(j, slot):
        off = pl.multiple_of(j * tk, tk)
        cms = []
        for hh in heads:
            sl = slice(HEAD_BLOCK * hh, HEAD_BLOCK * (hh + 1))
            st = _dotf(k_ref[pl.ds(off, tk), sl], qt_ref[sl, :])
            st_scr[slot, hh] = st
            cms.append(jnp.max(st, axis=0, keepdims=True))
        return tuple(cms)

    def absorb(j, slot, cms, ms):
        out = []
        for hh in heads:
            m_new = jnp.maximum(ms[hh], cms[hh])
            alpha = jnp.exp2(ms[hh] - m_new)
            pt = jnp.exp2(st_scr[slot, hh] - m_new)
            pv = _dotf(vt_ref[j, V_ROWS * hh:V_ROWS * (hh + 1), :], pt.astype(BF16))
            acc_scr[hh] = alpha * acc_scr[hh] + pv
            out.append(m_new)
        return tuple(out)

    def body(i, carry):
        cms, ms = carry
        for r in range(unroll):
            j = unroll * i + r
            cms_next = scores(j + 1, (r + 1) % 2)
            ms = absorb(j, r % 2, cms, ms)
            cms = cms_next
        return cms, ms

    acc_scr[...] = jnp.zeros(acc_scr.shape, F32)
    m0 = jnp.full((1, tq), NEG_BIG, F32)
    cms, ms = lax.fori_loop(0, (nk - 1) // unroll, body, (scores(0, 0), (m0, m0)))
    absorb(nk - 1, 0, cms, ms)
    for hh in heads:
        acc = acc_scr[hh]
        o_ref[MLA_V * hh:MLA_V * (hh + 1), :] = (acc[:MLA_V, :] / acc[MLA_V:MLA_V + 1, :]).astype(o_ref.dtype)


def _attention(qt, k3, vt4, batch, tq, key_len, key_off):
    nq = qt.shape[1] // batch // tq
    assert key_off % key_len == 0 and key_len % KEY_CHUNK == 0
    kb = key_off // key_len
    nk = key_len // KEY_CHUNK
    tk = KEY_CHUNK
    return pl.pallas_call(
        _attn_kernel,
        grid=(batch, MLA_HEADS // 2, nq),
        in_specs=[
            pl.BlockSpec((2 * HEAD_BLOCK, tq), lambda b, hp, i: (hp, b * nq + i)),
            pl.BlockSpec((None, key_len, 2 * HEAD_BLOCK), lambda b, hp, i: (b, kb, hp)),
            pl.BlockSpec((None, nk, 2 * V_ROWS, tk), lambda b, hp, i: (b, kb, hp, 0)),
        ],
        out_specs=pl.BlockSpec((2 * MLA_V, tq), lambda b, hp, i: (hp, b * nq + i)),
        out_shape=jax.ShapeDtypeStruct((MLA_HEADS * MLA_V, qt.shape[1]), BF16),
        scratch_shapes=[pltpu.VMEM((2, 2, tk, tq), F32), pltpu.VMEM((2, V_ROWS, tq), F32)],
        compiler_params=_cparams("parallel", "parallel", "arbitrary"),
        name="mla_attention",
    )(qt, k3, vt4)


def _mlstm_kernel(qtf_ref, kf_ref, vtf_ref, grf_ref, gcf_ref, qtb_ref, kb_ref, vtb_ref, grb_ref, gcb_ref,
                  s0_ref, m0_ref, hf_ref, hb_ref, s_out_ref, m_out_ref, s_scr, m_scr):
    j = pl.program_id(1)
    nj = pl.num_programs(1)
    L = kf_ref.shape[0]

    @pl.when(j == 0)
    def _():
        s_scr[...] = s0_ref[...]
        m_scr[...] = m0_ref[...]

    row = lax.broadcasted_iota(jnp.int32, (L, L), 0)
    col = lax.broadcasted_iota(jnp.int32, (L, L), 1)
    sub = lax.broadcasted_iota(jnp.int32, (MLSTM_ST - MLSTM_DH, L), 0)

    dirs = ((qtf_ref, kf_ref, vtf_ref, grf_ref, gcf_ref, hf_ref), (qtb_ref, kb_ref, vtb_ref, grb_ref, gcb_ref, hb_ref))
    combos = [(d, hh) for d in range(2) for hh in range(MLSTM_HEADS)]


    masks, cum_rows, cum_cols, g_rows, g_cols = [], [], [], [], []
    for d in range(2):
        s_le_t = (row <= col) if d == 0 else (row >= col)
        tri = jnp.where((col <= row) if d == 0 else (col >= row), 1.0, 0.0).astype(BF16)
        tri_t = jnp.where(s_le_t, 1.0, 0.0).astype(BF16)
        gr = dirs[d][3][...]
        gc = dirs[d][4][...]
        r1, r2, r3 = _split3(gr)
        c1, c2, c3 = _split3(gc)
        masks.append(s_le_t)
        g_rows.append(gr)
        g_cols.append(gc)
        cum_rows.append(_dotf(tri, r1) + _dotf(tri, r2) + _dotf(tri, r3))
        cum_cols.append(_dotf(c1, tri_t) + _dotf(c2, tri_t) + _dotf(c3, tri_t))
    kq, inter, s_prev, m_prev = [], [], [], []
    for d, hh in combos:
        idx = d * MLSTM_HEADS + hh
        sl = slice(MLSTM_DH * hh, MLSTM_DH * (hh + 1))
        qth = dirs[d][0][sl, :]
        s_prev.append(s_scr[idx])
        m_prev.append(m_scr[idx:idx + 1, 0:1])
        kq.append(_dotf(dirs[d][1][:, sl], qth))
        inter.append(_dotf(s_prev[-1].astype(BF16), qth))

    sw, den, w_inter, floor, vw, decay, m_new = [], [], [], [], [], [], []
    for c, (d, hh) in enumerate(combos):
        il = (0 if d == 0 else 2 * MLSTM_HEADS) + hh
        fl = il + MLSTM_HEADS
        sl = slice(MLSTM_DH * hh, MLSTM_DH * (hh + 1))
        c_col = g_rows[d][:, il:il + 1] - cum_rows[d][:, fl:fl + 1]
        b_row = cum_cols[d][fl:fl + 1, :]
        c_row = g_cols[d][il:il + 1, :] - b_row
        c_mat = jnp.where(masks[d], c_col, NEG_BIG)
        u = jnp.maximum(jnp.max(c_mat, axis=0, keepdims=True), m_prev[c])
        s = kq[c] * jnp.exp(c_mat - u)
        sw.append(s.astype(BF16))
        den.append(jnp.sum(s, axis=0, keepdims=True))
        w_inter.append(jnp.exp(m_prev[c] - u))
        floor.append(jnp.exp(-(b_row + u)))
        b_end = b_row[:, L - 1:L] if d == 0 else b_row[:, 0:1]
        g_row = b_end + c_row
        m_new.append(jnp.maximum(b_end + m_prev[c], jnp.max(g_row, axis=-1, keepdims=True)))
        decay.append(jnp.exp(b_end + m_prev[c] - m_new[c]))
        w_s = jnp.exp(g_row - m_new[c])
        n_rows = jnp.where(sub == 0, w_s, 0.0)
        vw.append(jnp.concatenate([dirs[d][2][sl, :].astype(F32) * w_s, n_rows], axis=0).astype(BF16))

    for c, (d, hh) in enumerate(combos):
        idx = d * MLSTM_HEADS + hh
        sl = slice(MLSTM_DH * hh, MLSTM_DH * (hh + 1))
        num = _dotf(dirs[d][2][sl, :], sw[c]) + w_inter[c] * inter[c][:MLSTM_DH, :]
        dn = den[c] + w_inter[c] * inter[c][MLSTM_DH:MLSTM_DH + 1, :]
        dirs[d][5][sl, :] = (num / jnp.maximum(jnp.abs(dn), floor[c])).astype(BF16)
        s_scr[idx] = decay[c] * s_prev[c] + _dotf(vw[c], dirs[d][1][:, sl])
        m_scr[idx:idx + 1, :] = jnp.broadcast_to(m_new[c], (1, MLSTM_DH))

    @pl.when(j == nj - 1)
    def _():
        s_out_ref[...] = s_scr[...]
        m_out_ref[...] = m_scr[...]


def _mlstm(mqt, mk, mvt, g_rows, g_cols, s0, m0, batch, chunk):
    n = mk.shape[0]
    nc = n // batch // chunk
    fwd = lambda b, j: (b * nc + j, 0)
    bwd = lambda b, j: (b * nc + (nc - 1 - j), 0)
    fwd_t = lambda b, j: (0, b * nc + j)
    bwd_t = lambda b, j: (0, b * nc + (nc - 1 - j))
    nst = 2 * MLSTM_HEADS
    tok = lambda fn: pl.BlockSpec((chunk, MLSTM_WIDTH), fn)
    tok_t = lambda fn: pl.BlockSpec((MLSTM_WIDTH, chunk), fn)
    state_s = pl.BlockSpec((None, nst, MLSTM_ST, MLSTM_DH), lambda b, j: (b, 0, 0, 0))
    state_m = pl.BlockSpec((None, nst, MLSTM_DH), lambda b, j: (b, 0, 0))
    return pl.pallas_call(
        _mlstm_kernel,
        grid=(batch, nc),
        in_specs=[tok_t(fwd_t), tok(fwd), tok_t(fwd_t), pl.BlockSpec((chunk, N_GATES), fwd),
                  pl.BlockSpec((N_GATES, chunk), fwd_t),
                  tok_t(bwd_t), tok(bwd), tok_t(bwd_t), pl.BlockSpec((chunk, N_GATES), bwd),
                  pl.BlockSpec((N_GATES, chunk), bwd_t),
                  state_s, state_m],
        out_specs=[tok_t(fwd_t), tok_t(bwd_t), state_s, state_m],
        out_shape=[jax.ShapeDtypeStruct((MLSTM_WIDTH, n), BF16), jax.ShapeDtypeStruct((MLSTM_WIDTH, n), BF16),
                   jax.ShapeDtypeStruct(s0.shape, F32), jax.ShapeDtypeStruct(m0.shape, F32)],
        scratch_shapes=[pltpu.VMEM((nst, MLSTM_ST, MLSTM_DH), F32), pltpu.VMEM((nst, MLSTM_DH), F32)],
        compiler_params=_cparams("parallel", "arbitrary"),
        name="mlstm",
    )(mqt, mk, mvt, g_rows, g_cols, mqt, mk, mvt, g_rows, g_cols, s0, m0)


def _even_out_kernel(x_ref, gate_ref, att_ref, hf_ref, hb_ref, mo_ref, og_ref, w_ref,
                     fmod_ref, fg_ref, fwg_ref, fwu_ref, fwd_ref, o_ref):
    hs = hf_ref[...].astype(F32) + hb_ref[...].astype(F32)
    og = og_ref[...]
    parts = [att_ref[...]]
    for hh in range(MLSTM_HEADS):
        sl = slice(MLSTM_DH * hh, MLSTM_DH * (hh + 1))
        blk = hs[sl, :]
        hn = blk * lax.rsqrt(jnp.mean(blk * blk, axis=0, keepdims=True) + EPS) * og[sl, :]
        parts.append((hn * jax.nn.sigmoid(mo_ref[sl, :].astype(F32))).astype(BF16))
    mixed = jnp.concatenate(parts, axis=0)
    y = lax.dot_general(mixed, w_ref[...], (((0,), (0,)), ((), ())), preferred_element_type=F32)
    o_ref[...] = _ffn_math(x_ref[...] + gate_ref[...] * y, fmod_ref, fg_ref, fwg_ref, fwu_ref, fwd_ref)


def _even_out(x2d, gates, row_fn, att, hf, hb, mo, out_g, w_out, ffn_args, tm):
    n = x2d.shape[0]
    tok = pl.BlockSpec((tm, D_MODEL), lambda i: (i, 0))
    tok_t = pl.BlockSpec((MLSTM_WIDTH, tm), lambda i: (0, i))
    return pl.pallas_call(
        _even_out_kernel,
        grid=(n // tm,),
        in_specs=[tok, pl.BlockSpec((None, 1, D_MODEL), lambda i: (row_fn(i), 0, 0)),
                  tok_t, tok_t, tok_t, tok_t, _const_spec((MLSTM_WIDTH, 1)), _const_spec((D_MODEL, D_MODEL))]
        + _ffn_specs(row_fn, *ffn_args[0]),
        out_specs=tok,
        out_shape=jax.ShapeDtypeStruct((n, D_MODEL), F32),
        compiler_params=_cparams("parallel"),
        name="even_out_ffn",
    )(x2d, gates, att, hf, hb, mo, out_g.reshape(MLSTM_WIDTH, 1), w_out, *ffn_args[1:])


def _gelu_tanh(x):
    return 0.5 * x * (1.0 + jnp.tanh(0.7978845608028654 * (x + 0.044715 * (x * x * x))))


def _odd_in_kernel(x_ref, xp_ref, xn_ref, mod_ref, g_ref, w_ref, cw_ref, cb_ref, gate_ref, xc_ref, ext_scr, *, nt):
    tm = x_ref.shape[0]
    ti = pl.program_id(0) % nt
    g, shift, scale = g_ref[...], mod_ref[0:1, :], mod_ref[1:2, :]
    halo = jnp.concatenate([xp_ref[...], xn_ref[...]], axis=0)
    ph = _dotf(_norm_mod(halo, g, shift, scale).astype(BF16), w_ref[:, RNN_WIDTH:])
    ext_scr[0:8, :] = jnp.where(ti > 0, ph[0:8, :], 0.0)
    ext_scr[8 + tm:16 + tm, :] = jnp.where(ti < nt - 1, ph[8:16, :], 0.0)
    cw = cw_ref[...]
    cb = cb_ref[...]
    rc = tm // ODD_IN_CHUNKS

    def project(c):
        rows = slice(rc * c, rc * (c + 1))
        p = _dotf(_norm_mod(x_ref[rows, :], g, shift, scale).astype(BF16), w_ref[...])
        gate_ref[rows, :] = _gelu_tanh(p[:, :RNN_WIDTH]).astype(BF16)
        ext_scr[8 + rc * c:8 + rc * (c + 1), :] = p[:, RNN_WIDTH:]

    def conv(c):
        win = ext_scr[rc * c:rc * (c + 1) + 16, :]
        xc = cb + cw[2:3, :] * win[8:8 + rc, :]
        for k in (0, 1, 3):
            xc = xc + cw[k:k + 1, :] * pltpu.roll(win, (2 - k) % (rc + 16), axis=0)[8:8 + rc, :]
        xc_ref[rc * c:rc * (c + 1), :] = xc

    project(0)
    for c in range(1, ODD_IN_CHUNKS):
        project(c)
        conv(c - 1)
    conv(ODD_IN_CHUNKS - 1)


def _odd_in(x2d, mods, row_fn, g, w_in, conv_w, conv_b, batch, tm):
    n = x2d.shape[0]
    nt = n // batch // tm
    t8 = tm // 8
    tok = pl.BlockSpec((tm, D_MODEL), lambda i: (i, 0))
    prev = pl.BlockSpec((8, D_MODEL), lambda i: (jnp.maximum(i * t8 - 1, 0), 0))
    nxt = pl.BlockSpec((8, D_MODEL), lambda i: (jnp.minimum((i + 1) * t8, n // 8 - 1), 0))
    return pl.pallas_call(
        functools.partial(_odd_in_kernel, nt=nt),
        grid=(n // tm,),
        in_specs=[tok, prev, nxt, pl.BlockSpec((None, 3, D_MODEL), lambda i: (row_fn(i), 0, 0)),
                  _const_spec((1, D_MODEL)), _const_spec((D_MODEL, 2 * RNN_WIDTH)),
                  _const_spec((CONV_W, RNN_WIDTH)), _const_spec((1, RNN_WIDTH))],
        out_specs=[tok, tok],
        out_shape=[jax.ShapeDtypeStruct((n, RNN_WIDTH), BF16), jax.ShapeDtypeStruct((n, RNN_WIDTH), F32)],
        scratch_shapes=[pltpu.VMEM((tm + 16, RNN_WIDTH), F32)],
        compiler_params=_cparams("parallel"),
        name="odd_in",
    )(x2d, x2d, x2d, mods, g.reshape(1, D_MODEL), w_in, conv_w, conv_b)


def _rglru_kernel(xc_ref, wax_ref, ba_ref, bx_ref, lam_ref, h0_ref, h_ref, ht_ref, a_scr, b_scr, carry_scr, *, nt):
    d = pl.program_id(0)
    i = pl.program_id(2)
    tm = xc_ref.shape[0]
    nb8 = tm // 8

    xc = xc_ref[...]
    neg_lam = -lam_ref[...]
    half_rate = (-0.5 * LRU_C) * (jnp.maximum(neg_lam, 0.0) + jnp.log1p(jnp.exp(-jnp.abs(neg_lam))))
    ba = ba_ref[...]
    bx = bx_ref[...]
    for nb in range(RNN_BLOCKS):
        sl = slice(RNN_BLOCK_DIM * nb, RNN_BLOCK_DIM * (nb + 1))
        u = xc[:, sl]
        rg = _dotf(u.astype(BF16), wax_ref[nb])
        log_a = jnp.tanh(rg[:, :RNN_BLOCK_DIM] + ba[:, sl]) * half_rate[:, sl] + half_rate[:, sl]
        ig = 0.5 * jnp.tanh(rg[:, RNN_BLOCK_DIM:] + bx[:, sl]) + 0.5
        a = jnp.exp(log_a)
        a_scr[:, sl] = a
        b_scr[:, sl] = jnp.sqrt(-jnp.tanh(log_a) * (a * a + 1.0)) * (ig * u)

    @pl.when(i == 0)
    def _():
        carry_scr[...] = h0_ref[...]

    def scan_tile(reverse):
        a = a_scr[...].reshape(nb8, 8, RNN_WIDTH)
        b = b_scr[...].reshape(nb8, 8, RNN_WIDTH)
        r8 = lax.broadcasted_iota(jnp.int32, (nb8, 8, RNN_WIDTH), 1)
        for s in (1, 2, 4):
            keep = (r8 < 8 - s) if reverse else (r8 >= s)
            shift = 8 - s if reverse else s
            a_in = jnp.where(keep, pltpu.roll(a, shift, axis=1), 1.0)
            b_in = jnp.where(keep, pltpu.roll(b, shift, axis=1), 0.0)
            b = a * b_in + b
            a = a * a_in
        a_scr[...] = a.reshape(tm, RNN_WIDTH)
        b_scr[...] = b.reshape(tm, RNN_WIDTH)

        def block_pair(jp, h):
            base = pl.multiple_of((nb8 // 2 - 1 - jp if reverse else jp) * 16, 16)
            halves = [None, None]
            for half in ((1, 0) if reverse else (0, 1)):
                rows = pl.ds(base + 8 * half, 8)
                hb = a_scr[rows, :] * h + b_scr[rows, :]
                halves[half] = hb
                h = hb[0:1, :] if reverse else hb[7:8, :]
            h_ref[pl.ds(base, 16), :] = jnp.concatenate(halves, axis=0).astype(h_ref.dtype)
            return h

        carry_scr[...] = lax.fori_loop(0, nb8 // 2, block_pair, carry_scr[...], unroll=2)

    @pl.when(d == 0)
    def _():
        scan_tile(False)

    @pl.when(d == 1)
    def _():
        scan_tile(True)

    @pl.when(i == nt - 1)
    def _():
        ht_ref[...] = carry_scr[...]


def _rglru(xc2d, h0, w, batch, tm):
    n = xc2d.shape[0]
    nt = n // batch // tm
    tile = lambda d, i: jnp.where(d == 0, i, nt - 1 - i)
    per_dir = lambda shape: pl.BlockSpec((None,) + shape, lambda d, b, i: (d,) + (0,) * len(shape))
    state = pl.BlockSpec((None, None, 1, RNN_WIDTH), lambda d, b, i: (d, b, 0, 0))
    return pl.pallas_call(
        functools.partial(_rglru_kernel, nt=nt),
        grid=(2, batch, nt),
        in_specs=[pl.BlockSpec((tm, RNN_WIDTH), lambda d, b, i: (b * nt + tile(d, i), 0)),
                  per_dir((RNN_BLOCKS, RNN_BLOCK_DIM, 2 * RNN_BLOCK_DIM)), per_dir((1, RNN_WIDTH)), per_dir((1, RNN_WIDTH)),
                  per_dir((1, RNN_WIDTH)), state],
        out_specs=[pl.BlockSpec((None, tm, RNN_WIDTH), lambda d, b, i: (d, b * nt + tile(d, i), 0)), state],
        out_shape=[jax.ShapeDtypeStruct((2, n, RNN_WIDTH), BF16), jax.ShapeDtypeStruct((2, batch, 1, RNN_WIDTH), F32)],
        scratch_shapes=[pltpu.VMEM((tm, RNN_WIDTH), F32), pltpu.VMEM((tm, RNN_WIDTH), F32),
                        pltpu.VMEM((1, RNN_WIDTH), F32)],
        compiler_params=_cparams("parallel", "parallel", "arbitrary"),
        name="rglru",
    )(xc2d, w["w_ax"], w["b_a"], w["b_x"], w["lam"], h0)


def _odd_out_kernel(x_ref, gate_ref, h_ref, gg_ref, w_ref, fmod_ref, fg_ref, fwg_ref, fwu_ref, fwd_ref, o_ref):
    y = ((h_ref[0].astype(F32) + h_ref[1].astype(F32)) * gg_ref[...].astype(F32)).astype(BF16)
    x_mid = x_ref[...] + gate_ref[...] * _dotf(y, w_ref[...])
    o_ref[...] = _ffn_math(x_mid, fmod_ref, fg_ref, fwg_ref, fwu_ref, fwd_ref)


def _odd_out(x2d, gates, row_fn, h2, gelu_gate, w_out, ffn_args, tm):
    n = x2d.shape[0]
    tok = pl.BlockSpec((tm, D_MODEL), lambda i: (i, 0))
    return pl.pallas_call(
        _odd_out_kernel,
        grid=(n // tm,),
        in_specs=[tok, pl.BlockSpec((None, 1, D_MODEL), lambda i: (row_fn(i), 0, 0)),
                  pl.BlockSpec((2, tm, RNN_WIDTH), lambda i: (0, i, 0)), tok, _const_spec((RNN_WIDTH, D_MODEL))]
        + _ffn_specs(row_fn, *ffn_args[0]),
        out_specs=tok,
        out_shape=jax.ShapeDtypeStruct((n, D_MODEL), F32),
        compiler_params=_cparams("parallel"),
        name="odd_out_ffn",
    )(x2d, gates, h2, gelu_gate, w_out, *ffn_args[1:])


def _rope_tables(seq):
    t = np.arange(seq)
    inv = np.float32(ROPE_BASE) ** (-np.arange(0, ROPE_AXIS_DIM, 2, dtype=np.float32) / np.float32(ROPE_AXIS_DIM))
    ang_r = (t // GRID_W).astype(np.float32)[:, None] * inv
    ang_c = (t % GRID_W).astype(np.float32)[:, None] * inv
    cos32 = np.concatenate([np.cos(ang_r), np.cos(ang_r), np.cos(ang_c), np.cos(ang_c)], axis=1)
    sin32 = np.concatenate([-np.sin(ang_r), np.sin(ang_r), -np.sin(ang_c), np.sin(ang_c)], axis=1)
    pad = HEAD_BLOCK - MLA_QK
    cos = np.concatenate([np.ones((seq, MLA_NOPE), np.float32), cos32, np.ones((seq, pad), np.float32)], axis=1)
    sin = np.concatenate([np.zeros((seq, MLA_NOPE), np.float32), sin32, np.zeros((seq, pad), np.float32)], axis=1)
    cost = np.concatenate([np.cos(ang_r), np.cos(ang_c)], axis=1).T
    sint = np.concatenate([np.sin(ang_r), np.sin(ang_c)], axis=1).T
    return tuple(jnp.asarray(a, F32) for a in (cos, sin, cost, sint))


def _even_weights(w_in, cq_g, w_uq, ckv_g, w_ukv, q_g, k_g, gate_b):
    pad = HEAD_BLOCK - MLA_QK
    w_in, w_uq, w_ukv = w_in.astype(BF16), w_uq.astype(BF16), w_ukv.astype(BF16)
    kr_cols = w_in[:, OFF_KR:OFF_KR + MLA_ROPE]
    w_kr = jnp.pad(kr_cols, ((0, 0), (MLA_NOPE, pad)))
    m_cols = w_in[:, OFF_KR + MLA_ROPE:OFF_KR + MLA_ROPE + 4 * MLSTM_WIDTH]
    g_cols = w_in[:, OFF_KR + MLA_ROPE + 4 * MLSTM_WIDTH:]
    mq_cols, mk_cols, mv_cols, mo_cols = jnp.split(m_cols, 4, axis=1)
    w_all = jnp.concatenate([w_in[:, :OFF_KR], w_kr, mk_cols, jnp.pad(g_cols, ((0, 0), (0, HEAD_BLOCK - N_GATES)))], axis=1)
    uq = jnp.pad(w_uq.reshape(MLA_Q_RANK, MLA_HEADS, MLA_QK), ((0, 0), (0, 0), (0, pad)))
    ukv = w_ukv.reshape(MLA_KV_RANK, MLA_HEADS, MLA_NOPE + MLA_V)
    uk = jnp.pad(ukv[:, :, :MLA_NOPE], ((0, 0), (0, 0), (0, HEAD_BLOCK - MLA_NOPE)))
    uv = ukv[:, :, MLA_NOPE:]
    kg_blk = jnp.pad(k_g[:MLA_NOPE], (0, HEAD_BLOCK - MLA_NOPE))
    krg_blk = jnp.pad(k_g[MLA_NOPE:], (MLA_NOPE, pad))
    return {
        "w_all": w_all,
        "w_gt": jnp.concatenate([g_cols, mq_cols, mv_cols, mo_cols], axis=1).T,
        "gb_row": gate_b.reshape(1, N_GATES),
        "gb_col": gate_b.reshape(N_GATES, 1),
        "cq_g": cq_g.reshape(1, MLA_Q_RANK),
        "w_uqt": uq.reshape(MLA_Q_RANK, MLA_HEADS * HEAD_BLOCK).T,
        "ckv_g": ckv_g.reshape(1, MLA_KV_RANK),
        "w_uk": uk.reshape(MLA_KV_RANK, MLA_HEADS * HEAD_BLOCK),
        "w_uvt": uv.reshape(MLA_KV_RANK, MLA_HEADS * MLA_V).T,
        "qg_nope": q_g[:MLA_NOPE].reshape(MLA_NOPE, 1),
        "qg_rope": q_g[MLA_NOPE:].reshape(MLA_ROPE, 1),
        "k_g": jnp.tile(kg_blk, MLA_HEADS).reshape(1, -1),
        "kr_g": krg_blk.reshape(1, HEAD_BLOCK),
    }


def kernel(x, c, ctx, c_ctx, mod_w, mod_b, norm_g, ffn_w_gate, ffn_w_up, ffn_w_down, even_w_in, even_w_out, mla_cq_g, mla_w_uq, mla_ckv_g, mla_w_ukv, mla_q_g, mla_k_g, mlstm_gate_b, mlstm_out_g, odd_w_in, odd_conv_w, odd_conv_b, lru_w_a, lru_b_a, lru_w_x, lru_b_x, lru_lam, odd_w_out):
    batch, seq, _ = x.shape
    ctx_len = ctx.shape[1]
    depth = mod_w.shape[0]
    assert depth == 2 and batch <= 7

    cond8 = jnp.zeros((8, D_MODEL), F32).at[:batch].set(c).at[batch].set(c_ctx)
    mods = _ada_params(cond8, mod_w, mod_b)
    mods = mods.reshape(depth, 8, 3, 3, D_MODEL).transpose(0, 2, 1, 3, 4)

    tm_x, tm_c = 512, 256
    x_row = lambda i: i // (seq // tm_x)
    c_row = lambda i: batch
    x2 = x.reshape(batch * seq, D_MODEL)
    c2 = ctx.reshape(batch * ctx_len, D_MODEL)

    wg = ffn_w_gate.astype(BF16)
    wu = ffn_w_up.astype(BF16)
    wd = ffn_w_down.astype(BF16)

    def ffn_args(layer, which):
        return ((layer, which), mods[layer, 2 * which], norm_g[layer, 2 * which].reshape(1, D_MODEL), wg, wu, wd)

    x2 = _ffn(x2, ffn_args(0, 0), x_row, tm_x)
    c2 = _ffn(c2, ffn_args(0, 0), c_row, tm_c)

    ew = _even_weights(even_w_in[0], mla_cq_g[0], mla_w_uq[0], mla_ckv_g[0], mla_w_ukv[0], mla_q_g[0], mla_k_g[0],
                       mlstm_gate_b[0])
    tabs_x = _rope_tables(seq)
    tabs_c = (jnp.ones((tm_c, HEAD_BLOCK), F32), jnp.zeros((tm_c, HEAD_BLOCK), F32),
              jnp.ones((ROPE_AXIS_DIM, tm_c), F32), jnp.zeros((ROPE_AXIS_DIM, tm_c), F32))
    key_len = seq + ctx_len
    px = _even_prep(x2, mods[0, 1], x_row, norm_g[0, 1], tabs_x, lambda i: i % (seq // tm_x), ew, tm_x,
                    batch, key_len, 0)
    qt_x, k_all, vt_all, mqt_x, mk_x, mvt_x, mot_x, gr_x, gc_x = px
    pc = _even_prep(c2, mods[0, 1], c_row, norm_g[0, 1], tabs_c, lambda i: 0, ew, tm_c,
                    batch, key_len, seq, kv_bufs=(k_all, vt_all))
    qt_c, k_all, vt_all, mqt_c, mk_c, mvt_c, mot_c, gr_c, gc_c = pc

    att_x = _attention(qt_x, k_all, vt_all, batch, 512, key_len, 0)
    att_c = _attention(qt_c, k_all, vt_all, batch, 256, ctx_len, seq)

    chunk = 128
    nst = 2 * MLSTM_HEADS
    s0 = jnp.zeros((batch, nst, MLSTM_ST, MLSTM_DH), F32)
    m0 = jnp.zeros((batch, nst, MLSTM_DH), F32)
    hcf, hcb, s_c, m_c = _mlstm(mqt_c, mk_c, mvt_c, gr_c, gc_c, s0, m0, batch, chunk)
    hxf, hxb, _, _ = _mlstm(mqt_x, mk_x, mvt_x, gr_x, gc_x, s_c, m_c, batch, chunk)

    w_out0 = even_w_out[0].astype(BF16)
    gate0 = mods[0, 1][:, 2:3, :]
    x2 = _even_out(x2, gate0, x_row, att_x, hxf, hxb, mot_x, mlstm_out_g[0], w_out0, ffn_args(0, 1), tm_x)
    c2 = _even_out(c2, gate0, c_row, att_c, hcf, hcb, mot_c, mlstm_out_g[0], w_out0, ffn_args(0, 1), tm_c)

    x2 = _ffn(x2, ffn_args(1, 0), x_row, tm_x)
    c2 = _ffn(c2, ffn_args(1, 0), c_row, tm_c)
    w_in1 = odd_w_in[0].astype(BF16)
    conv_b = odd_conv_b[0].reshape(1, RNN_WIDTH)
    gg_x, xc_x = _odd_in(x2, mods[1, 1], x_row, norm_g[1, 1], w_in1, odd_conv_w[0], conv_b, batch, tm_x)
    _, xc_c = _odd_in(c2, mods[1, 1], c_row, norm_g[1, 1], w_in1, odd_conv_w[0], conv_b, batch, tm_c)
    rw = {
        "w_ax": (0.5 * jnp.concatenate([lru_w_a[0], lru_w_x[0]], axis=-1)).astype(BF16),
        "b_a": 0.5 * lru_b_a[0].reshape(2, 1, RNN_WIDTH),
        "b_x": 0.5 * lru_b_x[0].reshape(2, 1, RNN_WIDTH),
        "lam": lru_lam[0].reshape(2, 1, RNN_WIDTH),
    }
    h0 = jnp.zeros((2, batch, 1, RNN_WIDTH), F32)
    _, st_c = _rglru(xc_c, h0, rw, batch, 256)
    h_x, _ = _rglru(xc_x, st_c, rw, batch, 256)
    x2 = _odd_out(x2, mods[1, 1][:, 2:3, :], x_row, h_x, gg_x, odd_w_out[0].astype(BF16), ffn_args(1, 1), tm_x)
    return x2.reshape(batch, seq, D_MODEL)
```

```python
import functools

import jax
import jax.numpy as jnp
import numpy as np
from jax import lax
from jax.experimental import pallas as pl
from jax.experimental.pallas import tpu as pltpu

F32 = jnp.float32
BF16 = jnp.bfloat16

D_MODEL = 1024
GRID_W = 64
EPS = 1e-6
N_MOD = 9
D_FF = 2816
FFN_RESIDUAL = 0.5
FFN_CHUNK_ROWS = 128

MLA_HEADS = 8
MLA_Q_RANK = 384
MLA_KV_RANK = 256
MLA_NOPE = 64
MLA_ROPE = 32
MLA_V = 64
MLA_QK = MLA_NOPE + MLA_ROPE
ROPE_AXIS_DIM = MLA_ROPE // 2
ROPE_BASE = 10000.0
HEAD_BLOCK = 128
KEY_CHUNK = 256
ATTN_HEADS_PER_STEP = 2
V_ROWS = MLA_V + 16

MLSTM_HEADS = 4
MLSTM_DH = 128
MLSTM_WIDTH = MLSTM_HEADS * MLSTM_DH
N_GATES = 4 * MLSTM_HEADS
MLSTM_ST = MLSTM_DH + 16

RNN_WIDTH = 1024
RNN_BLOCKS = 8
RNN_BLOCK_DIM = RNN_WIDTH // RNN_BLOCKS
CONV_W = 4
LRU_C = 8.0
ODD_IN_CHUNKS = 4

NEG_BIG = -1e30
LOG2_E = 1.4426950408889634
VMEM_LIMIT = 56 * 1024 * 1024


def _cparams(*sem):
    return pltpu.CompilerParams(dimension_semantics=sem, vmem_limit_bytes=VMEM_LIMIT)


def _const_spec(shape):
    zeros = (0,) * len(shape)
    return pl.BlockSpec(shape, lambda *_: zeros, pipeline_mode=pl.Buffered(1))


def _dotf(a, b):
    return jnp.dot(a, b, preferred_element_type=F32)


def _split3(a):
    a1 = a.astype(BF16)
    r1 = a - a1.astype(F32)
    a2 = r1.astype(BF16)
    a3 = (r1 - a2.astype(F32)).astype(BF16)
    return a1, a2, a3


def _norm_mod(x, g, shift, scale):
    ms = jnp.mean(x * x, axis=-1, keepdims=True)
    return (x * lax.rsqrt(ms + EPS)) * (g * (1.0 + scale)) + shift


def _rms(x, g):
    ms = jnp.mean(x * x, axis=-1, keepdims=True)
    return x * lax.rsqrt(ms + EPS) * g


def _log_sigmoid(x):
    return jnp.minimum(x, 0.0) - jnp.log1p(jnp.exp(-jnp.abs(x)))


def _ada_kernel(cond_ref, w_ref, b_ref, o_ref):
    c = cond_ref[...]
    s = c * jax.nn.sigmoid(c)
    s1, s2, _ = _split3(s)
    w = w_ref[...]
    w1 = w.astype(BF16)
    w2 = (w - w1.astype(F32)).astype(BF16)
    o_ref[...] = _dotf(s1, w1) + _dotf(s1, w2) + _dotf(s2, w1) + b_ref[...]


def _ada_params(cond8, mod_w, mod_b):
    depth, _, n = mod_w.shape
    tn = 1152
    return pl.pallas_call(
        _ada_kernel,
        grid=(depth, n // tn),
        in_specs=[
            pl.BlockSpec((8, D_MODEL), lambda l, j: (0, 0)),
            pl.BlockSpec((None, D_MODEL, tn), lambda l, j: (l, 0, j)),
            pl.BlockSpec((None, 1, tn), lambda l, j: (l, 0, j)),
        ],
        out_specs=pl.BlockSpec((None, 8, tn), lambda l, j: (l, 0, j)),
        out_shape=jax.ShapeDtypeStruct((depth, 8, n), F32),
        compiler_params=_cparams("parallel", "parallel"),
        name="ada_params",
    )(cond8, mod_w, mod_b.reshape(depth, 1, n))


def _ffn_math(x, mod_ref, g_ref, wg_ref, wu_ref, wd_ref):
    outs = []
    rc = min(FFN_CHUNK_ROWS, x.shape[0])
    for c in range(x.shape[0] // rc):
        xc = x[rc * c:rc * (c + 1), :]
        h = _norm_mod(xc, g_ref[...], mod_ref[0:1, :], mod_ref[1:2, :]).astype(BF16)
        g = _dotf(h, wg_ref[...])
        u = _dotf(h, wu_ref[...])
        a = (g * jax.nn.sigmoid(g) * u).astype(BF16)
        outs.append(xc + (FFN_RESIDUAL * mod_ref[2:3, :]) * _dotf(a, wd_ref[...]))
    return jnp.concatenate(outs, axis=0)


def _ffn_specs(row_fn, layer, which):
    weight = lambda r, c: pl.BlockSpec((None, None, r, c), lambda i: (layer, which, 0, 0), pipeline_mode=pl.Buffered(1))
    return [pl.BlockSpec((None, 3, D_MODEL), lambda i: (row_fn(i), 0, 0)), _const_spec((1, D_MODEL)),
            weight(D_MODEL, D_FF), weight(D_MODEL, D_FF), weight(D_FF, D_MODEL)]


def _ffn_kernel(x_ref, mod_ref, g_ref, wg_ref, wu_ref, wd_ref, o_ref):
    o_ref[...] = _ffn_math(x_ref[...], mod_ref, g_ref, wg_ref, wu_ref, wd_ref)


def _ffn(x2d, ffn_args, row_fn, tm):
    n = x2d.shape[0]
    tok = pl.BlockSpec((tm, D_MODEL), lambda i: (i, 0))
    return pl.pallas_call(
        _ffn_kernel,
        grid=(n // tm,),
        in_specs=[tok] + _ffn_specs(row_fn, *ffn_args[0]),
        out_specs=tok,
        out_shape=jax.ShapeDtypeStruct((n, D_MODEL), F32),
        compiler_params=_cparams("parallel"),
        name="ffn",
    )(x2d, *ffn_args[1:])


W_ALL = MLA_Q_RANK + MLA_KV_RANK + HEAD_BLOCK + MLSTM_WIDTH + HEAD_BLOCK
OFF_CKV = MLA_Q_RANK
OFF_KR = OFF_CKV + MLA_KV_RANK
OFF_MK = OFF_KR + HEAD_BLOCK
OFF_GATES = OFF_MK + MLSTM_WIDTH
W_T_ROWS = N_GATES + 3 * MLSTM_WIDTH


def _rope_block(xb, cos, sin, first_half):
    partner = jnp.where(first_half, pltpu.roll(xb, HEAD_BLOCK - 8, axis=1), pltpu.roll(xb, 8, axis=1))
    return xb * cos + partner * sin


def _even_prep_kernel(x_ref, mod_ref, g_ref, cos_ref, sin_ref, cost_ref, sint_ref, wall_ref, wgt_ref, gbr_ref, gbc_ref,
                      cqg_ref, wuqt_ref, ckvg_ref, wuk_ref, wuvt_ref, qgn_ref, qgr_ref, kg_ref, krg_ref,
                      qt_ref, k_ref, vt_ref, mqt_ref, mk_ref, mvt_ref, mot_ref, gr_ref, gc_ref):
    tm = x_ref.shape[0]
    x = x_ref[...]
    h = _norm_mod(x, g_ref[...], mod_ref[0:1, :], mod_ref[1:2, :]).astype(BF16)
    p = _dotf(h, wall_ref[...])
    pt = lax.dot_general(wgt_ref[...], h, (((1,), (1,)), ((), ())), preferred_element_type=F32)

    mk_ref[...] = (p[:, OFF_MK:OFF_MK + MLSTM_WIDTH] * (MLSTM_DH ** -0.5)).astype(BF16)
    mqt_ref[...] = pt[N_GATES:N_GATES + MLSTM_WIDTH, :].astype(BF16)
    mvt_ref[...] = pt[N_GATES + MLSTM_WIDTH:N_GATES + 2 * MLSTM_WIDTH, :].astype(BF16)
    mot_ref[...] = pt[N_GATES + 2 * MLSTM_WIDTH:, :].astype(BF16)
    graw = p[:, OFF_GATES:OFF_GATES + HEAD_BLOCK][:, :N_GATES] + gbr_ref[...]
    lane = lax.broadcasted_iota(jnp.int32, (tm, N_GATES), 1)
    gr_ref[...] = jnp.where((lane // MLSTM_HEADS) % 2 == 1, _log_sigmoid(graw), graw)
    gt = pt[:N_GATES, :] + gbc_ref[...]
    sub = lax.broadcasted_iota(jnp.int32, (N_GATES, tm), 0)
    gc_ref[...] = jnp.where((sub // MLSTM_HEADS) % 2 == 1, _log_sigmoid(gt), gt)

    cos = cos_ref[...]
    sin = sin_ref[...]
    lane_b = lax.broadcasted_iota(jnp.int32, (tm, HEAD_BLOCK), 1)
    first_half = (lane_b % ROPE_AXIS_DIM) < (ROPE_AXIS_DIM // 2)

    cqn = _rms(p[:, 0:MLA_Q_RANK], cqg_ref[...]).astype(BF16)
    ckvn = _rms(p[:, OFF_CKV:OFF_CKV + MLA_KV_RANK], ckvg_ref[...]).astype(BF16)
    nt_dims = (((1,), (1,)), ((), ()))
    k_raw = _dotf(ckvn, wuk_ref[...])
    vt = lax.dot_general(wuvt_ref[...], ckvn, nt_dims, preferred_element_type=F32).astype(BF16)
    ones_rows = jnp.where(lax.broadcasted_iota(jnp.int32, (V_ROWS - MLA_V, KEY_CHUNK), 0) == 0, 1.0, 0.0).astype(BF16)
    for cc in range(tm // KEY_CHUNK):
        for hh in range(MLA_HEADS):
            vt_ref[cc, V_ROWS * hh:V_ROWS * hh + MLA_V, :] = vt[MLA_V * hh:MLA_V * (hh + 1),
                                                                KEY_CHUNK * cc:KEY_CHUNK * (cc + 1)]
            vt_ref[cc, V_ROWS * hh + MLA_V:V_ROWS * (hh + 1), :] = ones_rows

    kr = p[:, OFF_KR:OFF_KR + HEAD_BLOCK]
    kr_ms = jnp.sum(kr * kr, axis=-1, keepdims=True) * (1.0 / MLA_ROPE)
    kr_rot = _rope_block(kr * lax.rsqrt(kr_ms + EPS) * krg_ref[...], cos, sin, first_half)
    kg = kg_ref[...]
    for hh in range(MLA_HEADS):
        sl = slice(HEAD_BLOCK * hh, HEAD_BLOCK * (hh + 1))
        kb = k_raw[:, sl]
        kb_ms = jnp.sum(kb * kb, axis=-1, keepdims=True) * (1.0 / MLA_NOPE)
        k_ref[:, sl] = (kb * lax.rsqrt(kb_ms + EPS) * kg[:, sl] + kr_rot).astype(BF16)

    qt_raw = lax.dot_general(wuqt_ref[...], cqn, nt_dims, preferred_element_type=F32)
    q_scale = (MLA_QK ** -0.5) * LOG2_E
    half = ROPE_AXIS_DIM // 2
    cost = cost_ref[...]
    sint = sint_ref[...]
    for hh in range(MLA_HEADS):
        r0 = HEAD_BLOCK * hh
        nope = qt_raw[r0:r0 + MLA_NOPE, :]
        rope = qt_raw[r0 + MLA_NOPE:r0 + MLA_QK, :]
        nope = nope * lax.rsqrt(jnp.mean(nope * nope, axis=0, keepdims=True) + EPS) * qgn_ref[...]
        rope = rope * lax.rsqrt(jnp.mean(rope * rope, axis=0, keepdims=True) + EPS) * qgr_ref[...]
        rot = []
        for ax in range(2):
            x1 = rope[ROPE_AXIS_DIM * ax:ROPE_AXIS_DIM * ax + half, :]
            x2 = rope[ROPE_AXIS_DIM * ax + half:ROPE_AXIS_DIM * (ax + 1), :]
            cs = cost[half * ax:half * (ax + 1), :]
            sn = sint[half * ax:half * (ax + 1), :]
            rot += [x1 * cs - x2 * sn, x2 * cs + x1 * sn]
        qt_ref[r0:r0 + MLA_NOPE, :] = (nope * q_scale).astype(BF16)
        qt_ref[r0 + MLA_NOPE:r0 + MLA_QK, :] = (jnp.concatenate(rot, axis=0) * q_scale).astype(BF16)
        qt_ref[r0 + MLA_QK:r0 + HEAD_BLOCK, :] = jnp.zeros((HEAD_BLOCK - MLA_QK, tm), BF16)


def _even_prep_into_kernel(*refs):
    n_in = 20
    _even_prep_kernel(*refs[:n_in], *refs[n_in + 2:])


def _even_prep(x2d, mods, row_fn, g, tabs, tab_fn, w, tm, batch, key_len, key_off, kv_bufs=None):
    n = x2d.shape[0]
    nt = n // batch // tm
    assert key_off % tm == 0 and tm % KEY_CHUNK == 0
    cos, sin, cost, sint = tabs
    tok = lambda width: pl.BlockSpec((tm, width), lambda i: (i, 0))
    tok_t = lambda rows: pl.BlockSpec((rows, tm), lambda i: (0, i))
    k_spec = pl.BlockSpec((None, tm, MLA_HEADS * HEAD_BLOCK), lambda i: (i // nt, key_off // tm + i % nt, 0))
    vt_spec = pl.BlockSpec((None, tm // KEY_CHUNK, MLA_HEADS * V_ROWS, KEY_CHUNK),
                           lambda i: (i // nt, key_off // tm + i % nt, 0, 0))
    out_shape = [
        jax.ShapeDtypeStruct((MLA_HEADS * HEAD_BLOCK, n), BF16),
        jax.ShapeDtypeStruct((batch, key_len, MLA_HEADS * HEAD_BLOCK), BF16),
        jax.ShapeDtypeStruct((batch, key_len // KEY_CHUNK, MLA_HEADS * V_ROWS, KEY_CHUNK), BF16),
        jax.ShapeDtypeStruct((MLSTM_WIDTH, n), BF16),
        jax.ShapeDtypeStruct((n, MLSTM_WIDTH), BF16),
        jax.ShapeDtypeStruct((MLSTM_WIDTH, n), BF16),
        jax.ShapeDtypeStruct((MLSTM_WIDTH, n), BF16),
        jax.ShapeDtypeStruct((n, N_GATES), F32),
        jax.ShapeDtypeStruct((N_GATES, n), F32),
    ]
    out_specs = [tok_t(1024), k_spec, vt_spec, tok_t(512), tok(512), tok_t(512), tok_t(512),
                 tok(N_GATES), tok_t(N_GATES)]
    in_specs = [
        tok(D_MODEL),
        pl.BlockSpec((None, 3, D_MODEL), lambda i: (row_fn(i), 0, 0)),
        _const_spec((1, D_MODEL)),
        pl.BlockSpec((tm, HEAD_BLOCK), lambda i: (tab_fn(i), 0)),
        pl.BlockSpec((tm, HEAD_BLOCK), lambda i: (tab_fn(i), 0)),
        pl.BlockSpec((ROPE_AXIS_DIM, tm), lambda i: (0, tab_fn(i))),
        pl.BlockSpec((ROPE_AXIS_DIM, tm), lambda i: (0, tab_fn(i))),
        _const_spec((D_MODEL, W_ALL)),
        _const_spec((W_T_ROWS, D_MODEL)),
        _const_spec((1, N_GATES)),
        _const_spec((N_GATES, 1)),
        _const_spec((1, MLA_Q_RANK)),
        _const_spec((1024, MLA_Q_RANK)),
        _const_spec((1, MLA_KV_RANK)),
        _const_spec((MLA_KV_RANK, 1024)),
        _const_spec((512, MLA_KV_RANK)),
        _const_spec((MLA_NOPE, 1)),
        _const_spec((MLA_ROPE, 1)),
        _const_spec((1, 1024)),
        _const_spec((1, HEAD_BLOCK)),
    ]
    args = [x2d, mods, g.reshape(1, D_MODEL), cos, sin, cost, sint, w["w_all"], w["w_gt"], w["gb_row"], w["gb_col"],
            w["cq_g"], w["w_uqt"], w["ckv_g"], w["w_uk"], w["w_uvt"], w["qg_nope"], w["qg_rope"], w["k_g"], w["kr_g"]]
    aliases = {}
    if kv_bufs is not None:
        aliases = {len(args): 1, len(args) + 1: 2}
        in_specs = in_specs + [pl.BlockSpec(memory_space=pl.ANY)] * 2
        args = args + list(kv_bufs)
    return pl.pallas_call(
        _even_prep_kernel if kv_bufs is None else _even_prep_into_kernel,
        grid=(n // tm,),
        in_specs=in_specs,
        out_specs=out_specs,
        out_shape=out_shape,
        input_output_aliases=aliases,
        compiler_params=_cparams("parallel"),
        name="even_prep",
    )(*args)


def _attn_kernel(qt_ref, k_ref, vt_ref, o_ref, st_scr, acc_scr):
    tq = qt_ref.shape[1]
    nk, _, tk = vt_ref.shape
    unroll = max(u for u in (16, 8, 4, 2) if (nk - 1) % u == 0)
    heads = range(acc_scr.shape[0])

    def scores(j, slot):
        off = pl.multiple_of(j * tk, tk)
        cms = []
        for hh in heads:
            sl = slice(HEAD_BLOCK * hh, HEAD_BLOCK * (hh + 1))
            st = _dotf(k_ref[pl.ds(off, tk), sl], qt_ref[sl, :])
            st_scr[slot, hh] = st
            cms.append(jnp.max(st, axis=0, keepdims=True))
        return tuple(cms)

    def absorb(j, slot, cms, ms):
        out = []
        for hh in heads:
            m_new = jnp.maximum(ms[hh], cms[hh])
            alpha = jnp.exp2(ms[hh] - m_new)
            pt = jnp.exp2(st_scr[slot, hh] - m_new)
            pv = _dotf(vt_ref[j, V_ROWS * hh:V_ROWS * (hh + 1), :], pt.astype(BF16))
            acc_scr[hh] = alpha * acc_scr[hh] + pv
            out.append(m_new)
        return tuple(out)

    def body(i, carry):
        cms, ms = carry
        for r in range(unroll):
            j = unroll * i + r
            cms_next = scores(j + 1, (r + 1) % 2)
            ms = absorb(j, r % 2, cms, ms)
            cms = cms_next
        return cms, ms

    acc_scr[...] = jnp.zeros(acc_scr.shape, F32)
    m0 = jnp.full((1, tq), NEG_BIG, F32)
    cms, ms = lax.fori_loop(0, (nk - 1) // unroll, body, (scores(0, 0), (m0,) * len(heads)))
    absorb(nk - 1, 0, cms, ms)
    for hh in heads:
        acc = acc_scr[hh]
        o_ref[MLA_V * hh:MLA_V * (hh + 1), :] = (acc[:MLA_V, :] / acc[MLA_V:MLA_V + 1, :]).astype(o_ref.dtype)


def _attention(qt, k3, vt4, batch, tq, key_len, key_off):
    nq = qt.shape[1] // batch // tq
    assert key_off % key_len == 0 and key_len % KEY_CHUNK == 0
    kb = key_off // key_len
    nk = key_len // KEY_CHUNK
    tk = KEY_CHUNK
    g = ATTN_HEADS_PER_STEP
    return pl.pallas_call(
        _attn_kernel,
        grid=(batch, MLA_HEADS // g, nq),
        in_specs=[
            pl.BlockSpec((g * HEAD_BLOCK, tq), lambda b, hp, i: (hp, b * nq + i)),
            pl.BlockSpec((None, key_len, g * HEAD_BLOCK), lambda b, hp, i: (b, kb, hp)),
            pl.BlockSpec((None, nk, g * V_ROWS, tk), lambda b, hp, i: (b, kb, hp, 0)),
        ],
        out_specs=pl.BlockSpec((g * MLA_V, tq), lambda b, hp, i: (hp, b * nq + i)),
        out_shape=jax.ShapeDtypeStruct((MLA_HEADS * MLA_V, qt.shape[1]), BF16),
        scratch_shapes=[pltpu.VMEM((2, g, tk, tq), F32), pltpu.VMEM((g, V_ROWS, tq), F32)],
        compiler_params=_cparams("parallel", "parallel", "arbitrary"),
        name="mla_attention",
    )(qt, k3, vt4)


def _mlstm_kernel(qtf_ref, kf_ref, vtf_ref, grf_ref, gcf_ref, qtb_ref, kb_ref, vtb_ref, grb_ref, gcb_ref,
                  s0_ref, m0_ref, hf_ref, hb_ref, s_out_ref, m_out_ref, s_scr, m_scr):
    j = pl.program_id(1)
    nj = pl.num_programs(1)
    L = kf_ref.shape[0]

    @pl.when(j == 0)
    def _():
        s_scr[...] = s0_ref[...]
        m_scr[...] = m0_ref[...]

    row = lax.broadcasted_iota(jnp.int32, (L, L), 0)
    col = lax.broadcasted_iota(jnp.int32, (L, L), 1)
    sub = lax.broadcasted_iota(jnp.int32, (MLSTM_ST - MLSTM_DH, L), 0)

    dirs = ((qtf_ref, kf_ref, vtf_ref, grf_ref, gcf_ref, hf_ref), (qtb_ref, kb_ref, vtb_ref, grb_ref, gcb_ref, hb_ref))
    combos = [(d, hh) for d in range(2) for hh in range(MLSTM_HEADS)]


    masks, cum_rows, cum_cols, g_rows, g_cols = [], [], [], [], []
    for d in range(2):
        s_le_t = (row <= col) if d == 0 else (row >= col)
        tri = jnp.where((col <= row) if d == 0 else (col >= row), 1.0, 0.0).astype(BF16)
        tri_t = jnp.where(s_le_t, 1.0, 0.0).astype(BF16)
        gr = dirs[d][3][...]
        gc = dirs[d][4][...]
        r1, r2, r3 = _split3(gr)
        c1, c2, c3 = _split3(gc)
        masks.append(s_le_t)
        g_rows.append(gr)
        g_cols.append(gc)
        cum_rows.append(_dotf(tri, r1) + _dotf(tri, r2) + _dotf(tri, r3))
        cum_cols.append(_dotf(c1, tri_t) + _dotf(c2, tri_t) + _dotf(c3, tri_t))
    kq, inter, s_prev, m_prev = [], [], [], []
    for d, hh in combos:
        idx = d * MLSTM_HEADS + hh
        sl = slice(MLSTM_DH * hh, MLSTM_DH * (hh + 1))
        qth = dirs[d][0][sl, :]
        s_prev.append(s_scr[idx])
        m_prev.append(m_scr[idx:idx + 1, 0:1])
        kq.append(_dotf(dirs[d][1][:, sl], qth))
        inter.append(_dotf(s_prev[-1].astype(BF16), qth))

    sw, den, w_inter, floor, vw, decay, m_new = [], [], [], [], [], [], []
    for c, (d, hh) in enumerate(combos):
        il = (0 if d == 0 else 2 * MLSTM_HEADS) + hh
        fl = il + MLSTM_HEADS
        sl = slice(MLSTM_DH * hh, MLSTM_DH * (hh + 1))
        c_col = g_rows[d][:, il:il + 1] - cum_rows[d][:, fl:fl + 1]
        b_row = cum_cols[d][fl:fl + 1, :]
        c_row = g_cols[d][il:il + 1, :] - b_row
        c_mat = jnp.where(masks[d], c_col, NEG_BIG)
        u = jnp.maximum(jnp.max(c_mat, axis=0, keepdims=True), m_prev[c])
        s = kq[c] * jnp.exp(c_mat - u)
        sw.append(s.astype(BF16))
        den.append(jnp.sum(s, axis=0, keepdims=True))
        w_inter.append(jnp.exp(m_prev[c] - u))
        floor.append(jnp.exp(-(b_row + u)))
        b_end = b_row[:, L - 1:L] if d == 0 else b_row[:, 0:1]
        g_row = b_end + c_row
        m_new.append(jnp.maximum(b_end + m_prev[c], jnp.max(g_row, axis=-1, keepdims=True)))
        decay.append(jnp.exp(b_end + m_prev[c] - m_new[c]))
        w_s = jnp.exp(g_row - m_new[c])
        n_rows = jnp.where(sub == 0, w_s, 0.0)
        vw.append(jnp.concatenate([dirs[d][2][sl, :].astype(F32) * w_s, n_rows], axis=0).astype(BF16))

    for c, (d, hh) in enumerate(combos):
        idx = d * MLSTM_HEADS + hh
        sl = slice(MLSTM_DH * hh, MLSTM_DH * (hh + 1))
        num = _dotf(dirs[d][2][sl, :], sw[c]) + w_inter[c] * inter[c][:MLSTM_DH, :]
        dn = den[c] + w_inter[c] * inter[c][MLSTM_DH:MLSTM_DH + 1, :]
        dirs[d][5][sl, :] = (num / jnp.maximum(jnp.abs(dn), floor[c])).astype(BF16)
        s_scr[idx] = decay[c] * s_prev[c] + _dotf(vw[c], dirs[d][1][:, sl])
        m_scr[idx:idx + 1, :] = jnp.broadcast_to(m_new[c], (1, MLSTM_DH))

    @pl.when(j == nj - 1)
    def _():
        s_out_ref[...] = s_scr[...]
        m_out_ref[...] = m_scr[...]


def _mlstm(mqt, mk, mvt, g_rows, g_cols, s0, m0, batch, chunk):
    n = mk.shape[0]
    nc = n // batch // chunk
    fwd = lambda b, j: (b * nc + j, 0)
    bwd = lambda b, j: (b * nc + (nc - 1 - j), 0)
    fwd_t = lambda b, j: (0, b * nc + j)
    bwd_t = lambda b, j: (0, b * nc + (nc - 1 - j))
    nst = 2 * MLSTM_HEADS
    tok = lambda fn: pl.BlockSpec((chunk, MLSTM_WIDTH), fn)
    tok_t = lambda fn: pl.BlockSpec((MLSTM_WIDTH, chunk), fn)
    state_s = pl.BlockSpec((None, nst, MLSTM_ST, MLSTM_DH), lambda b, j: (b, 0, 0, 0))
    state_m = pl.BlockSpec((None, nst, MLSTM_DH), lambda b, j: (b, 0, 0))
    return pl.pallas_call(
        _mlstm_kernel,
        grid=(batch, nc),
        in_specs=[tok_t(fwd_t), tok(fwd), tok_t(fwd_t), pl.BlockSpec((chunk, N_GATES), fwd),
                  pl.BlockSpec((N_GATES, chunk), fwd_t),
                  tok_t(bwd_t), tok(bwd), tok_t(bwd_t), pl.BlockSpec((chunk, N_GATES), bwd),
                  pl.BlockSpec((N_GATES, chunk), bwd_t),
                  state_s, state_m],
        out_specs=[tok_t(fwd_t), tok_t(bwd_t), state_s, state_m],
        out_shape=[jax.ShapeDtypeStruct((MLSTM_WIDTH, n), BF16), jax.ShapeDtypeStruct((MLSTM_WIDTH, n), BF16),
                   jax.ShapeDtypeStruct(s0.shape, F32), jax.ShapeDtypeStruct(m0.shape, F32)],
        scratch_shapes=[pltpu.VMEM((nst, MLSTM_ST, MLSTM_DH), F32), pltpu.VMEM((nst, MLSTM_DH), F32)],
        compiler_params=_cparams("parallel", "arbitrary"),
        name="mlstm",
    )(mqt, mk, mvt, g_rows, g_cols, mqt, mk, mvt, g_rows, g_cols, s0, m0)


def _even_out_kernel(x_ref, gate_ref, att_ref, hf_ref, hb_ref, mo_ref, og_ref, w_ref,
                     fmod_ref, fg_ref, fwg_ref, fwu_ref, fwd_ref, o_ref):
    hs = hf_ref[...].astype(F32) + hb_ref[...].astype(F32)
    og = og_ref[...]
    parts = [att_ref[...]]
    for hh in range(MLSTM_HEADS):
        sl = slice(MLSTM_DH * hh, MLSTM_DH * (hh + 1))
        blk = hs[sl, :]
        hn = blk * lax.rsqrt(jnp.mean(blk * blk, axis=0, keepdims=True) + EPS) * og[sl, :]
        parts.append((hn * jax.nn.sigmoid(mo_ref[sl, :].astype(F32))).astype(BF16))
    mixed = jnp.concatenate(parts, axis=0)
    y = lax.dot_general(mixed, w_ref[...], (((0,), (0,)), ((), ())), preferred_element_type=F32)
    o_ref[...] = _ffn_math(x_ref[...] + gate_ref[...] * y, fmod_ref, fg_ref, fwg_ref, fwu_ref, fwd_ref)


def _even_out(x2d, gates, row_fn, att, hf, hb, mo, out_g, w_out, ffn_args, tm):
    n = x2d.shape[0]
    tok = pl.BlockSpec((tm, D_MODEL), lambda i: (i, 0))
    tok_t = pl.BlockSpec((MLSTM_WIDTH, tm), lambda i: (0, i))
    return pl.pallas_call(
        _even_out_kernel,
        grid=(n // tm,),
        in_specs=[tok, pl.BlockSpec((None, 1, D_MODEL), lambda i: (row_fn(i), 0, 0)),
                  tok_t, tok_t, tok_t, tok_t, _const_spec((MLSTM_WIDTH, 1)), _const_spec((D_MODEL, D_MODEL))]
        + _ffn_specs(row_fn, *ffn_args[0]),
        out_specs=tok,
        out_shape=jax.ShapeDtypeStruct((n, D_MODEL), F32),
        compiler_params=_cparams("parallel"),
        name="even_out_ffn",
    )(x2d, gates, att, hf, hb, mo, out_g.reshape(MLSTM_WIDTH, 1), w_out, *ffn_args[1:])


def _gelu_tanh(x):
    return 0.5 * x * (1.0 + jnp.tanh(0.7978845608028654 * (x + 0.044715 * (x * x * x))))


def _odd_in_kernel(x_ref, xp_ref, xn_ref, mod_ref, g_ref, w_ref, cw_ref, cb_ref, gate_ref, xc_ref, ext_scr, *, nt):
    tm = x_ref.shape[0]
    ti = pl.program_id(0) % nt
    g, shift, scale = g_ref[...], mod_ref[0:1, :], mod_ref[1:2, :]
    halo = jnp.concatenate([xp_ref[...], xn_ref[...]], axis=0)
    ph = _dotf(_norm_mod(halo, g, shift, scale).astype(BF16), w_ref[:, RNN_WIDTH:])
    ext_scr[0:8, :] = jnp.where(ti > 0, ph[0:8, :], 0.0)
    ext_scr[8 + tm:16 + tm, :] = jnp.where(ti < nt - 1, ph[8:16, :], 0.0)
    cw = cw_ref[...]
    cb = cb_ref[...]
    rc = tm // ODD_IN_CHUNKS

    def project(c):
        rows = slice(rc * c, rc * (c + 1))
        p = _dotf(_norm_mod(x_ref[rows, :], g, shift, scale).astype(BF16), w_ref[...])
        gate_ref[rows, :] = _gelu_tanh(p[:, :RNN_WIDTH]).astype(BF16)
        ext_scr[8 + rc * c:8 + rc * (c + 1), :] = p[:, RNN_WIDTH:]

    def conv(c):
        win = ext_scr[rc * c:rc * (c + 1) + 16, :]
        xc = cb + cw[2:3, :] * win[8:8 + rc, :]
        for k in (0, 1, 3):
            xc = xc + cw[k:k + 1, :] * pltpu.roll(win, (2 - k) % (rc + 16), axis=0)[8:8 + rc, :]
        xc_ref[rc * c:rc * (c + 1), :] = xc

    project(0)
    for c in range(1, ODD_IN_CHUNKS):
        project(c)
        conv(c - 1)
    conv(ODD_IN_CHUNKS - 1)


def _odd_in(x2d, mods, row_fn, g, w_in, conv_w, conv_b, batch, tm):
    n = x2d.shape[0]
    nt = n // batch // tm
    t8 = tm // 8
    tok = pl.BlockSpec((tm, D_MODEL), lambda i: (i, 0))
    prev = pl.BlockSpec((8, D_MODEL), lambda i: (jnp.maximum(i * t8 - 1, 0), 0))
    nxt = pl.BlockSpec((8, D_MODEL), lambda i: (jnp.minimum((i + 1) * t8, n // 8 - 1), 0))
    return pl.pallas_call(
        functools.partial(_odd_in_kernel, nt=nt),
        grid=(n // tm,),
        in_specs=[tok, prev, nxt, pl.BlockSpec((None, 3, D_MODEL), lambda i: (row_fn(i), 0, 0)),
                  _const_spec((1, D_MODEL)), _const_spec((D_MODEL, 2 * RNN_WIDTH)),
                  _const_spec((CONV_W, RNN_WIDTH)), _const_spec((1, RNN_WIDTH))],
        out_specs=[tok, tok],
        out_shape=[jax.ShapeDtypeStruct((n, RNN_WIDTH), BF16), jax.ShapeDtypeStruct((n, RNN_WIDTH), F32)],
        scratch_shapes=[pltpu.VMEM((tm + 16, RNN_WIDTH), F32)],
        compiler_params=_cparams("parallel"),
        name="odd_in",
    )(x2d, x2d, x2d, mods, g.reshape(1, D_MODEL), w_in, conv_w, conv_b)


def _rglru_kernel(xc_ref, wax_ref, ba_ref, bx_ref, lam_ref, h0_ref, h_ref, ht_ref, a_scr, b_scr, carry_scr, *, nt):
    d = pl.program_id(0)
    i = pl.program_id(2)
    tm = xc_ref.shape[0]
    nb8 = tm // 8

    xc = xc_ref[...]
    neg_lam = -lam_ref[...]
    half_rate = (-0.5 * LRU_C) * (jnp.maximum(neg_lam, 0.0) + jnp.log1p(jnp.exp(-jnp.abs(neg_lam))))
    ba = ba_ref[...]
    bx = bx_ref[...]
    for nb in range(RNN_BLOCKS):
        sl = slice(RNN_BLOCK_DIM * nb, RNN_BLOCK_DIM * (nb + 1))
        u = xc[:, sl]
        rg = _dotf(u.astype(BF16), wax_ref[nb])
        log_a = jnp.tanh(rg[:, :RNN_BLOCK_DIM] + ba[:, sl]) * half_rate[:, sl] + half_rate[:, sl]
        ig = 0.5 * jnp.tanh(rg[:, RNN_BLOCK_DIM:] + bx[:, sl]) + 0.5
        a = jnp.exp(log_a)
        a_scr[:, sl] = a
        b_scr[:, sl] = jnp.sqrt(-jnp.tanh(log_a) * (a * a + 1.0)) * (ig * u)

    @pl.when(i == 0)
    def _():
        carry_scr[...] = h0_ref[...]

    def scan_tile(reverse):
        a = a_scr[...].reshape(nb8, 8, RNN_WIDTH)
        b = b_scr[...].reshape(nb8, 8, RNN_WIDTH)
        r8 = lax.broadcasted_iota(jnp.int32, (nb8, 8, RNN_WIDTH), 1)
        for s in (1, 2, 4):
            keep = (r8 < 8 - s) if reverse else (r8 >= s)
            shift = 8 - s if reverse else s
            a_in = jnp.where(keep, pltpu.roll(a, shift, axis=1), 1.0)
            b_in = jnp.where(keep, pltpu.roll(b, shift, axis=1), 0.0)
            b = a * b_in + b
            a = a * a_in
        a_scr[...] = a.reshape(tm, RNN_WIDTH)
        b_scr[...] = b.reshape(tm, RNN_WIDTH)

        def block_pair(jp, h):
            base = pl.multiple_of((nb8 // 2 - 1 - jp if reverse else jp) * 16, 16)
            halves = [None, None]
            for half in ((1, 0) if reverse else (0, 1)):
                rows = pl.ds(base + 8 * half, 8)
                hb = a_scr[rows, :] * h + b_scr[rows, :]
                halves[half] = hb
                h = hb[0:1, :] if reverse else hb[7:8, :]
            h_ref[pl.ds(base, 16), :] = jnp.concatenate(halves, axis=0).astype(h_ref.dtype)
            return h

        carry_scr[...] = lax.fori_loop(0, nb8 // 2, block_pair, carry_scr[...], unroll=2)

    @pl.when(d == 0)
    def _():
        scan_tile(False)

    @pl.when(d == 1)
    def _():
        scan_tile(True)

    @pl.when(i == nt - 1)
    def _():
        ht_ref[...] = carry_scr[...]


def _rglru(xc2d, h0, w, batch, tm):
    n = xc2d.shape[0]
    nt = n // batch // tm
    tile = lambda d, i: jnp.where(d == 0, i, nt - 1 - i)
    per_dir = lambda shape: pl.BlockSpec((None,) + shape, lambda d, b, i: (d,) + (0,) * len(shape))
    state = pl.BlockSpec((None, None, 1, RNN_WIDTH), lambda d, b, i: (d, b, 0, 0))
    return pl.pallas_call(
        functools.partial(_rglru_kernel, nt=nt),
        grid=(2, batch, nt),
        in_specs=[pl.BlockSpec((tm, RNN_WIDTH), lambda d, b, i: (b * nt + tile(d, i), 0)),
                  per_dir((RNN_BLOCKS, RNN_BLOCK_DIM, 2 * RNN_BLOCK_DIM)), per_dir((1, RNN_WIDTH)), per_dir((1, RNN_WIDTH)),
                  per_dir((1, RNN_WIDTH)), state],
        out_specs=[pl.BlockSpec((None, tm, RNN_WIDTH), lambda d, b, i: (d, b * nt + tile(d, i), 0)), state],
        out_shape=[jax.ShapeDtypeStruct((2, n, RNN_WIDTH), BF16), jax.ShapeDtypeStruct((2, batch, 1, RNN_WIDTH), F32)],
        scratch_shapes=[pltpu.VMEM((tm, RNN_WIDTH), F32), pltpu.VMEM((tm, RNN_WIDTH), F32),
                        pltpu.VMEM((1, RNN_WIDTH), F32)],
        compiler_params=_cparams("parallel", "parallel", "arbitrary"),
        name="rglru",
    )(xc2d, w["w_ax"], w["b_a"], w["b_x"], w["lam"], h0)


def _odd_out_kernel(x_ref, gate_ref, h_ref, gg_ref, w_ref, fmod_ref, fg_ref, fwg_ref, fwu_ref, fwd_ref, o_ref):
    y = ((h_ref[0].astype(F32) + h_ref[1].astype(F32)) * gg_ref[...].astype(F32)).astype(BF16)
    x_mid = x_ref[...] + gate_ref[...] * _dotf(y, w_ref[...])
    o_ref[...] = _ffn_math(x_mid, fmod_ref, fg_ref, fwg_ref, fwu_ref, fwd_ref)


def _odd_out(x2d, gates, row_fn, h2, gate_branch, w_out, ffn_args, tm):
    n = x2d.shape[0]
    tok = pl.BlockSpec((tm, D_MODEL), lambda i: (i, 0))
    return pl.pallas_call(
        _odd_out_kernel,
        grid=(n // tm,),
        in_specs=[tok, pl.BlockSpec((None, 1, D_MODEL), lambda i: (row_fn(i), 0, 0)),
                  pl.BlockSpec((2, tm, RNN_WIDTH), lambda i: (0, i, 0)), tok, _const_spec((RNN_WIDTH, D_MODEL))]
        + _ffn_specs(row_fn, *ffn_args[0]),
        out_specs=tok,
        out_shape=jax.ShapeDtypeStruct((n, D_MODEL), F32),
        compiler_params=_cparams("parallel"),
        name="odd_out_ffn",
    )(x2d, gates, h2, gate_branch, w_out, *ffn_args[1:])


def _rope_tables(seq):
    t = np.arange(seq)
    inv = np.float32(ROPE_BASE) ** (-np.arange(0, ROPE_AXIS_DIM, 2, dtype=np.float32) / np.float32(ROPE_AXIS_DIM))
    ang_r = (t // GRID_W).astype(np.float32)[:, None] * inv
    ang_c = (t % GRID_W).astype(np.float32)[:, None] * inv
    cos32 = np.concatenate([np.cos(ang_r), np.cos(ang_r), np.cos(ang_c), np.cos(ang_c)], axis=1)
    sin32 = np.concatenate([-np.sin(ang_r), np.sin(ang_r), -np.sin(ang_c), np.sin(ang_c)], axis=1)
    pad = HEAD_BLOCK - MLA_QK
    cos = np.concatenate([np.ones((seq, MLA_NOPE), np.float32), cos32, np.ones((seq, pad), np.float32)], axis=1)
    sin = np.concatenate([np.zeros((seq, MLA_NOPE), np.float32), sin32, np.zeros((seq, pad), np.float32)], axis=1)
    cost = np.concatenate([np.cos(ang_r), np.cos(ang_c)], axis=1).T
    sint = np.concatenate([np.sin(ang_r), np.sin(ang_c)], axis=1).T
    return tuple(jnp.asarray(a, F32) for a in (cos, sin, cost, sint))


def _even_weights(w_in, cq_g, w_uq, ckv_g, w_ukv, q_g, k_g, gate_b):
    pad = HEAD_BLOCK - MLA_QK
    w_in, w_uq, w_ukv = w_in.astype(BF16), w_uq.astype(BF16), w_ukv.astype(BF16)
    kr_cols = w_in[:, OFF_KR:OFF_KR + MLA_ROPE]
    w_kr = jnp.pad(kr_cols, ((0, 0), (MLA_NOPE, pad)))
    m_cols = w_in[:, OFF_KR + MLA_ROPE:OFF_KR + MLA_ROPE + 4 * MLSTM_WIDTH]
    g_cols = w_in[:, OFF_KR + MLA_ROPE + 4 * MLSTM_WIDTH:]
    mq_cols, mk_cols, mv_cols, mo_cols = jnp.split(m_cols, 4, axis=1)
    w_all = jnp.concatenate([w_in[:, :OFF_KR], w_kr, mk_cols, jnp.pad(g_cols, ((0, 0), (0, HEAD_BLOCK - N_GATES)))], axis=1)
    uq = jnp.pad(w_uq.reshape(MLA_Q_RANK, MLA_HEADS, MLA_QK), ((0, 0), (0, 0), (0, pad)))
    ukv = w_ukv.reshape(MLA_KV_RANK, MLA_HEADS, MLA_NOPE + MLA_V)
    uk = jnp.pad(ukv[:, :, :MLA_NOPE], ((0, 0), (0, 0), (0, HEAD_BLOCK - MLA_NOPE)))
    uv = ukv[:, :, MLA_NOPE:]
    kg_blk = jnp.pad(k_g[:MLA_NOPE], (0, HEAD_BLOCK - MLA_NOPE))
    krg_blk = jnp.pad(k_g[MLA_NOPE:], (MLA_NOPE, pad))
    return {
        "w_all": w_all,
        "w_gt": jnp.concatenate([g_cols, mq_cols, mv_cols, mo_cols], axis=1).T,
        "gb_row": gate_b.reshape(1, N_GATES),
        "gb_col": gate_b.reshape(N_GATES, 1),
        "cq_g": cq_g.reshape(1, MLA_Q_RANK),
        "w_uqt": uq.reshape(MLA_Q_RANK, MLA_HEADS * HEAD_BLOCK).T,
        "ckv_g": ckv_g.reshape(1, MLA_KV_RANK),
        "w_uk": uk.reshape(MLA_KV_RANK, MLA_HEADS * HEAD_BLOCK),
        "w_uvt": uv.reshape(MLA_KV_RANK, MLA_HEADS * MLA_V).T,
        "qg_nope": q_g[:MLA_NOPE].reshape(MLA_NOPE, 1),
        "qg_rope": q_g[MLA_NOPE:].reshape(MLA_ROPE, 1),
        "k_g": jnp.tile(kg_blk, MLA_HEADS).reshape(1, -1),
        "kr_g": krg_blk.reshape(1, HEAD_BLOCK),
    }


def kernel(x, c, ctx, c_ctx, mod_w, mod_b, norm_g, ffn_w_gate, ffn_w_up, ffn_w_down, even_w_in, even_w_out, mla_cq_g, mla_w_uq, mla_ckv_g, mla_w_ukv, mla_q_g, mla_k_g, mlstm_gate_b, mlstm_out_g, odd_w_in, odd_conv_w, odd_conv_b, lru_w_a, lru_b_a, lru_w_x, lru_b_x, lru_lam, odd_w_out):
    batch, seq, _ = x.shape
    ctx_len = ctx.shape[1]
    depth = mod_w.shape[0]
    assert depth == 2 and batch <= 7

    cond8 = jnp.zeros((8, D_MODEL), F32).at[:batch].set(c).at[batch].set(c_ctx)
    mods = _ada_params(cond8, mod_w, mod_b)
    mods = mods.reshape(depth, 8, 3, 3, D_MODEL).transpose(0, 2, 1, 3, 4)

    tm_x, tm_c = 512, 256
    x_row = lambda i: i // (seq // tm_x)
    c_row = lambda i: batch
    x2 = x.reshape(batch * seq, D_MODEL)
    c2 = ctx.reshape(batch * ctx_len, D_MODEL)

    wg = ffn_w_gate.astype(BF16)
    wu = ffn_w_up.astype(BF16)
    wd = ffn_w_down.astype(BF16)

    def ffn_args(layer, which):
        return ((layer, which), mods[layer, 2 * which], norm_g[layer, 2 * which].reshape(1, D_MODEL), wg, wu, wd)

    x2 = _ffn(x2, ffn_args(0, 0), x_row, tm_x)
    c2 = _ffn(c2, ffn_args(0, 0), c_row, tm_c)

    ew = _even_weights(even_w_in[0], mla_cq_g[0], mla_w_uq[0], mla_ckv_g[0], mla_w_ukv[0], mla_q_g[0], mla_k_g[0],
                       mlstm_gate_b[0])
    tabs_x = _rope_tables(seq)
    tabs_c = (jnp.ones((tm_c, HEAD_BLOCK), F32), jnp.zeros((tm_c, HEAD_BLOCK), F32),
              jnp.ones((ROPE_AXIS_DIM, tm_c), F32), jnp.zeros((ROPE_AXIS_DIM, tm_c), F32))
    key_len = seq + ctx_len
    px = _even_prep(x2, mods[0, 1], x_row, norm_g[0, 1], tabs_x, lambda i: i % (seq // tm_x), ew, tm_x,
                    batch, key_len, 0)
    qt_x, k_all, vt_all, mqt_x, mk_x, mvt_x, mot_x, gr_x, gc_x = px
    pc = _even_prep(c2, mods[0, 1], c_row, norm_g[0, 1], tabs_c, lambda i: 0, ew, tm_c,
                    batch, key_len, seq, kv_bufs=(k_all, vt_all))
    qt_c, k_all, vt_all, mqt_c, mk_c, mvt_c, mot_c, gr_c, gc_c = pc

    att_x = _attention(qt_x, k_all, vt_all, batch, 512, key_len, 0)
    att_c = _attention(qt_c, k_all, vt_all, batch, 256, ctx_len, seq)

    chunk = 256
    nst = 2 * MLSTM_HEADS
    s0 = jnp.zeros((batch, nst, MLSTM_ST, MLSTM_DH), F32)
    m0 = jnp.zeros((batch, nst, MLSTM_DH), F32)
    hcf, hcb, s_c, m_c = _mlstm(mqt_c, mk_c, mvt_c, gr_c, gc_c, s0, m0, batch, chunk)
    hxf, hxb, _, _ = _mlstm(mqt_x, mk_x, mvt_x, gr_x, gc_x, s_c, m_c, batch, chunk)

    w_out0 = even_w_out[0].astype(BF16)
    gate0 = mods[0, 1][:, 2:3, :]
    x2 = _even_out(x2, gate0, x_row, att_x, hxf, hxb, mot_x, mlstm_out_g[0], w_out0, ffn_args(0, 1), tm_x)
    c2 = _even_out(c2, gate0, c_row, att_c, hcf, hcb, mot_c, mlstm_out_g[0], w_out0, ffn_args(0, 1), tm_c)

    x2 = _ffn(x2, ffn_args(1, 0), x_row, tm_x)
    c2 = _ffn(c2, ffn_args(1, 0), c_row, tm_c)
    w_in1 = odd_w_in[0].astype(BF16)
    conv_b = odd_conv_b[0].reshape(1, RNN_WIDTH)
    gg_x, xc_x = _odd_in(x2, mods[1, 1], x_row, norm_g[1, 1], w_in1, odd_conv_w[0], conv_b, batch, tm_x)
    _, xc_c = _odd_in(c2, mods[1, 1], c_row, norm_g[1, 1], w_in1, odd_conv_w[0], conv_b, batch, tm_c)
    rw = {
        "w_ax": (0.5 * jnp.concatenate([lru_w_a[0], lru_w_x[0]], axis=-1)).astype(BF16),
        "b_a": 0.5 * lru_b_a[0].reshape(2, 1, RNN_WIDTH),
        "b_x": 0.5 * lru_b_x[0].reshape(2, 1, RNN_WIDTH),
        "lam": lru_lam[0].reshape(2, 1, RNN_WIDTH),
    }
    h0 = jnp.zeros((2, batch, 1, RNN_WIDTH), F32)
    _, st_c = _rglru(xc_c, h0, rw, batch, 256)
    h_x, _ = _rglru(xc_x, st_c, rw, batch, 256)
    x2 = _odd_out(x2, mods[1, 1][:, 2:3, :], x_row, h_x, gg_x, odd_w_out[0].astype(BF16), ffn_args(1, 1), tm_x)
    return x2.reshape(batch, seq, D_MODEL)
```

```python
import functools

import jax
import jax.numpy as jnp
import numpy as np
from jax import lax
from jax.experimental import pallas as pl
from jax.experimental.pallas import tpu as pltpu

F32 = jnp.float32
BF16 = jnp.bfloat16

D_MODEL = 1024
GRID_W = 64
EPS = 1e-6
D_FF = 2816
FFN_RESIDUAL = 0.5

MLA_HEADS = 8
MLA_Q_RANK = 384
MLA_KV_RANK = 256
MLA_NOPE = 64
MLA_ROPE = 32
MLA_V = 64
MLA_QK = MLA_NOPE + MLA_ROPE
ROPE_AXIS_DIM = MLA_ROPE // 2
ROPE_BASE = 10000.0
HEAD_BLOCK = 128
V_ROWS = MLA_V + 16

MLSTM_HEADS = 4
MLSTM_DH = 128
MLSTM_WIDTH = MLSTM_HEADS * MLSTM_DH
N_GATES = 4 * MLSTM_HEADS
MLSTM_ST = MLSTM_DH + 16

RNN_WIDTH = 1024
RNN_BLOCKS = 8
RNN_BLOCK_DIM = RNN_WIDTH // RNN_BLOCKS
CONV_W = 4
LRU_C = 8.0

NEG_BIG = -1e30
LOG2_E = 1.4426950408889634
VMEM_LIMIT = 56 * 1024 * 1024

ROWS_X = 512
ROWS_X_FFN = 1024
ROWS_C = 256
FFN_CHUNK_ROWS = 128
ODD_IN_CHUNKS = 4
KEY_CHUNK = 256
ATTN_HEADS_PER_STEP = 2
ATTN_QUERIES_X = 512
MLSTM_CHUNK = 256
RGLRU_ROWS = 256


def _cparams(*sem):
    return pltpu.CompilerParams(dimension_semantics=sem, vmem_limit_bytes=VMEM_LIMIT)


def _const_spec(shape):
    zeros = (0,) * len(shape)
    return pl.BlockSpec(shape, lambda *_: zeros, pipeline_mode=pl.Buffered(1))


def _dotf(a, b):
    return jnp.dot(a, b, preferred_element_type=F32)


def _split3(a):
    a1 = a.astype(BF16)
    r1 = a - a1.astype(F32)
    a2 = r1.astype(BF16)
    a3 = (r1 - a2.astype(F32)).astype(BF16)
    return a1, a2, a3


def _norm_mod(x, g, shift, scale):
    ms = jnp.mean(x * x, axis=-1, keepdims=True)
    return (x * lax.rsqrt(ms + EPS)) * (g * (1.0 + scale)) + shift


def _rms(x, g):
    ms = jnp.mean(x * x, axis=-1, keepdims=True)
    return x * lax.rsqrt(ms + EPS) * g


def _log_sigmoid(x):
    return jnp.minimum(x, 0.0) - jnp.log1p(jnp.exp(-jnp.abs(x)))


def _ada_kernel(cond_ref, w_ref, b_ref, o_ref):
    c = cond_ref[...]
    s = c * jax.nn.sigmoid(c)
    s1, s2, _ = _split3(s)
    w = w_ref[...]
    w1 = w.astype(BF16)
    w2 = (w - w1.astype(F32)).astype(BF16)
    o_ref[...] = _dotf(s1, w1) + _dotf(s1, w2) + _dotf(s2, w1) + b_ref[...]


def _ada_params(cond8, mod_w, mod_b):
    depth, _, n = mod_w.shape
    tn = 1152
    return pl.pallas_call(
        _ada_kernel,
        grid=(depth, n // tn),
        in_specs=[
            pl.BlockSpec((8, D_MODEL), lambda l, j: (0, 0)),
            pl.BlockSpec((None, D_MODEL, tn), lambda l, j: (l, 0, j)),
            pl.BlockSpec((None, 1, tn), lambda l, j: (l, 0, j)),
        ],
        out_specs=pl.BlockSpec((None, 8, tn), lambda l, j: (l, 0, j)),
        out_shape=jax.ShapeDtypeStruct((depth, 8, n), F32),
        compiler_params=_cparams("parallel", "parallel"),
        name="ada_params",
    )(cond8, mod_w, mod_b.reshape(depth, 1, n))


def _ffn_math(x, mod_ref, g_ref, wg_ref, wu_ref, wd_ref):
    outs = []
    rc = min(FFN_CHUNK_ROWS, x.shape[0])
    for c in range(x.shape[0] // rc):
        xc = x[rc * c:rc * (c + 1), :]
        h = _norm_mod(xc, g_ref[...], mod_ref[0:1, :], mod_ref[1:2, :]).astype(BF16)
        g = _dotf(h, wg_ref[...])
        u = _dotf(h, wu_ref[...])
        a = (g * jax.nn.sigmoid(g) * u).astype(BF16)
        outs.append(xc + (FFN_RESIDUAL * mod_ref[2:3, :]) * _dotf(a, wd_ref[...]))
    return jnp.concatenate(outs, axis=0)


def _ffn_specs(row_fn, layer, which):
    weight = lambda r, c: pl.BlockSpec((None, None, r, c), lambda i: (layer, which, 0, 0), pipeline_mode=pl.Buffered(1))
    return [pl.BlockSpec((None, 3, D_MODEL), lambda i: (row_fn(i), 0, 0)), _const_spec((1, D_MODEL)),
            weight(D_MODEL, D_FF), weight(D_MODEL, D_FF), weight(D_FF, D_MODEL)]


def _ffn_kernel(x_ref, mod_ref, g_ref, wg_ref, wu_ref, wd_ref, o_ref):
    o_ref[...] = _ffn_math(x_ref[...], mod_ref, g_ref, wg_ref, wu_ref, wd_ref)


def _ffn(x2d, ffn_args, row_fn, tm):
    n = x2d.shape[0]
    tok = pl.BlockSpec((tm, D_MODEL), lambda i: (i, 0))
    return pl.pallas_call(
        _ffn_kernel,
        grid=(n // tm,),
        in_specs=[tok] + _ffn_specs(row_fn, *ffn_args[0]),
        out_specs=tok,
        out_shape=jax.ShapeDtypeStruct((n, D_MODEL), F32),
        compiler_params=_cparams("parallel"),
        name="ffn",
    )(x2d, *ffn_args[1:])


W_ALL = MLA_Q_RANK + MLA_KV_RANK + HEAD_BLOCK + MLSTM_WIDTH + HEAD_BLOCK
OFF_CKV = MLA_Q_RANK
OFF_KR = OFF_CKV + MLA_KV_RANK
OFF_MK = OFF_KR + HEAD_BLOCK
OFF_GATES = OFF_MK + MLSTM_WIDTH
W_T_ROWS = N_GATES + 3 * MLSTM_WIDTH


def _rope_block(xb, cos, sin, first_half):
    partner = jnp.where(first_half, pltpu.roll(xb, HEAD_BLOCK - 8, axis=1), pltpu.roll(xb, 8, axis=1))
    return xb * cos + partner * sin


def _even_prep_kernel(x_ref, mod_ref, g_ref, cos_ref, sin_ref, cost_ref, sint_ref, wall_ref, wgt_ref, gbr_ref, gbc_ref,
                      cqg_ref, wuqt_ref, ckvg_ref, wuk_ref, wuvt_ref, qgn_ref, qgr_ref, kg_ref, krg_ref,
                      qt_ref, k_ref, vt_ref, mqt_ref, mk_ref, mvt_ref, mot_ref, gr_ref, gc_ref):
    tm = x_ref.shape[0]
    x = x_ref[...]
    h = _norm_mod(x, g_ref[...], mod_ref[0:1, :], mod_ref[1:2, :]).astype(BF16)
    p = _dotf(h, wall_ref[...])
    pt = lax.dot_general(wgt_ref[...], h, (((1,), (1,)), ((), ())), preferred_element_type=F32)

    mk_ref[...] = (p[:, OFF_MK:OFF_MK + MLSTM_WIDTH] * (MLSTM_DH ** -0.5)).astype(BF16)
    mqt_ref[...] = pt[N_GATES:N_GATES + MLSTM_WIDTH, :].astype(BF16)
    mvt_ref[...] = pt[N_GATES + MLSTM_WIDTH:N_GATES + 2 * MLSTM_WIDTH, :].astype(BF16)
    mot_ref[...] = pt[N_GATES + 2 * MLSTM_WIDTH:, :].astype(BF16)
    graw = p[:, OFF_GATES:OFF_GATES + HEAD_BLOCK][:, :N_GATES] + gbr_ref[...]
    lane = lax.broadcasted_iota(jnp.int32, (tm, N_GATES), 1)
    gr_ref[...] = jnp.where((lane // MLSTM_HEADS) % 2 == 1, _log_sigmoid(graw), graw)
    gt = pt[:N_GATES, :] + gbc_ref[...]
    sub = lax.broadcasted_iota(jnp.int32, (N_GATES, tm), 0)
    gc_ref[...] = jnp.where((sub // MLSTM_HEADS) % 2 == 1, _log_sigmoid(gt), gt)

    cos = cos_ref[...]
    sin = sin_ref[...]
    lane_b = lax.broadcasted_iota(jnp.int32, (tm, HEAD_BLOCK), 1)
    first_half = (lane_b % ROPE_AXIS_DIM) < (ROPE_AXIS_DIM // 2)

    cqn = _rms(p[:, 0:MLA_Q_RANK], cqg_ref[...]).astype(BF16)
    ckvn = _rms(p[:, OFF_CKV:OFF_CKV + MLA_KV_RANK], ckvg_ref[...]).astype(BF16)
    nt_dims = (((1,), (1,)), ((), ()))
    k_raw = _dotf(ckvn, wuk_ref[...])
    vt = lax.dot_general(wuvt_ref[...], ckvn, nt_dims, preferred_element_type=F32).astype(BF16)
    ones_rows = jnp.where(lax.broadcasted_iota(jnp.int32, (V_ROWS - MLA_V, KEY_CHUNK), 0) == 0, 1.0, 0.0).astype(BF16)
    for cc in range(tm // KEY_CHUNK):
        for hh in range(MLA_HEADS):
            vt_ref[cc, V_ROWS * hh:V_ROWS * hh + MLA_V, :] = vt[MLA_V * hh:MLA_V * (hh + 1),
                                                                KEY_CHUNK * cc:KEY_CHUNK * (cc + 1)]
            vt_ref[cc, V_ROWS * hh + MLA_V:V_ROWS * (hh + 1), :] = ones_rows

    kr = p[:, OFF_KR:OFF_KR + HEAD_BLOCK]
    kr_ms = jnp.sum(kr * kr, axis=-1, keepdims=True) * (1.0 / MLA_ROPE)
    kr_rot = _rope_block(kr * lax.rsqrt(kr_ms + EPS) * krg_ref[...], cos, sin, first_half)
    kg = kg_ref[...]
    for hh in range(MLA_HEADS):
        sl = slice(HEAD_BLOCK * hh, HEAD_BLOCK * (hh + 1))
        kb = k_raw[:, sl]
        kb_ms = jnp.sum(kb * kb, axis=-1, keepdims=True) * (1.0 / MLA_NOPE)
        k_ref[:, sl] = (kb * lax.rsqrt(kb_ms + EPS) * kg[:, sl] + kr_rot).astype(BF16)

    qt_raw = lax.dot_general(wuqt_ref[...], cqn, nt_dims, preferred_element_type=F32)
    q_scale = (MLA_QK ** -0.5) * LOG2_E
    half = ROPE_AXIS_DIM // 2
    cost = cost_ref[...]
    sint = sint_ref[...]
    for hh in range(MLA_HEADS):
        r0 = HEAD_BLOCK * hh
        nope = qt_raw[r0:r0 + MLA_NOPE, :]
        rope = qt_raw[r0 + MLA_NOPE:r0 + MLA_QK, :]
        nope = nope * lax.rsqrt(jnp.mean(nope * nope, axis=0, keepdims=True) + EPS) * qgn_ref[...]
        rope = rope * lax.rsqrt(jnp.mean(rope * rope, axis=0, keepdims=True) + EPS) * qgr_ref[...]
        rot = []
        for ax in range(2):
            x1 = rope[ROPE_AXIS_DIM * ax:ROPE_AXIS_DIM * ax + half, :]
            x2 = rope[ROPE_AXIS_DIM * ax + half:ROPE_AXIS_DIM * (ax + 1), :]
            cs = cost[half * ax:half * (ax + 1), :]
            sn = sint[half * ax:half * (ax + 1), :]
            rot += [x1 * cs - x2 * sn, x2 * cs + x1 * sn]
        qt_ref[r0:r0 + MLA_NOPE, :] = (nope * q_scale).astype(BF16)
        qt_ref[r0 + MLA_NOPE:r0 + MLA_QK, :] = (jnp.concatenate(rot, axis=0) * q_scale).astype(BF16)
        qt_ref[r0 + MLA_QK:r0 + HEAD_BLOCK, :] = jnp.zeros((HEAD_BLOCK - MLA_QK, tm), BF16)


def _even_prep_into_kernel(*refs):
    n_in = 20
    _even_prep_kernel(*refs[:n_in], *refs[n_in + 2:])


def _even_prep(x2d, mods, row_fn, g, tabs, tab_fn, w, tm, batch, key_len, key_off, kv_bufs=None):
    n = x2d.shape[0]
    nt = n // batch // tm
    assert key_off % tm == 0 and tm % KEY_CHUNK == 0
    cos, sin, cost, sint = tabs
    tok = lambda width: pl.BlockSpec((tm, width), lambda i: (i, 0))
    tok_t = lambda rows: pl.BlockSpec((rows, tm), lambda i: (0, i))
    k_spec = pl.BlockSpec((None, tm, MLA_HEADS * HEAD_BLOCK), lambda i: (i // nt, key_off // tm + i % nt, 0))
    vt_spec = pl.BlockSpec((None, tm // KEY_CHUNK, MLA_HEADS * V_ROWS, KEY_CHUNK),
                           lambda i: (i // nt, key_off // tm + i % nt, 0, 0))
    out_shape = [
        jax.ShapeDtypeStruct((MLA_HEADS * HEAD_BLOCK, n), BF16),
        jax.ShapeDtypeStruct((batch, key_len, MLA_HEADS * HEAD_BLOCK), BF16),
        jax.ShapeDtypeStruct((batch, key_len // KEY_CHUNK, MLA_HEADS * V_ROWS, KEY_CHUNK), BF16),
        jax.ShapeDtypeStruct((MLSTM_WIDTH, n), BF16),
        jax.ShapeDtypeStruct((n, MLSTM_WIDTH), BF16),
        jax.ShapeDtypeStruct((MLSTM_WIDTH, n), BF16),
        jax.ShapeDtypeStruct((MLSTM_WIDTH, n), BF16),
        jax.ShapeDtypeStruct((n, N_GATES), F32),
        jax.ShapeDtypeStruct((N_GATES, n), F32),
    ]
    out_specs = [tok_t(1024), k_spec, vt_spec, tok_t(512), tok(512), tok_t(512), tok_t(512),
                 tok(N_GATES), tok_t(N_GATES)]
    in_specs = [
        tok(D_MODEL),
        pl.BlockSpec((None, 3, D_MODEL), lambda i: (row_fn(i), 0, 0)),
        _const_spec((1, D_MODEL)),
        pl.BlockSpec((tm, HEAD_BLOCK), lambda i: (tab_fn(i), 0)),
        pl.BlockSpec((tm, HEAD_BLOCK), lambda i: (tab_fn(i), 0)),
        pl.BlockSpec((ROPE_AXIS_DIM, tm), lambda i: (0, tab_fn(i))),
        pl.BlockSpec((ROPE_AXIS_DIM, tm), lambda i: (0, tab_fn(i))),
        _const_spec((D_MODEL, W_ALL)),
        _const_spec((W_T_ROWS, D_MODEL)),
        _const_spec((1, N_GATES)),
        _const_spec((N_GATES, 1)),
        _const_spec((1, MLA_Q_RANK)),
        _const_spec((1024, MLA_Q_RANK)),
        _const_spec((1, MLA_KV_RANK)),
        _const_spec((MLA_KV_RANK, 1024)),
        _const_spec((512, MLA_KV_RANK)),
        _const_spec((MLA_NOPE, 1)),
        _const_spec((MLA_ROPE, 1)),
        _const_spec((1, 1024)),
        _const_spec((1, HEAD_BLOCK)),
    ]
    args = [x2d, mods, g.reshape(1, D_MODEL), cos, sin, cost, sint, w["w_all"], w["w_gt"], w["gb_row"], w["gb_col"],
            w["cq_g"], w["w_uqt"], w["ckv_g"], w["w_uk"], w["w_uvt"], w["qg_nope"], w["qg_rope"], w["k_g"], w["kr_g"]]
    aliases = {}
    if kv_bufs is not None:
        aliases = {len(args): 1, len(args) + 1: 2}
        in_specs = in_specs + [pl.BlockSpec(memory_space=pl.ANY)] * 2
        args = args + list(kv_bufs)
    return pl.pallas_call(
        _even_prep_kernel if kv_bufs is None else _even_prep_into_kernel,
        grid=(n // tm,),
        in_specs=in_specs,
        out_specs=out_specs,
        out_shape=out_shape,
        input_output_aliases=aliases,
        compiler_params=_cparams("parallel"),
        name="even_prep",
    )(*args)


def _attn_kernel(qt_ref, k_ref, vt_ref, o_ref, st_scr, acc_scr):
    tq = qt_ref.shape[1]
    nk, _, tk = vt_ref.shape
    unroll = max(u for u in (16, 8, 4, 2) if (nk - 1) % u == 0)
    heads = range(acc_scr.shape[0])

    def scores(j, slot):
        off = pl.multiple_of(j * tk, tk)
        cms = []
        for hh in heads:
            sl = slice(HEAD_BLOCK * hh, HEAD_BLOCK * (hh + 1))
            st = _dotf(k_ref[pl.ds(off, tk), sl], qt_ref[sl, :])
            st_scr[slot, hh] = st
            cms.append(jnp.max(st, axis=0, keepdims=True))
        return tuple(cms)

    def absorb(j, slot, cms, ms):
        out = []
        for hh in heads:
            m_new = jnp.maximum(ms[hh], cms[hh])
            alpha = jnp.exp2(ms[hh] - m_new)
            pt = jnp.exp2(st_scr[slot, hh] - m_new)
            pv = _dotf(vt_ref[j, V_ROWS * hh:V_ROWS * (hh + 1), :], pt.astype(BF16))
            acc_scr[hh] = alpha * acc_scr[hh] + pv
            out.append(m_new)
        return tuple(out)

    def body(i, carry):
        cms, ms = carry
        for r in range(unroll):
            j = unroll * i + r
            cms_next = scores(j + 1, (r + 1) % 2)
            ms = absorb(j, r % 2, cms, ms)
            cms = cms_next
        return cms, ms

    acc_scr[...] = jnp.zeros(acc_scr.shape, F32)
    m0 = jnp.full((1, tq), NEG_BIG, F32)
    cms, ms = lax.fori_loop(0, (nk - 1) // unroll, body, (scores(0, 0), (m0,) * len(heads)))
    absorb(nk - 1, 0, cms, ms)
    for hh in heads:
        acc = acc_scr[hh]
        o_ref[MLA_V * hh:MLA_V * (hh + 1), :] = (acc[:MLA_V, :] / acc[MLA_V:MLA_V + 1, :]).astype(o_ref.dtype)


def _attention(qt, k3, vt4, batch, tq, key_len, key_off):
    nq = qt.shape[1] // batch // tq
    assert key_off % key_len == 0 and key_len % KEY_CHUNK == 0
    kb = key_off // key_len
    nk = key_len // KEY_CHUNK
    tk = KEY_CHUNK
    g = ATTN_HEADS_PER_STEP
    return pl.pallas_call(
        _attn_kernel,
        grid=(batch, MLA_HEADS // g, nq),
        in_specs=[
            pl.BlockSpec((g * HEAD_BLOCK, tq), lambda b, hp, i: (hp, b * nq + i)),
            pl.BlockSpec((None, key_len, g * HEAD_BLOCK), lambda b, hp, i: (b, kb, hp)),
            pl.BlockSpec((None, nk, g * V_ROWS, tk), lambda b, hp, i: (b, kb, hp, 0)),
        ],
        out_specs=pl.BlockSpec((g * MLA_V, tq), lambda b, hp, i: (hp, b * nq + i)),
        out_shape=jax.ShapeDtypeStruct((MLA_HEADS * MLA_V, qt.shape[1]), BF16),
        scratch_shapes=[pltpu.VMEM((2, g, tk, tq), F32), pltpu.VMEM((g, V_ROWS, tq), F32)],
        compiler_params=_cparams("parallel", "parallel", "arbitrary"),
        name="mla_attention",
    )(qt, k3, vt4)


def _mlstm_kernel(qtf_ref, kf_ref, vtf_ref, grf_ref, gcf_ref, qtb_ref, kb_ref, vtb_ref, grb_ref, gcb_ref,
                  s0_ref, m0_ref, hf_ref, hb_ref, s_out_ref, m_out_ref, s_scr, m_scr):
    j = pl.program_id(1)
    nj = pl.num_programs(1)
    L = kf_ref.shape[0]

    @pl.when(j == 0)
    def _():
        s_scr[...] = s0_ref[...]
        m_scr[...] = m0_ref[...]

    row = lax.broadcasted_iota(jnp.int32, (L, L), 0)
    col = lax.broadcasted_iota(jnp.int32, (L, L), 1)
    sub = lax.broadcasted_iota(jnp.int32, (MLSTM_ST - MLSTM_DH, L), 0)

    dirs = ((qtf_ref, kf_ref, vtf_ref, grf_ref, gcf_ref, hf_ref), (qtb_ref, kb_ref, vtb_ref, grb_ref, gcb_ref, hb_ref))
    combos = [(d, hh) for d in range(2) for hh in range(MLSTM_HEADS)]


    masks, cum_rows, cum_cols, g_rows, g_cols = [], [], [], [], []
    for d in range(2):
        s_le_t = (row <= col) if d == 0 else (row >= col)
        tri = jnp.where((col <= row) if d == 0 else (col >= row), 1.0, 0.0).astype(BF16)
        tri_t = jnp.where(s_le_t, 1.0, 0.0).astype(BF16)
        gr = dirs[d][3][...]
        gc = dirs[d][4][...]
        r1, r2, r3 = _split3(gr)
        c1, c2, c3 = _split3(gc)
        masks.append(s_le_t)
        g_rows.append(gr)
        g_cols.append(gc)
        cum_rows.append(_dotf(tri, r1) + _dotf(tri, r2) + _dotf(tri, r3))
        cum_cols.append(_dotf(c1, tri_t) + _dotf(c2, tri_t) + _dotf(c3, tri_t))
    kq, inter, s_prev, m_prev = [], [], [], []
    for d, hh in combos:
        idx = d * MLSTM_HEADS + hh
        sl = slice(MLSTM_DH * hh, MLSTM_DH * (hh + 1))
        qth = dirs[d][0][sl, :]
        s_prev.append(s_scr[idx])
        m_prev.append(m_scr[idx:idx + 1, 0:1])
        kq.append(_dotf(dirs[d][1][:, sl], qth))
        inter.append(_dotf(s_prev[-1].astype(BF16), qth))

    sw, den, w_inter, floor, vw, decay, m_new = [], [], [], [], [], [], []
    for c, (d, hh) in enumerate(combos):
        il = (0 if d == 0 else 2 * MLSTM_HEADS) + hh
        fl = il + MLSTM_HEADS
        sl = slice(MLSTM_DH * hh, MLSTM_DH * (hh + 1))
        c_col = g_rows[d][:, il:il + 1] - cum_rows[d][:, fl:fl + 1]
        b_row = cum_cols[d][fl:fl + 1, :]
        c_row = g_cols[d][il:il + 1, :] - b_row
        c_mat = jnp.where(masks[d], c_col, NEG_BIG)
        u = jnp.maximum(jnp.max(c_mat, axis=0, keepdims=True), m_prev[c])
        s = kq[c] * jnp.exp(c_mat - u)
        sw.append(s.astype(BF16))
        den.append(jnp.sum(s, axis=0, keepdims=True))
        w_inter.append(jnp.exp(m_prev[c] - u))
        floor.append(jnp.exp(-(b_row + u)))
        b_end = b_row[:, L - 1:L] if d == 0 else b_row[:, 0:1]
        g_row = b_end + c_row
        m_new.append(jnp.maximum(b_end + m_prev[c], jnp.max(g_row, axis=-1, keepdims=True)))
        decay.append(jnp.exp(b_end + m_prev[c] - m_new[c]))
        w_s = jnp.exp(g_row - m_new[c])
        n_rows = jnp.where(sub == 0, w_s, 0.0)
        vw.append(jnp.concatenate([dirs[d][2][sl, :].astype(F32) * w_s, n_rows], axis=0).astype(BF16))

    for c, (d, hh) in enumerate(combos):
        idx = d * MLSTM_HEADS + hh
        sl = slice(MLSTM_DH * hh, MLSTM_DH * (hh + 1))
        num = _dotf(dirs[d][2][sl, :], sw[c]) + w_inter[c] * inter[c][:MLSTM_DH, :]
        dn = den[c] + w_inter[c] * inter[c][MLSTM_DH:MLSTM_DH + 1, :]
        dirs[d][5][sl, :] = (num / jnp.maximum(jnp.abs(dn), floor[c])).astype(BF16)
        s_scr[idx] = decay[c] * s_prev[c] + _dotf(vw[c], dirs[d][1][:, sl])
        m_scr[idx:idx + 1, :] = jnp.broadcast_to(m_new[c], (1, MLSTM_DH))

    @pl.when(j == nj - 1)
    def _():
        s_out_ref[...] = s_scr[...]
        m_out_ref[...] = m_scr[...]


def _mlstm(mqt, mk, mvt, g_rows, g_cols, s0, m0, batch, chunk):
    n = mk.shape[0]
    nc = n // batch // chunk
    fwd = lambda b, j: (b * nc + j, 0)
    bwd = lambda b, j: (b * nc + (nc - 1 - j), 0)
    fwd_t = lambda b, j: (0, b * nc + j)
    bwd_t = lambda b, j: (0, b * nc + (nc - 1 - j))
    nst = 2 * MLSTM_HEADS
    tok = lambda fn: pl.BlockSpec((chunk, MLSTM_WIDTH), fn)
    tok_t = lambda fn: pl.BlockSpec((MLSTM_WIDTH, chunk), fn)
    state_s = pl.BlockSpec((None, nst, MLSTM_ST, MLSTM_DH), lambda b, j: (b, 0, 0, 0))
    state_m = pl.BlockSpec((None, nst, MLSTM_DH), lambda b, j: (b, 0, 0))
    return pl.pallas_call(
        _mlstm_kernel,
        grid=(batch, nc),
        in_specs=[tok_t(fwd_t), tok(fwd), tok_t(fwd_t), pl.BlockSpec((chunk, N_GATES), fwd),
                  pl.BlockSpec((N_GATES, chunk), fwd_t),
                  tok_t(bwd_t), tok(bwd), tok_t(bwd_t), pl.BlockSpec((chunk, N_GATES), bwd),
                  pl.BlockSpec((N_GATES, chunk), bwd_t),
                  state_s, state_m],
        out_specs=[tok_t(fwd_t), tok_t(bwd_t), state_s, state_m],
        out_shape=[jax.ShapeDtypeStruct((MLSTM_WIDTH, n), BF16), jax.ShapeDtypeStruct((MLSTM_WIDTH, n), BF16),
                   jax.ShapeDtypeStruct(s0.shape, F32), jax.ShapeDtypeStruct(m0.shape, F32)],
        scratch_shapes=[pltpu.VMEM((nst, MLSTM_ST, MLSTM_DH), F32), pltpu.VMEM((nst, MLSTM_DH), F32)],
        compiler_params=_cparams("parallel", "arbitrary"),
        name="mlstm",
    )(mqt, mk, mvt, g_rows, g_cols, mqt, mk, mvt, g_rows, g_cols, s0, m0)


def _even_out_kernel(x_ref, gate_ref, att_ref, hf_ref, hb_ref, mo_ref, og_ref, w_ref,
                     fmod_ref, fg_ref, fwg_ref, fwu_ref, fwd_ref, o_ref):
    hs = hf_ref[...].astype(F32) + hb_ref[...].astype(F32)
    og = og_ref[...]
    parts = [att_ref[...]]
    for hh in range(MLSTM_HEADS):
        sl = slice(MLSTM_DH * hh, MLSTM_DH * (hh + 1))
        blk = hs[sl, :]
        hn = blk * lax.rsqrt(jnp.mean(blk * blk, axis=0, keepdims=True) + EPS) * og[sl, :]
        parts.append((hn * jax.nn.sigmoid(mo_ref[sl, :].astype(F32))).astype(BF16))
    mixed = jnp.concatenate(parts, axis=0)
    y = lax.dot_general(mixed, w_ref[...], (((0,), (0,)), ((), ())), preferred_element_type=F32)
    o_ref[...] = _ffn_math(x_ref[...] + gate_ref[...] * y, fmod_ref, fg_ref, fwg_ref, fwu_ref, fwd_ref)


def _even_out(x2d, gates, row_fn, att, hf, hb, mo, out_g, w_out, ffn_args, tm):
    n = x2d.shape[0]
    tok = pl.BlockSpec((tm, D_MODEL), lambda i: (i, 0))
    tok_t = pl.BlockSpec((MLSTM_WIDTH, tm), lambda i: (0, i))
    return pl.pallas_call(
        _even_out_kernel,
        grid=(n // tm,),
        in_specs=[tok, pl.BlockSpec((None, 1, D_MODEL), lambda i: (row_fn(i), 0, 0)),
                  tok_t, tok_t, tok_t, tok_t, _const_spec((MLSTM_WIDTH, 1)), _const_spec((D_MODEL, D_MODEL))]
        + _ffn_specs(row_fn, *ffn_args[0]),
        out_specs=tok,
        out_shape=jax.ShapeDtypeStruct((n, D_MODEL), F32),
        compiler_params=_cparams("parallel"),
        name="even_out_ffn",
    )(x2d, gates, att, hf, hb, mo, out_g.reshape(MLSTM_WIDTH, 1), w_out, *ffn_args[1:])


def _gelu_tanh(x):
    return 0.5 * x * (1.0 + jnp.tanh(0.7978845608028654 * (x + 0.044715 * (x * x * x))))


def _odd_in_kernel(x_ref, xp_ref, xn_ref, mod_ref, g_ref, w_ref, cw_ref, cb_ref, gate_ref, xc_ref, ext_scr, *, nt):
    tm = x_ref.shape[0]
    ti = pl.program_id(0) % nt
    g, shift, scale = g_ref[...], mod_ref[0:1, :], mod_ref[1:2, :]
    halo = jnp.concatenate([xp_ref[...], xn_ref[...]], axis=0)
    ph = _dotf(_norm_mod(halo, g, shift, scale).astype(BF16), w_ref[:, RNN_WIDTH:])
    ext_scr[0:8, :] = jnp.where(ti > 0, ph[0:8, :], 0.0)
    ext_scr[8 + tm:16 + tm, :] = jnp.where(ti < nt - 1, ph[8:16, :], 0.0)
    cw = cw_ref[...]
    cb = cb_ref[...]
    rc = tm // ODD_IN_CHUNKS

    def project(c):
        rows = slice(rc * c, rc * (c + 1))
        p = _dotf(_norm_mod(x_ref[rows, :], g, shift, scale).astype(BF16), w_ref[...])
        gate_ref[rows, :] = _gelu_tanh(p[:, :RNN_WIDTH]).astype(BF16)
        ext_scr[8 + rc * c:8 + rc * (c + 1), :] = p[:, RNN_WIDTH:]

    def conv(c):
        win = ext_scr[rc * c:rc * (c + 1) + 16, :]
        xc = cb + cw[2:3, :] * win[8:8 + rc, :]
        for k in (0, 1, 3):
            xc = xc + cw[k:k + 1, :] * pltpu.roll(win, (2 - k) % (rc + 16), axis=0)[8:8 + rc, :]
        xc_ref[rc * c:rc * (c + 1), :] = xc

    project(0)
    for c in range(1, ODD_IN_CHUNKS):
        project(c)
        conv(c - 1)
    conv(ODD_IN_CHUNKS - 1)


def _odd_in(x2d, mods, row_fn, g, w_in, conv_w, conv_b, batch, tm):
    n = x2d.shape[0]
    nt = n // batch // tm
    t8 = tm // 8
    tok = pl.BlockSpec((tm, D_MODEL), lambda i: (i, 0))
    prev = pl.BlockSpec((8, D_MODEL), lambda i: (jnp.maximum(i * t8 - 1, 0), 0))
    nxt = pl.BlockSpec((8, D_MODEL), lambda i: (jnp.minimum((i + 1) * t8, n // 8 - 1), 0))
    return pl.pallas_call(
        functools.partial(_odd_in_kernel, nt=nt),
        grid=(n // tm,),
        in_specs=[tok, prev, nxt, pl.BlockSpec((None, 3, D_MODEL), lambda i: (row_fn(i), 0, 0)),
                  _const_spec((1, D_MODEL)), _const_spec((D_MODEL, 2 * RNN_WIDTH)),
                  _const_spec((CONV_W, RNN_WIDTH)), _const_spec((1, RNN_WIDTH))],
        out_specs=[tok, tok],
        out_shape=[jax.ShapeDtypeStruct((n, RNN_WIDTH), BF16), jax.ShapeDtypeStruct((n, RNN_WIDTH), F32)],
        scratch_shapes=[pltpu.VMEM((tm + 16, RNN_WIDTH), F32)],
        compiler_params=_cparams("parallel"),
        name="odd_in",
    )(x2d, x2d, x2d, mods, g.reshape(1, D_MODEL), w_in, conv_w, conv_b)


def _rglru_kernel(xc_ref, wax_ref, ba_ref, bx_ref, lam_ref, h0_ref, h_ref, ht_ref, a_scr, b_scr, carry_scr, *, nt):
    d = pl.program_id(0)
    i = pl.program_id(2)
    tm = xc_ref.shape[0]
    nb8 = tm // 8

    xc = xc_ref[...]
    neg_lam = -lam_ref[...]
    half_rate = (-0.5 * LRU_C) * (jnp.maximum(neg_lam, 0.0) + jnp.log1p(jnp.exp(-jnp.abs(neg_lam))))
    ba = ba_ref[...]
    bx = bx_ref[...]
    for nb in range(RNN_BLOCKS):
        sl = slice(RNN_BLOCK_DIM * nb, RNN_BLOCK_DIM * (nb + 1))
        u = xc[:, sl]
        rg = _dotf(u.astype(BF16), wax_ref[nb])
        log_a = jnp.tanh(rg[:, :RNN_BLOCK_DIM] + ba[:, sl]) * half_rate[:, sl] + half_rate[:, sl]
        ig = 0.5 * jnp.tanh(rg[:, RNN_BLOCK_DIM:] + bx[:, sl]) + 0.5
        a = jnp.exp(log_a)
        a_scr[:, sl] = a
        b_scr[:, sl] = jnp.sqrt(-jnp.tanh(log_a) * (a * a + 1.0)) * (ig * u)

    @pl.when(i == 0)
    def _():
        carry_scr[...] = h0_ref[...]

    def scan_tile(reverse):
        a = a_scr[...].reshape(nb8, 8, RNN_WIDTH)
        b = b_scr[...].reshape(nb8, 8, RNN_WIDTH)
        r8 = lax.broadcasted_iota(jnp.int32, (nb8, 8, RNN_WIDTH), 1)
        for s in (1, 2, 4):
            keep = (r8 < 8 - s) if reverse else (r8 >= s)
            shift = 8 - s if reverse else s
            a_in = jnp.where(keep, pltpu.roll(a, shift, axis=1), 1.0)
            b_in = jnp.where(keep, pltpu.roll(b, shift, axis=1), 0.0)
            b = a * b_in + b
            a = a * a_in
        a_scr[...] = a.reshape(tm, RNN_WIDTH)
        b_scr[...] = b.reshape(tm, RNN_WIDTH)

        def block_pair(jp, h):
            base = pl.multiple_of((nb8 // 2 - 1 - jp if reverse else jp) * 16, 16)
            halves = [None, None]
            for half in ((1, 0) if reverse else (0, 1)):
                rows = pl.ds(base + 8 * half, 8)
                hb = a_scr[rows, :] * h + b_scr[rows, :]
                halves[half] = hb
                h = hb[0:1, :] if reverse else hb[7:8, :]
            h_ref[pl.ds(base, 16), :] = jnp.concatenate(halves, axis=0).astype(h_ref.dtype)
            return h

        carry_scr[...] = lax.fori_loop(0, nb8 // 2, block_pair, carry_scr[...], unroll=2)

    @pl.when(d == 0)
    def _():
        scan_tile(False)

    @pl.when(d == 1)
    def _():
        scan_tile(True)

    @pl.when(i == nt - 1)
    def _():
        ht_ref[...] = carry_scr[...]


def _rglru(xc2d, h0, w, batch, tm):
    n = xc2d.shape[0]
    nt = n // batch // tm
    tile = lambda d, i: jnp.where(d == 0, i, nt - 1 - i)
    per_dir = lambda shape: pl.BlockSpec((None,) + shape, lambda d, b, i: (d,) + (0,) * len(shape))
    state = pl.BlockSpec((None, None, 1, RNN_WIDTH), lambda d, b, i: (d, b, 0, 0))
    return pl.pallas_call(
        functools.partial(_rglru_kernel, nt=nt),
        grid=(2, batch, nt),
        in_specs=[pl.BlockSpec((tm, RNN_WIDTH), lambda d, b, i: (b * nt + tile(d, i), 0)),
                  per_dir((RNN_BLOCKS, RNN_BLOCK_DIM, 2 * RNN_BLOCK_DIM)), per_dir((1, RNN_WIDTH)), per_dir((1, RNN_WIDTH)),
                  per_dir((1, RNN_WIDTH)), state],
        out_specs=[pl.BlockSpec((None, tm, RNN_WIDTH), lambda d, b, i: (d, b * nt + tile(d, i), 0)), state],
        out_shape=[jax.ShapeDtypeStruct((2, n, RNN_WIDTH), BF16), jax.ShapeDtypeStruct((2, batch, 1, RNN_WIDTH), F32)],
        scratch_shapes=[pltpu.VMEM((tm, RNN_WIDTH), F32), pltpu.VMEM((tm, RNN_WIDTH), F32),
                        pltpu.VMEM((1, RNN_WIDTH), F32)],
        compiler_params=_cparams("parallel", "parallel", "arbitrary"),
        name="rglru",
    )(xc2d, w["w_ax"], w["b_a"], w["b_x"], w["lam"], h0)


def _odd_out_kernel(x_ref, gate_ref, h_ref, gg_ref, w_ref, fmod_ref, fg_ref, fwg_ref, fwu_ref, fwd_ref, o_ref):
    y = ((h_ref[0].astype(F32) + h_ref[1].astype(F32)) * gg_ref[...].astype(F32)).astype(BF16)
    x_mid = x_ref[...] + gate_ref[...] * _dotf(y, w_ref[...])
    o_ref[...] = _ffn_math(x_mid, fmod_ref, fg_ref, fwg_ref, fwu_ref, fwd_ref)


def _odd_out(x2d, gates, row_fn, h2, gate_branch, w_out, ffn_args, tm):
    n = x2d.shape[0]
    tok = pl.BlockSpec((tm, D_MODEL), lambda i: (i, 0))
    return pl.pallas_call(
        _odd_out_kernel,
        grid=(n // tm,),
        in_specs=[tok, pl.BlockSpec((None, 1, D_MODEL), lambda i: (row_fn(i), 0, 0)),
                  pl.BlockSpec((2, tm, RNN_WIDTH), lambda i: (0, i, 0)), tok, _const_spec((RNN_WIDTH, D_MODEL))]
        + _ffn_specs(row_fn, *ffn_args[0]),
        out_specs=tok,
        out_shape=jax.ShapeDtypeStruct((n, D_MODEL), F32),
        compiler_params=_cparams("parallel"),
        name="odd_out_ffn",
    )(x2d, gates, h2, gate_branch, w_out, *ffn_args[1:])


def _rope_tables(seq):
    t = np.arange(seq)
    inv = np.float32(ROPE_BASE) ** (-np.arange(0, ROPE_AXIS_DIM, 2, dtype=np.float32) / np.float32(ROPE_AXIS_DIM))
    ang_r = (t // GRID_W).astype(np.float32)[:, None] * inv
    ang_c = (t % GRID_W).astype(np.float32)[:, None] * inv
    cos32 = np.concatenate([np.cos(ang_r), np.cos(ang_r), np.cos(ang_c), np.cos(ang_c)], axis=1)
    sin32 = np.concatenate([-np.sin(ang_r), np.sin(ang_r), -np.sin(ang_c), np.sin(ang_c)], axis=1)
    pad = HEAD_BLOCK - MLA_QK
    cos = np.concatenate([np.ones((seq, MLA_NOPE), np.float32), cos32, np.ones((seq, pad), np.float32)], axis=1)
    sin = np.concatenate([np.zeros((seq, MLA_NOPE), np.float32), sin32, np.zeros((seq, pad), np.float32)], axis=1)
    cost = np.concatenate([np.cos(ang_r), np.cos(ang_c)], axis=1).T
    sint = np.concatenate([np.sin(ang_r), np.sin(ang_c)], axis=1).T
    return tuple(jnp.asarray(a, F32) for a in (cos, sin, cost, sint))


def _even_weights(w_in, cq_g, w_uq, ckv_g, w_ukv, q_g, k_g, gate_b):
    pad = HEAD_BLOCK - MLA_QK
    w_in, w_uq, w_ukv = w_in.astype(BF16), w_uq.astype(BF16), w_ukv.astype(BF16)
    kr_cols = w_in[:, OFF_KR:OFF_KR + MLA_ROPE]
    w_kr = jnp.pad(kr_cols, ((0, 0), (MLA_NOPE, pad)))
    m_cols = w_in[:, OFF_KR + MLA_ROPE:OFF_KR + MLA_ROPE + 4 * MLSTM_WIDTH]
    g_cols = w_in[:, OFF_KR + MLA_ROPE + 4 * MLSTM_WIDTH:]
    mq_cols, mk_cols, mv_cols, mo_cols = jnp.split(m_cols, 4, axis=1)
    w_all = jnp.concatenate([w_in[:, :OFF_KR], w_kr, mk_cols, jnp.pad(g_cols, ((0, 0), (0, HEAD_BLOCK - N_GATES)))], axis=1)
    uq = jnp.pad(w_uq.reshape(MLA_Q_RANK, MLA_HEADS, MLA_QK), ((0, 0), (0, 0), (0, pad)))
    ukv = w_ukv.reshape(MLA_KV_RANK, MLA_HEADS, MLA_NOPE + MLA_V)
    uk = jnp.pad(ukv[:, :, :MLA_NOPE], ((0, 0), (0, 0), (0, HEAD_BLOCK - MLA_NOPE)))
    uv = ukv[:, :, MLA_NOPE:]
    kg_blk = jnp.pad(k_g[:MLA_NOPE], (0, HEAD_BLOCK - MLA_NOPE))
    krg_blk = jnp.pad(k_g[MLA_NOPE:], (MLA_NOPE, pad))
    return {
        "w_all": w_all,
        "w_gt": jnp.concatenate([g_cols, mq_cols, mv_cols, mo_cols], axis=1).T,
        "gb_row": gate_b.reshape(1, N_GATES),
        "gb_col": gate_b.reshape(N_GATES, 1),
        "cq_g": cq_g.reshape(1, MLA_Q_RANK),
        "w_uqt": uq.reshape(MLA_Q_RANK, MLA_HEADS * HEAD_BLOCK).T,
        "ckv_g": ckv_g.reshape(1, MLA_KV_RANK),
        "w_uk": uk.reshape(MLA_KV_RANK, MLA_HEADS * HEAD_BLOCK),
        "w_uvt": uv.reshape(MLA_KV_RANK, MLA_HEADS * MLA_V).T,
        "qg_nope": q_g[:MLA_NOPE].reshape(MLA_NOPE, 1),
        "qg_rope": q_g[MLA_NOPE:].reshape(MLA_ROPE, 1),
        "k_g": jnp.tile(kg_blk, MLA_HEADS).reshape(1, -1),
        "kr_g": krg_blk.reshape(1, HEAD_BLOCK),
    }


def kernel(x, c, ctx, c_ctx, mod_w, mod_b, norm_g, ffn_w_gate, ffn_w_up, ffn_w_down, even_w_in, even_w_out, mla_cq_g, mla_w_uq, mla_ckv_g, mla_w_ukv, mla_q_g, mla_k_g, mlstm_gate_b, mlstm_out_g, odd_w_in, odd_conv_w, odd_conv_b, lru_w_a, lru_b_a, lru_w_x, lru_b_x, lru_lam, odd_w_out):
    batch, seq, _ = x.shape
    ctx_len = ctx.shape[1]
    depth = mod_w.shape[0]
    assert depth == 2 and batch <= 7 and ctx_len == ROWS_C

    cond8 = jnp.zeros((8, D_MODEL), F32).at[:batch].set(c).at[batch].set(c_ctx)
    mods = _ada_params(cond8, mod_w, mod_b)
    mods = mods.reshape(depth, 8, 3, 3, D_MODEL).transpose(0, 2, 1, 3, 4)

    x_row = lambda tm: (lambda i: i // (seq // tm))
    c_row = lambda i: batch
    x2 = x.reshape(batch * seq, D_MODEL)
    c2 = ctx.reshape(batch * ctx_len, D_MODEL)

    wg = ffn_w_gate.astype(BF16)
    wu = ffn_w_up.astype(BF16)
    wd = ffn_w_down.astype(BF16)

    def ffn_args(layer, which):
        return ((layer, which), mods[layer, 2 * which], norm_g[layer, 2 * which].reshape(1, D_MODEL), wg, wu, wd)

    x2 = _ffn(x2, ffn_args(0, 0), x_row(ROWS_X_FFN), ROWS_X_FFN)
    c2 = _ffn(c2, ffn_args(0, 0), c_row, ROWS_C)

    ew = _even_weights(even_w_in[0], mla_cq_g[0], mla_w_uq[0], mla_ckv_g[0], mla_w_ukv[0], mla_q_g[0], mla_k_g[0],
                       mlstm_gate_b[0])
    tabs_x = _rope_tables(seq)
    tabs_c = (jnp.ones((ROWS_C, HEAD_BLOCK), F32), jnp.zeros((ROWS_C, HEAD_BLOCK), F32),
              jnp.ones((ROPE_AXIS_DIM, ROWS_C), F32), jnp.zeros((ROPE_AXIS_DIM, ROWS_C), F32))
    key_len = seq + ctx_len
    px = _even_prep(x2, mods[0, 1], x_row(ROWS_X), norm_g[0, 1], tabs_x, lambda i: i % (seq // ROWS_X), ew, ROWS_X,
                    batch, key_len, 0)
    qt_x, k_all, vt_all, mqt_x, mk_x, mvt_x, mot_x, gr_x, gc_x = px
    pc = _even_prep(c2, mods[0, 1], c_row, norm_g[0, 1], tabs_c, lambda i: 0, ew, ROWS_C,
                    batch, key_len, seq, kv_bufs=(k_all, vt_all))
    qt_c, k_all, vt_all, mqt_c, mk_c, mvt_c, mot_c, gr_c, gc_c = pc

    att_x = _attention(qt_x, k_all, vt_all, batch, ATTN_QUERIES_X, key_len, 0)
    att_c = _attention(qt_c, k_all, vt_all, batch, ROWS_C, ctx_len, seq)

    nst = 2 * MLSTM_HEADS
    s0 = jnp.zeros((batch, nst, MLSTM_ST, MLSTM_DH), F32)
    m0 = jnp.zeros((batch, nst, MLSTM_DH), F32)
    hcf, hcb, s_c, m_c = _mlstm(mqt_c, mk_c, mvt_c, gr_c, gc_c, s0, m0, batch, MLSTM_CHUNK)
    hxf, hxb, _, _ = _mlstm(mqt_x, mk_x, mvt_x, gr_x, gc_x, s_c, m_c, batch, MLSTM_CHUNK)

    w_out0 = even_w_out[0].astype(BF16)
    gate0 = mods[0, 1][:, 2:3, :]
    x2 = _even_out(x2, gate0, x_row(ROWS_X), att_x, hxf, hxb, mot_x, mlstm_out_g[0], w_out0, ffn_args(0, 1), ROWS_X)
    c2 = _even_out(c2, gate0, c_row, att_c, hcf, hcb, mot_c, mlstm_out_g[0], w_out0, ffn_args(0, 1), ROWS_C)

    x2 = _ffn(x2, ffn_args(1, 0), x_row(ROWS_X_FFN), ROWS_X_FFN)
    c2 = _ffn(c2, ffn_args(1, 0), c_row, ROWS_C)
    w_in1 = odd_w_in[0].astype(BF16)
    conv_b = odd_conv_b[0].reshape(1, RNN_WIDTH)
    gg_x, xc_x = _odd_in(x2, mods[1, 1], x_row(ROWS_X), norm_g[1, 1], w_in1, odd_conv_w[0], conv_b, batch, ROWS_X)
    _, xc_c = _odd_in(c2, mods[1, 1], c_row, norm_g[1, 1], w_in1, odd_conv_w[0], conv_b, batch, ROWS_C)
    rw = {
        "w_ax": (0.5 * jnp.concatenate([lru_w_a[0], lru_w_x[0]], axis=-1)).astype(BF16),
        "b_a": 0.5 * lru_b_a[0].reshape(2, 1, RNN_WIDTH),
        "b_x": 0.5 * lru_b_x[0].reshape(2, 1, RNN_WIDTH),
        "lam": lru_lam[0].reshape(2, 1, RNN_WIDTH),
    }
    h0 = jnp.zeros((2, batch, 1, RNN_WIDTH), F32)
    _, st_c = _rglru(xc_c, h0, rw, batch, RGLRU_ROWS)
    h_x, _ = _rglru(xc_x, st_c, rw, batch, RGLRU_ROWS)
    x2 = _odd_out(x2, mods[1, 1][:, 2:3, :], x_row(ROWS_X), h_x, gg_x, odd_w_out[0].astype(BF16), ffn_args(1, 1),
                  ROWS_X)
    return x2.reshape(batch, seq, D_MODEL)
```

```python
import functools

import jax
import jax.numpy as jnp
import numpy as np
from jax import lax
from jax.experimental import pallas as pl
from jax.experimental.pallas import tpu as pltpu

F32 = jnp.float32
BF16 = jnp.bfloat16

D_MODEL = 1024
GRID_W = 64
EPS = 1e-6
D_FF = 2816
FFN_RESIDUAL = 0.5

MLA_HEADS = 8
MLA_Q_RANK = 384
MLA_KV_RANK = 256
MLA_NOPE = 64
MLA_ROPE = 32
MLA_V = 64
MLA_QK = MLA_NOPE + MLA_ROPE
ROPE_AXIS_DIM = MLA_ROPE // 2
ROPE_BASE = 10000.0
HEAD_BLOCK = 128
V_ROWS = MLA_V + 16

MLSTM_HEADS = 4
MLSTM_DH = 128
MLSTM_WIDTH = MLSTM_HEADS * MLSTM_DH
N_GATES = 4 * MLSTM_HEADS
MLSTM_ST = MLSTM_DH + 16

RNN_WIDTH = 1024
RNN_BLOCKS = 8
RNN_BLOCK_DIM = RNN_WIDTH // RNN_BLOCKS
CONV_W = 4
LRU_C = 8.0

NEG_BIG = -1e30
LOG2_E = 1.4426950408889634
VMEM_LIMIT = 56 * 1024 * 1024

ROWS_X = 512
ROWS_X_FFN = 1024
ROWS_C = 256
FFN_CHUNK_ROWS = 128
ODD_IN_CHUNKS = 4
KEY_CHUNK = 256
ATTN_HEADS_PER_STEP = 2
ATTN_QUERIES_X = 512
ATTN_SCORE_PAD = 128
MLSTM_CHUNK = 256
RGLRU_ROWS = 256


def _cparams(*sem):
    return pltpu.CompilerParams(dimension_semantics=sem, vmem_limit_bytes=VMEM_LIMIT)


def _const_spec(shape):
    zeros = (0,) * len(shape)
    return pl.BlockSpec(shape, lambda *_: zeros, pipeline_mode=pl.Buffered(1))


def _dotf(a, b):
    return jnp.dot(a, b, preferred_element_type=F32)


def _split3(a):
    a1 = a.astype(BF16)
    r1 = a - a1.astype(F32)
    a2 = r1.astype(BF16)
    a3 = (r1 - a2.astype(F32)).astype(BF16)
    return a1, a2, a3


def _norm_mod(x, g, shift, scale):
    ms = jnp.mean(x * x, axis=-1, keepdims=True)
    return (x * lax.rsqrt(ms + EPS)) * (g * (1.0 + scale)) + shift


def _rms(x, g):
    ms = jnp.mean(x * x, axis=-1, keepdims=True)
    return x * lax.rsqrt(ms + EPS) * g


def _log_sigmoid(x):
    return jnp.minimum(x, 0.0) - jnp.log1p(jnp.exp(-jnp.abs(x)))


def _ada_kernel(cond_ref, w_ref, b_ref, o_ref):
    c = cond_ref[...]
    s = c * jax.nn.sigmoid(c)
    s1, s2, _ = _split3(s)
    w = w_ref[...]
    w1 = w.astype(BF16)
    w2 = (w - w1.astype(F32)).astype(BF16)
    o_ref[...] = _dotf(s1, w1) + _dotf(s1, w2) + _dotf(s2, w1) + b_ref[...]


def _ada_params(cond8, mod_w, mod_b):
    depth, _, n = mod_w.shape
    tn = 1152
    return pl.pallas_call(
        _ada_kernel,
        grid=(depth, n // tn),
        in_specs=[
            pl.BlockSpec((8, D_MODEL), lambda l, j: (0, 0)),
            pl.BlockSpec((None, D_MODEL, tn), lambda l, j: (l, 0, j)),
            pl.BlockSpec((None, 1, tn), lambda l, j: (l, 0, j)),
        ],
        out_specs=pl.BlockSpec((None, 8, tn), lambda l, j: (l, 0, j)),
        out_shape=jax.ShapeDtypeStruct((depth, 8, n), F32),
        compiler_params=_cparams("parallel", "parallel"),
        name="ada_params",
    )(cond8, mod_w, mod_b.reshape(depth, 1, n))


def _ffn_math(x, mod_ref, g_ref, wg_ref, wu_ref, wd_ref):
    outs = []
    rc = min(FFN_CHUNK_ROWS, x.shape[0])
    for c in range(x.shape[0] // rc):
        xc = x[rc * c:rc * (c + 1), :]
        h = _norm_mod(xc, g_ref[...], mod_ref[0:1, :], mod_ref[1:2, :]).astype(BF16)
        g = _dotf(h, wg_ref[...])
        u = _dotf(h, wu_ref[...])
        a = (g * jax.nn.sigmoid(g) * u).astype(BF16)
        outs.append(xc + (FFN_RESIDUAL * mod_ref[2:3, :]) * _dotf(a, wd_ref[...]))
    return jnp.concatenate(outs, axis=0)


def _ffn_specs(row_fn, layer, which):
    weight = lambda r, c: pl.BlockSpec((None, None, r, c), lambda i: (layer, which, 0, 0), pipeline_mode=pl.Buffered(1))
    return [pl.BlockSpec((None, 3, D_MODEL), lambda i: (row_fn(i), 0, 0)), _const_spec((1, D_MODEL)),
            weight(D_MODEL, D_FF), weight(D_MODEL, D_FF), weight(D_FF, D_MODEL)]


def _ffn_kernel(x_ref, mod_ref, g_ref, wg_ref, wu_ref, wd_ref, o_ref):
    o_ref[...] = _ffn_math(x_ref[...], mod_ref, g_ref, wg_ref, wu_ref, wd_ref)


def _ffn(x2d, ffn_args, row_fn, tm):
    n = x2d.shape[0]
    tok = pl.BlockSpec((tm, D_MODEL), lambda i: (i, 0))
    return pl.pallas_call(
        _ffn_kernel,
        grid=(n // tm,),
        in_specs=[tok] + _ffn_specs(row_fn, *ffn_args[0]),
        out_specs=tok,
        out_shape=jax.ShapeDtypeStruct((n, D_MODEL), F32),
        compiler_params=_cparams("parallel"),
        name="ffn",
    )(x2d, *ffn_args[1:])


W_ALL = MLA_Q_RANK + MLA_KV_RANK + HEAD_BLOCK + MLSTM_WIDTH + HEAD_BLOCK
OFF_CKV = MLA_Q_RANK
OFF_KR = OFF_CKV + MLA_KV_RANK
OFF_MK = OFF_KR + HEAD_BLOCK
OFF_GATES = OFF_MK + MLSTM_WIDTH
W_T_ROWS = N_GATES + 3 * MLSTM_WIDTH


def _rope_block(xb, cos, sin, first_half):
    partner = jnp.where(first_half, pltpu.roll(xb, HEAD_BLOCK - 8, axis=1), pltpu.roll(xb, 8, axis=1))
    return xb * cos + partner * sin


def _even_prep_kernel(x_ref, mod_ref, g_ref, cos_ref, sin_ref, cost_ref, sint_ref, wall_ref, wgt_ref, gbr_ref, gbc_ref,
                      cqg_ref, wuqt_ref, ckvg_ref, wuk_ref, wuvt_ref, qgn_ref, qgr_ref, kg_ref, krg_ref,
                      qt_ref, k_ref, vt_ref, mqt_ref, mk_ref, mvt_ref, mot_ref, gr_ref, gc_ref):
    tm = x_ref.shape[0]
    x = x_ref[...]
    h = _norm_mod(x, g_ref[...], mod_ref[0:1, :], mod_ref[1:2, :]).astype(BF16)
    p = _dotf(h, wall_ref[...])
    pt = lax.dot_general(wgt_ref[...], h, (((1,), (1,)), ((), ())), preferred_element_type=F32)

    mk_ref[...] = (p[:, OFF_MK:OFF_MK + MLSTM_WIDTH] * (MLSTM_DH ** -0.5)).astype(BF16)
    mqt_ref[...] = pt[N_GATES:N_GATES + MLSTM_WIDTH, :].astype(BF16)
    mvt_ref[...] = pt[N_GATES + MLSTM_WIDTH:N_GATES + 2 * MLSTM_WIDTH, :].astype(BF16)
    mot_ref[...] = pt[N_GATES + 2 * MLSTM_WIDTH:, :].astype(BF16)
    graw = p[:, OFF_GATES:OFF_GATES + HEAD_BLOCK][:, :N_GATES] + gbr_ref[...]
    lane = lax.broadcasted_iota(jnp.int32, (tm, N_GATES), 1)
    gr_ref[...] = jnp.where((lane // MLSTM_HEADS) % 2 == 1, _log_sigmoid(graw), graw)
    gt = pt[:N_GATES, :] + gbc_ref[...]
    sub = lax.broadcasted_iota(jnp.int32, (N_GATES, tm), 0)
    gc_ref[...] = jnp.where((sub // MLSTM_HEADS) % 2 == 1, _log_sigmoid(gt), gt)

    cos = cos_ref[...]
    sin = sin_ref[...]
    lane_b = lax.broadcasted_iota(jnp.int32, (tm, HEAD_BLOCK), 1)
    first_half = (lane_b % ROPE_AXIS_DIM) < (ROPE_AXIS_DIM // 2)

    cqn = _rms(p[:, 0:MLA_Q_RANK], cqg_ref[...]).astype(BF16)
    ckvn = _rms(p[:, OFF_CKV:OFF_CKV + MLA_KV_RANK], ckvg_ref[...]).astype(BF16)
    nt_dims = (((1,), (1,)), ((), ()))
    k_raw = _dotf(ckvn, wuk_ref[...])
    vt = lax.dot_general(wuvt_ref[...], ckvn, nt_dims, preferred_element_type=F32).astype(BF16)
    ones_rows = jnp.where(lax.broadcasted_iota(jnp.int32, (V_ROWS - MLA_V, KEY_CHUNK), 0) == 0, 1.0, 0.0).astype(BF16)
    for cc in range(tm // KEY_CHUNK):
        for hh in range(MLA_HEADS):
            vt_ref[cc, V_ROWS * hh:V_ROWS * hh + MLA_V, :] = vt[MLA_V * hh:MLA_V * (hh + 1),
                                                                KEY_CHUNK * cc:KEY_CHUNK * (cc + 1)]
            vt_ref[cc, V_ROWS * hh + MLA_V:V_ROWS * (hh + 1), :] = ones_rows

    kr = p[:, OFF_KR:OFF_KR + HEAD_BLOCK]
    kr_ms = jnp.sum(kr * kr, axis=-1, keepdims=True) * (1.0 / MLA_ROPE)
    kr_rot = _rope_block(kr * lax.rsqrt(kr_ms + EPS) * krg_ref[...], cos, sin, first_half)
    kg = kg_ref[...]
    for hh in range(MLA_HEADS):
        sl = slice(HEAD_BLOCK * hh, HEAD_BLOCK * (hh + 1))
        kb = k_raw[:, sl]
        kb_ms = jnp.sum(kb * kb, axis=-1, keepdims=True) * (1.0 / MLA_NOPE)
        k_ref[:, sl] = (kb * lax.rsqrt(kb_ms + EPS) * kg[:, sl] + kr_rot).astype(BF16)

    qt_raw = lax.dot_general(wuqt_ref[...], cqn, nt_dims, preferred_element_type=F32)
    q_scale = (MLA_QK ** -0.5) * LOG2_E
    half = ROPE_AXIS_DIM // 2
    cost = cost_ref[...]
    sint = sint_ref[...]
    for hh in range(MLA_HEADS):
        r0 = HEAD_BLOCK * hh
        nope = qt_raw[r0:r0 + MLA_NOPE, :]
        rope = qt_raw[r0 + MLA_NOPE:r0 + MLA_QK, :]
        nope = nope * lax.rsqrt(jnp.mean(nope * nope, axis=0, keepdims=True) + EPS) * qgn_ref[...]
        rope = rope * lax.rsqrt(jnp.mean(rope * rope, axis=0, keepdims=True) + EPS) * qgr_ref[...]
        rot = []
        for ax in range(2):
            x1 = rope[ROPE_AXIS_DIM * ax:ROPE_AXIS_DIM * ax + half, :]
            x2 = rope[ROPE_AXIS_DIM * ax + half:ROPE_AXIS_DIM * (ax + 1), :]
            cs = cost[half * ax:half * (ax + 1), :]
            sn = sint[half * ax:half * (ax + 1), :]
            rot += [x1 * cs - x2 * sn, x2 * cs + x1 * sn]
        qt_ref[r0:r0 + MLA_NOPE, :] = (nope * q_scale).astype(BF16)
        qt_ref[r0 + MLA_NOPE:r0 + MLA_QK, :] = (jnp.concatenate(rot, axis=0) * q_scale).astype(BF16)
        qt_ref[r0 + MLA_QK:r0 + HEAD_BLOCK, :] = jnp.zeros((HEAD_BLOCK - MLA_QK, tm), BF16)


def _even_prep_into_kernel(*refs):
    n_in = 20
    _even_prep_kernel(*refs[:n_in], *refs[n_in + 2:])


def _even_prep(x2d, mods, row_fn, g, tabs, tab_fn, w, tm, batch, key_len, key_off, kv_bufs=None):
    n = x2d.shape[0]
    nt = n // batch // tm
    assert key_off % tm == 0 and tm % KEY_CHUNK == 0
    cos, sin, cost, sint = tabs
    tok = lambda width: pl.BlockSpec((tm, width), lambda i: (i, 0))
    tok_t = lambda rows: pl.BlockSpec((rows, tm), lambda i: (0, i))
    k_spec = pl.BlockSpec((None, tm, MLA_HEADS * HEAD_BLOCK), lambda i: (i // nt, key_off // tm + i % nt, 0))
    vt_spec = pl.BlockSpec((None, tm // KEY_CHUNK, MLA_HEADS * V_ROWS, KEY_CHUNK),
                           lambda i: (i // nt, key_off // tm + i % nt, 0, 0))
    out_shape = [
        jax.ShapeDtypeStruct((MLA_HEADS * HEAD_BLOCK, n), BF16),
        jax.ShapeDtypeStruct((batch, key_len, MLA_HEADS * HEAD_BLOCK), BF16),
        jax.ShapeDtypeStruct((batch, key_len // KEY_CHUNK, MLA_HEADS * V_ROWS, KEY_CHUNK), BF16),
        jax.ShapeDtypeStruct((MLSTM_WIDTH, n), BF16),
        jax.ShapeDtypeStruct((n, MLSTM_WIDTH), BF16),
        jax.ShapeDtypeStruct((MLSTM_WIDTH, n), BF16),
        jax.ShapeDtypeStruct((MLSTM_WIDTH, n), BF16),
        jax.ShapeDtypeStruct((n, N_GATES), F32),
        jax.ShapeDtypeStruct((N_GATES, n), F32),
    ]
    out_specs = [tok_t(1024), k_spec, vt_spec, tok_t(512), tok(512), tok_t(512), tok_t(512),
                 tok(N_GATES), tok_t(N_GATES)]
    in_specs = [
        tok(D_MODEL),
        pl.BlockSpec((None, 3, D_MODEL), lambda i: (row_fn(i), 0, 0)),
        _const_spec((1, D_MODEL)),
        pl.BlockSpec((tm, HEAD_BLOCK), lambda i: (tab_fn(i), 0)),
        pl.BlockSpec((tm, HEAD_BLOCK), lambda i: (tab_fn(i), 0)),
        pl.BlockSpec((ROPE_AXIS_DIM, tm), lambda i: (0, tab_fn(i))),
        pl.BlockSpec((ROPE_AXIS_DIM, tm), lambda i: (0, tab_fn(i))),
        _const_spec((D_MODEL, W_ALL)),
        _const_spec((W_T_ROWS, D_MODEL)),
        _const_spec((1, N_GATES)),
        _const_spec((N_GATES, 1)),
        _const_spec((1, MLA_Q_RANK)),
        _const_spec((1024, MLA_Q_RANK)),
        _const_spec((1, MLA_KV_RANK)),
        _const_spec((MLA_KV_RANK, 1024)),
        _const_spec((512, MLA_KV_RANK)),
        _const_spec((MLA_NOPE, 1)),
        _const_spec((MLA_ROPE, 1)),
        _const_spec((1, 1024)),
        _const_spec((1, HEAD_BLOCK)),
    ]
    args = [x2d, mods, g.reshape(1, D_MODEL), cos, sin, cost, sint, w["w_all"], w["w_gt"], w["gb_row"], w["gb_col"],
            w["cq_g"], w["w_uqt"], w["ckv_g"], w["w_uk"], w["w_uvt"], w["qg_nope"], w["qg_rope"], w["k_g"], w["kr_g"]]
    aliases = {}
    if kv_bufs is not None:
        aliases = {len(args): 1, len(args) + 1: 2}
        in_specs = in_specs + [pl.BlockSpec(memory_space=pl.ANY)] * 2
        args = args + list(kv_bufs)
    return pl.pallas_call(
        _even_prep_kernel if kv_bufs is None else _even_prep_into_kernel,
        grid=(n // tm,),
        in_specs=in_specs,
        out_specs=out_specs,
        out_shape=out_shape,
        input_output_aliases=aliases,
        compiler_params=_cparams("parallel"),
        name="even_prep",
    )(*args)


def _attn_kernel(qt_ref, k_ref, vt_ref, o_ref, st_scr, acc_scr):
    tq = qt_ref.shape[1]
    nk, _, tk = vt_ref.shape
    unroll = max(u for u in (16, 8, 4, 2) if (nk - 1) % u == 0)
    heads = range(acc_scr.shape[0])

    def scores(j, slot):
        off = pl.multiple_of(j * tk, tk)
        cms = []
        for hh in heads:
            sl = slice(HEAD_BLOCK * hh, HEAD_BLOCK * (hh + 1))
            st = _dotf(k_ref[pl.ds(off, tk), sl], qt_ref[sl, :])
            st_scr[slot, hh, :, :tq] = st
            cms.append(jnp.max(st, axis=0, keepdims=True))
        return tuple(cms)

    def absorb(j, slot, cms, ms):
        out = []
        for hh in heads:
            m_new = jnp.maximum(ms[hh], cms[hh])
            alpha = jnp.exp2(ms[hh] - m_new)
            pt = jnp.exp2(st_scr[slot, hh, :, :tq] - m_new)
            pv = _dotf(vt_ref[j, V_ROWS * hh:V_ROWS * (hh + 1), :], pt.astype(BF16))
            acc_scr[hh] = alpha * acc_scr[hh] + pv
            out.append(m_new)
        return tuple(out)

    def body(i, carry):
        cms, ms = carry
        for r in range(unroll):
            j = unroll * i + r
            cms_next = scores(j + 1, (r + 1) % 2)
            ms = absorb(j, r % 2, cms, ms)
            cms = cms_next
        return cms, ms

    acc_scr[...] = jnp.zeros(acc_scr.shape, F32)
    m0 = jnp.full((1, tq), NEG_BIG, F32)
    cms, ms = lax.fori_loop(0, (nk - 1) // unroll, body, (scores(0, 0), (m0,) * len(heads)))
    absorb(nk - 1, 0, cms, ms)
    for hh in heads:
        acc = acc_scr[hh]
        o_ref[MLA_V * hh:MLA_V * (hh + 1), :] = (acc[:MLA_V, :] / acc[MLA_V:MLA_V + 1, :]).astype(o_ref.dtype)


def _attention(qt, k3, vt4, batch, tq, key_len, key_off):
    nq = qt.shape[1] // batch // tq
    assert key_off % key_len == 0 and key_len % KEY_CHUNK == 0
    kb = key_off // key_len
    nk = key_len // KEY_CHUNK
    tk = KEY_CHUNK
    g = ATTN_HEADS_PER_STEP
    return pl.pallas_call(
        _attn_kernel,
        grid=(batch, MLA_HEADS // g, nq),
        in_specs=[
            pl.BlockSpec((g * HEAD_BLOCK, tq), lambda b, hp, i: (hp, b * nq + i)),
            pl.BlockSpec((None, key_len, g * HEAD_BLOCK), lambda b, hp, i: (b, kb, hp)),
            pl.BlockSpec((None, nk, g * V_ROWS, tk), lambda b, hp, i: (b, kb, hp, 0)),
        ],
        out_specs=pl.BlockSpec((g * MLA_V, tq), lambda b, hp, i: (hp, b * nq + i)),
        out_shape=jax.ShapeDtypeStruct((MLA_HEADS * MLA_V, qt.shape[1]), BF16),
        scratch_shapes=[pltpu.VMEM((2, g, tk, tq + ATTN_SCORE_PAD), F32), pltpu.VMEM((g, V_ROWS, tq), F32)],
        compiler_params=_cparams("parallel", "parallel", "arbitrary"),
        name="mla_attention",
    )(qt, k3, vt4)


def _mlstm_kernel(qtf_ref, kf_ref, vtf_ref, grf_ref, gcf_ref, qtb_ref, kb_ref, vtb_ref, grb_ref, gcb_ref,
                  s0_ref, m0_ref, hf_ref, hb_ref, s_out_ref, m_out_ref, s_scr, m_scr):
    j = pl.program_id(1)
    nj = pl.num_programs(1)
    L = kf_ref.shape[0]

    @pl.when(j == 0)
    def _():
        s_scr[...] = s0_ref[...]
        m_scr[...] = m0_ref[...]

    row = lax.broadcasted_iota(jnp.int32, (L, L), 0)
    col = lax.broadcasted_iota(jnp.int32, (L, L), 1)
    sub = lax.broadcasted_iota(jnp.int32, (MLSTM_ST - MLSTM_DH, L), 0)

    dirs = ((qtf_ref, kf_ref, vtf_ref, grf_ref, gcf_ref, hf_ref), (qtb_ref, kb_ref, vtb_ref, grb_ref, gcb_ref, hb_ref))
    combos = [(d, hh) for d in range(2) for hh in range(MLSTM_HEADS)]


    masks, cum_rows, cum_cols, g_rows, g_cols = [], [], [], [], []
    for d in range(2):
        s_le_t = (row <= col) if d == 0 else (row >= col)
        tri = jnp.where((col <= row) if d == 0 else (col >= row), 1.0, 0.0).astype(BF16)
        tri_t = jnp.where(s_le_t, 1.0, 0.0).astype(BF16)
        gr = dirs[d][3][...]
        gc = dirs[d][4][...]
        r1, r2, r3 = _split3(gr)
        c1, c2, c3 = _split3(gc)
        masks.append(s_le_t)
        g_rows.append(gr)
        g_cols.append(gc)
        cum_rows.append(_dotf(tri, r1) + _dotf(tri, r2) + _dotf(tri, r3))
        cum_cols.append(_dotf(c1, tri_t) + _dotf(c2, tri_t) + _dotf(c3, tri_t))
    kq, inter, s_prev, m_prev = [], [], [], []
    for d, hh in combos:
        idx = d * MLSTM_HEADS + hh
        sl = slice(MLSTM_DH * hh, MLSTM_DH * (hh + 1))
        qth = dirs[d][0][sl, :]
        s_prev.append(s_scr[idx])
        m_prev.append(m_scr[idx:idx + 1, 0:1])
        kq.append(_dotf(dirs[d][1][:, sl], qth))
        inter.append(_dotf(s_prev[-1].astype(BF16), qth))

    sw, den, w_inter, floor, vw, decay, m_new = [], [], [], [], [], [], []
    for c, (d, hh) in enumerate(combos):
        il = (0 if d == 0 else 2 * MLSTM_HEADS) + hh
        fl = il + MLSTM_HEADS
        sl = slice(MLSTM_DH * hh, MLSTM_DH * (hh + 1))
        c_col = g_rows[d][:, il:il + 1] - cum_rows[d][:, fl:fl + 1]
        b_row = cum_cols[d][fl:fl + 1, :]
        c_row = g_cols[d][il:il + 1, :] - b_row
        c_mat = jnp.where(masks[d], c_col, NEG_BIG)
        u = jnp.maximum(jnp.max(c_mat, axis=0, keepdims=True), m_prev[c])
        s = kq[c] * jnp.exp(c_mat - u)
        sw.append(s.astype(BF16))
        den.append(jnp.sum(s, axis=0, keepdims=True))
        w_inter.append(jnp.exp(m_prev[c] - u))
        floor.append(jnp.exp(-(b_row + u)))
        b_end = b_row[:, L - 1:L] if d == 0 else b_row[:, 0:1]
        g_row = b_end + c_row
        m_new.append(jnp.maximum(b_end + m_prev[c], jnp.max(g_row, axis=-1, keepdims=True)))
        decay.append(jnp.exp(b_end + m_prev[c] - m_new[c]))
        w_s = jnp.exp(g_row - m_new[c])
        n_rows = jnp.where(sub == 0, w_s, 0.0)
        vw.append(jnp.concatenate([dirs[d][2][sl, :].astype(F32) * w_s, n_rows], axis=0).astype(BF16))

    for c, (d, hh) in enumerate(combos):
        idx = d * MLSTM_HEADS + hh
        sl = slice(MLSTM_DH * hh, MLSTM_DH * (hh + 1))
        num = _dotf(dirs[d][2][sl, :], sw[c]) + w_inter[c] * inter[c][:MLSTM_DH, :]
        dn = den[c] + w_inter[c] * inter[c][MLSTM_DH:MLSTM_DH + 1, :]
        dirs[d][5][sl, :] = (num / jnp.maximum(jnp.abs(dn), floor[c])).astype(BF16)
        s_scr[idx] = decay[c] * s_prev[c] + _dotf(vw[c], dirs[d][1][:, sl])
        m_scr[idx:idx + 1, :] = jnp.broadcast_to(m_new[c], (1, MLSTM_DH))

    @pl.when(j == nj - 1)
    def _():
        s_out_ref[...] = s_scr[...]
        m_out_ref[...] = m_scr[...]


def _mlstm(mqt, mk, mvt, g_rows, g_cols, s0, m0, batch, chunk):
    n = mk.shape[0]
    nc = n // batch // chunk
    fwd = lambda b, j: (b * nc + j, 0)
    bwd = lambda b, j: (b * nc + (nc - 1 - j), 0)
    fwd_t = lambda b, j: (0, b * nc + j)
    bwd_t = lambda b, j: (0, b * nc + (nc - 1 - j))
    nst = 2 * MLSTM_HEADS
    tok = lambda fn: pl.BlockSpec((chunk, MLSTM_WIDTH), fn)
    tok_t = lambda fn: pl.BlockSpec((MLSTM_WIDTH, chunk), fn)
    state_s = pl.BlockSpec((None, nst, MLSTM_ST, MLSTM_DH), lambda b, j: (b, 0, 0, 0))
    state_m = pl.BlockSpec((None, nst, MLSTM_DH), lambda b, j: (b, 0, 0))
    return pl.pallas_call(
        _mlstm_kernel,
        grid=(batch, nc),
        in_specs=[tok_t(fwd_t), tok(fwd), tok_t(fwd_t), pl.BlockSpec((chunk, N_GATES), fwd),
                  pl.BlockSpec((N_GATES, chunk), fwd_t),
                  tok_t(bwd_t), tok(bwd), tok_t(bwd_t), pl.BlockSpec((chunk, N_GATES), bwd),
                  pl.BlockSpec((N_GATES, chunk), bwd_t),
                  state_s, state_m],
        out_specs=[tok_t(fwd_t), tok_t(bwd_t), state_s, state_m],
        out_shape=[jax.ShapeDtypeStruct((MLSTM_WIDTH, n), BF16), jax.ShapeDtypeStruct((MLSTM_WIDTH, n), BF16),
                   jax.ShapeDtypeStruct(s0.shape, F32), jax.ShapeDtypeStruct(m0.shape, F32)],
        scratch_shapes=[pltpu.VMEM((nst, MLSTM_ST, MLSTM_DH), F32), pltpu.VMEM((nst, MLSTM_DH), F32)],
        compiler_params=_cparams("parallel", "arbitrary"),
        name="mlstm",
    )(mqt, mk, mvt, g_rows, g_cols, mqt, mk, mvt, g_rows, g_cols, s0, m0)


def _even_out_kernel(x_ref, gate_ref, att_ref, hf_ref, hb_ref, mo_ref, og_ref, w_ref,
                     fmod_ref, fg_ref, fwg_ref, fwu_ref, fwd_ref, o_ref):
    hs = hf_ref[...].astype(F32) + hb_ref[...].astype(F32)
    og = og_ref[...]
    parts = [att_ref[...]]
    for hh in range(MLSTM_HEADS):
        sl = slice(MLSTM_DH * hh, MLSTM_DH * (hh + 1))
        blk = hs[sl, :]
        hn = blk * lax.rsqrt(jnp.mean(blk * blk, axis=0, keepdims=True) + EPS) * og[sl, :]
        parts.append((hn * jax.nn.sigmoid(mo_ref[sl, :].astype(F32))).astype(BF16))
    mixed = jnp.concatenate(parts, axis=0)
    y = lax.dot_general(mixed, w_ref[...], (((0,), (0,)), ((), ())), preferred_element_type=F32)
    o_ref[...] = _ffn_math(x_ref[...] + gate_ref[...] * y, fmod_ref, fg_ref, fwg_ref, fwu_ref, fwd_ref)


def _even_out(x2d, gates, row_fn, att, hf, hb, mo, out_g, w_out, ffn_args, tm):
    n = x2d.shape[0]
    tok = pl.BlockSpec((tm, D_MODEL), lambda i: (i, 0))
    tok_t = pl.BlockSpec((MLSTM_WIDTH, tm), lambda i: (0, i))
    return pl.pallas_call(
        _even_out_kernel,
        grid=(n // tm,),
        in_specs=[tok, pl.BlockSpec((None, 1, D_MODEL), lambda i: (row_fn(i), 0, 0)),
                  tok_t, tok_t, tok_t, tok_t, _const_spec((MLSTM_WIDTH, 1)), _const_spec((D_MODEL, D_MODEL))]
        + _ffn_specs(row_fn, *ffn_args[0]),
        out_specs=tok,
        out_shape=jax.ShapeDtypeStruct((n, D_MODEL), F32),
        compiler_params=_cparams("parallel"),
        name="even_out_ffn",
    )(x2d, gates, att, hf, hb, mo, out_g.reshape(MLSTM_WIDTH, 1), w_out, *ffn_args[1:])


def _gelu_tanh(x):
    return 0.5 * x * (1.0 + jnp.tanh(0.7978845608028654 * (x + 0.044715 * (x * x * x))))


def _odd_in_kernel(x_ref, xp_ref, xn_ref, mod_ref, g_ref, w_ref, cw_ref, cb_ref, gate_ref, xc_ref, ext_scr, *, nt):
    tm = x_ref.shape[0]
    ti = pl.program_id(0) % nt
    g, shift, scale = g_ref[...], mod_ref[0:1, :], mod_ref[1:2, :]
    halo = jnp.concatenate([xp_ref[...], xn_ref[...]], axis=0)
    ph = _dotf(_norm_mod(halo, g, shift, scale).astype(BF16), w_ref[:, RNN_WIDTH:])
    ext_scr[0:8, :] = jnp.where(ti > 0, ph[0:8, :], 0.0)
    ext_scr[8 + tm:16 + tm, :] = jnp.where(ti < nt - 1, ph[8:16, :], 0.0)
    cw = cw_ref[...]
    cb = cb_ref[...]
    rc = tm // ODD_IN_CHUNKS

    def project(c):
        rows = slice(rc * c, rc * (c + 1))
        p = _dotf(_norm_mod(x_ref[rows, :], g, shift, scale).astype(BF16), w_ref[...])
        gate_ref[rows, :] = _gelu_tanh(p[:, :RNN_WIDTH]).astype(BF16)
        ext_scr[8 + rc * c:8 + rc * (c + 1), :] = p[:, RNN_WIDTH:]

    def conv(c):
        win = ext_scr[rc * c:rc * (c + 1) + 16, :]
        xc = cb + cw[2:3, :] * win[8:8 + rc, :]
        for k in (0, 1, 3):
            xc = xc + cw[k:k + 1, :] * pltpu.roll(win, (2 - k) % (rc + 16), axis=0)[8:8 + rc, :]
        xc_ref[rc * c:rc * (c + 1), :] = xc

    project(0)
    for c in range(1, ODD_IN_CHUNKS):
        project(c)
        conv(c - 1)
    conv(ODD_IN_CHUNKS - 1)


def _odd_in(x2d, mods, row_fn, g, w_in, conv_w, conv_b, batch, tm):
    n = x2d.shape[0]
    nt = n // batch // tm
    t8 = tm // 8
    tok = pl.BlockSpec((tm, D_MODEL), lambda i: (i, 0))
    prev = pl.BlockSpec((8, D_MODEL), lambda i: (jnp.maximum(i * t8 - 1, 0), 0))
    nxt = pl.BlockSpec((8, D_MODEL), lambda i: (jnp.minimum((i + 1) * t8, n // 8 - 1), 0))
    return pl.pallas_call(
        functools.partial(_odd_in_kernel, nt=nt),
        grid=(n // tm,),
        in_specs=[tok, prev, nxt, pl.BlockSpec((None, 3, D_MODEL), lambda i: (row_fn(i), 0, 0)),
                  _const_spec((1, D_MODEL)), _const_spec((D_MODEL, 2 * RNN_WIDTH)),
                  _const_spec((CONV_W, RNN_WIDTH)), _const_spec((1, RNN_WIDTH))],
        out_specs=[tok, tok],
        out_shape=[jax.ShapeDtypeStruct((n, RNN_WIDTH), BF16), jax.ShapeDtypeStruct((n, RNN_WIDTH), F32)],
        scratch_shapes=[pltpu.VMEM((tm + 16, RNN_WIDTH), F32)],
        compiler_params=_cparams("parallel"),
        name="odd_in",
    )(x2d, x2d, x2d, mods, g.reshape(1, D_MODEL), w_in, conv_w, conv_b)


def _rglru_kernel(xc_ref, wax_ref, ba_ref, bx_ref, lam_ref, h0_ref, h_ref, ht_ref, a_scr, b_scr, carry_scr, *, nt):
    d = pl.program_id(0)
    i = pl.program_id(2)
    tm = xc_ref.shape[0]
    nb8 = tm // 8

    xc = xc_ref[...]
    neg_lam = -lam_ref[...]
    half_rate = (-0.5 * LRU_C) * (jnp.maximum(neg_lam, 0.0) + jnp.log1p(jnp.exp(-jnp.abs(neg_lam))))
    ba = ba_ref[...]
    bx = bx_ref[...]
    for nb in range(RNN_BLOCKS):
        sl = slice(RNN_BLOCK_DIM * nb, RNN_BLOCK_DIM * (nb + 1))
        u = xc[:, sl]
        rg = _dotf(u.astype(BF16), wax_ref[nb])
        log_a = jnp.tanh(rg[:, :RNN_BLOCK_DIM] + ba[:, sl]) * half_rate[:, sl] + half_rate[:, sl]
        ig = 0.5 * jnp.tanh(rg[:, RNN_BLOCK_DIM:] + bx[:, sl]) + 0.5
        a = jnp.exp(log_a)
        a_scr[:, sl] = a
        b_scr[:, sl] = jnp.sqrt(-jnp.tanh(log_a) * (a * a + 1.0)) * (ig * u)

    @pl.when(i == 0)
    def _():
        carry_scr[...] = h0_ref[...]

    def scan_tile(reverse):
        a = a_scr[...].reshape(nb8, 8, RNN_WIDTH)
        b = b_scr[...].reshape(nb8, 8, RNN_WIDTH)
        r8 = lax.broadcasted_iota(jnp.int32, (nb8, 8, RNN_WIDTH), 1)
        for s in (1, 2, 4):
            keep = (r8 < 8 - s) if reverse else (r8 >= s)
            shift = 8 - s if reverse else s
            a_in = jnp.where(keep, pltpu.roll(a, shift, axis=1), 1.0)
            b_in = jnp.where(keep, pltpu.roll(b, shift, axis=1), 0.0)
            b = a * b_in + b
            a = a * a_in
        a_scr[...] = a.reshape(tm, RNN_WIDTH)
        b_scr[...] = b.reshape(tm, RNN_WIDTH)

        def block_pair(jp, h):
            base = pl.multiple_of((nb8 // 2 - 1 - jp if reverse else jp) * 16, 16)
            halves = [None, None]
            for half in ((1, 0) if reverse else (0, 1)):
                rows = pl.ds(base + 8 * half, 8)
                hb = a_scr[rows, :] * h + b_scr[rows, :]
                halves[half] = hb
                h = hb[0:1, :] if reverse else hb[7:8, :]
            h_ref[pl.ds(base, 16), :] = jnp.concatenate(halves, axis=0).astype(h_ref.dtype)
            return h

        carry_scr[...] = lax.fori_loop(0, nb8 // 2, block_pair, carry_scr[...], unroll=2)

    @pl.when(d == 0)
    def _():
        scan_tile(False)

    @pl.when(d == 1)
    def _():
        scan_tile(True)

    @pl.when(i == nt - 1)
    def _():
        ht_ref[...] = carry_scr[...]


def _rglru(xc2d, h0, w, batch, tm):
    n = xc2d.shape[0]
    nt = n // batch // tm
    tile = lambda d, i: jnp.where(d == 0, i, nt - 1 - i)
    per_dir = lambda shape: pl.BlockSpec((None,) + shape, lambda d, b, i: (d,) + (0,) * len(shape))
    state = pl.BlockSpec((None, None, 1, RNN_WIDTH), lambda d, b, i: (d, b, 0, 0))
    return pl.pallas_call(
        functools.partial(_rglru_kernel, nt=nt),
        grid=(2, batch, nt),
        in_specs=[pl.BlockSpec((tm, RNN_WIDTH), lambda d, b, i: (b * nt + tile(d, i), 0)),
                  per_dir((RNN_BLOCKS, RNN_BLOCK_DIM, 2 * RNN_BLOCK_DIM)), per_dir((1, RNN_WIDTH)), per_dir((1, RNN_WIDTH)),
                  per_dir((1, RNN_WIDTH)), state],
        out_specs=[pl.BlockSpec((None, tm, RNN_WIDTH), lambda d, b, i: (d, b * nt + tile(d, i), 0)), state],
        out_shape=[jax.ShapeDtypeStruct((2, n, RNN_WIDTH), BF16), jax.ShapeDtypeStruct((2, batch, 1, RNN_WIDTH), F32)],
        scratch_shapes=[pltpu.VMEM((tm, RNN_WIDTH), F32), pltpu.VMEM((tm, RNN_WIDTH), F32),
                        pltpu.VMEM((1, RNN_WIDTH), F32)],
        compiler_params=_cparams("parallel", "parallel", "arbitrary"),
        name="rglru",
    )(xc2d, w["w_ax"], w["b_a"], w["b_x"], w["lam"], h0)


def _odd_out_kernel(x_ref, gate_ref, h_ref, gg_ref, w_ref, fmod_ref, fg_ref, fwg_ref, fwu_ref, fwd_ref, o_ref):
    y = ((h_ref[0].astype(F32) + h_ref[1].astype(F32)) * gg_ref[...].astype(F32)).astype(BF16)
    x_mid = x_ref[...] + gate_ref[...] * _dotf(y, w_ref[...])
    o_ref[...] = _ffn_math(x_mid, fmod_ref, fg_ref, fwg_ref, fwu_ref, fwd_ref)


def _odd_out(x2d, gates, row_fn, h2, gate_branch, w_out, ffn_args, tm):
    n = x2d.shape[0]
    tok = pl.BlockSpec((tm, D_MODEL), lambda i: (i, 0))
    return pl.pallas_call(
        _odd_out_kernel,
        grid=(n // tm,),
        in_specs=[tok, pl.BlockSpec((None, 1, D_MODEL), lambda i: (row_fn(i), 0, 0)),
                  pl.BlockSpec((2, tm, RNN_WIDTH), lambda i: (0, i, 0)), tok, _const_spec((RNN_WIDTH, D_MODEL))]
        + _ffn_specs(row_fn, *ffn_args[0]),
        out_specs=tok,
        out_shape=jax.ShapeDtypeStruct((n, D_MODEL), F32),
        compiler_params=_cparams("parallel"),
        name="odd_out_ffn",
    )(x2d, gates, h2, gate_branch, w_out, *ffn_args[1:])


def _rope_tables(seq):
    t = np.arange(seq)
    inv = np.float32(ROPE_BASE) ** (-np.arange(0, ROPE_AXIS_DIM, 2, dtype=np.float32) / np.float32(ROPE_AXIS_DIM))
    ang_r = (t // GRID_W).astype(np.float32)[:, None] * inv
    ang_c = (t % GRID_W).astype(np.float32)[:, None] * inv
    cos32 = np.concatenate([np.cos(ang_r), np.cos(ang_r), np.cos(ang_c), np.cos(ang_c)], axis=1)
    sin32 = np.concatenate([-np.sin(ang_r), np.sin(ang_r), -np.sin(ang_c), np.sin(ang_c)], axis=1)
    pad = HEAD_BLOCK - MLA_QK
    cos = np.concatenate([np.ones((seq, MLA_NOPE), np.float32), cos32, np.ones((seq, pad), np.float32)], axis=1)
    sin = np.concatenate([np.zeros((seq, MLA_NOPE), np.float32), sin32, np.zeros((seq, pad), np.float32)], axis=1)
    cost = np.concatenate([np.cos(ang_r), np.cos(ang_c)], axis=1).T
    sint = np.concatenate([np.sin(ang_r), np.sin(ang_c)], axis=1).T
    return tuple(jnp.asarray(a, F32) for a in (cos, sin, cost, sint))


def _even_weights(w_in, cq_g, w_uq, ckv_g, w_ukv, q_g, k_g, gate_b):
    pad = HEAD_BLOCK - MLA_QK
    w_in, w_uq, w_ukv = w_in.astype(BF16), w_uq.astype(BF16), w_ukv.astype(BF16)
    kr_cols = w_in[:, OFF_KR:OFF_KR + MLA_ROPE]
    w_kr = jnp.pad(kr_cols, ((0, 0), (MLA_NOPE, pad)))
    m_cols = w_in[:, OFF_KR + MLA_ROPE:OFF_KR + MLA_ROPE + 4 * MLSTM_WIDTH]
    g_cols = w_in[:, OFF_KR + MLA_ROPE + 4 * MLSTM_WIDTH:]
    mq_cols, mk_cols, mv_cols, mo_cols = jnp.split(m_cols, 4, axis=1)
    w_all = jnp.concatenate([w_in[:, :OFF_KR], w_kr, mk_cols, jnp.pad(g_cols, ((0, 0), (0, HEAD_BLOCK - N_GATES)))], axis=1)
    uq = jnp.pad(w_uq.reshape(MLA_Q_RANK, MLA_HEADS, MLA_QK), ((0, 0), (0, 0), (0, pad)))
    ukv = w_ukv.reshape(MLA_KV_RANK, MLA_HEADS, MLA_NOPE + MLA_V)
    uk = jnp.pad(ukv[:, :, :MLA_NOPE], ((0, 0), (0, 0), (0, HEAD_BLOCK - MLA_NOPE)))
    uv = ukv[:, :, MLA_NOPE:]
    kg_blk = jnp.pad(k_g[:MLA_NOPE], (0, HEAD_BLOCK - MLA_NOPE))
    krg_blk = jnp.pad(k_g[MLA_NOPE:], (MLA_NOPE, pad))
    return {
        "w_all": w_all,
        "w_gt": jnp.concatenate([g_cols, mq_cols, mv_cols, mo_cols], axis=1).T,
        "gb_row": gate_b.reshape(1, N_GATES),
        "gb_col": gate_b.reshape(N_GATES, 1),
        "cq_g": cq_g.reshape(1, MLA_Q_RANK),
        "w_uqt": uq.reshape(MLA_Q_RANK, MLA_HEADS * HEAD_BLOCK).T,
        "ckv_g": ckv_g.reshape(1, MLA_KV_RANK),
        "w_uk": uk.reshape(MLA_KV_RANK, MLA_HEADS * HEAD_BLOCK),
        "w_uvt": uv.reshape(MLA_KV_RANK, MLA_HEADS * MLA_V).T,
        "qg_nope": q_g[:MLA_NOPE].reshape(MLA_NOPE, 1),
        "qg_rope": q_g[MLA_NOPE:].reshape(MLA_ROPE, 1),
        "k_g": jnp.tile(kg_blk, MLA_HEADS).reshape(1, -1),
        "kr_g": krg_blk.reshape(1, HEAD_BLOCK),
    }


def kernel(x, c, ctx, c_ctx, mod_w, mod_b, norm_g, ffn_w_gate, ffn_w_up, ffn_w_down, even_w_in, even_w_out, mla_cq_g, mla_w_uq, mla_ckv_g, mla_w_ukv, mla_q_g, mla_k_g, mlstm_gate_b, mlstm_out_g, odd_w_in, odd_conv_w, odd_conv_b, lru_w_a, lru_b_a, lru_w_x, lru_b_x, lru_lam, odd_w_out):
    batch, seq, _ = x.shape
    ctx_len = ctx.shape[1]
    depth = mod_w.shape[0]
    assert depth == 2 and batch <= 7 and ctx_len == ROWS_C

    cond8 = jnp.zeros((8, D_MODEL), F32).at[:batch].set(c).at[batch].set(c_ctx)
    mods = _ada_params(cond8, mod_w, mod_b)
    mods = mods.reshape(depth, 8, 3, 3, D_MODEL).transpose(0, 2, 1, 3, 4)

    x_row = lambda tm: (lambda i: i // (seq // tm))
    c_row = lambda i: batch
    x2 = x.reshape(batch * seq, D_MODEL)
    c2 = ctx.reshape(batch * ctx_len, D_MODEL)

    wg = ffn_w_gate.astype(BF16)
    wu = ffn_w_up.astype(BF16)
    wd = ffn_w_down.astype(BF16)

    def ffn_args(layer, which):
        return ((layer, which), mods[layer, 2 * which], norm_g[layer, 2 * which].reshape(1, D_MODEL), wg, wu, wd)

    x2 = _ffn(x2, ffn_args(0, 0), x_row(ROWS_X_FFN), ROWS_X_FFN)
    c2 = _ffn(c2, ffn_args(0, 0), c_row, ROWS_C)

    ew = _even_weights(even_w_in[0], mla_cq_g[0], mla_w_uq[0], mla_ckv_g[0], mla_w_ukv[0], mla_q_g[0], mla_k_g[0],
                       mlstm_gate_b[0])
    tabs_x = _rope_tables(seq)
    tabs_c = (jnp.ones((ROWS_C, HEAD_BLOCK), F32), jnp.zeros((ROWS_C, HEAD_BLOCK), F32),
              jnp.ones((ROPE_AXIS_DIM, ROWS_C), F32), jnp.zeros((ROPE_AXIS_DIM, ROWS_C), F32))
    key_len = seq + ctx_len
    px = _even_prep(x2, mods[0, 1], x_row(ROWS_X), norm_g[0, 1], tabs_x, lambda i: i % (seq // ROWS_X), ew, ROWS_X,
                    batch, key_len, 0)
    qt_x, k_all, vt_all, mqt_x, mk_x, mvt_x, mot_x, gr_x, gc_x = px
    pc = _even_prep(c2, mods[0, 1], c_row, norm_g[0, 1], tabs_c, lambda i: 0, ew, ROWS_C,
                    batch, key_len, seq, kv_bufs=(k_all, vt_all))
    qt_c, k_all, vt_all, mqt_c, mk_c, mvt_c, mot_c, gr_c, gc_c = pc

    att_x = _attention(qt_x, k_all, vt_all, batch, ATTN_QUERIES_X, key_len, 0)
    att_c = _attention(qt_c, k_all, vt_all, batch, ROWS_C, ctx_len, seq)

    nst = 2 * MLSTM_HEADS
    s0 = jnp.zeros((batch, nst, MLSTM_ST, MLSTM_DH), F32)
    m0 = jnp.zeros((batch, nst, MLSTM_DH), F32)
    hcf, hcb, s_c, m_c = _mlstm(mqt_c, mk_c, mvt_c, gr_c, gc_c, s0, m0, batch, MLSTM_CHUNK)
    hxf, hxb, _, _ = _mlstm(mqt_x, mk_x, mvt_x, gr_x, gc_x, s_c, m_c, batch, MLSTM_CHUNK)

    w_out0 = even_w_out[0].astype(BF16)
    gate0 = mods[0, 1][:, 2:3, :]
    x2 = _even_out(x2, gate0, x_row(ROWS_X), att_x, hxf, hxb, mot_x, mlstm_out_g[0], w_out0, ffn_args(0, 1), ROWS_X)
    c2 = _even_out(c2, gate0, c_row, att_c, hcf, hcb, mot_c, mlstm_out_g[0], w_out0, ffn_args(0, 1), ROWS_C)

    x2 = _ffn(x2, ffn_args(1, 0), x_row(ROWS_X_FFN), ROWS_X_FFN)
    c2 = _ffn(c2, ffn_args(1, 0), c_row, ROWS_C)
    w_in1 = odd_w_in[0].astype(BF16)
    conv_b = odd_conv_b[0].reshape(1, RNN_WIDTH)
    gg_x, xc_x = _odd_in(x2, mods[1, 1], x_row(ROWS_X), norm_g[1, 1], w_in1, odd_conv_w[0], conv_b, batch, ROWS_X)
    _, xc_c = _odd_in(c2, mods[1, 1], c_row, norm_g[1, 1], w_in1, odd_conv_w[0], conv_b, batch, ROWS_C)
    rw = {
        "w_ax": (0.5 * jnp.concatenate([lru_w_a[0], lru_w_x[0]], axis=-1)).astype(BF16),
        "b_a": 0.5 * lru_b_a[0].reshape(2, 1, RNN_WIDTH),
        "b_x": 0.5 * lru_b_x[0].reshape(2, 1, RNN_WIDTH),
        "lam": lru_lam[0].reshape(2, 1, RNN_WIDTH),
    }
    h0 = jnp.zeros((2, batch, 1, RNN_WIDTH), F32)
    _, st_c = _rglru(xc_c, h0, rw, batch, RGLRU_ROWS)
    h_x, _ = _rglru(xc_x, st_c, rw, batch, RGLRU_ROWS)
    x2 = _odd_out(x2, mods[1, 1][:, 2:3, :], x_row(ROWS_X), h_x, gg_x, odd_w_out[0].astype(BF16), ffn_args(1, 1),
                  ROWS_X)
    return x2.reshape(batch, seq, D_MODEL)
```

```python
import functools

import jax
import jax.numpy as jnp
import numpy as np
from jax import lax
from jax.experimental import pallas as pl
from jax.experimental.pallas import tpu as pltpu

F32 = jnp.float32
BF16 = jnp.bfloat16

D_MODEL = 1024
GRID_W = 64
EPS = 1e-6
D_FF = 2816
FFN_RESIDUAL = 0.5

MLA_HEADS = 8
MLA_Q_RANK = 384
MLA_KV_RANK = 256
MLA_NOPE = 64
MLA_ROPE = 32
MLA_V = 64
MLA_QK = MLA_NOPE + MLA_ROPE
ROPE_AXIS_DIM = MLA_ROPE // 2
ROPE_BASE = 10000.0
HEAD_BLOCK = 128
V_ROWS = MLA_V + 16

MLSTM_HEADS = 4
MLSTM_DH = 128
MLSTM_WIDTH = MLSTM_HEADS * MLSTM_DH
N_GATES = 4 * MLSTM_HEADS
MLSTM_ST = MLSTM_DH + 16

RNN_WIDTH = 1024
RNN_BLOCKS = 8
RNN_BLOCK_DIM = RNN_WIDTH // RNN_BLOCKS
CONV_W = 4
LRU_C = 8.0

NEG_BIG = -1e30
LOG2_E = 1.4426950408889634
VMEM_LIMIT = 56 * 1024 * 1024

ADA_COLS = 1152
ROWS_X = 512
ROWS_X_FFN = 1024
ROWS_C = 256
FFN_CHUNK_ROWS = 128
ODD_IN_CHUNKS = 4
KEY_CHUNK = 256
ATTN_HEADS_PER_STEP = 2
ATTN_QUERIES_X = 512
MLSTM_CHUNK = 256
RGLRU_ROWS = 256


def _cparams(*sem):
    return pltpu.CompilerParams(dimension_semantics=sem, vmem_limit_bytes=VMEM_LIMIT)


def _const_spec(shape):
    zeros = (0,) * len(shape)
    return pl.BlockSpec(shape, lambda *_: zeros, pipeline_mode=pl.Buffered(1))


def _dotf(a, b):
    return jnp.dot(a, b, preferred_element_type=F32)


def _split3(a):
    a1 = a.astype(BF16)
    r1 = a - a1.astype(F32)
    a2 = r1.astype(BF16)
    a3 = (r1 - a2.astype(F32)).astype(BF16)
    return a1, a2, a3


def _norm_mod(x, g, shift, scale):
    ms = jnp.mean(x * x, axis=-1, keepdims=True)
    return (x * lax.rsqrt(ms + EPS)) * (g * (1.0 + scale)) + shift


def _rms(x, g):
    ms = jnp.mean(x * x, axis=-1, keepdims=True)
    return x * lax.rsqrt(ms + EPS) * g


def _log_sigmoid(x):
    return jnp.minimum(x, 0.0) - jnp.log1p(jnp.exp(-jnp.abs(x)))


def _ada_kernel(cond_ref, w_ref, b_ref, o_ref):
    c = cond_ref[...]
    s = c * jax.nn.sigmoid(c)
    s1, s2, _ = _split3(s)
    w = w_ref[...]
    w1 = w.astype(BF16)
    w2 = (w - w1.astype(F32)).astype(BF16)
    o_ref[...] = _dotf(s1, w1) + _dotf(s1, w2) + _dotf(s2, w1) + b_ref[...]


def _ada_params(cond8, mod_w, mod_b):
    depth, _, n = mod_w.shape
    tn = ADA_COLS
    return pl.pallas_call(
        _ada_kernel,
        grid=(depth, n // tn),
        in_specs=[
            pl.BlockSpec((8, D_MODEL), lambda l, j: (0, 0)),
            pl.BlockSpec((None, D_MODEL, tn), lambda l, j: (l, 0, j)),
            pl.BlockSpec((None, 1, tn), lambda l, j: (l, 0, j)),
        ],
        out_specs=pl.BlockSpec((None, 8, tn), lambda l, j: (l, 0, j)),
        out_shape=jax.ShapeDtypeStruct((depth, 8, n), F32),
        compiler_params=_cparams("parallel", "parallel"),
        name="ada_params",
    )(cond8, mod_w, mod_b.reshape(depth, 1, n))


def _ffn_math(x, mod_ref, g_ref, wg_ref, wu_ref, wd_ref):
    outs = []
    rc = min(FFN_CHUNK_ROWS, x.shape[0])
    for c in range(x.shape[0] // rc):
        xc = x[rc * c:rc * (c + 1), :]
        h = _norm_mod(xc, g_ref[...], mod_ref[0:1, :], mod_ref[1:2, :]).astype(BF16)
        g = _dotf(h, wg_ref[...])
        u = _dotf(h, wu_ref[...])
        a = (g * jax.nn.sigmoid(g) * u).astype(BF16)
        outs.append(xc + (FFN_RESIDUAL * mod_ref[2:3, :]) * _dotf(a, wd_ref[...]))
    return jnp.concatenate(outs, axis=0)


def _ffn_specs(row_fn, layer, which):
    weight = lambda r, c: pl.BlockSpec((None, None, r, c), lambda i: (layer, which, 0, 0), pipeline_mode=pl.Buffered(1))
    return [pl.BlockSpec((None, 3, D_MODEL), lambda i: (row_fn(i), 0, 0)), _const_spec((1, D_MODEL)),
            weight(D_MODEL, D_FF), weight(D_MODEL, D_FF), weight(D_FF, D_MODEL)]


def _ffn_kernel(x_ref, mod_ref, g_ref, wg_ref, wu_ref, wd_ref, o_ref):
    o_ref[...] = _ffn_math(x_ref[...], mod_ref, g_ref, wg_ref, wu_ref, wd_ref)


def _ffn(x2d, ffn_args, row_fn, tm):
    n = x2d.shape[0]
    tok = pl.BlockSpec((tm, D_MODEL), lambda i: (i, 0))
    return pl.pallas_call(
        _ffn_kernel,
        grid=(n // tm,),
        in_specs=[tok] + _ffn_specs(row_fn, *ffn_args[0]),
        out_specs=tok,
        out_shape=jax.ShapeDtypeStruct((n, D_MODEL), F32),
        compiler_params=_cparams("parallel"),
        name="ffn",
    )(x2d, *ffn_args[1:])


W_ALL = MLA_Q_RANK + MLA_KV_RANK + HEAD_BLOCK + MLSTM_WIDTH + HEAD_BLOCK
OFF_CKV = MLA_Q_RANK
OFF_KR = OFF_CKV + MLA_KV_RANK
OFF_MK = OFF_KR + HEAD_BLOCK
OFF_GATES = OFF_MK + MLSTM_WIDTH
W_T_ROWS = N_GATES + 3 * MLSTM_WIDTH


def _rope_block(xb, cos, sin, first_half):
    partner = jnp.where(first_half, pltpu.roll(xb, HEAD_BLOCK - 8, axis=1), pltpu.roll(xb, 8, axis=1))
    return xb * cos + partner * sin


def _even_prep_kernel(x_ref, mod_ref, g_ref, cos_ref, sin_ref, cost_ref, sint_ref, wall_ref, wgt_ref, gbr_ref, gbc_ref,
                      cqg_ref, wuqt_ref, ckvg_ref, wuk_ref, wuvt_ref, qgn_ref, qgr_ref, kg_ref, krg_ref,
                      qt_ref, k_ref, vt_ref, mqt_ref, mk_ref, mvt_ref, mot_ref, gr_ref, gc_ref):
    tm = x_ref.shape[0]
    x = x_ref[...]
    h = _norm_mod(x, g_ref[...], mod_ref[0:1, :], mod_ref[1:2, :]).astype(BF16)
    p = _dotf(h, wall_ref[...])
    pt = lax.dot_general(wgt_ref[...], h, (((1,), (1,)), ((), ())), preferred_element_type=F32)

    mk_ref[...] = (p[:, OFF_MK:OFF_MK + MLSTM_WIDTH] * (MLSTM_DH ** -0.5)).astype(BF16)
    mqt_ref[...] = pt[N_GATES:N_GATES + MLSTM_WIDTH, :].astype(BF16)
    mvt_ref[...] = pt[N_GATES + MLSTM_WIDTH:N_GATES + 2 * MLSTM_WIDTH, :].astype(BF16)
    mot_ref[...] = pt[N_GATES + 2 * MLSTM_WIDTH:, :].astype(BF16)
    graw = p[:, OFF_GATES:OFF_GATES + HEAD_BLOCK][:, :N_GATES] + gbr_ref[...]
    lane = lax.broadcasted_iota(jnp.int32, (tm, N_GATES), 1)
    gr_ref[...] = jnp.where((lane // MLSTM_HEADS) % 2 == 1, _log_sigmoid(graw), graw)
    gt = pt[:N_GATES, :] + gbc_ref[...]
    sub = lax.broadcasted_iota(jnp.int32, (N_GATES, tm), 0)
    gc_ref[...] = jnp.where((sub // MLSTM_HEADS) % 2 == 1, _log_sigmoid(gt), gt)

    cos = cos_ref[...]
    sin = sin_ref[...]
    lane_b = lax.broadcasted_iota(jnp.int32, (tm, HEAD_BLOCK), 1)
    first_half = (lane_b % ROPE_AXIS_DIM) < (ROPE_AXIS_DIM // 2)

    cqn = _rms(p[:, 0:MLA_Q_RANK], cqg_ref[...]).astype(BF16)
    ckvn = _rms(p[:, OFF_CKV:OFF_CKV + MLA_KV_RANK], ckvg_ref[...]).astype(BF16)
    nt_dims = (((1,), (1,)), ((), ()))
    k_raw = _dotf(ckvn, wuk_ref[...])
    vt = lax.dot_general(wuvt_ref[...], ckvn, nt_dims, preferred_element_type=F32).astype(BF16)
    ones_rows = jnp.where(lax.broadcasted_iota(jnp.int32, (V_ROWS - MLA_V, KEY_CHUNK), 0) == 0, 1.0, 0.0).astype(BF16)
    for cc in range(tm // KEY_CHUNK):
        for hh in range(MLA_HEADS):
            vt_ref[cc, V_ROWS * hh:V_ROWS * hh + MLA_V, :] = vt[MLA_V * hh:MLA_V * (hh + 1),
                                                                KEY_CHUNK * cc:KEY_CHUNK * (cc + 1)]
            vt_ref[cc, V_ROWS * hh + MLA_V:V_ROWS * (hh + 1), :] = ones_rows

    kr = p[:, OFF_KR:OFF_KR + HEAD_BLOCK]
    kr_ms = jnp.sum(kr * kr, axis=-1, keepdims=True) * (1.0 / MLA_ROPE)
    kr_rot = _rope_block(kr * lax.rsqrt(kr_ms + EPS) * krg_ref[...], cos, sin, first_half)
    kg = kg_ref[...]
    for hh in range(MLA_HEADS):
        sl = slice(HEAD_BLOCK * hh, HEAD_BLOCK * (hh + 1))
        kb = k_raw[:, sl]
        kb_ms = jnp.sum(kb * kb, axis=-1, keepdims=True) * (1.0 / MLA_NOPE)
        k_ref[:, sl] = (kb * lax.rsqrt(kb_ms + EPS) * kg[:, sl] + kr_rot).astype(BF16)

    qt_raw = lax.dot_general(wuqt_ref[...], cqn, nt_dims, preferred_element_type=F32)
    q_scale = (MLA_QK ** -0.5) * LOG2_E
    half = ROPE_AXIS_DIM // 2
    cost = cost_ref[...]
    sint = sint_ref[...]
    for hh in range(MLA_HEADS):
        r0 = HEAD_BLOCK * hh
        nope = qt_raw[r0:r0 + MLA_NOPE, :]
        rope = qt_raw[r0 + MLA_NOPE:r0 + MLA_QK, :]
        nope = nope * lax.rsqrt(jnp.mean(nope * nope, axis=0, keepdims=True) + EPS) * qgn_ref[...]
        rope = rope * lax.rsqrt(jnp.mean(rope * rope, axis=0, keepdims=True) + EPS) * qgr_ref[...]
        rot = []
        for ax in range(2):
            x1 = rope[ROPE_AXIS_DIM * ax:ROPE_AXIS_DIM * ax + half, :]
            x2 = rope[ROPE_AXIS_DIM * ax + half:ROPE_AXIS_DIM * (ax + 1), :]
            cs = cost[half * ax:half * (ax + 1), :]
            sn = sint[half * ax:half * (ax + 1), :]
            rot += [x1 * cs - x2 * sn, x2 * cs + x1 * sn]
        qt_ref[r0:r0 + MLA_NOPE, :] = (nope * q_scale).astype(BF16)
        qt_ref[r0 + MLA_NOPE:r0 + MLA_QK, :] = (jnp.concatenate(rot, axis=0) * q_scale).astype(BF16)
        qt_ref[r0 + MLA_QK:r0 + HEAD_BLOCK, :] = jnp.zeros((HEAD_BLOCK - MLA_QK, tm), BF16)


def _even_prep_into_kernel(*refs):
    n_in = 20
    _even_prep_kernel(*refs[:n_in], *refs[n_in + 2:])


def _even_prep(x2d, mods, row_fn, g, tabs, tab_fn, w, tm, batch, key_len, key_off, kv_bufs=None):
    n = x2d.shape[0]
    nt = n // batch // tm
    assert key_off % tm == 0 and tm % KEY_CHUNK == 0
    cos, sin, cost, sint = tabs
    tok = lambda width: pl.BlockSpec((tm, width), lambda i: (i, 0))
    tok_t = lambda rows: pl.BlockSpec((rows, tm), lambda i: (0, i))
    k_spec = pl.BlockSpec((None, tm, MLA_HEADS * HEAD_BLOCK), lambda i: (i // nt, key_off // tm + i % nt, 0))
    vt_spec = pl.BlockSpec((None, tm // KEY_CHUNK, MLA_HEADS * V_ROWS, KEY_CHUNK),
                           lambda i: (i // nt, key_off // tm + i % nt, 0, 0))
    out_shape = [
        jax.ShapeDtypeStruct((MLA_HEADS * HEAD_BLOCK, n), BF16),
        jax.ShapeDtypeStruct((batch, key_len, MLA_HEADS * HEAD_BLOCK), BF16),
        jax.ShapeDtypeStruct((batch, key_len // KEY_CHUNK, MLA_HEADS * V_ROWS, KEY_CHUNK), BF16),
        jax.ShapeDtypeStruct((MLSTM_WIDTH, n), BF16),
        jax.ShapeDtypeStruct((n, MLSTM_WIDTH), BF16),
        jax.ShapeDtypeStruct((MLSTM_WIDTH, n), BF16),
        jax.ShapeDtypeStruct((MLSTM_WIDTH, n), BF16),
        jax.ShapeDtypeStruct((n, N_GATES), F32),
        jax.ShapeDtypeStruct((N_GATES, n), F32),
    ]
    out_specs = [tok_t(1024), k_spec, vt_spec, tok_t(512), tok(512), tok_t(512), tok_t(512),
                 tok(N_GATES), tok_t(N_GATES)]
    in_specs = [
        tok(D_MODEL),
        pl.BlockSpec((None, 3, D_MODEL), lambda i: (row_fn(i), 0, 0)),
        _const_spec((1, D_MODEL)),
        pl.BlockSpec((tm, HEAD_BLOCK), lambda i: (tab_fn(i), 0)),
        pl.BlockSpec((tm, HEAD_BLOCK), lambda i: (tab_fn(i), 0)),
        pl.BlockSpec((ROPE_AXIS_DIM, tm), lambda i: (0, tab_fn(i))),
        pl.BlockSpec((ROPE_AXIS_DIM, tm), lambda i: (0, tab_fn(i))),
        _const_spec((D_MODEL, W_ALL)),
        _const_spec((W_T_ROWS, D_MODEL)),
        _const_spec((1, N_GATES)),
        _const_spec((N_GATES, 1)),
        _const_spec((1, MLA_Q_RANK)),
        _const_spec((1024, MLA_Q_RANK)),
        _const_spec((1, MLA_KV_RANK)),
        _const_spec((MLA_KV_RANK, 1024)),
        _const_spec((512, MLA_KV_RANK)),
        _const_spec((MLA_NOPE, 1)),
        _const_spec((MLA_ROPE, 1)),
        _const_spec((1, 1024)),
        _const_spec((1, HEAD_BLOCK)),
    ]
    args = [x2d, mods, g.reshape(1, D_MODEL), cos, sin, cost, sint, w["w_all"], w["w_gt"], w["gb_row"], w["gb_col"],
            w["cq_g"], w["w_uqt"], w["ckv_g"], w["w_uk"], w["w_uvt"], w["qg_nope"], w["qg_rope"], w["k_g"], w["kr_g"]]
    aliases = {}
    if kv_bufs is not None:
        aliases = {len(args): 1, len(args) + 1: 2}
        in_specs = in_specs + [pl.BlockSpec(memory_space=pl.ANY)] * 2
        args = args + list(kv_bufs)
    return pl.pallas_call(
        _even_prep_kernel if kv_bufs is None else _even_prep_into_kernel,
        grid=(n // tm,),
        in_specs=in_specs,
        out_specs=out_specs,
        out_shape=out_shape,
        input_output_aliases=aliases,
        compiler_params=_cparams("parallel"),
        name="even_prep",
    )(*args)


def _attn_kernel(qt_ref, k_ref, vt_ref, o_ref, st_scr, acc_scr):
    tq = qt_ref.shape[1]
    nk, _, tk = vt_ref.shape
    unroll = max(u for u in (16, 8, 4, 2) if (nk - 1) % u == 0)
    heads = range(acc_scr.shape[0])

    def scores(j, slot):
        off = pl.multiple_of(j * tk, tk)
        cms = []
        for hh in heads:
            sl = slice(HEAD_BLOCK * hh, HEAD_BLOCK * (hh + 1))
            st = _dotf(k_ref[pl.ds(off, tk), sl], qt_ref[sl, :])
            st_scr[slot, hh] = st
            cms.append(jnp.max(st, axis=0, keepdims=True))
        return tuple(cms)

    def absorb(j, slot, cms, ms):
        out = []
        for hh in heads:
            m_new = jnp.maximum(ms[hh], cms[hh])
            alpha = jnp.exp2(ms[hh] - m_new)
            pt = jnp.exp2(st_scr[slot, hh] - m_new)
            pv = _dotf(vt_ref[j, V_ROWS * hh:V_ROWS * (hh + 1), :], pt.astype(BF16))
            acc_scr[hh] = alpha * acc_scr[hh] + pv
            out.append(m_new)
        return tuple(out)

    def body(i, carry):
        cms, ms = carry
        for r in range(unroll):
            j = unroll * i + r
            cms_next = scores(j + 1, (r + 1) % 2)
            ms = absorb(j, r % 2, cms, ms)
            cms = cms_next
        return cms, ms

    acc_scr[...] = jnp.zeros(acc_scr.shape, F32)
    m0 = jnp.full((1, tq), NEG_BIG, F32)
    cms, ms = lax.fori_loop(0, (nk - 1) // unroll, body, (scores(0, 0), (m0,) * len(heads)))
    absorb(nk - 1, 0, cms, ms)
    for hh in heads:
        acc = acc_scr[hh]
        o_ref[MLA_V * hh:MLA_V * (hh + 1), :] = (acc[:MLA_V, :] / acc[MLA_V:MLA_V + 1, :]).astype(o_ref.dtype)


def _attention(qt, k3, vt4, batch, tq, key_len, key_off):
    nq = qt.shape[1] // batch // tq
    assert key_off % key_len == 0 and key_len % KEY_CHUNK == 0
    kb = key_off // key_len
    nk = key_len // KEY_CHUNK
    tk = KEY_CHUNK
    g = ATTN_HEADS_PER_STEP
    return pl.pallas_call(
        _attn_kernel,
        grid=(batch, MLA_HEADS // g, nq),
        in_specs=[
            pl.BlockSpec((g * HEAD_BLOCK, tq), lambda b, hp, i: (hp, b * nq + i)),
            pl.BlockSpec((None, key_len, g * HEAD_BLOCK), lambda b, hp, i: (b, kb, hp)),
            pl.BlockSpec((None, nk, g * V_ROWS, tk), lambda b, hp, i: (b, kb, hp, 0)),
        ],
        out_specs=pl.BlockSpec((g * MLA_V, tq), lambda b, hp, i: (hp, b * nq + i)),
        out_shape=jax.ShapeDtypeStruct((MLA_HEADS * MLA_V, qt.shape[1]), BF16),
        scratch_shapes=[pltpu.VMEM((2, g, tk, tq), F32), pltpu.VMEM((g, V_ROWS, tq), F32)],
        compiler_params=_cparams("parallel", "parallel", "arbitrary"),
        name="mla_attention",
    )(qt, k3, vt4)


def _mlstm_kernel(qtf_ref, kf_ref, vtf_ref, grf_ref, gcf_ref, qtb_ref, kb_ref, vtb_ref, grb_ref, gcb_ref,
                  s0_ref, m0_ref, hf_ref, hb_ref, s_out_ref, m_out_ref, s_scr, m_scr):
    j = pl.program_id(1)
    nj = pl.num_programs(1)
    L = kf_ref.shape[0]

    @pl.when(j == 0)
    def _():
        s_scr[...] = s0_ref[...]
        m_scr[...] = m0_ref[...]

    row = lax.broadcasted_iota(jnp.int32, (L, L), 0)
    col = lax.broadcasted_iota(jnp.int32, (L, L), 1)
    sub = lax.broadcasted_iota(jnp.int32, (MLSTM_ST - MLSTM_DH, L), 0)

    dirs = ((qtf_ref, kf_ref, vtf_ref, grf_ref, gcf_ref, hf_ref), (qtb_ref, kb_ref, vtb_ref, grb_ref, gcb_ref, hb_ref))
    combos = [(d, hh) for d in range(2) for hh in range(MLSTM_HEADS)]


    masks, cum_rows, cum_cols, g_rows, g_cols = [], [], [], [], []
    for d in range(2):
        s_le_t = (row <= col) if d == 0 else (row >= col)
        tri = jnp.where((col <= row) if d == 0 else (col >= row), 1.0, 0.0).astype(BF16)
        tri_t = jnp.where(s_le_t, 1.0, 0.0).astype(BF16)
        gr = dirs[d][3][...]
        gc = dirs[d][4][...]
        r1, r2, r3 = _split3(gr)
        c1, c2, c3 = _split3(gc)
        masks.append(s_le_t)
        g_rows.append(gr)
        g_cols.append(gc)
        cum_rows.append(_dotf(tri, r1) + _dotf(tri, r2) + _dotf(tri, r3))
        cum_cols.append(_dotf(c1, tri_t) + _dotf(c2, tri_t) + _dotf(c3, tri_t))
    kq, inter, s_prev, m_prev = [], [], [], []
    for d, hh in combos:
        idx = d * MLSTM_HEADS + hh
        sl = slice(MLSTM_DH * hh, MLSTM_DH * (hh + 1))
        qth = dirs[d][0][sl, :]
        s_prev.append(s_scr[idx])
        m_prev.append(m_scr[idx:idx + 1, 0:1])
        kq.append(_dotf(dirs[d][1][:, sl], qth))
        inter.append(_dotf(s_prev[-1].astype(BF16), qth))

    sw, den, w_inter, floor, vw, decay, m_new = [], [], [], [], [], [], []
    for c, (d, hh) in enumerate(combos):
        il = (0 if d == 0 else 2 * MLSTM_HEADS) + hh
        fl = il + MLSTM_HEADS
        sl = slice(MLSTM_DH * hh, MLSTM_DH * (hh + 1))
        c_col = g_rows[d][:, il:il + 1] - cum_rows[d][:, fl:fl + 1]
        b_row = cum_cols[d][fl:fl + 1, :]
        c_row = g_cols[d][il:il + 1, :] - b_row
        c_mat = jnp.where(masks[d], c_col, NEG_BIG)
        u = jnp.maximum(jnp.max(c_mat, axis=0, keepdims=True), m_prev[c])
        s = kq[c] * jnp.exp(c_mat - u)
        sw.append(s.astype(BF16))
        den.append(jnp.sum(s, axis=0, keepdims=True))
        w_inter.append(jnp.exp(m_prev[c] - u))
        floor.append(jnp.exp(-(b_row + u)))
        b_end = b_row[:, L - 1:L] if d == 0 else b_row[:, 0:1]
        g_row = b_end + c_row
        m_new.append(jnp.maximum(b_end + m_prev[c], jnp.max(g_row, axis=-1, keepdims=True)))
        decay.append(jnp.exp(b_end + m_prev[c] - m_new[c]))
        w_s = jnp.exp(g_row - m_new[c])
        n_rows = jnp.where(sub == 0, w_s, 0.0)
        vw.append(jnp.concatenate([dirs[d][2][sl, :].astype(F32) * w_s, n_rows], axis=0).astype(BF16))

    for c, (d, hh) in enumerate(combos):
        idx = d * MLSTM_HEADS + hh
        sl = slice(MLSTM_DH * hh, MLSTM_DH * (hh + 1))
        num = _dotf(dirs[d][2][sl, :], sw[c]) + w_inter[c] * inter[c][:MLSTM_DH, :]
        dn = den[c] + w_inter[c] * inter[c][MLSTM_DH:MLSTM_DH + 1, :]
        dirs[d][5][sl, :] = (num / jnp.maximum(jnp.abs(dn), floor[c])).astype(BF16)
        s_scr[idx] = decay[c] * s_prev[c] + _dotf(vw[c], dirs[d][1][:, sl])
        m_scr[idx:idx + 1, :] = jnp.broadcast_to(m_new[c], (1, MLSTM_DH))

    @pl.when(j == nj - 1)
    def _():
        s_out_ref[...] = s_scr[...]
        m_out_ref[...] = m_scr[...]


def _mlstm(mqt, mk, mvt, g_rows, g_cols, s0, m0, batch, chunk):
    n = mk.shape[0]
    nc = n // batch // chunk
    fwd = lambda b, j: (b * nc + j, 0)
    bwd = lambda b, j: (b * nc + (nc - 1 - j), 0)
    fwd_t = lambda b, j: (0, b * nc + j)
    bwd_t = lambda b, j: (0, b * nc + (nc - 1 - j))
    nst = 2 * MLSTM_HEADS
    tok = lambda fn: pl.BlockSpec((chunk, MLSTM_WIDTH), fn)
    tok_t = lambda fn: pl.BlockSpec((MLSTM_WIDTH, chunk), fn)
    state_s = pl.BlockSpec((None, nst, MLSTM_ST, MLSTM_DH), lambda b, j: (b, 0, 0, 0))
    state_m = pl.BlockSpec((None, nst, MLSTM_DH), lambda b, j: (b, 0, 0))
    return pl.pallas_call(
        _mlstm_kernel,
        grid=(batch, nc),
        in_specs=[tok_t(fwd_t), tok(fwd), tok_t(fwd_t), pl.BlockSpec((chunk, N_GATES), fwd),
                  pl.BlockSpec((N_GATES, chunk), fwd_t),
                  tok_t(bwd_t), tok(bwd), tok_t(bwd_t), pl.BlockSpec((chunk, N_GATES), bwd),
                  pl.BlockSpec((N_GATES, chunk), bwd_t),
                  state_s, state_m],
        out_specs=[tok_t(fwd_t), tok_t(bwd_t), state_s, state_m],
        out_shape=[jax.ShapeDtypeStruct((MLSTM_WIDTH, n), BF16), jax.ShapeDtypeStruct((MLSTM_WIDTH, n), BF16),
                   jax.ShapeDtypeStruct(s0.shape, F32), jax.ShapeDtypeStruct(m0.shape, F32)],
        scratch_shapes=[pltpu.VMEM((nst, MLSTM_ST, MLSTM_DH), F32), pltpu.VMEM((nst, MLSTM_DH), F32)],
        compiler_params=_cparams("parallel", "arbitrary"),
        name="mlstm",
    )(mqt, mk, mvt, g_rows, g_cols, mqt, mk, mvt, g_rows, g_cols, s0, m0)


def _even_out_kernel(x_ref, gate_ref, att_ref, hf_ref, hb_ref, mo_ref, og_ref, w_ref,
                     fmod_ref, fg_ref, fwg_ref, fwu_ref, fwd_ref, o_ref):
    hs = hf_ref[...].astype(F32) + hb_ref[...].astype(F32)
    og = og_ref[...]
    parts = [att_ref[...]]
    for hh in range(MLSTM_HEADS):
        sl = slice(MLSTM_DH * hh, MLSTM_DH * (hh + 1))
        blk = hs[sl, :]
        hn = blk * lax.rsqrt(jnp.mean(blk * blk, axis=0, keepdims=True) + EPS) * og[sl, :]
        parts.append((hn * jax.nn.sigmoid(mo_ref[sl, :].astype(F32))).astype(BF16))
    mixed = jnp.concatenate(parts, axis=0)
    y = lax.dot_general(mixed, w_ref[...], (((0,), (0,)), ((), ())), preferred_element_type=F32)
    o_ref[...] = _ffn_math(x_ref[...] + gate_ref[...] * y, fmod_ref, fg_ref, fwg_ref, fwu_ref, fwd_ref)


def _even_out(x2d, gates, row_fn, att, hf, hb, mo, out_g, w_out, ffn_args, tm):
    n = x2d.shape[0]
    tok = pl.BlockSpec((tm, D_MODEL), lambda i: (i, 0))
    tok_t = pl.BlockSpec((MLSTM_WIDTH, tm), lambda i: (0, i))
    return pl.pallas_call(
        _even_out_kernel,
        grid=(n // tm,),
        in_specs=[tok, pl.BlockSpec((None, 1, D_MODEL), lambda i: (row_fn(i), 0, 0)),
                  tok_t, tok_t, tok_t, tok_t, _const_spec((MLSTM_WIDTH, 1)), _const_spec((D_MODEL, D_MODEL))]
        + _ffn_specs(row_fn, *ffn_args[0]),
        out_specs=tok,
        out_shape=jax.ShapeDtypeStruct((n, D_MODEL), F32),
        compiler_params=_cparams("parallel"),
        name="even_out_ffn",
    )(x2d, gates, att, hf, hb, mo, out_g.reshape(MLSTM_WIDTH, 1), w_out, *ffn_args[1:])


def _gelu_tanh(x):
    return 0.5 * x * (1.0 + jnp.tanh(0.7978845608028654 * (x + 0.044715 * (x * x * x))))


def _odd_in_kernel(x_ref, xp_ref, xn_ref, mod_ref, g_ref, w_ref, cw_ref, cb_ref, gate_ref, xc_ref, ext_scr, *, nt):
    tm = x_ref.shape[0]
    ti = pl.program_id(0) % nt
    g, shift, scale = g_ref[...], mod_ref[0:1, :], mod_ref[1:2, :]
    halo = jnp.concatenate([xp_ref[...], xn_ref[...]], axis=0)
    ph = _dotf(_norm_mod(halo, g, shift, scale).astype(BF16), w_ref[:, RNN_WIDTH:])
    ext_scr[0:8, :] = jnp.where(ti > 0, ph[0:8, :], 0.0)
    ext_scr[8 + tm:16 + tm, :] = jnp.where(ti < nt - 1, ph[8:16, :], 0.0)
    cw = cw_ref[...]
    cb = cb_ref[...]
    rc = tm // ODD_IN_CHUNKS

    def project(c):
        rows = slice(rc * c, rc * (c + 1))
        p = _dotf(_norm_mod(x_ref[rows, :], g, shift, scale).astype(BF16), w_ref[...])
        gate_ref[rows, :] = _gelu_tanh(p[:, :RNN_WIDTH]).astype(BF16)
        ext_scr[8 + rc * c:8 + rc * (c + 1), :] = p[:, RNN_WIDTH:]

    def conv(c):
        win = ext_scr[rc * c:rc * (c + 1) + 16, :]
        xc = cb + cw[2:3, :] * win[8:8 + rc, :]
        for k in (0, 1, 3):
            xc = xc + cw[k:k + 1, :] * pltpu.roll(win, (2 - k) % (rc + 16), axis=0)[8:8 + rc, :]
        xc_ref[rc * c:rc * (c + 1), :] = xc

    project(0)
    for c in range(1, ODD_IN_CHUNKS):
        project(c)
        conv(c - 1)
    conv(ODD_IN_CHUNKS - 1)


def _odd_in(x2d, mods, row_fn, g, w_in, conv_w, conv_b, batch, tm):
    n = x2d.shape[0]
    nt = n // batch // tm
    t8 = tm // 8
    tok = pl.BlockSpec((tm, D_MODEL), lambda i: (i, 0))
    prev = pl.BlockSpec((8, D_MODEL), lambda i: (jnp.maximum(i * t8 - 1, 0), 0))
    nxt = pl.BlockSpec((8, D_MODEL), lambda i: (jnp.minimum((i + 1) * t8, n // 8 - 1), 0))
    return pl.pallas_call(
        functools.partial(_odd_in_kernel, nt=nt),
        grid=(n // tm,),
        in_specs=[tok, prev, nxt, pl.BlockSpec((None, 3, D_MODEL), lambda i: (row_fn(i), 0, 0)),
                  _const_spec((1, D_MODEL)), _const_spec((D_MODEL, 2 * RNN_WIDTH)),
                  _const_spec((CONV_W, RNN_WIDTH)), _const_spec((1, RNN_WIDTH))],
        out_specs=[tok, tok],
        out_shape=[jax.ShapeDtypeStruct((n, RNN_WIDTH), BF16), jax.ShapeDtypeStruct((n, RNN_WIDTH), F32)],
        scratch_shapes=[pltpu.VMEM((tm + 16, RNN_WIDTH), F32)],
        compiler_params=_cparams("parallel"),
        name="odd_in",
    )(x2d, x2d, x2d, mods, g.reshape(1, D_MODEL), w_in, conv_w, conv_b)


def _rglru_kernel(xc_ref, wax_ref, ba_ref, bx_ref, lam_ref, h0_ref, h_ref, ht_ref, a_scr, b_scr, carry_scr, *, nt):
    d = pl.program_id(0)
    i = pl.program_id(2)
    tm = xc_ref.shape[0]
    nb8 = tm // 8

    xc = xc_ref[...]
    neg_lam = -lam_ref[...]
    half_rate = (-0.5 * LRU_C) * (jnp.maximum(neg_lam, 0.0) + jnp.log1p(jnp.exp(-jnp.abs(neg_lam))))
    ba = ba_ref[...]
    bx = bx_ref[...]
    for nb in range(RNN_BLOCKS):
        sl = slice(RNN_BLOCK_DIM * nb, RNN_BLOCK_DIM * (nb + 1))
        u = xc[:, sl]
        rg = _dotf(u.astype(BF16), wax_ref[nb])
        log_a = jnp.tanh(rg[:, :RNN_BLOCK_DIM] + ba[:, sl]) * half_rate[:, sl] + half_rate[:, sl]
        ig = 0.5 * jnp.tanh(rg[:, RNN_BLOCK_DIM:] + bx[:, sl]) + 0.5
        a = jnp.exp(log_a)
        a_scr[:, sl] = a
        b_scr[:, sl] = jnp.sqrt(-jnp.tanh(log_a) * (a * a + 1.0)) * (ig * u)

    @pl.when(i == 0)
    def _():
        carry_scr[...] = h0_ref[...]

    def scan_tile(reverse):
        a = a_scr[...].reshape(nb8, 8, RNN_WIDTH)
        b = b_scr[...].reshape(nb8, 8, RNN_WIDTH)
        r8 = lax.broadcasted_iota(jnp.int32, (nb8, 8, RNN_WIDTH), 1)
        for s in (1, 2, 4):
            keep = (r8 < 8 - s) if reverse else (r8 >= s)
            shift = 8 - s if reverse else s
            a_in = jnp.where(keep, pltpu.roll(a, shift, axis=1), 1.0)
            b_in = jnp.where(keep, pltpu.roll(b, shift, axis=1), 0.0)
            b = a * b_in + b
            a = a * a_in
        a_scr[...] = a.reshape(tm, RNN_WIDTH)
        b_scr[...] = b.reshape(tm, RNN_WIDTH)

        def block_pair(jp, h):
            base = pl.multiple_of((nb8 // 2 - 1 - jp if reverse else jp) * 16, 16)
            halves = [None, None]
            for half in ((1, 0) if reverse else (0, 1)):
                rows = pl.ds(base + 8 * half, 8)
                hb = a_scr[rows, :] * h + b_scr[rows, :]
                halves[half] = hb
                h = hb[0:1, :] if reverse else hb[7:8, :]
            h_ref[pl.ds(base, 16), :] = jnp.concatenate(halves, axis=0).astype(h_ref.dtype)
            return h

        carry_scr[...] = lax.fori_loop(0, nb8 // 2, block_pair, carry_scr[...], unroll=2)

    @pl.when(d == 0)
    def _():
        scan_tile(False)

    @pl.when(d == 1)
    def _():
        scan_tile(True)

    @pl.when(i == nt - 1)
    def _():
        ht_ref[...] = carry_scr[...]


def _rglru(xc2d, h0, w, batch, tm):
    n = xc2d.shape[0]
    nt = n // batch // tm
    tile = lambda d, i: jnp.where(d == 0, i, nt - 1 - i)
    per_dir = lambda shape: pl.BlockSpec((None,) + shape, lambda d, b, i: (d,) + (0,) * len(shape))
    state = pl.BlockSpec((None, None, 1, RNN_WIDTH), lambda d, b, i: (d, b, 0, 0))
    return pl.pallas_call(
        functools.partial(_rglru_kernel, nt=nt),
        grid=(2, batch, nt),
        in_specs=[pl.BlockSpec((tm, RNN_WIDTH), lambda d, b, i: (b * nt + tile(d, i), 0)),
                  per_dir((RNN_BLOCKS, RNN_BLOCK_DIM, 2 * RNN_BLOCK_DIM)), per_dir((1, RNN_WIDTH)), per_dir((1, RNN_WIDTH)),
                  per_dir((1, RNN_WIDTH)), state],
        out_specs=[pl.BlockSpec((None, tm, RNN_WIDTH), lambda d, b, i: (d, b * nt + tile(d, i), 0)), state],
        out_shape=[jax.ShapeDtypeStruct((2, n, RNN_WIDTH), BF16), jax.ShapeDtypeStruct((2, batch, 1, RNN_WIDTH), F32)],
        scratch_shapes=[pltpu.VMEM((tm, RNN_WIDTH), F32), pltpu.VMEM((tm, RNN_WIDTH), F32),
                        pltpu.VMEM((1, RNN_WIDTH), F32)],
        compiler_params=_cparams("parallel", "parallel", "arbitrary"),
        name="rglru",
    )(xc2d, w["w_ax"], w["b_a"], w["b_x"], w["lam"], h0)


def _odd_out_kernel(x_ref, gate_ref, h_ref, gg_ref, w_ref, fmod_ref, fg_ref, fwg_ref, fwu_ref, fwd_ref, o_ref):
    y = ((h_ref[0].astype(F32) + h_ref[1].astype(F32)) * gg_ref[...].astype(F32)).astype(BF16)
    x_mid = x_ref[...] + gate_ref[...] * _dotf(y, w_ref[...])
    o_ref[...] = _ffn_math(x_mid, fmod_ref, fg_ref, fwg_ref, fwu_ref, fwd_ref)


def _odd_out(x2d, gates, row_fn, h2, gate_branch, w_out, ffn_args, tm):
    n = x2d.shape[0]
    tok = pl.BlockSpec((tm, D_MODEL), lambda i: (i, 0))
    return pl.pallas_call(
        _odd_out_kernel,
        grid=(n // tm,),
        in_specs=[tok, pl.BlockSpec((None, 1, D_MODEL), lambda i: (row_fn(i), 0, 0)),
                  pl.BlockSpec((2, tm, RNN_WIDTH), lambda i: (0, i, 0)), tok, _const_spec((RNN_WIDTH, D_MODEL))]
        + _ffn_specs(row_fn, *ffn_args[0]),
        out_specs=tok,
        out_shape=jax.ShapeDtypeStruct((n, D_MODEL), F32),
        compiler_params=_cparams("parallel"),
        name="odd_out_ffn",
    )(x2d, gates, h2, gate_branch, w_out, *ffn_args[1:])


def _rope_tables(seq):
    t = np.arange(seq)
    inv = np.float32(ROPE_BASE) ** (-np.arange(0, ROPE_AXIS_DIM, 2, dtype=np.float32) / np.float32(ROPE_AXIS_DIM))
    ang_r = (t // GRID_W).astype(np.float32)[:, None] * inv
    ang_c = (t % GRID_W).astype(np.float32)[:, None] * inv
    cos32 = np.concatenate([np.cos(ang_r), np.cos(ang_r), np.cos(ang_c), np.cos(ang_c)], axis=1)
    sin32 = np.concatenate([-np.sin(ang_r), np.sin(ang_r), -np.sin(ang_c), np.sin(ang_c)], axis=1)
    pad = HEAD_BLOCK - MLA_QK
    cos = np.concatenate([np.ones((seq, MLA_NOPE), np.float32), cos32, np.ones((seq, pad), np.float32)], axis=1)
    sin = np.concatenate([np.zeros((seq, MLA_NOPE), np.float32), sin32, np.zeros((seq, pad), np.float32)], axis=1)
    cost = np.concatenate([np.cos(ang_r), np.cos(ang_c)], axis=1).T
    sint = np.concatenate([np.sin(ang_r), np.sin(ang_c)], axis=1).T
    return tuple(jnp.asarray(a, F32) for a in (cos, sin, cost, sint))


def _even_weights(w_in, cq_g, w_uq, ckv_g, w_ukv, q_g, k_g, gate_b):
    pad = HEAD_BLOCK - MLA_QK
    w_in, w_uq, w_ukv = w_in.astype(BF16), w_uq.astype(BF16), w_ukv.astype(BF16)
    kr_cols = w_in[:, OFF_KR:OFF_KR + MLA_ROPE]
    w_kr = jnp.pad(kr_cols, ((0, 0), (MLA_NOPE, pad)))
    m_cols = w_in[:, OFF_KR + MLA_ROPE:OFF_KR + MLA_ROPE + 4 * MLSTM_WIDTH]
    g_cols = w_in[:, OFF_KR + MLA_ROPE + 4 * MLSTM_WIDTH:]
    mq_cols, mk_cols, mv_cols, mo_cols = jnp.split(m_cols, 4, axis=1)
    w_all = jnp.concatenate([w_in[:, :OFF_KR], w_kr, mk_cols, jnp.pad(g_cols, ((0, 0), (0, HEAD_BLOCK - N_GATES)))], axis=1)
    uq = jnp.pad(w_uq.reshape(MLA_Q_RANK, MLA_HEADS, MLA_QK), ((0, 0), (0, 0), (0, pad)))
    ukv = w_ukv.reshape(MLA_KV_RANK, MLA_HEADS, MLA_NOPE + MLA_V)
    uk = jnp.pad(ukv[:, :, :MLA_NOPE], ((0, 0), (0, 0), (0, HEAD_BLOCK - MLA_NOPE)))
    uv = ukv[:, :, MLA_NOPE:]
    kg_blk = jnp.pad(k_g[:MLA_NOPE], (0, HEAD_BLOCK - MLA_NOPE))
    krg_blk = jnp.pad(k_g[MLA_NOPE:], (MLA_NOPE, pad))
    return {
        "w_all": w_all,
        "w_gt": jnp.concatenate([g_cols, mq_cols, mv_cols, mo_cols], axis=1).T,
        "gb_row": gate_b.reshape(1, N_GATES),
        "gb_col": gate_b.reshape(N_GATES, 1),
        "cq_g": cq_g.reshape(1, MLA_Q_RANK),
        "w_uqt": uq.reshape(MLA_Q_RANK, MLA_HEADS * HEAD_BLOCK).T,
        "ckv_g": ckv_g.reshape(1, MLA_KV_RANK),
        "w_uk": uk.reshape(MLA_KV_RANK, MLA_HEADS * HEAD_BLOCK),
        "w_uvt": uv.reshape(MLA_KV_RANK, MLA_HEADS * MLA_V).T,
        "qg_nope": q_g[:MLA_NOPE].reshape(MLA_NOPE, 1),
        "qg_rope": q_g[MLA_NOPE:].reshape(MLA_ROPE, 1),
        "k_g": jnp.tile(kg_blk, MLA_HEADS).reshape(1, -1),
        "kr_g": krg_blk.reshape(1, HEAD_BLOCK),
    }


def kernel(x, c, ctx, c_ctx, mod_w, mod_b, norm_g, ffn_w_gate, ffn_w_up, ffn_w_down, even_w_in, even_w_out, mla_cq_g, mla_w_uq, mla_ckv_g, mla_w_ukv, mla_q_g, mla_k_g, mlstm_gate_b, mlstm_out_g, odd_w_in, odd_conv_w, odd_conv_b, lru_w_a, lru_b_a, lru_w_x, lru_b_x, lru_lam, odd_w_out):
    batch, seq, _ = x.shape
    ctx_len = ctx.shape[1]
    depth = mod_w.shape[0]
    assert depth == 2 and batch <= 7 and ctx_len == ROWS_C

    cond8 = jnp.zeros((8, D_MODEL), F32).at[:batch].set(c).at[batch].set(c_ctx)
    mods = _ada_params(cond8, mod_w, mod_b)
    mods = mods.reshape(depth, 8, 3, 3, D_MODEL).transpose(0, 2, 1, 3, 4)

    x_row = lambda tm: (lambda i: i // (seq // tm))
    c_row = lambda i: batch
    x2 = x.reshape(batch * seq, D_MODEL)
    c2 = ctx.reshape(batch * ctx_len, D_MODEL)

    wg = ffn_w_gate.astype(BF16)
    wu = ffn_w_up.astype(BF16)
    wd = ffn_w_down.astype(BF16)

    def ffn_args(layer, which):
        return ((layer, which), mods[layer, 2 * which], norm_g[layer, 2 * which].reshape(1, D_MODEL), wg, wu, wd)

    x2 = _ffn(x2, ffn_args(0, 0), x_row(ROWS_X), ROWS_X)
    c2 = _ffn(c2, ffn_args(0, 0), c_row, ROWS_C)

    ew = _even_weights(even_w_in[0], mla_cq_g[0], mla_w_uq[0], mla_ckv_g[0], mla_w_ukv[0], mla_q_g[0], mla_k_g[0],
                       mlstm_gate_b[0])
    tabs_x = _rope_tables(seq)
    tabs_c = (jnp.ones((ROWS_C, HEAD_BLOCK), F32), jnp.zeros((ROWS_C, HEAD_BLOCK), F32),
              jnp.ones((ROPE_AXIS_DIM, ROWS_C), F32), jnp.zeros((ROPE_AXIS_DIM, ROWS_C), F32))
    key_len = seq + ctx_len
    px = _even_prep(x2, mods[0, 1], x_row(ROWS_X), norm_g[0, 1], tabs_x, lambda i: i % (seq // ROWS_X), ew, ROWS_X,
                    batch, key_len, 0)
    qt_x, k_all, vt_all, mqt_x, mk_x, mvt_x, mot_x, gr_x, gc_x = px
    pc = _even_prep(c2, mods[0, 1], c_row, norm_g[0, 1], tabs_c, lambda i: 0, ew, ROWS_C,
                    batch, key_len, seq, kv_bufs=(k_all, vt_all))
    qt_c, k_all, vt_all, mqt_c, mk_c, mvt_c, mot_c, gr_c, gc_c = pc

    att_x = _attention(qt_x, k_all, vt_all, batch, ATTN_QUERIES_X, key_len, 0)
    att_c = _attention(qt_c, k_all, vt_all, batch, ROWS_C, ctx_len, seq)

    nst = 2 * MLSTM_HEADS
    s0 = jnp.zeros((batch, nst, MLSTM_ST, MLSTM_DH), F32)
    m0 = jnp.zeros((batch, nst, MLSTM_DH), F32)
    hcf, hcb, s_c, m_c = _mlstm(mqt_c, mk_c, mvt_c, gr_c, gc_c, s0, m0, batch, MLSTM_CHUNK)
    hxf, hxb, _, _ = _mlstm(mqt_x, mk_x, mvt_x, gr_x, gc_x, s_c, m_c, batch, MLSTM_CHUNK)

    w_out0 = even_w_out[0].astype(BF16)
    gate0 = mods[0, 1][:, 2:3, :]
    x2 = _even_out(x2, gate0, x_row(ROWS_X), att_x, hxf, hxb, mot_x, mlstm_out_g[0], w_out0, ffn_args(0, 1), ROWS_X)
    c2 = _even_out(c2, gate0, c_row, att_c, hcf, hcb, mot_c, mlstm_out_g[0], w_out0, ffn_args(0, 1), ROWS_C)

    x2 = _ffn(x2, ffn_args(1, 0), x_row(ROWS_X_FFN), ROWS_X_FFN)
    c2 = _ffn(c2, ffn_args(1, 0), c_row, ROWS_C)
    w_in1 = odd_w_in[0].astype(BF16)
    conv_b = odd_conv_b[0].reshape(1, RNN_WIDTH)
    gg_x, xc_x = _odd_in(x2, mods[1, 1], x_row(ROWS_X), norm_g[1, 1], w_in1, odd_conv_w[0], conv_b, batch, ROWS_X)
    _, xc_c = _odd_in(c2, mods[1, 1], c_row, norm_g[1, 1], w_in1, odd_conv_w[0], conv_b, batch, ROWS_C)
    rw = {
        "w_ax": (0.5 * jnp.concatenate([lru_w_a[0], lru_w_x[0]], axis=-1)).astype(BF16),
        "b_a": 0.5 * lru_b_a[0].reshape(2, 1, RNN_WIDTH),
        "b_x": 0.5 * lru_b_x[0].reshape(2, 1, RNN_WIDTH),
        "lam": lru_lam[0].reshape(2, 1, RNN_WIDTH),
    }
    h0 = jnp.zeros((2, batch, 1, RNN_WIDTH), F32)
    _, st_c = _rglru(xc_c, h0, rw, batch, RGLRU_ROWS)
    h_x, _ = _rglru(xc_x, st_c, rw, batch, RGLRU_ROWS)
    x2 = _odd_out(x2, mods[1, 1][:, 2:3, :], x_row(ROWS_X), h_x, gg_x, odd_w_out[0].astype(BF16), ffn_args(1, 1),
                  ROWS_X)
    return x2.reshape(batch, seq, D_MODEL)
```

```python
import functools

import jax
import jax.numpy as jnp
import numpy as np
from jax import lax
from jax.experimental import pallas as pl
from jax.experimental.pallas import tpu as pltpu

F32 = jnp.float32
BF16 = jnp.bfloat16

D_MODEL = 1024
GRID_W = 64
EPS = 1e-6
D_FF = 2816
FFN_RESIDUAL = 0.5

MLA_HEADS = 8
MLA_Q_RANK = 384
MLA_KV_RANK = 256
MLA_NOPE = 64
MLA_ROPE = 32
MLA_V = 64
MLA_QK = MLA_NOPE + MLA_ROPE
ROPE_AXIS_DIM = MLA_ROPE // 2
ROPE_BASE = 10000.0
HEAD_BLOCK = 128
V_ROWS = MLA_V + 16

MLSTM_HEADS = 4
MLSTM_DH = 128
MLSTM_WIDTH = MLSTM_HEADS * MLSTM_DH
N_GATES = 4 * MLSTM_HEADS
MLSTM_ST = MLSTM_DH + 16

RNN_WIDTH = 1024
RNN_BLOCKS = 8
RNN_BLOCK_DIM = RNN_WIDTH // RNN_BLOCKS
CONV_W = 4
LRU_C = 8.0

NEG_BIG = -1e30
LOG2_E = 1.4426950408889634
VMEM_LIMIT = 56 * 1024 * 1024

ADA_COLS = 1152
ROWS_X = 512
ROWS_X_FFN = 1024
ROWS_C = 256
FFN_CHUNK_ROWS = 128
ODD_IN_CHUNKS = 4
KEY_CHUNK = 256
ATTN_HEADS_PER_STEP = 2
ATTN_QUERIES_X = 512
MLSTM_CHUNK = 256
RGLRU_ROWS = 512


def _cparams(*sem):
    return pltpu.CompilerParams(dimension_semantics=sem, vmem_limit_bytes=VMEM_LIMIT)


def _const_spec(shape):
    zeros = (0,) * len(shape)
    return pl.BlockSpec(shape, lambda *_: zeros, pipeline_mode=pl.Buffered(1))


def _dotf(a, b):
    return jnp.dot(a, b, preferred_element_type=F32)


def _split3(a):
    a1 = a.astype(BF16)
    r1 = a - a1.astype(F32)
    a2 = r1.astype(BF16)
    a3 = (r1 - a2.astype(F32)).astype(BF16)
    return a1, a2, a3


def _norm_mod(x, g, shift, scale):
    ms = jnp.mean(x * x, axis=-1, keepdims=True)
    return (x * lax.rsqrt(ms + EPS)) * (g * (1.0 + scale)) + shift


def _rms(x, g):
    ms = jnp.mean(x * x, axis=-1, keepdims=True)
    return x * lax.rsqrt(ms + EPS) * g


def _log_sigmoid(x):
    return jnp.minimum(x, 0.0) - jnp.log1p(jnp.exp(-jnp.abs(x)))


def _ada_kernel(cond_ref, w_ref, b_ref, o_ref):
    c = cond_ref[...]
    s = c * jax.nn.sigmoid(c)
    s1, s2, _ = _split3(s)
    w = w_ref[...]
    w1 = w.astype(BF16)
    w2 = (w - w1.astype(F32)).astype(BF16)
    o_ref[...] = _dotf(s1, w1) + _dotf(s1, w2) + _dotf(s2, w1) + b_ref[...]


def _ada_params(cond8, mod_w, mod_b):
    depth, _, n = mod_w.shape
    tn = ADA_COLS
    return pl.pallas_call(
        _ada_kernel,
        grid=(depth, n // tn),
        in_specs=[
            pl.BlockSpec((8, D_MODEL), lambda l, j: (0, 0)),
            pl.BlockSpec((None, D_MODEL, tn), lambda l, j: (l, 0, j)),
            pl.BlockSpec((None, 1, tn), lambda l, j: (l, 0, j)),
        ],
        out_specs=pl.BlockSpec((None, 8, tn), lambda l, j: (l, 0, j)),
        out_shape=jax.ShapeDtypeStruct((depth, 8, n), F32),
        compiler_params=_cparams("parallel", "parallel"),
        name="ada_params",
    )(cond8, mod_w, mod_b.reshape(depth, 1, n))


def _ffn_math(x, mod_ref, g_ref, wg_ref, wu_ref, wd_ref):
    outs = []
    rc = min(FFN_CHUNK_ROWS, x.shape[0])
    for c in range(x.shape[0] // rc):
        xc = x[rc * c:rc * (c + 1), :]
        h = _norm_mod(xc, g_ref[...], mod_ref[0:1, :], mod_ref[1:2, :]).astype(BF16)
        g = _dotf(h, wg_ref[...])
        u = _dotf(h, wu_ref[...])
        a = (g * jax.nn.sigmoid(g) * u).astype(BF16)
        outs.append(xc + (FFN_RESIDUAL * mod_ref[2:3, :]) * _dotf(a, wd_ref[...]))
    return jnp.concatenate(outs, axis=0)


def _ffn_specs(row_fn, layer, which):
    weight = lambda r, c: pl.BlockSpec((None, None, r, c), lambda i: (layer, which, 0, 0), pipeline_mode=pl.Buffered(1))
    return [pl.BlockSpec((None, 3, D_MODEL), lambda i: (row_fn(i), 0, 0)), _const_spec((1, D_MODEL)),
            weight(D_MODEL, D_FF), weight(D_MODEL, D_FF), weight(D_FF, D_MODEL)]


def _ffn_kernel(x_ref, mod_ref, g_ref, wg_ref, wu_ref, wd_ref, o_ref):
    o_ref[...] = _ffn_math(x_ref[...], mod_ref, g_ref, wg_ref, wu_ref, wd_ref)


def _ffn(x2d, ffn_args, row_fn, tm):
    n = x2d.shape[0]
    tok = pl.BlockSpec((tm, D_MODEL), lambda i: (i, 0))
    return pl.pallas_call(
        _ffn_kernel,
        grid=(n // tm,),
        in_specs=[tok] + _ffn_specs(row_fn, *ffn_args[0]),
        out_specs=tok,
        out_shape=jax.ShapeDtypeStruct((n, D_MODEL), F32),
        compiler_params=_cparams("parallel"),
        name="ffn",
    )(x2d, *ffn_args[1:])


W_ALL = MLA_Q_RANK + MLA_KV_RANK + HEAD_BLOCK + MLSTM_WIDTH + HEAD_BLOCK
OFF_CKV = MLA_Q_RANK
OFF_KR = OFF_CKV + MLA_KV_RANK
OFF_MK = OFF_KR + HEAD_BLOCK
OFF_GATES = OFF_MK + MLSTM_WIDTH
W_T_ROWS = N_GATES + 3 * MLSTM_WIDTH


def _rope_block(xb, cos, sin, first_half):
    partner = jnp.where(first_half, pltpu.roll(xb, HEAD_BLOCK - 8, axis=1), pltpu.roll(xb, 8, axis=1))
    return xb * cos + partner * sin


def _even_prep_kernel(x_ref, mod_ref, g_ref, cos_ref, sin_ref, cost_ref, sint_ref, wall_ref, wgt_ref, gbr_ref, gbc_ref,
                      cqg_ref, wuqt_ref, ckvg_ref, wuk_ref, wuvt_ref, qgn_ref, qgr_ref, kg_ref, krg_ref,
                      qt_ref, k_ref, vt_ref, mqt_ref, mk_ref, mvt_ref, mot_ref, gr_ref, gc_ref):
    tm = x_ref.shape[0]
    x = x_ref[...]
    h = _norm_mod(x, g_ref[...], mod_ref[0:1, :], mod_ref[1:2, :]).astype(BF16)
    p = _dotf(h, wall_ref[...])
    pt = lax.dot_general(wgt_ref[...], h, (((1,), (1,)), ((), ())), preferred_element_type=F32)

    mk_ref[...] = (p[:, OFF_MK:OFF_MK + MLSTM_WIDTH] * (MLSTM_DH ** -0.5)).astype(BF16)
    mqt_ref[...] = pt[N_GATES:N_GATES + MLSTM_WIDTH, :].astype(BF16)
    mvt_ref[...] = pt[N_GATES + MLSTM_WIDTH:N_GATES + 2 * MLSTM_WIDTH, :].astype(BF16)
    mot_ref[...] = pt[N_GATES + 2 * MLSTM_WIDTH:, :].astype(BF16)
    graw = p[:, OFF_GATES:OFF_GATES + HEAD_BLOCK][:, :N_GATES] + gbr_ref[...]
    lane = lax.broadcasted_iota(jnp.int32, (tm, N_GATES), 1)
    gr_ref[...] = jnp.where((lane // MLSTM_HEADS) % 2 == 1, _log_sigmoid(graw), graw)
    gt = pt[:N_GATES, :] + gbc_ref[...]
    sub = lax.broadcasted_iota(jnp.int32, (N_GATES, tm), 0)
    gc_ref[...] = jnp.where((sub // MLSTM_HEADS) % 2 == 1, _log_sigmoid(gt), gt)

    cos = cos_ref[...]
    sin = sin_ref[...]
    lane_b = lax.broadcasted_iota(jnp.int32, (tm, HEAD_BLOCK), 1)
    first_half = (lane_b % ROPE_AXIS_DIM) < (ROPE_AXIS_DIM // 2)

    cqn = _rms(p[:, 0:MLA_Q_RANK], cqg_ref[...]).astype(BF16)
    ckvn = _rms(p[:, OFF_CKV:OFF_CKV + MLA_KV_RANK], ckvg_ref[...]).astype(BF16)
    nt_dims = (((1,), (1,)), ((), ()))
    k_raw = _dotf(ckvn, wuk_ref[...])
    vt = lax.dot_general(wuvt_ref[...], ckvn, nt_dims, preferred_element_type=F32).astype(BF16)
    ones_rows = jnp.where(lax.broadcasted_iota(jnp.int32, (V_ROWS - MLA_V, KEY_CHUNK), 0) == 0, 1.0, 0.0).astype(BF16)
    for cc in range(tm // KEY_CHUNK):
        for hh in range(MLA_HEADS):
            vt_ref[cc, V_ROWS * hh:V_ROWS * hh + MLA_V, :] = vt[MLA_V * hh:MLA_V * (hh + 1),
                                                                KEY_CHUNK * cc:KEY_CHUNK * (cc + 1)]
            vt_ref[cc, V_ROWS * hh + MLA_V:V_ROWS * (hh + 1), :] = ones_rows

    kr = p[:, OFF_KR:OFF_KR + HEAD_BLOCK]
    kr_ms = jnp.sum(kr * kr, axis=-1, keepdims=True) * (1.0 / MLA_ROPE)
    kr_rot = _rope_block(kr * lax.rsqrt(kr_ms + EPS) * krg_ref[...], cos, sin, first_half)
    kg = kg_ref[...]
    for hh in range(MLA_HEADS):
        sl = slice(HEAD_BLOCK * hh, HEAD_BLOCK * (hh + 1))
        kb = k_raw[:, sl]
        kb_ms = jnp.sum(kb * kb, axis=-1, keepdims=True) * (1.0 / MLA_NOPE)
        k_ref[:, sl] = (kb * lax.rsqrt(kb_ms + EPS) * kg[:, sl] + kr_rot).astype(BF16)

    qt_raw = lax.dot_general(wuqt_ref[...], cqn, nt_dims, preferred_element_type=F32)
    q_scale = (MLA_QK ** -0.5) * LOG2_E
    half = ROPE_AXIS_DIM // 2
    cost = cost_ref[...]
    sint = sint_ref[...]
    for hh in range(MLA_HEADS):
        r0 = HEAD_BLOCK * hh
        nope = qt_raw[r0:r0 + MLA_NOPE, :]
        rope = qt_raw[r0 + MLA_NOPE:r0 + MLA_QK, :]
        nope = nope * lax.rsqrt(jnp.mean(nope * nope, axis=0, keepdims=True) + EPS) * qgn_ref[...]
        rope = rope * lax.rsqrt(jnp.mean(rope * rope, axis=0, keepdims=True) + EPS) * qgr_ref[...]
        rot = []
        for ax in range(2):
            x1 = rope[ROPE_AXIS_DIM * ax:ROPE_AXIS_DIM * ax + half, :]
            x2 = rope[ROPE_AXIS_DIM * ax + half:ROPE_AXIS_DIM * (ax + 1), :]
            cs = cost[half * ax:half * (ax + 1), :]
            sn = sint[half * ax:half * (ax + 1), :]
            rot += [x1 * cs - x2 * sn, x2 * cs + x1 * sn]
        qt_ref[r0:r0 + MLA_NOPE, :] = (nope * q_scale).astype(BF16)
        qt_ref[r0 + MLA_NOPE:r0 + MLA_QK, :] = (jnp.concatenate(rot, axis=0) * q_scale).astype(BF16)
        qt_ref[r0 + MLA_QK:r0 + HEAD_BLOCK, :] = jnp.zeros((HEAD_BLOCK - MLA_QK, tm), BF16)


def _even_prep_into_kernel(*refs):
    n_in = 20
    _even_prep_kernel(*refs[:n_in], *refs[n_in + 2:])


def _even_prep(x2d, mods, row_fn, g, tabs, tab_fn, w, tm, batch, key_len, key_off, kv_bufs=None):
    n = x2d.shape[0]
    nt = n // batch // tm
    assert key_off % tm == 0 and tm % KEY_CHUNK == 0
    cos, sin, cost, sint = tabs
    tok = lambda width: pl.BlockSpec((tm, width), lambda i: (i, 0))
    tok_t = lambda rows: pl.BlockSpec((rows, tm), lambda i: (0, i))
    k_spec = pl.BlockSpec((None, tm, MLA_HEADS * HEAD_BLOCK), lambda i: (i // nt, key_off // tm + i % nt, 0))
    vt_spec = pl.BlockSpec((None, tm // KEY_CHUNK, MLA_HEADS * V_ROWS, KEY_CHUNK),
                           lambda i: (i // nt, key_off // tm + i % nt, 0, 0))
    out_shape = [
        jax.ShapeDtypeStruct((MLA_HEADS * HEAD_BLOCK, n), BF16),
        jax.ShapeDtypeStruct((batch, key_len, MLA_HEADS * HEAD_BLOCK), BF16),
        jax.ShapeDtypeStruct((batch, key_len // KEY_CHUNK, MLA_HEADS * V_ROWS, KEY_CHUNK), BF16),
        jax.ShapeDtypeStruct((MLSTM_WIDTH, n), BF16),
        jax.ShapeDtypeStruct((n, MLSTM_WIDTH), BF16),
        jax.ShapeDtypeStruct((MLSTM_WIDTH, n), BF16),
        jax.ShapeDtypeStruct((MLSTM_WIDTH, n), BF16),
        jax.ShapeDtypeStruct((n, N_GATES), F32),
        jax.ShapeDtypeStruct((N_GATES, n), F32),
    ]
    out_specs = [tok_t(1024), k_spec, vt_spec, tok_t(512), tok(512), tok_t(512), tok_t(512),
                 tok(N_GATES), tok_t(N_GATES)]
    in_specs = [
        tok(D_MODEL),
        pl.BlockSpec((None, 3, D_MODEL), lambda i: (row_fn(i), 0, 0)),
        _const_spec((1, D_MODEL)),
        pl.BlockSpec((tm, HEAD_BLOCK), lambda i: (tab_fn(i), 0)),
        pl.BlockSpec((tm, HEAD_BLOCK), lambda i: (tab_fn(i), 0)),
        pl.BlockSpec((ROPE_AXIS_DIM, tm), lambda i: (0, tab_fn(i))),
        pl.BlockSpec((ROPE_AXIS_DIM, tm), lambda i: (0, tab_fn(i))),
        _const_spec((D_MODEL, W_ALL)),
        _const_spec((W_T_ROWS, D_MODEL)),
        _const_spec((1, N_GATES)),
        _const_spec((N_GATES, 1)),
        _const_spec((1, MLA_Q_RANK)),
        _const_spec((1024, MLA_Q_RANK)),
        _const_spec((1, MLA_KV_RANK)),
        _const_spec((MLA_KV_RANK, 1024)),
        _const_spec((512, MLA_KV_RANK)),
        _const_spec((MLA_NOPE, 1)),
        _const_spec((MLA_ROPE, 1)),
        _const_spec((1, 1024)),
        _const_spec((1, HEAD_BLOCK)),
    ]
    args = [x2d, mods, g.reshape(1, D_MODEL), cos, sin, cost, sint, w["w_all"], w["w_gt"], w["gb_row"], w["gb_col"],
            w["cq_g"], w["w_uqt"], w["ckv_g"], w["w_uk"], w["w_uvt"], w["qg_nope"], w["qg_rope"], w["k_g"], w["kr_g"]]
    aliases = {}
    if kv_bufs is not None:
        aliases = {len(args): 1, len(args) + 1: 2}
        in_specs = in_specs + [pl.BlockSpec(memory_space=pl.ANY)] * 2
        args = args + list(kv_bufs)
    return pl.pallas_call(
        _even_prep_kernel if kv_bufs is None else _even_prep_into_kernel,
        grid=(n // tm,),
        in_specs=in_specs,
        out_specs=out_specs,
        out_shape=out_shape,
        input_output_aliases=aliases,
        compiler_params=_cparams("parallel"),
        name="even_prep",
    )(*args)


def _attn_kernel(qt_ref, k_ref, vt_ref, o_ref, st_scr, acc_scr):
    tq = qt_ref.shape[1]
    nk, _, tk = vt_ref.shape
    unroll = max(u for u in (16, 8, 4, 2) if (nk - 1) % u == 0)
    heads = range(acc_scr.shape[0])

    def scores(j, slot):
        off = pl.multiple_of(j * tk, tk)
        cms = []
        for hh in heads:
            sl = slice(HEAD_BLOCK * hh, HEAD_BLOCK * (hh + 1))
            st = _dotf(k_ref[pl.ds(off, tk), sl], qt_ref[sl, :])
            st_scr[slot, hh] = st
            cms.append(jnp.max(st, axis=0, keepdims=True))
        return tuple(cms)

    def absorb(j, slot, cms, ms):
        out = []
        for hh in heads:
            m_new = jnp.maximum(ms[hh], cms[hh])
            alpha = jnp.exp2(ms[hh] - m_new)
            pt = jnp.exp2(st_scr[slot, hh] - m_new)
            pv = _dotf(vt_ref[j, V_ROWS * hh:V_ROWS * (hh + 1), :], pt.astype(BF16))
            acc_scr[hh] = alpha * acc_scr[hh] + pv
            out.append(m_new)
        return tuple(out)

    def body(i, carry):
        cms, ms = carry
        for r in range(unroll):
            j = unroll * i + r
            cms_next = scores(j + 1, (r + 1) % 2)
            ms = absorb(j, r % 2, cms, ms)
            cms = cms_next
        return cms, ms

    acc_scr[...] = jnp.zeros(acc_scr.shape, F32)
    m0 = jnp.full((1, tq), NEG_BIG, F32)
    cms, ms = lax.fori_loop(0, (nk - 1) // unroll, body, (scores(0, 0), (m0,) * len(heads)))
    absorb(nk - 1, 0, cms, ms)
    for hh in heads:
        acc = acc_scr[hh]
        o_ref[MLA_V * hh:MLA_V * (hh + 1), :] = (acc[:MLA_V, :] / acc[MLA_V:MLA_V + 1, :]).astype(o_ref.dtype)


def _attention(qt, k3, vt4, batch, tq, key_len, key_off):
    nq = qt.shape[1] // batch // tq
    assert key_off % key_len == 0 and key_len % KEY_CHUNK == 0
    kb = key_off // key_len
    nk = key_len // KEY_CHUNK
    tk = KEY_CHUNK
    g = ATTN_HEADS_PER_STEP
    return pl.pallas_call(
        _attn_kernel,
        grid=(batch, MLA_HEADS // g, nq),
        in_specs=[
            pl.BlockSpec((g * HEAD_BLOCK, tq), lambda b, hp, i: (hp, b * nq + i)),
            pl.BlockSpec((None, key_len, g * HEAD_BLOCK), lambda b, hp, i: (b, kb, hp)),
            pl.BlockSpec((None, nk, g * V_ROWS, tk), lambda b, hp, i: (b, kb, hp, 0)),
        ],
        out_specs=pl.BlockSpec((g * MLA_V, tq), lambda b, hp, i: (hp, b * nq + i)),
        out_shape=jax.ShapeDtypeStruct((MLA_HEADS * MLA_V, qt.shape[1]), BF16),
        scratch_shapes=[pltpu.VMEM((2, g, tk, tq), F32), pltpu.VMEM((g, V_ROWS, tq), F32)],
        compiler_params=_cparams("parallel", "parallel", "arbitrary"),
        name="mla_attention",
    )(qt, k3, vt4)


def _mlstm_kernel(qtf_ref, kf_ref, vtf_ref, grf_ref, gcf_ref, qtb_ref, kb_ref, vtb_ref, grb_ref, gcb_ref,
                  s0_ref, m0_ref, hf_ref, hb_ref, s_out_ref, m_out_ref, s_scr, m_scr):
    j = pl.program_id(1)
    nj = pl.num_programs(1)
    L = kf_ref.shape[0]

    @pl.when(j == 0)
    def _():
        s_scr[...] = s0_ref[...]
        m_scr[...] = m0_ref[...]

    row = lax.broadcasted_iota(jnp.int32, (L, L), 0)
    col = lax.broadcasted_iota(jnp.int32, (L, L), 1)
    sub = lax.broadcasted_iota(jnp.int32, (MLSTM_ST - MLSTM_DH, L), 0)

    dirs = ((qtf_ref, kf_ref, vtf_ref, grf_ref, gcf_ref, hf_ref), (qtb_ref, kb_ref, vtb_ref, grb_ref, gcb_ref, hb_ref))
    combos = [(d, hh) for d in range(2) for hh in range(MLSTM_HEADS)]


    masks, cum_rows, cum_cols, g_rows, g_cols = [], [], [], [], []
    for d in range(2):
        s_le_t = (row <= col) if d == 0 else (row >= col)
        tri = jnp.where((col <= row) if d == 0 else (col >= row), 1.0, 0.0).astype(BF16)
        tri_t = jnp.where(s_le_t, 1.0, 0.0).astype(BF16)
        gr = dirs[d][3][...]
        gc = dirs[d][4][...]
        r1, r2, r3 = _split3(gr)
        c1, c2, c3 = _split3(gc)
        masks.append(s_le_t)
        g_rows.append(gr)
        g_cols.append(gc)
        cum_rows.append(_dotf(tri, r1) + _dotf(tri, r2) + _dotf(tri, r3))
        cum_cols.append(_dotf(c1, tri_t) + _dotf(c2, tri_t) + _dotf(c3, tri_t))
    kq, inter, s_prev, m_prev = [], [], [], []
    for d, hh in combos:
        idx = d * MLSTM_HEADS + hh
        sl = slice(MLSTM_DH * hh, MLSTM_DH * (hh + 1))
        qth = dirs[d][0][sl, :]
        s_prev.append(s_scr[idx])
        m_prev.append(m_scr[idx:idx + 1, 0:1])
        kq.append(_dotf(dirs[d][1][:, sl], qth))
        inter.append(_dotf(s_prev[-1].astype(BF16), qth))

    sw, den, w_inter, floor, vw, decay, m_new = [], [], [], [], [], [], []
    for c, (d, hh) in enumerate(combos):
        il = (0 if d == 0 else 2 * MLSTM_HEADS) + hh
        fl = il + MLSTM_HEADS
        sl = slice(MLSTM_DH * hh, MLSTM_DH * (hh + 1))
        c_col = g_rows[d][:, il:il + 1] - cum_rows[d][:, fl:fl + 1]
        b_row = cum_cols[d][fl:fl + 1, :]
        c_row = g_cols[d][il:il + 1, :] - b_row
        c_mat = jnp.where(masks[d], c_col, NEG_BIG)
        u = jnp.maximum(jnp.max(c_mat, axis=0, keepdims=True), m_prev[c])
        s = kq[c] * jnp.exp(c_mat - u)
        sw.append(s.astype(BF16))
        den.append(jnp.sum(s, axis=0, keepdims=True))
        w_inter.append(jnp.exp(m_prev[c] - u))
        floor.append(jnp.exp(-(b_row + u)))
        b_end = b_row[:, L - 1:L] if d == 0 else b_row[:, 0:1]
        g_row = b_end + c_row
        m_new.append(jnp.maximum(b_end + m_prev[c], jnp.max(g_row, axis=-1, keepdims=True)))
        decay.append(jnp.exp(b_end + m_prev[c] - m_new[c]))
        w_s = jnp.exp(g_row - m_new[c])
        n_rows = jnp.where(sub == 0, w_s, 0.0)
        vw.append(jnp.concatenate([dirs[d][2][sl, :].astype(F32) * w_s, n_rows], axis=0).astype(BF16))

    for c, (d, hh) in enumerate(combos):
        idx = d * MLSTM_HEADS + hh
        sl = slice(MLSTM_DH * hh, MLSTM_DH * (hh + 1))
        num = _dotf(dirs[d][2][sl, :], sw[c]) + w_inter[c] * inter[c][:MLSTM_DH, :]
        dn = den[c] + w_inter[c] * inter[c][MLSTM_DH:MLSTM_DH + 1, :]
        dirs[d][5][sl, :] = (num / jnp.maximum(jnp.abs(dn), floor[c])).astype(BF16)
        s_scr[idx] = decay[c] * s_prev[c] + _dotf(vw[c], dirs[d][1][:, sl])
        m_scr[idx:idx + 1, :] = jnp.broadcast_to(m_new[c], (1, MLSTM_DH))

    @pl.when(j == nj - 1)
    def _():
        s_out_ref[...] = s_scr[...]
        m_out_ref[...] = m_scr[...]


def _mlstm(mqt, mk, mvt, g_rows, g_cols, s0, m0, batch, chunk):
    n = mk.shape[0]
    nc = n // batch // chunk
    fwd = lambda b, j: (b * nc + j, 0)
    bwd = lambda b, j: (b * nc + (nc - 1 - j), 0)
    fwd_t = lambda b, j: (0, b * nc + j)
    bwd_t = lambda b, j: (0, b * nc + (nc - 1 - j))
    nst = 2 * MLSTM_HEADS
    tok = lambda fn: pl.BlockSpec((chunk, MLSTM_WIDTH), fn)
    tok_t = lambda fn: pl.BlockSpec((MLSTM_WIDTH, chunk), fn)
    state_s = pl.BlockSpec((None, nst, MLSTM_ST, MLSTM_DH), lambda b, j: (b, 0, 0, 0))
    state_m = pl.BlockSpec((None, nst, MLSTM_DH), lambda b, j: (b, 0, 0))
    return pl.pallas_call(
        _mlstm_kernel,
        grid=(batch, nc),
        in_specs=[tok_t(fwd_t), tok(fwd), tok_t(fwd_t), pl.BlockSpec((chunk, N_GATES), fwd),
                  pl.BlockSpec((N_GATES, chunk), fwd_t),
                  tok_t(bwd_t), tok(bwd), tok_t(bwd_t), pl.BlockSpec((chunk, N_GATES), bwd),
                  pl.BlockSpec((N_GATES, chunk), bwd_t),
                  state_s, state_m],
        out_specs=[tok_t(fwd_t), tok_t(bwd_t), state_s, state_m],
        out_shape=[jax.ShapeDtypeStruct((MLSTM_WIDTH, n), BF16), jax.ShapeDtypeStruct((MLSTM_WIDTH, n), BF16),
                   jax.ShapeDtypeStruct(s0.shape, F32), jax.ShapeDtypeStruct(m0.shape, F32)],
        scratch_shapes=[pltpu.VMEM((nst, MLSTM_ST, MLSTM_DH), F32), pltpu.VMEM((nst, MLSTM_DH), F32)],
        compiler_params=_cparams("parallel", "arbitrary"),
        name="mlstm",
    )(mqt, mk, mvt, g_rows, g_cols, mqt, mk, mvt, g_rows, g_cols, s0, m0)


def _even_out_kernel(x_ref, gate_ref, att_ref, hf_ref, hb_ref, mo_ref, og_ref, w_ref,
                     fmod_ref, fg_ref, fwg_ref, fwu_ref, fwd_ref, o_ref):
    hs = hf_ref[...].astype(F32) + hb_ref[...].astype(F32)
    og = og_ref[...]
    parts = [att_ref[...]]
    for hh in range(MLSTM_HEADS):
        sl = slice(MLSTM_DH * hh, MLSTM_DH * (hh + 1))
        blk = hs[sl, :]
        hn = blk * lax.rsqrt(jnp.mean(blk * blk, axis=0, keepdims=True) + EPS) * og[sl, :]
        parts.append((hn * jax.nn.sigmoid(mo_ref[sl, :].astype(F32))).astype(BF16))
    mixed = jnp.concatenate(parts, axis=0)
    y = lax.dot_general(mixed, w_ref[...], (((0,), (0,)), ((), ())), preferred_element_type=F32)
    o_ref[...] = _ffn_math(x_ref[...] + gate_ref[...] * y, fmod_ref, fg_ref, fwg_ref, fwu_ref, fwd_ref)


def _even_out(x2d, gates, row_fn, att, hf, hb, mo, out_g, w_out, ffn_args, tm):
    n = x2d.shape[0]
    tok = pl.BlockSpec((tm, D_MODEL), lambda i: (i, 0))
    tok_t = pl.BlockSpec((MLSTM_WIDTH, tm), lambda i: (0, i))
    return pl.pallas_call(
        _even_out_kernel,
        grid=(n // tm,),
        in_specs=[tok, pl.BlockSpec((None, 1, D_MODEL), lambda i: (row_fn(i), 0, 0)),
                  tok_t, tok_t, tok_t, tok_t, _const_spec((MLSTM_WIDTH, 1)), _const_spec((D_MODEL, D_MODEL))]
        + _ffn_specs(row_fn, *ffn_args[0]),
        out_specs=tok,
        out_shape=jax.ShapeDtypeStruct((n, D_MODEL), F32),
        compiler_params=_cparams("parallel"),
        name="even_out_ffn",
    )(x2d, gates, att, hf, hb, mo, out_g.reshape(MLSTM_WIDTH, 1), w_out, *ffn_args[1:])


def _gelu_tanh(x):
    return 0.5 * x * (1.0 + jnp.tanh(0.7978845608028654 * (x + 0.044715 * (x * x * x))))


def _odd_in_kernel(x_ref, xp_ref, xn_ref, mod_ref, g_ref, w_ref, cw_ref, cb_ref, gate_ref, xc_ref, ext_scr, *, nt):
    tm = x_ref.shape[0]
    ti = pl.program_id(0) % nt
    g, shift, scale = g_ref[...], mod_ref[0:1, :], mod_ref[1:2, :]
    halo = jnp.concatenate([xp_ref[...], xn_ref[...]], axis=0)
    ph = _dotf(_norm_mod(halo, g, shift, scale).astype(BF16), w_ref[:, RNN_WIDTH:])
    ext_scr[0:8, :] = jnp.where(ti > 0, ph[0:8, :], 0.0)
    ext_scr[8 + tm:16 + tm, :] = jnp.where(ti < nt - 1, ph[8:16, :], 0.0)
    cw = cw_ref[...]
    cb = cb_ref[...]
    rc = tm // ODD_IN_CHUNKS

    def project(c):
        rows = slice(rc * c, rc * (c + 1))
        p = _dotf(_norm_mod(x_ref[rows, :], g, shift, scale).astype(BF16), w_ref[...])
        gate_ref[rows, :] = _gelu_tanh(p[:, :RNN_WIDTH]).astype(BF16)
        ext_scr[8 + rc * c:8 + rc * (c + 1), :] = p[:, RNN_WIDTH:]

    def conv(c):
        win = ext_scr[rc * c:rc * (c + 1) + 16, :]
        xc = cb + cw[2:3, :] * win[8:8 + rc, :]
        for k in (0, 1, 3):
            xc = xc + cw[k:k + 1, :] * pltpu.roll(win, (2 - k) % (rc + 16), axis=0)[8:8 + rc, :]
        xc_ref[rc * c:rc * (c + 1), :] = xc

    project(0)
    for c in range(1, ODD_IN_CHUNKS):
        project(c)
        conv(c - 1)
    conv(ODD_IN_CHUNKS - 1)


def _odd_in(x2d, mods, row_fn, g, w_in, conv_w, conv_b, batch, tm):
    n = x2d.shape[0]
    nt = n // batch // tm
    t8 = tm // 8
    tok = pl.BlockSpec((tm, D_MODEL), lambda i: (i, 0))
    prev = pl.BlockSpec((8, D_MODEL), lambda i: (jnp.maximum(i * t8 - 1, 0), 0))
    nxt = pl.BlockSpec((8, D_MODEL), lambda i: (jnp.minimum((i + 1) * t8, n // 8 - 1), 0))
    return pl.pallas_call(
        functools.partial(_odd_in_kernel, nt=nt),
        grid=(n // tm,),
        in_specs=[tok, prev, nxt, pl.BlockSpec((None, 3, D_MODEL), lambda i: (row_fn(i), 0, 0)),
                  _const_spec((1, D_MODEL)), _const_spec((D_MODEL, 2 * RNN_WIDTH)),
                  _const_spec((CONV_W, RNN_WIDTH)), _const_spec((1, RNN_WIDTH))],
        out_specs=[tok, tok],
        out_shape=[jax.ShapeDtypeStruct((n, RNN_WIDTH), BF16), jax.ShapeDtypeStruct((n, RNN_WIDTH), F32)],
        scratch_shapes=[pltpu.VMEM((tm + 16, RNN_WIDTH), F32)],
        compiler_params=_cparams("parallel"),
        name="odd_in",
    )(x2d, x2d, x2d, mods, g.reshape(1, D_MODEL), w_in, conv_w, conv_b)


def _rglru_kernel(xc_ref, wax_ref, ba_ref, bx_ref, lam_ref, h0_ref, h_ref, ht_ref, a_scr, b_scr, carry_scr, *, nt):
    d = pl.program_id(0)
    i = pl.program_id(2)
    tm = xc_ref.shape[0]
    nb8 = tm // 8

    xc = xc_ref[...]
    neg_lam = -lam_ref[...]
    half_rate = (-0.5 * LRU_C) * (jnp.maximum(neg_lam, 0.0) + jnp.log1p(jnp.exp(-jnp.abs(neg_lam))))
    ba = ba_ref[...]
    bx = bx_ref[...]
    for nb in range(RNN_BLOCKS):
        sl = slice(RNN_BLOCK_DIM * nb, RNN_BLOCK_DIM * (nb + 1))
        u = xc[:, sl]
        rg = _dotf(u.astype(BF16), wax_ref[nb])
        log_a = jnp.tanh(rg[:, :RNN_BLOCK_DIM] + ba[:, sl]) * half_rate[:, sl] + half_rate[:, sl]
        ig = 0.5 * jnp.tanh(rg[:, RNN_BLOCK_DIM:] + bx[:, sl]) + 0.5
        a = jnp.exp(log_a)
        a_scr[:, sl] = a
        b_scr[:, sl] = jnp.sqrt(-jnp.tanh(log_a) * (a * a + 1.0)) * (ig * u)

    @pl.when(i == 0)
    def _():
        carry_scr[...] = h0_ref[...]

    def scan_tile(reverse):
        a = a_scr[...].reshape(nb8, 8, RNN_WIDTH)
        b = b_scr[...].reshape(nb8, 8, RNN_WIDTH)
        r8 = lax.broadcasted_iota(jnp.int32, (nb8, 8, RNN_WIDTH), 1)
        for s in (1, 2, 4):
            keep = (r8 < 8 - s) if reverse else (r8 >= s)
            shift = 8 - s if reverse else s
            a_in = jnp.where(keep, pltpu.roll(a, shift, axis=1), 1.0)
            b_in = jnp.where(keep, pltpu.roll(b, shift, axis=1), 0.0)
            b = a * b_in + b
            a = a * a_in
        a_scr[...] = a.reshape(tm, RNN_WIDTH)
        b_scr[...] = b.reshape(tm, RNN_WIDTH)

        def block_pair(jp, h):
            base = pl.multiple_of((nb8 // 2 - 1 - jp if reverse else jp) * 16, 16)
            halves = [None, None]
            for half in ((1, 0) if reverse else (0, 1)):
                rows = pl.ds(base + 8 * half, 8)
                hb = a_scr[rows, :] * h + b_scr[rows, :]
                halves[half] = hb
                h = hb[0:1, :] if reverse else hb[7:8, :]
            h_ref[pl.ds(base, 16), :] = jnp.concatenate(halves, axis=0).astype(h_ref.dtype)
            return h

        carry_scr[...] = lax.fori_loop(0, nb8 // 2, block_pair, carry_scr[...], unroll=2)

    @pl.when(d == 0)
    def _():
        scan_tile(False)

    @pl.when(d == 1)
    def _():
        scan_tile(True)

    @pl.when(i == nt - 1)
    def _():
        ht_ref[...] = carry_scr[...]


def _rglru(xc2d, h0, w, batch, tm):
    n = xc2d.shape[0]
    nt = n // batch // tm
    tile = lambda d, i: jnp.where(d == 0, i, nt - 1 - i)
    per_dir = lambda shape: pl.BlockSpec((None,) + shape, lambda d, b, i: (d,) + (0,) * len(shape))
    state = pl.BlockSpec((None, None, 1, RNN_WIDTH), lambda d, b, i: (d, b, 0, 0))
    return pl.pallas_call(
        functools.partial(_rglru_kernel, nt=nt),
        grid=(2, batch, nt),
        in_specs=[pl.BlockSpec((tm, RNN_WIDTH), lambda d, b, i: (b * nt + tile(d, i), 0)),
                  per_dir((RNN_BLOCKS, RNN_BLOCK_DIM, 2 * RNN_BLOCK_DIM)), per_dir((1, RNN_WIDTH)), per_dir((1, RNN_WIDTH)),
                  per_dir((1, RNN_WIDTH)), state],
        out_specs=[pl.BlockSpec((None, tm, RNN_WIDTH), lambda d, b, i: (d, b * nt + tile(d, i), 0)), state],
        out_shape=[jax.ShapeDtypeStruct((2, n, RNN_WIDTH), BF16), jax.ShapeDtypeStruct((2, batch, 1, RNN_WIDTH), F32)],
        scratch_shapes=[pltpu.VMEM((tm, RNN_WIDTH), F32), pltpu.VMEM((tm, RNN_WIDTH), F32),
                        pltpu.VMEM((1, RNN_WIDTH), F32)],
        compiler_params=_cparams("parallel", "parallel", "arbitrary"),
        name="rglru",
    )(xc2d, w["w_ax"], w["b_a"], w["b_x"], w["lam"], h0)


def _odd_out_kernel(x_ref, gate_ref, h_ref, gg_ref, w_ref, fmod_ref, fg_ref, fwg_ref, fwu_ref, fwd_ref, o_ref):
    y = ((h_ref[0].astype(F32) + h_ref[1].astype(F32)) * gg_ref[...].astype(F32)).astype(BF16)
    x_mid = x_ref[...] + gate_ref[...] * _dotf(y, w_ref[...])
    o_ref[...] = _ffn_math(x_mid, fmod_ref, fg_ref, fwg_ref, fwu_ref, fwd_ref)


def _odd_out(x2d, gates, row_fn, h2, gate_branch, w_out, ffn_args, tm):
    n = x2d.shape[0]
    tok = pl.BlockSpec((tm, D_MODEL), lambda i: (i, 0))
    return pl.pallas_call(
        _odd_out_kernel,
        grid=(n // tm,),
        in_specs=[tok, pl.BlockSpec((None, 1, D_MODEL), lambda i: (row_fn(i), 0, 0)),
                  pl.BlockSpec((2, tm, RNN_WIDTH), lambda i: (0, i, 0)), tok, _const_spec((RNN_WIDTH, D_MODEL))]
        + _ffn_specs(row_fn, *ffn_args[0]),
        out_specs=tok,
        out_shape=jax.ShapeDtypeStruct((n, D_MODEL), F32),
        compiler_params=_cparams("parallel"),
        name="odd_out_ffn",
    )(x2d, gates, h2, gate_branch, w_out, *ffn_args[1:])


def _rope_tables(seq):
    t = np.arange(seq)
    inv = np.float32(ROPE_BASE) ** (-np.arange(0, ROPE_AXIS_DIM, 2, dtype=np.float32) / np.float32(ROPE_AXIS_DIM))
    ang_r = (t // GRID_W).astype(np.float32)[:, None] * inv
    ang_c = (t % GRID_W).astype(np.float32)[:, None] * inv
    cos32 = np.concatenate([np.cos(ang_r), np.cos(ang_r), np.cos(ang_c), np.cos(ang_c)], axis=1)
    sin32 = np.concatenate([-np.sin(ang_r), np.sin(ang_r), -np.sin(ang_c), np.sin(ang_c)], axis=1)
    pad = HEAD_BLOCK - MLA_QK
    cos = np.concatenate([np.ones((seq, MLA_NOPE), np.float32), cos32, np.ones((seq, pad), np.float32)], axis=1)
    sin = np.concatenate([np.zeros((seq, MLA_NOPE), np.float32), sin32, np.zeros((seq, pad), np.float32)], axis=1)
    cost = np.concatenate([np.cos(ang_r), np.cos(ang_c)], axis=1).T
    sint = np.concatenate([np.sin(ang_r), np.sin(ang_c)], axis=1).T
    return tuple(jnp.asarray(a, F32) for a in (cos, sin, cost, sint))


def _even_weights(w_in, cq_g, w_uq, ckv_g, w_ukv, q_g, k_g, gate_b):
    pad = HEAD_BLOCK - MLA_QK
    w_in, w_uq, w_ukv = w_in.astype(BF16), w_uq.astype(BF16), w_ukv.astype(BF16)
    kr_cols = w_in[:, OFF_KR:OFF_KR + MLA_ROPE]
    w_kr = jnp.pad(kr_cols, ((0, 0), (MLA_NOPE, pad)))
    m_cols = w_in[:, OFF_KR + MLA_ROPE:OFF_KR + MLA_ROPE + 4 * MLSTM_WIDTH]
    g_cols = w_in[:, OFF_KR + MLA_ROPE + 4 * MLSTM_WIDTH:]
    mq_cols, mk_cols, mv_cols, mo_cols = jnp.split(m_cols, 4, axis=1)
    w_all = jnp.concatenate([w_in[:, :OFF_KR], w_kr, mk_cols, jnp.pad(g_cols, ((0, 0), (0, HEAD_BLOCK - N_GATES)))], axis=1)
    uq = jnp.pad(w_uq.reshape(MLA_Q_RANK, MLA_HEADS, MLA_QK), ((0, 0), (0, 0), (0, pad)))
    ukv = w_ukv.reshape(MLA_KV_RANK, MLA_HEADS, MLA_NOPE + MLA_V)
    uk = jnp.pad(ukv[:, :, :MLA_NOPE], ((0, 0), (0, 0), (0, HEAD_BLOCK - MLA_NOPE)))
    uv = ukv[:, :, MLA_NOPE:]
    kg_blk = jnp.pad(k_g[:MLA_NOPE], (0, HEAD_BLOCK - MLA_NOPE))
    krg_blk = jnp.pad(k_g[MLA_NOPE:], (MLA_NOPE, pad))
    return {
        "w_all": w_all,
        "w_gt": jnp.concatenate([g_cols, mq_cols, mv_cols, mo_cols], axis=1).T,
        "gb_row": gate_b.reshape(1, N_GATES),
        "gb_col": gate_b.reshape(N_GATES, 1),
        "cq_g": cq_g.reshape(1, MLA_Q_RANK),
        "w_uqt": uq.reshape(MLA_Q_RANK, MLA_HEADS * HEAD_BLOCK).T,
        "ckv_g": ckv_g.reshape(1, MLA_KV_RANK),
        "w_uk": uk.reshape(MLA_KV_RANK, MLA_HEADS * HEAD_BLOCK),
        "w_uvt": uv.reshape(MLA_KV_RANK, MLA_HEADS * MLA_V).T,
        "qg_nope": q_g[:MLA_NOPE].reshape(MLA_NOPE, 1),
        "qg_rope": q_g[MLA_NOPE:].reshape(MLA_ROPE, 1),
        "k_g": jnp.tile(kg_blk, MLA_HEADS).reshape(1, -1),
        "kr_g": krg_blk.reshape(1, HEAD_BLOCK),
    }


def kernel(x, c, ctx, c_ctx, mod_w, mod_b, norm_g, ffn_w_gate, ffn_w_up, ffn_w_down, even_w_in, even_w_out, mla_cq_g, mla_w_uq, mla_ckv_g, mla_w_ukv, mla_q_g, mla_k_g, mlstm_gate_b, mlstm_out_g, odd_w_in, odd_conv_w, odd_conv_b, lru_w_a, lru_b_a, lru_w_x, lru_b_x, lru_lam, odd_w_out):
    batch, seq, _ = x.shape
    ctx_len = ctx.shape[1]
    depth = mod_w.shape[0]
    assert depth == 2 and batch <= 7 and ctx_len == ROWS_C

    cond8 = jnp.zeros((8, D_MODEL), F32).at[:batch].set(c).at[batch].set(c_ctx)
    mods = _ada_params(cond8, mod_w, mod_b)
    mods = mods.reshape(depth, 8, 3, 3, D_MODEL).transpose(0, 2, 1, 3, 4)

    x_row = lambda tm: (lambda i: i // (seq // tm))
    c_row = lambda i: batch
    x2 = x.reshape(batch * seq, D_MODEL)
    c2 = ctx.reshape(batch * ctx_len, D_MODEL)

    wg = ffn_w_gate.astype(BF16)
    wu = ffn_w_up.astype(BF16)
    wd = ffn_w_down.astype(BF16)

    def ffn_args(layer, which):
        return ((layer, which), mods[layer, 2 * which], norm_g[layer, 2 * which].reshape(1, D_MODEL), wg, wu, wd)

    x2 = _ffn(x2, ffn_args(0, 0), x_row(ROWS_X), ROWS_X)
    c2 = _ffn(c2, ffn_args(0, 0), c_row, ROWS_C)

    ew = _even_weights(even_w_in[0], mla_cq_g[0], mla_w_uq[0], mla_ckv_g[0], mla_w_ukv[0], mla_q_g[0], mla_k_g[0],
                       mlstm_gate_b[0])
    tabs_x = _rope_tables(seq)
    tabs_c = (jnp.ones((ROWS_C, HEAD_BLOCK), F32), jnp.zeros((ROWS_C, HEAD_BLOCK), F32),
              jnp.ones((ROPE_AXIS_DIM, ROWS_C), F32), jnp.zeros((ROPE_AXIS_DIM, ROWS_C), F32))
    key_len = seq + ctx_len
    px = _even_prep(x2, mods[0, 1], x_row(ROWS_X), norm_g[0, 1], tabs_x, lambda i: i % (seq // ROWS_X), ew, ROWS_X,
                    batch, key_len, 0)
    qt_x, k_all, vt_all, mqt_x, mk_x, mvt_x, mot_x, gr_x, gc_x = px
    pc = _even_prep(c2, mods[0, 1], c_row, norm_g[0, 1], tabs_c, lambda i: 0, ew, ROWS_C,
                    batch, key_len, seq, kv_bufs=(k_all, vt_all))
    qt_c, k_all, vt_all, mqt_c, mk_c, mvt_c, mot_c, gr_c, gc_c = pc

    att_x = _attention(qt_x, k_all, vt_all, batch, ATTN_QUERIES_X, key_len, 0)
    att_c = _attention(qt_c, k_all, vt_all, batch, ROWS_C, ctx_len, seq)

    nst = 2 * MLSTM_HEADS
    s0 = jnp.zeros((batch, nst, MLSTM_ST, MLSTM_DH), F32)
    m0 = jnp.zeros((batch, nst, MLSTM_DH), F32)
    hcf, hcb, s_c, m_c = _mlstm(mqt_c, mk_c, mvt_c, gr_c, gc_c, s0, m0, batch, MLSTM_CHUNK)
    hxf, hxb, _, _ = _mlstm(mqt_x, mk_x, mvt_x, gr_x, gc_x, s_c, m_c, batch, MLSTM_CHUNK)

    w_out0 = even_w_out[0].astype(BF16)
    gate0 = mods[0, 1][:, 2:3, :]
    x2 = _even_out(x2, gate0, x_row(ROWS_X), att_x, hxf, hxb, mot_x, mlstm_out_g[0], w_out0, ffn_args(0, 1), ROWS_X)
    c2 = _even_out(c2, gate0, c_row, att_c, hcf, hcb, mot_c, mlstm_out_g[0], w_out0, ffn_args(0, 1), ROWS_C)

    x2 = _ffn(x2, ffn_args(1, 0), x_row(ROWS_X_FFN), ROWS_X_FFN)
    c2 = _ffn(c2, ffn_args(1, 0), c_row, ROWS_C)
    w_in1 = odd_w_in[0].astype(BF16)
    conv_b = odd_conv_b[0].reshape(1, RNN_WIDTH)
    gg_x, xc_x = _odd_in(x2, mods[1, 1], x_row(ROWS_X), norm_g[1, 1], w_in1, odd_conv_w[0], conv_b, batch, ROWS_X)
    _, xc_c = _odd_in(c2, mods[1, 1], c_row, norm_g[1, 1], w_in1, odd_conv_w[0], conv_b, batch, ROWS_C)
    rw = {
        "w_ax": (0.5 * jnp.concatenate([lru_w_a[0], lru_w_x[0]], axis=-1)).astype(BF16),
        "b_a": 0.5 * lru_b_a[0].reshape(2, 1, RNN_WIDTH),
        "b_x": 0.5 * lru_b_x[0].reshape(2, 1, RNN_WIDTH),
        "lam": lru_lam[0].reshape(2, 1, RNN_WIDTH),
    }
    h0 = jnp.zeros((2, batch, 1, RNN_WIDTH), F32)
    _, st_c = _rglru(xc_c, h0, rw, batch, ROWS_C)
    h_x, _ = _rglru(xc_x, st_c, rw, batch, RGLRU_ROWS)
    x2 = _odd_out(x2, mods[1, 1][:, 2:3, :], x_row(ROWS_X), h_x, gg_x, odd_w_out[0].astype(BF16), ffn_args(1, 1),
                  ROWS_X)
    return x2.reshape(batch, seq, D_MODEL)
```
